```python
import jax, jax.numpy as jnp
from jax import lax
import numpy as np

D_MODEL = 1024
BATCH = 4
SEQ = 4096
DEPTH = 1

GRID_W = 64
EPS = 1e-6
D_CONV = 1024
CONV_WIDTH = 3
N_HEADS = 8
N_KV_HEADS = 2
HEAD_DIM = 128
AXIS_DIM = HEAD_DIM // 2
ROPE_THETA = 10000.0
Q_BLOCK = 128
D_Q = N_HEADS * HEAD_DIM
D_KV = N_KV_HEADS * HEAD_DIM
N_GROUPS = 8
EXPERTS_PER_GROUP = 8
N_EXPERTS = N_GROUPS * EXPERTS_PER_GROUP
TOP_K = 2
D_EXPERT = 512
MOE_BLOCK = 128
IN_SPLITS = (D_CONV, D_CONV, D_CONV, D_Q, D_KV, D_KV, D_MODEL, D_MODEL)
D_IN = D_CONV * 3 + D_Q + 2 * D_KV + 2 * D_MODEL

kernel_name = "hybrid_conv_attn_hiermoe_block"


def rmsnorm(x, g):
    xf = x.astype(jnp.float32)
    y = xf * lax.rsqrt(jnp.mean(xf * xf, axis=-1, keepdims=True) + EPS)
    return (y * g.astype(jnp.float32)).astype(x.dtype)


def split_columns(p):
    parts, off = [], 0
    for w in IN_SPLITS:
        parts.append(p[..., off:off + w])
        off += w
    return parts


def axial_rope_tables(S):
    rows = S // GRID_W
    row = jnp.repeat(jnp.arange(rows, dtype=jnp.float32), GRID_W)
    col = jnp.tile(jnp.arange(GRID_W, dtype=jnp.float32), rows)
    inv = ROPE_THETA ** (-jnp.arange(0, AXIS_DIM, 2, dtype=jnp.float32) / AXIS_DIM)
    ang = jnp.concatenate([row[:, None] * inv, col[:, None] * inv], axis=-1)
    return jnp.cos(ang), jnp.sin(ang)


def apply_rope(x, cos, sin):
    xf = x.astype(jnp.float32).reshape(*x.shape[:-1], HEAD_DIM // 2, 2)
    x0, x1 = xf[..., 0], xf[..., 1]
    c = cos[None, :, None, :]
    s = sin[None, :, None, :]
    out = jnp.stack([x0 * c - x1 * s, x0 * s + x1 * c], axis=-1)
    return out.reshape(x.shape).astype(x.dtype)


def bidirectional_gqa(q, k, v):
    B, S = q.shape[0], q.shape[1]
    G = N_HEADS // N_KV_HEADS
    nq = S // Q_BLOCK
    qb = q.reshape(B, nq, Q_BLOCK, N_KV_HEADS, G, HEAD_DIM).transpose(1, 0, 2, 3, 4, 5)

    def attend(qblk):
        s = jnp.einsum('bqkgd,bskd->bkgqs', qblk, k, preferred_element_type=jnp.float32)
        p = jax.nn.softmax(s, axis=-1)
        return jnp.einsum('bkgqs,bskd->bqkgd', p.astype(v.dtype), v)

    o = lax.map(attend, qb)
    return o.transpose(1, 0, 2, 3, 4, 5).reshape(B, S, D_Q)


def depthwise_conv_centred(z, w):
    rhs = w.reshape(CONV_WIDTH, 1, D_CONV)
    pad = (CONV_WIDTH - 1) // 2
    return lax.conv_general_dilated(z, rhs, window_strides=(1,), padding=((pad, pad),),
                                    dimension_numbers=('NWC', 'WIO', 'NWC'),
                                    feature_group_count=D_CONV)


def mixer_block(u, cos, sin, w_in, conv_w, q_norm_g, k_norm_g, w_conv_out, w_attn_out, w_o):
    B, S, _ = u.shape
    proj = u @ w_in
    cb, cc, cx, q, k, v, gc, ga = split_columns(proj)
    z = depthwise_conv_centred(cc * cx, conv_w)
    y_conv = (cb * z) @ w_conv_out
    q = rmsnorm(q.reshape(B, S, N_HEADS, HEAD_DIM), q_norm_g)
    k = rmsnorm(k.reshape(B, S, N_KV_HEADS, HEAD_DIM), k_norm_g)
    v = v.reshape(B, S, N_KV_HEADS, HEAD_DIM)
    q = apply_rope(q, cos, sin) * (HEAD_DIM ** -0.5)
    k = apply_rope(k, cos, sin)
    y_attn = bidirectional_gqa(q, k, v) @ w_attn_out
    merged = jax.nn.sigmoid(gc) * y_conv + jax.nn.sigmoid(ga) * y_attn
    return merged @ w_o


def hier_moe(u, w_group, w_router, w1, w3, w2):
    B, S, D = u.shape
    T = B * S
    hf = u.reshape(T, D)
    grp_logits = (hf @ w_group).astype(jnp.float32)
    grp_prob = jax.nn.softmax(grp_logits, axis=-1)
    g = jnp.argmax(grp_logits, axis=-1)
    pg = jnp.take_along_axis(grp_prob, g[:, None], axis=1)
    exp_logits = (hf @ w_router).astype(jnp.float32).reshape(T, N_GROUPS, EXPERTS_PER_GROUP)
    sel = jnp.take_along_axis(exp_logits, g[:, None, None], axis=1)[:, 0]
    top_v, top_i = lax.top_k(sel, TOP_K)
    weights = pg * jax.nn.softmax(top_v, axis=-1)
    expert_ids = g[:, None] * EXPERTS_PER_GROUP + top_i
    A = T * TOP_K
    e_flat = expert_ids.reshape(A).astype(jnp.int32)
    w_flat = weights.reshape(A)
    tok_flat = jnp.arange(A, dtype=jnp.int32) // TOP_K
    order = jnp.argsort(e_flat)
    se, st, sw = e_flat[order], tok_flat[order], w_flat[order]
    counts = jnp.bincount(e_flat, length=N_EXPERTS).astype(jnp.int32)
    starts = jnp.cumsum(counts) - counts
    pcounts = ((counts + MOE_BLOCK - 1) // MOE_BLOCK) * MOE_BLOCK
    pends = jnp.cumsum(pcounts)
    pstarts = pends - pcounts
    dest = pstarts[se] + (jnp.arange(A, dtype=jnp.int32) - starts[se])
    P = A + N_EXPERTS * MOE_BLOCK
    NB = P // MOE_BLOCK
    row_tok = jnp.full((P,), T, jnp.int32).at[dest].set(st)
    row_w = jnp.zeros((P,), jnp.float32).at[dest].set(sw)
    blk_expert = jnp.minimum(
        jnp.searchsorted(pends, jnp.arange(NB, dtype=jnp.int32) * MOE_BLOCK, side='right'),
        N_EXPERTS - 1).astype(jnp.int32)
    x_pad = jnp.concatenate([hf, jnp.zeros((1, D), hf.dtype)], axis=0)
    xs = x_pad[row_tok].reshape(NB, MOE_BLOCK, D)

    def expert_block(args):
        xb, e = args
        return (jax.nn.silu(xb @ w1[e]) * (xb @ w3[e])) @ w2[e]

    ys = lax.map(expert_block, (xs, blk_expert)).reshape(P, D)
    ys = ys * row_w[:, None].astype(ys.dtype)
    out = jnp.zeros((T + 1, D), ys.dtype).at[row_tok].add(ys)[:T]
    return out.reshape(B, S, D)


def setup_inputs(seed: int = 0) -> dict:
    key = jax.random.key(seed)
    ks = jax.random.split(key, 16)

    def nrm(k, shape, scale):
        return jax.random.normal(k, shape, jnp.float32) * scale

    return {
        "x": nrm(ks[0], (BATCH, SEQ, D_MODEL), 1.0),
        "norm1_g": 1.0 + nrm(ks[1], (DEPTH, D_MODEL), 0.02),
        "w_in": nrm(ks[2], (DEPTH, D_MODEL, D_IN), D_MODEL ** -0.5),
        "conv_w": nrm(ks[3], (DEPTH, CONV_WIDTH, D_CONV), CONV_WIDTH ** -0.5),
        "q_norm_g": 1.0 + nrm(ks[4], (DEPTH, HEAD_DIM), 0.02),
        "k_norm_g": 1.0 + nrm(ks[5], (DEPTH, HEAD_DIM), 0.02),
        "w_conv_out": nrm(ks[6], (DEPTH, D_CONV, D_MODEL), D_CONV ** -0.5),
        "w_attn_out": nrm(ks[7], (DEPTH, D_Q, D_MODEL), D_Q ** -0.5),
        "w_o": nrm(ks[8], (DEPTH, D_MODEL, D_MODEL), D_MODEL ** -0.5),
        "norm2_g": 1.0 + nrm(ks[9], (DEPTH, D_MODEL), 0.02),
        "w_group": nrm(ks[10], (DEPTH, D_MODEL, N_GROUPS), D_MODEL ** -0.5),
        "w_router": nrm(ks[11], (DEPTH, D_MODEL, N_EXPERTS), D_MODEL ** -0.5),
        "w1": nrm(ks[12], (DEPTH, N_EXPERTS, D_MODEL, D_EXPERT), D_MODEL ** -0.5),
        "w3": nrm(ks[13], (DEPTH, N_EXPERTS, D_MODEL, D_EXPERT), D_MODEL ** -0.5),
        "w2": nrm(ks[14], (DEPTH, N_EXPERTS, D_EXPERT, D_MODEL), D_EXPERT ** -0.5),
    }


def reference(x, norm1_g, w_in, conv_w, q_norm_g, k_norm_g, w_conv_out, w_attn_out, w_o,
              norm2_g, w_group, w_router, w1, w3, w2):
    S = x.shape[1]
    cos, sin = axial_rope_tables(S)
    h = x
    for l in range(DEPTH):
        h = h + mixer_block(rmsnorm(h, norm1_g[l]), cos, sin, w_in[l], conv_w[l],
                            q_norm_g[l], k_norm_g[l], w_conv_out[l], w_attn_out[l], w_o[l])
        h = h + hier_moe(rmsnorm(h, norm2_g[l]), w_group[l], w_router[l], w1[l], w3[l], w2[l])
    return h
```

```python
import functools

import jax
import jax.numpy as jnp
from jax import lax
from jax.experimental import pallas as pl
from jax.experimental.pallas import tpu as pltpu

F32 = jnp.float32
BF16 = jnp.bfloat16

GRID_W = 64
EPS = 1e-6
N_HEADS = 8
N_KV_HEADS = 2
HEAD_DIM = 128
ROPE_THETA = 10000.0
N_GROUPS = 8
EXPERTS_PER_GROUP = 8
N_EXPERTS = N_GROUPS * EXPERTS_PER_GROUP
TOP_K = 2

LANES = 128
V7X_VMEM_LIMIT_BYTES = 56000 * 1024

TM_PROJ = 512
TQ_ATTN = 256
TK_ATTN = 512
TB_RANK = 512
TB_ROWS = 256
MOE_ROWS = 256
BF16_SUBLANES = 16


def _cparams(sem):
    return pltpu.CompilerParams(dimension_semantics=sem,
                                vmem_limit_bytes=V7X_VMEM_LIMIT_BYTES)


def _resident(shape):
    nd = len(shape)
    return pl.BlockSpec(shape, lambda *_: (0,) * nd, pipeline_mode=pl.Buffered(1))


def _head_norm_rope(xh, g, c, sa, sb):
    ms = jnp.mean(xh * xh, axis=-1, keepdims=True)
    y = xh * lax.rsqrt(ms + EPS) * g
    y_next = pltpu.roll(y, HEAD_DIM - 1, axis=1)
    y_prev = pltpu.roll(y, 1, axis=1)
    return y * c + y_next * sa + y_prev * sb


def _sigmoid(x):
    return 1.0 / (1.0 + jnp.exp(-x))


def _inproj_kernel(x_ref, g1_ref, w_ref, gq_ref, gk_ref,
                   cq_ref, saq_ref, sbq_ref, ck_ref, sak_ref, sbk_ref,
                   cb_ref, z_ref, q_ref, k_ref, v_ref, sgc_ref, sga_ref,
                   *, d_conv, d_q, d_kv, d_model):
    x = x_ref[...]
    ms = jnp.mean(x * x, axis=-1, keepdims=True)
    u = (x * lax.rsqrt(ms + EPS) * g1_ref[...]).astype(BF16)

    def proj(lo, width):
        return jnp.dot(u, w_ref[:, lo:lo + width], preferred_element_type=F32)

    off = 0
    cb_ref[...] = proj(off, d_conv).astype(BF16)
    off += d_conv
    cc = proj(off, d_conv)
    off += d_conv
    cx = proj(off, d_conv)
    off += d_conv
    z_ref[...] = (cc * cx).astype(BF16)

    q = proj(off, d_q)
    off += d_q
    gq = gq_ref[...]
    cq, saq, sbq = cq_ref[...], saq_ref[...], sbq_ref[...]
    for h in range(d_q // HEAD_DIM):
        sl = slice(h * HEAD_DIM, (h + 1) * HEAD_DIM)
        q_ref[:, sl] = _head_norm_rope(q[:, sl], gq, cq, saq, sbq).astype(BF16)

    k = proj(off, d_kv)
    off += d_kv
    gk = gk_ref[...]
    ck, sak, sbk = ck_ref[...], sak_ref[...], sbk_ref[...]
    for h in range(d_kv // HEAD_DIM):
        sl = slice(h * HEAD_DIM, (h + 1) * HEAD_DIM)
        k_ref[:, sl] = _head_norm_rope(k[:, sl], gk, ck, sak, sbk).astype(BF16)

    v_ref[...] = proj(off, d_kv).astype(BF16)
    off += d_kv
    sgc_ref[...] = _sigmoid(proj(off, d_model)).astype(BF16)
    off += d_model
    sga_ref[...] = _sigmoid(proj(off, d_model)).astype(BF16)


def _rope_tables(seq):
    rows = seq // GRID_W
    axis_dim = HEAD_DIM // 2
    row = jnp.repeat(jnp.arange(rows, dtype=F32), GRID_W)
    col = jnp.tile(jnp.arange(GRID_W, dtype=F32), rows)
    inv = ROPE_THETA ** (-jnp.arange(0, axis_dim, 2, dtype=F32) / axis_dim)
    ang = jnp.concatenate([row[:, None] * inv, col[:, None] * inv], axis=-1)
    cos, sin = jnp.cos(ang), jnp.sin(ang)
    zero = jnp.zeros_like(sin)
    c = jnp.repeat(cos, 2, axis=-1)
    sa = jnp.stack([-sin, zero], axis=-1).reshape(seq, HEAD_DIM)
    sb = jnp.stack([zero, sin], axis=-1).reshape(seq, HEAD_DIM)
    return c, sa, sb


def _inproj(x2, g1, w_in_bf, gq, gk, tables_q, tables_k, *, seq):
    t, d_model = x2.shape
    d_q = N_HEADS * HEAD_DIM
    d_kv = N_KV_HEADS * HEAD_DIM
    d_in = w_in_bf.shape[1]
    d_conv = (d_in - d_q - 2 * d_kv - 2 * d_model) // 3
    tm = TM_PROJ
    nseq = seq // tm

    def row(width):
        return pl.BlockSpec((tm, width), lambda i: (i, 0))

    table = pl.BlockSpec((tm, HEAD_DIM), lambda i: (i % nseq, 0))
    kern = functools.partial(_inproj_kernel, d_conv=d_conv, d_q=d_q, d_kv=d_kv,
                             d_model=d_model)
    out_shape = [jax.ShapeDtypeStruct((t, w), BF16)
                 for w in (d_conv, d_conv, d_q, d_kv, d_kv, d_model, d_model)]
    return pl.pallas_call(
        kern,
        grid=(t // tm,),
        in_specs=[row(d_model), _resident((1, d_model)), _resident((d_model, d_in)),
                  _resident((1, HEAD_DIM)), _resident((1, HEAD_DIM)),
                  table, table, table, table, table, table],
        out_specs=[row(d_conv), row(d_conv), row(d_q), row(d_kv), row(d_kv),
                   row(d_model), row(d_model)],
        out_shape=out_shape,
        compiler_params=_cparams(("arbitrary",)),
        name="inproj",
    )(x2, g1, w_in_bf, gq, gk, *tables_q, *tables_k)


def _attn_kernel(q_ref, k_ref, v_ref, o_ref, qs_ref, m_ref, l_ref, acc_ref,
                 *, tq, tk, nk, group):
    for g in range(group):
        qs_ref[g * tq:(g + 1) * tq, :] = q_ref[:, g * HEAD_DIM:(g + 1) * HEAD_DIM]
    m_ref[...] = jnp.full(m_ref.shape, -jnp.inf, F32)
    l_ref[...] = jnp.zeros(l_ref.shape, F32)
    acc_ref[...] = jnp.zeros(acc_ref.shape, F32)

    def body(j, carry):
        start = pl.multiple_of(j * tk, tk)
        kc = k_ref[pl.ds(start, tk), :]
        vc = v_ref[pl.ds(start, tk), :]
        s = lax.dot_general(qs_ref[...], kc, (((1,), (1,)), ((), ())),
                            preferred_element_type=F32)
        m_prev = m_ref[...]
        m_new = jnp.maximum(m_prev, jnp.max(s, axis=-1, keepdims=True))
        alpha = jnp.exp(m_prev - m_new)
        p = jnp.exp(s - m_new)
        l_ref[...] = alpha * l_ref[...] + jnp.sum(p, axis=-1, keepdims=True)
        acc_ref[...] = alpha * acc_ref[...] + jnp.dot(
            p.astype(BF16), vc, preferred_element_type=F32)
        m_ref[...] = m_new
        return carry

    lax.fori_loop(0, nk, body, 0)
    out = acc_ref[...] / l_ref[...]
    for g in range(group):
        o_ref[:, g * HEAD_DIM:(g + 1) * HEAD_DIM] = out[g * tq:(g + 1) * tq].astype(BF16)


def _attention(q, k, v, *, batch, seq):
    t = q.shape[0]
    group = N_HEADS // N_KV_HEADS
    tq, tk = TQ_ATTN, TK_ATTN
    nq = seq // tq
    gw = group * HEAD_DIM
    kern = functools.partial(_attn_kernel, tq=tq, tk=tk, nk=seq // tk, group=group)
    return pl.pallas_call(
        kern,
        grid=(batch, N_KV_HEADS, nq),
        in_specs=[pl.BlockSpec((tq, gw), lambda b, h, i: (b * nq + i, h)),
                  pl.BlockSpec((seq, HEAD_DIM), lambda b, h, i: (b, h)),
                  pl.BlockSpec((seq, HEAD_DIM), lambda b, h, i: (b, h))],
        out_specs=pl.BlockSpec((tq, gw), lambda b, h, i: (b * nq + i, h)),
        out_shape=jax.ShapeDtypeStruct((t, N_HEADS * HEAD_DIM), BF16),
        scratch_shapes=[pltpu.VMEM((group * tq, HEAD_DIM), BF16),
                        pltpu.VMEM((group * tq, 1), F32),
                        pltpu.VMEM((group * tq, 1), F32),
                        pltpu.VMEM((group * tq, HEAD_DIM), F32)],
        compiler_params=_cparams(("arbitrary", "arbitrary", "arbitrary")),
        name="attention",
    )(q, k, v)


def _post_kernel(x_ref, cb_ref, z_ref, zprev_ref, znext_ref, o_ref, sgc_ref, sga_ref,
                 cw_ref, wc_ref, wa_ref, wo_ref, g2_ref, wr_ref,
                 x1_ref, u2_ref, meta_ref, *, tm, nseq):
    i = pl.program_id(0)
    at_start = (i % nseq) == 0
    at_end = (i % nseq) == nseq - 1

    z = z_ref[...].astype(F32)
    prev_row = zprev_ref[BF16_SUBLANES - 1:BF16_SUBLANES, :].astype(F32)
    next_row = znext_ref[0:1, :].astype(F32)
    prev_row = jnp.where(at_start, 0.0, prev_row)
    next_row = jnp.where(at_end, 0.0, next_row)
    rowid = lax.broadcasted_iota(jnp.int32, (tm, 1), 0)
    zp = jnp.where(rowid == 0, prev_row, pltpu.roll(z, 1, axis=0))
    zn = jnp.where(rowid == tm - 1, next_row, pltpu.roll(z, tm - 1, axis=0))
    cw = cw_ref[...]
    conv = cw[0:1, :] * zp + cw[1:2, :] * z + cw[2:3, :] * zn
    cbz = (cb_ref[...].astype(F32) * conv).astype(BF16)

    y_conv = jnp.dot(cbz, wc_ref[...], preferred_element_type=F32)
    y_attn = jnp.dot(o_ref[...], wa_ref[...], preferred_element_type=F32)
    merged = (sgc_ref[...].astype(F32) * y_conv
              + sga_ref[...].astype(F32) * y_attn).astype(BF16)
    x1 = x_ref[...] + jnp.dot(merged, wo_ref[...], preferred_element_type=F32)
    x1_ref[...] = x1

    ms = jnp.mean(x1 * x1, axis=-1, keepdims=True)
    u2 = x1 * lax.rsqrt(ms + EPS) * g2_ref[...]
    u2_ref[...] = u2

    logits = jnp.dot(u2, wr_ref[...], precision=lax.Precision.HIGHEST,
                     preferred_element_type=F32)
    lane = lax.broadcasted_iota(jnp.int32, (tm, LANES), 1).astype(F32)
    neg = -jnp.inf
    big = float(2 * LANES)
    is_group = lane < N_GROUPS
    gl = jnp.where(is_group, logits, neg)
    gmax = jnp.max(gl, axis=-1, keepdims=True)
    gidx = jnp.min(jnp.where(gl == gmax, lane, big), axis=-1, keepdims=True)
    gsum = jnp.sum(jnp.where(is_group, jnp.exp(logits - gmax), 0.0), axis=-1,
                   keepdims=True)
    pg = 1.0 / gsum
    lane_group = jnp.floor(lane * (1.0 / EXPERTS_PER_GROUP)) - 1.0
    mine = (lane_group == gidx) & (lane >= N_GROUPS) & (lane < N_GROUPS + N_EXPERTS)
    sel = jnp.where(mine, logits, neg)
    v1 = jnp.max(sel, axis=-1, keepdims=True)
    i1 = jnp.min(jnp.where(sel == v1, lane, big), axis=-1, keepdims=True)
    sel2 = jnp.where(lane == i1, neg, sel)
    v2 = jnp.max(sel2, axis=-1, keepdims=True)
    i2 = jnp.min(jnp.where(sel2 == v2, lane, big), axis=-1, keepdims=True)
    t2 = jnp.exp(v2 - v1)
    den = 1.0 + t2
    wgt1 = pg * (1.0 / den)
    wgt2 = pg * (t2 / den)
    meta = jnp.where(lane == 0, i1 - N_GROUPS,
                     jnp.where(lane == 1, i2 - N_GROUPS,
                               jnp.where(lane == 2, wgt1,
                                         jnp.where(lane == 3, wgt2, 0.0))))
    meta_ref[...] = meta


def _post(x2, cb, z, o, sgc, sga, conv_w, wc, wa, wo, g2, wr, *, seq):
    t, d_model = x2.shape
    tm = TM_PROJ
    nseq = seq // tm
    hb = tm // BF16_SUBLANES
    nhalo = t // BF16_SUBLANES
    d_conv = cb.shape[1]
    d_q = o.shape[1]

    def row(width):
        return pl.BlockSpec((tm, width), lambda i: (i, 0))

    kern = functools.partial(_post_kernel, tm=tm, nseq=nseq)
    return pl.pallas_call(
        kern,
        grid=(t // tm,),
        in_specs=[row(d_model), row(d_conv), row(d_conv),
                  pl.BlockSpec((BF16_SUBLANES, d_conv),
                               lambda i: (jnp.maximum(i * hb - 1, 0), 0)),
                  pl.BlockSpec((BF16_SUBLANES, d_conv),
                               lambda i: (jnp.minimum((i + 1) * hb, nhalo - 1), 0)),
                  row(d_q), row(d_model), row(d_model),
                  _resident(conv_w.shape), _resident(wc.shape), _resident(wa.shape),
                  _resident(wo.shape), _resident(g2.shape), _resident(wr.shape)],
        out_specs=[row(d_model), row(d_model), row(LANES)],
        out_shape=[jax.ShapeDtypeStruct((t, d_model), F32),
                   jax.ShapeDtypeStruct((t, d_model), F32),
                   jax.ShapeDtypeStruct((t, LANES), F32)],
        compiler_params=_cparams(("arbitrary",)),
        name="post",
    )(x2, cb, z, z, z, o, sgc, sga, conv_w, wc, wa, wo, g2, wr)


def _rank_kernel(meta_ref, dst_ref, cnt_ref, carry_ref, pstart_ref, *, tb, blk):
    phase = pl.program_id(0)
    i = pl.program_id(1)
    lane = lax.broadcasted_iota(jnp.int32, (tb, LANES), 1).astype(F32)
    meta = meta_ref[...]
    oh1 = (lane == meta[:, 0:1]).astype(F32)
    oh2 = (lane == meta[:, 1:2]).astype(F32)
    c = oh1 + oh2
    csum = jnp.sum(c, axis=0, keepdims=True)

    @pl.when((phase == 0) & (i == 0))
    def _():
        carry_ref[...] = jnp.zeros(carry_ref.shape, F32)

    @pl.when(phase == 0)
    def _():
        carry_ref[...] += csum

    @pl.when((phase == 1) & (i == 0))
    def _():
        cnt = jnp.broadcast_to(carry_ref[...], cnt_ref.shape)
        cnt_ref[...] = cnt
        nblk = jnp.floor((cnt + (blk - 1)) * (1.0 / blk))
        r = lax.broadcasted_iota(jnp.int32, (LANES, LANES), 0)
        col = lax.broadcasted_iota(jnp.int32, (LANES, LANES), 1)
        upper = jnp.where(r < col, 1.0, 0.0).astype(BF16)
        pblk = jnp.dot(nblk.astype(BF16), upper, preferred_element_type=F32)
        pstart_ref[...] = pblk[0:1, :] * blk
        carry_ref[...] = jnp.zeros(carry_ref.shape, F32)

    @pl.when(phase == 1)
    def _():
        r = lax.broadcasted_iota(jnp.int32, (tb, tb), 0)
        col = lax.broadcasted_iota(jnp.int32, (tb, tb), 1)
        lower = jnp.where(col < r, 1.0, 0.0).astype(BF16)
        prefix = jnp.dot(lower, c.astype(BF16), preferred_element_type=F32)
        base = prefix + carry_ref[...] + pstart_ref[...]
        d1 = jnp.sum(oh1 * base, axis=-1, keepdims=True)
        d2 = jnp.sum(oh2 * base, axis=-1, keepdims=True)
        dst_ref[...] = jnp.where(lane == 0, d1, jnp.where(lane == 1, d2, 0.0))
        carry_ref[...] += csum


def _rank(meta, *, blk):
    t = meta.shape[0]
    tb = TB_RANK
    kern = functools.partial(_rank_kernel, tb=tb, blk=blk)
    return pl.pallas_call(
        kern,
        grid=(2, t // tb),
        in_specs=[pl.BlockSpec((tb, LANES), lambda p, i: (i, 0))],
        out_specs=[pl.BlockSpec((tb, LANES), lambda p, i: (i * p, 0)),
                   pl.BlockSpec((8, LANES), lambda p, i: (0, 0))],
        out_shape=[jax.ShapeDtypeStruct((t, LANES), F32),
                   jax.ShapeDtypeStruct((8, LANES), F32)],
        scratch_shapes=[pltpu.VMEM((1, LANES), F32), pltpu.VMEM((1, LANES), F32)],
        compiler_params=_cparams(("arbitrary", "arbitrary")),
        name="rank",
    )(meta)


def _dispatch_kernel(dst_sm, u2_ref, xs_in_ref, xs_ref, sem, *, tb):
    del xs_in_ref
    base = pl.program_id(0) * (tb * TOP_K)

    def row_copy(r, d):
        return pltpu.make_async_copy(u2_ref.at[pl.ds(r, 1)], xs_ref.at[pl.ds(d, 1)], sem)

    def issue(r, carry):
        for kk in range(TOP_K):
            row_copy(r, dst_sm[base + TOP_K * r + kk]).start()
        return carry

    lax.fori_loop(0, tb, issue, 0, unroll=8)
    for kk in range(TOP_K):
        pltpu.make_async_copy(u2_ref, xs_ref.at[pl.ds(0, tb)], sem).wait()


def _dispatch(dst_flat, u2, *, p_rows):
    t, d_model = u2.shape
    tb = TB_ROWS
    xs0 = jnp.zeros((p_rows, d_model), F32)
    kern = functools.partial(_dispatch_kernel, tb=tb)
    return pl.pallas_call(
        kern,
        grid_spec=pltpu.PrefetchScalarGridSpec(
            num_scalar_prefetch=1,
            grid=(t // tb,),
            in_specs=[pl.BlockSpec((tb, d_model), lambda i, dst: (i, 0)),
                      pl.BlockSpec(memory_space=pl.ANY)],
            out_specs=pl.BlockSpec(memory_space=pl.ANY),
            scratch_shapes=[pltpu.SemaphoreType.DMA]),
        out_shape=jax.ShapeDtypeStruct((p_rows, d_model), F32),
        input_output_aliases={2: 0},
        compiler_params=_cparams(("arbitrary",)),
        name="dispatch",
    )(dst_flat, u2, xs0)


def _experts_kernel(be_sm, nused_sm, xs_ref, w1_ref, w3_ref, w2_ref, ys_ref,
                    w13b_ref, w2b_ref, *, d_expert):
    b = pl.program_id(0)
    e = be_sm[b]
    e_prev = be_sm[jnp.maximum(b - 1, 0)]
    used = b < nused_sm[0]

    @pl.when((b == 0) | (e != e_prev))
    def _():
        w13b_ref[:, :d_expert] = w1_ref[...].astype(BF16)
        w13b_ref[:, d_expert:] = w3_ref[...].astype(BF16)
        w2b_ref[...] = w2_ref[...].astype(BF16)

    @pl.when(used)
    def _():
        xb = xs_ref[...].astype(BF16)
        h = jnp.dot(xb, w13b_ref[...], preferred_element_type=F32)
        h1 = h[:, :d_expert]
        h3 = h[:, d_expert:]
        a = (h1 * _sigmoid(h1) * h3).astype(BF16)
        ys_ref[...] = jnp.dot(a, w2b_ref[...], preferred_element_type=F32)

    @pl.when(jnp.logical_not(used))
    def _():
        ys_ref[...] = jnp.zeros(ys_ref.shape, F32)


def _experts(blk_expert, nused, xs, w1, w3, w2, *, blk):
    p_rows, d_model = xs.shape
    d_expert = w1.shape[-1]
    kern = functools.partial(_experts_kernel, d_expert=d_expert)
    return pl.pallas_call(
        kern,
        grid_spec=pltpu.PrefetchScalarGridSpec(
            num_scalar_prefetch=2,
            grid=(p_rows // blk,),
            in_specs=[pl.BlockSpec((blk, d_model), lambda b, be, nu: (b, 0)),
                      pl.BlockSpec((None, d_model, d_expert),
                                   lambda b, be, nu: (be[b], 0, 0)),
                      pl.BlockSpec((None, d_model, d_expert),
                                   lambda b, be, nu: (be[b], 0, 0)),
                      pl.BlockSpec((None, d_expert, d_model),
                                   lambda b, be, nu: (be[b], 0, 0))],
            out_specs=pl.BlockSpec((blk, d_model), lambda b, be, nu: (b, 0)),
            scratch_shapes=[pltpu.VMEM((d_model, 2 * d_expert), BF16),
                            pltpu.VMEM((d_expert, d_model), BF16)]),
        out_shape=jax.ShapeDtypeStruct((p_rows, d_model), F32),
        compiler_params=_cparams(("arbitrary",)),
        name="experts",
    )(blk_expert, nused, xs, w1, w3, w2)


def _combine_kernel(dst_sm, x1_ref, meta_ref, ys_ref, out_ref, buf_ref, sem, *, tb):
    base = pl.program_id(0) * (tb * TOP_K)

    def issue(r, carry):
        for kk in range(TOP_K):
            d = dst_sm[base + TOP_K * r + kk]
            pltpu.make_async_copy(ys_ref.at[pl.ds(d, 1)],
                                  buf_ref.at[kk, pl.ds(r, 1)], sem).start()
        return carry

    lax.fori_loop(0, tb, issue, 0, unroll=8)
    for kk in range(TOP_K):
        pltpu.make_async_copy(ys_ref.at[pl.ds(0, tb)], buf_ref.at[kk], sem).wait()
    meta = meta_ref[...]
    moe = buf_ref[0] * meta[:, 2:3] + buf_ref[1] * meta[:, 3:4]
    out_ref[...] = x1_ref[...] + moe


def _combine(dst_flat, x1, meta, ys):
    t, d_model = x1.shape
    tb = TB_ROWS
    kern = functools.partial(_combine_kernel, tb=tb)
    return pl.pallas_call(
        kern,
        grid_spec=pltpu.PrefetchScalarGridSpec(
            num_scalar_prefetch=1,
            grid=(t // tb,),
            in_specs=[pl.BlockSpec((tb, d_model), lambda i, dst: (i, 0)),
                      pl.BlockSpec((tb, LANES), lambda i, dst: (i, 0)),
                      pl.BlockSpec(memory_space=pl.ANY)],
            out_specs=pl.BlockSpec((tb, d_model), lambda i, dst: (i, 0)),
            scratch_shapes=[pltpu.VMEM((TOP_K, tb, d_model), F32),
                            pltpu.SemaphoreType.DMA]),
        out_shape=jax.ShapeDtypeStruct((t, d_model), F32),
        compiler_params=_cparams(("arbitrary",)),
        name="combine",
    )(dst_flat, x1, meta, ys)


def _layer(h2, *, batch, seq, norm1_g, w_in, conv_w, q_norm_g, k_norm_g,
           w_conv_out, w_attn_out, w_o, norm2_g, w_group, w_router, w1, w3, w2):
    t, d_model = h2.shape
    c, sa, sb = _rope_tables(seq)
    scale = HEAD_DIM ** -0.5
    tables_q = (c * scale, sa * scale, sb * scale)
    tables_k = (c, sa, sb)

    cb, z, q, k, v, sgc, sga = _inproj(
        h2, norm1_g[None, :], w_in.astype(BF16), q_norm_g[None, :], k_norm_g[None, :],
        tables_q, tables_k, seq=seq)
    o = _attention(q, k, v, batch=batch, seq=seq)

    n_route = N_GROUPS + N_EXPERTS
    wr = jnp.concatenate(
        [w_group, w_router, jnp.zeros((d_model, LANES - n_route), F32)], axis=1)
    x1, u2, meta = _post(h2, cb, z, o, sgc, sga, conv_w,
                         w_conv_out.astype(BF16), w_attn_out.astype(BF16),
                         w_o.astype(BF16), norm2_g[None, :], wr, seq=seq)

    blk = MOE_ROWS
    dst, cnt = _rank(meta, blk=blk)
    dst_flat = dst[:, :TOP_K].astype(jnp.int32).reshape(-1)
    counts = cnt[0, :N_EXPERTS].astype(jnp.int32)
    p_rows = t * TOP_K + N_EXPERTS * blk
    nb = p_rows // blk
    pend = jnp.cumsum((counts + blk - 1) // blk)
    nused = pend[-1]
    bidx = jnp.minimum(jnp.arange(nb, dtype=jnp.int32), nused - 1)
    blk_expert = jnp.minimum(jnp.searchsorted(pend, bidx, side='right'),
                             N_EXPERTS - 1).astype(jnp.int32)

    xs = _dispatch(dst_flat, u2, p_rows=p_rows)
    ys = _experts(blk_expert, nused.reshape(1).astype(jnp.int32), xs, w1, w3, w2, blk=blk)
    return _combine(dst_flat, x1, meta, ys)


def kernel(x, norm1_g, w_in, conv_w, q_norm_g, k_norm_g, w_conv_out, w_attn_out, w_o,
           norm2_g, w_group, w_router, w1, w3, w2):
    batch, seq, d_model = x.shape
    h2 = x.reshape(batch * seq, d_model)
    for l in range(norm1_g.shape[0]):
        h2 = _layer(h2, batch=batch, seq=seq, norm1_g=norm1_g[l], w_in=w_in[l],
                    conv_w=conv_w[l], q_norm_g=q_norm_g[l], k_norm_g=k_norm_g[l],
                    w_conv_out=w_conv_out[l], w_attn_out=w_attn_out[l], w_o=w_o[l],
                    norm2_g=norm2_g[l], w_group=w_group[l], w_router=w_router[l],
                    w1=w1[l], w3=w3[l], w2=w2[l])
    return h2.reshape(batch, seq, d_model)
```

```python
import functools

import jax
import jax.numpy as jnp
from jax import lax
from jax.experimental import pallas as pl
from jax.experimental.pallas import tpu as pltpu

F32 = jnp.float32
BF16 = jnp.bfloat16

GRID_W = 64
EPS = 1e-6
N_HEADS = 8
N_KV_HEADS = 2
HEAD_DIM = 128
ROPE_THETA = 10000.0
N_GROUPS = 8
EXPERTS_PER_GROUP = 8
N_EXPERTS = N_GROUPS * EXPERTS_PER_GROUP
TOP_K = 2
LOG2_E = 1.4426950408889634

LANES = 128
V7X_VMEM_LIMIT_BYTES = 56000 * 1024

TM_PROJ = 512
TQ_ATTN = 256
TK_ATTN = 512
TB_RANK = 512
TB_ROWS = 256
MOE_ROWS = 256
BF16_SUBLANES = 16


def _cparams(sem):
    return pltpu.CompilerParams(dimension_semantics=sem,
                                vmem_limit_bytes=V7X_VMEM_LIMIT_BYTES)


def _resident(shape):
    nd = len(shape)
    return pl.BlockSpec(shape, lambda *_: (0,) * nd, pipeline_mode=pl.Buffered(1))


def _head_norm_rope(xh, g, c, sa, sb):
    ms = jnp.mean(xh * xh, axis=-1, keepdims=True)
    y = xh * lax.rsqrt(ms + EPS) * g
    y_next = pltpu.roll(y, HEAD_DIM - 1, axis=1)
    y_prev = pltpu.roll(y, 1, axis=1)
    return y * c + y_next * sa + y_prev * sb


def _sigmoid(x):
    return 1.0 / (1.0 + jnp.exp(-x))


def _inproj_kernel(x_ref, g1_ref, w_ref, gq_ref, gk_ref,
                   cq_ref, saq_ref, sbq_ref, ck_ref, sak_ref, sbk_ref,
                   cb_ref, z_ref, q_ref, k_ref, v_ref, sgc_ref, sga_ref,
                   *, d_conv, d_q, d_kv, d_model):
    x = x_ref[...]
    ms = jnp.mean(x * x, axis=-1, keepdims=True)
    u = (x * lax.rsqrt(ms + EPS) * g1_ref[...]).astype(BF16)

    def proj(lo, width):
        return jnp.dot(u, w_ref[:, lo:lo + width], preferred_element_type=F32)

    off = 0
    cb_ref[...] = proj(off, d_conv).astype(BF16)
    off += d_conv
    cc = proj(off, d_conv)
    off += d_conv
    cx = proj(off, d_conv)
    off += d_conv
    z_ref[...] = (cc * cx).astype(BF16)

    q = proj(off, d_q)
    off += d_q
    gq = gq_ref[...]
    cq, saq, sbq = cq_ref[...], saq_ref[...], sbq_ref[...]
    for h in range(d_q // HEAD_DIM):
        sl = slice(h * HEAD_DIM, (h + 1) * HEAD_DIM)
        q_ref[:, sl] = _head_norm_rope(q[:, sl], gq, cq, saq, sbq).astype(BF16)

    k = proj(off, d_kv)
    off += d_kv
    gk = gk_ref[...]
    ck, sak, sbk = ck_ref[...], sak_ref[...], sbk_ref[...]
    for h in range(d_kv // HEAD_DIM):
        sl = slice(h * HEAD_DIM, (h + 1) * HEAD_DIM)
        k_ref[:, sl] = _head_norm_rope(k[:, sl], gk, ck, sak, sbk).astype(BF16)

    v_ref[...] = proj(off, d_kv).astype(BF16)
    off += d_kv
    sgc_ref[...] = _sigmoid(proj(off, d_model)).astype(BF16)
    off += d_model
    sga_ref[...] = _sigmoid(proj(off, d_model)).astype(BF16)


def _rope_tables(seq):
    rows = seq // GRID_W
    axis_dim = HEAD_DIM // 2
    row = jnp.repeat(jnp.arange(rows, dtype=F32), GRID_W)
    col = jnp.tile(jnp.arange(GRID_W, dtype=F32), rows)
    inv = ROPE_THETA ** (-jnp.arange(0, axis_dim, 2, dtype=F32) / axis_dim)
    ang = jnp.concatenate([row[:, None] * inv, col[:, None] * inv], axis=-1)
    cos, sin = jnp.cos(ang), jnp.sin(ang)
    zero = jnp.zeros_like(sin)
    c = jnp.repeat(cos, 2, axis=-1)
    sa = jnp.stack([-sin, zero], axis=-1).reshape(seq, HEAD_DIM)
    sb = jnp.stack([zero, sin], axis=-1).reshape(seq, HEAD_DIM)
    return c, sa, sb


def _inproj(x2, g1, w_in_bf, gq, gk, tables_q, tables_k, *, seq):
    t, d_model = x2.shape
    d_q = N_HEADS * HEAD_DIM
    d_kv = N_KV_HEADS * HEAD_DIM
    d_in = w_in_bf.shape[1]
    d_conv = (d_in - d_q - 2 * d_kv - 2 * d_model) // 3
    tm = TM_PROJ
    nseq = seq // tm

    def row(width):
        return pl.BlockSpec((tm, width), lambda i: (i, 0))

    table = pl.BlockSpec((tm, HEAD_DIM), lambda i: (i % nseq, 0))
    kern = functools.partial(_inproj_kernel, d_conv=d_conv, d_q=d_q, d_kv=d_kv,
                             d_model=d_model)
    out_shape = [jax.ShapeDtypeStruct((t, w), BF16)
                 for w in (d_conv, d_conv, d_q, d_kv, d_kv, d_model, d_model)]
    return pl.pallas_call(
        kern,
        grid=(t // tm,),
        in_specs=[row(d_model), _resident((1, d_model)), _resident((d_model, d_in)),
                  _resident((1, HEAD_DIM)), _resident((1, HEAD_DIM)),
                  table, table, table, table, table, table],
        out_specs=[row(d_conv), row(d_conv), row(d_q), row(d_kv), row(d_kv),
                   row(d_model), row(d_model)],
        out_shape=out_shape,
        compiler_params=_cparams(("arbitrary",)),
        name="inproj",
    )(x2, g1, w_in_bf, gq, gk, *tables_q, *tables_k)


def _attn_kernel(q_ref, k_ref, v_ref, o_ref, qs_ref, vext_ref, m_ref, acc_ref,
                 *, tq, tk, nk, group):
    @pl.when(pl.program_id(2) == 0)
    def _():
        vext_ref[:, :HEAD_DIM] = v_ref[...]
        vext_ref[:, HEAD_DIM:] = jnp.ones((vext_ref.shape[0], HEAD_DIM), BF16)

    for g in range(group):
        qs_ref[g * tq:(g + 1) * tq, :] = q_ref[:, g * HEAD_DIM:(g + 1) * HEAD_DIM]
    m_ref[...] = jnp.full(m_ref.shape, -jnp.inf, F32)
    acc_ref[...] = jnp.zeros(acc_ref.shape, F32)

    for j in range(nk):
        kc = k_ref[j * tk:(j + 1) * tk, :]
        s = lax.dot_general(qs_ref[...], kc, (((1,), (1,)), ((), ())),
                            preferred_element_type=F32)
        m_prev = m_ref[...]
        m_new = jnp.maximum(m_prev, jnp.max(s, axis=-1, keepdims=True))
        alpha = jnp.exp2(m_prev - m_new)
        p = jnp.concatenate(
            [jnp.exp2(s[:, c * LANES:(c + 1) * LANES] - m_new) for c in range(tk // LANES)],
            axis=1).astype(BF16)
        pv = jnp.dot(p, vext_ref[j * tk:(j + 1) * tk, :], preferred_element_type=F32)
        acc_ref[...] = jnp.concatenate([alpha, alpha], axis=1) * acc_ref[...] + pv
        m_ref[...] = m_new

    out = acc_ref[:, :HEAD_DIM] / acc_ref[:, HEAD_DIM:]
    for g in range(group):
        o_ref[:, g * HEAD_DIM:(g + 1) * HEAD_DIM] = out[g * tq:(g + 1) * tq].astype(BF16)


def _attention(q, k, v, *, batch, seq):
    t = q.shape[0]
    group = N_HEADS // N_KV_HEADS
    tq, tk = TQ_ATTN, TK_ATTN
    nq = seq // tq
    gw = group * HEAD_DIM
    kern = functools.partial(_attn_kernel, tq=tq, tk=tk, nk=seq // tk, group=group)
    return pl.pallas_call(
        kern,
        grid=(batch, N_KV_HEADS, nq),
        in_specs=[pl.BlockSpec((tq, gw), lambda b, h, i: (b * nq + i, h)),
                  pl.BlockSpec((seq, HEAD_DIM), lambda b, h, i: (b, h)),
                  pl.BlockSpec((seq, HEAD_DIM), lambda b, h, i: (b, h))],
        out_specs=pl.BlockSpec((tq, gw), lambda b, h, i: (b * nq + i, h)),
        out_shape=jax.ShapeDtypeStruct((t, N_HEADS * HEAD_DIM), BF16),
        scratch_shapes=[pltpu.VMEM((group * tq, HEAD_DIM), BF16),
                        pltpu.VMEM((seq, 2 * HEAD_DIM), BF16),
                        pltpu.VMEM((group * tq, LANES), F32),
                        pltpu.VMEM((group * tq, 2 * HEAD_DIM), F32)],
        compiler_params=_cparams(("arbitrary", "arbitrary", "arbitrary")),
        name="attention",
    )(q, k, v)


def _post_kernel(x_ref, cb_ref, z_ref, zprev_ref, znext_ref, o_ref, sgc_ref, sga_ref,
                 cw_ref, wc_ref, wa_ref, wo_ref, g2_ref, wr_ref,
                 x1_ref, u2_ref, meta_ref, *, tm, nseq):
    i = pl.program_id(0)
    at_start = (i % nseq) == 0
    at_end = (i % nseq) == nseq - 1

    z = z_ref[...].astype(F32)
    prev_row = zprev_ref[BF16_SUBLANES - 1:BF16_SUBLANES, :].astype(F32)
    next_row = znext_ref[0:1, :].astype(F32)
    prev_row = jnp.where(at_start, 0.0, prev_row)
    next_row = jnp.where(at_end, 0.0, next_row)
    rowid = lax.broadcasted_iota(jnp.int32, (tm, 1), 0)
    zp = jnp.where(rowid == 0, prev_row, pltpu.roll(z, 1, axis=0))
    zn = jnp.where(rowid == tm - 1, next_row, pltpu.roll(z, tm - 1, axis=0))
    cw = cw_ref[...]
    conv = cw[0:1, :] * zp + cw[1:2, :] * z + cw[2:3, :] * zn
    cbz = (cb_ref[...].astype(F32) * conv).astype(BF16)

    y_conv = jnp.dot(cbz, wc_ref[...], preferred_element_type=F32)
    y_attn = jnp.dot(o_ref[...], wa_ref[...], preferred_element_type=F32)
    merged = (sgc_ref[...].astype(F32) * y_conv
              + sga_ref[...].astype(F32) * y_attn).astype(BF16)
    x1 = x_ref[...] + jnp.dot(merged, wo_ref[...], preferred_element_type=F32)
    x1_ref[...] = x1

    ms = jnp.mean(x1 * x1, axis=-1, keepdims=True)
    u2 = x1 * lax.rsqrt(ms + EPS) * g2_ref[...]
    u2_ref[...] = u2

    logits = jnp.dot(u2, wr_ref[...], precision=lax.Precision.HIGHEST,
                     preferred_element_type=F32)
    lane = lax.broadcasted_iota(jnp.int32, (tm, LANES), 1).astype(F32)
    neg = -jnp.inf
    big = float(2 * LANES)
    is_group = lane < N_GROUPS
    gl = jnp.where(is_group, logits, neg)
    gmax = jnp.max(gl, axis=-1, keepdims=True)
    gidx = jnp.min(jnp.where(gl == gmax, lane, big), axis=-1, keepdims=True)
    gsum = jnp.sum(jnp.where(is_group, jnp.exp(logits - gmax), 0.0), axis=-1,
                   keepdims=True)
    pg = 1.0 / gsum
    lane_group = jnp.floor(lane * (1.0 / EXPERTS_PER_GROUP)) - 1.0
    mine = (lane_group == gidx) & (lane >= N_GROUPS) & (lane < N_GROUPS + N_EXPERTS)
    sel = jnp.where(mine, logits, neg)
    v1 = jnp.max(sel, axis=-1, keepdims=True)
    i1 = jnp.min(jnp.where(sel == v1, lane, big), axis=-1, keepdims=True)
    sel2 = jnp.where(lane == i1, neg, sel)
    v2 = jnp.max(sel2, axis=-1, keepdims=True)
    i2 = jnp.min(jnp.where(sel2 == v2, lane, big), axis=-1, keepdims=True)
    t2 = jnp.exp(v2 - v1)
    den = 1.0 + t2
    wgt1 = pg * (1.0 / den)
    wgt2 = pg * (t2 / den)
    meta = jnp.where(lane == 0, i1 - N_GROUPS,
                     jnp.where(lane == 1, i2 - N_GROUPS,
                               jnp.where(lane == 2, wgt1,
                                         jnp.where(lane == 3, wgt2, 0.0))))
    meta_ref[...] = meta


def _post(x2, cb, z, o, sgc, sga, conv_w, wc, wa, wo, g2, wr, *, seq):
    t, d_model = x2.shape
    tm = TM_PROJ
    nseq = seq // tm
    hb = tm // BF16_SUBLANES
    nhalo = t // BF16_SUBLANES
    d_conv = cb.shape[1]
    d_q = o.shape[1]

    def row(width):
        return pl.BlockSpec((tm, width), lambda i: (i, 0))

    kern = functools.partial(_post_kernel, tm=tm, nseq=nseq)
    return pl.pallas_call(
        kern,
        grid=(t // tm,),
        in_specs=[row(d_model), row(d_conv), row(d_conv),
                  pl.BlockSpec((BF16_SUBLANES, d_conv),
                               lambda i: (jnp.maximum(i * hb - 1, 0), 0)),
                  pl.BlockSpec((BF16_SUBLANES, d_conv),
                               lambda i: (jnp.minimum((i + 1) * hb, nhalo - 1), 0)),
                  row(d_q), row(d_model), row(d_model),
                  _resident(conv_w.shape), _resident(wc.shape), _resident(wa.shape),
                  _resident(wo.shape), _resident(g2.shape), _resident(wr.shape)],
        out_specs=[row(d_model), row(d_model), row(LANES)],
        out_shape=[jax.ShapeDtypeStruct((t, d_model), F32),
                   jax.ShapeDtypeStruct((t, d_model), F32),
                   jax.ShapeDtypeStruct((t, LANES), F32)],
        compiler_params=_cparams(("arbitrary",)),
        name="post",
    )(x2, cb, z, z, z, o, sgc, sga, conv_w, wc, wa, wo, g2, wr)


def _rank_kernel(meta_ref, dst_ref, cnt_ref, carry_ref, pstart_ref, *, tb, blk):
    phase = pl.program_id(0)
    i = pl.program_id(1)
    lane = lax.broadcasted_iota(jnp.int32, (tb, LANES), 1).astype(F32)
    meta = meta_ref[...]
    oh1 = (lane == meta[:, 0:1]).astype(F32)
    oh2 = (lane == meta[:, 1:2]).astype(F32)
    c = oh1 + oh2
    csum = jnp.sum(c, axis=0, keepdims=True)

    @pl.when((phase == 0) & (i == 0))
    def _():
        carry_ref[...] = jnp.zeros(carry_ref.shape, F32)

    @pl.when(phase == 0)
    def _():
        carry_ref[...] += csum

    @pl.when((phase == 1) & (i == 0))
    def _():
        cnt = jnp.broadcast_to(carry_ref[...], cnt_ref.shape)
        cnt_ref[...] = cnt
        nblk = jnp.floor((cnt + (blk - 1)) * (1.0 / blk))
        r = lax.broadcasted_iota(jnp.int32, (LANES, LANES), 0)
        col = lax.broadcasted_iota(jnp.int32, (LANES, LANES), 1)
        upper = jnp.where(r < col, 1.0, 0.0).astype(BF16)
        pblk = jnp.dot(nblk.astype(BF16), upper, preferred_element_type=F32)
        pstart_ref[...] = pblk[0:1, :] * blk
        carry_ref[...] = jnp.zeros(carry_ref.shape, F32)

    @pl.when(phase == 1)
    def _():
        r = lax.broadcasted_iota(jnp.int32, (tb, tb), 0)
        col = lax.broadcasted_iota(jnp.int32, (tb, tb), 1)
        lower = jnp.where(col < r, 1.0, 0.0).astype(BF16)
        prefix = jnp.dot(lower, c.astype(BF16), preferred_element_type=F32)
        base = prefix + carry_ref[...] + pstart_ref[...]
        d1 = jnp.sum(oh1 * base, axis=-1, keepdims=True)
        d2 = jnp.sum(oh2 * base, axis=-1, keepdims=True)
        dst_ref[...] = jnp.where(lane == 0, d1, jnp.where(lane == 1, d2, 0.0))
        carry_ref[...] += csum


def _rank(meta, *, blk):
    t = meta.shape[0]
    tb = TB_RANK
    kern = functools.partial(_rank_kernel, tb=tb, blk=blk)
    return pl.pallas_call(
        kern,
        grid=(2, t // tb),
        in_specs=[pl.BlockSpec((tb, LANES), lambda p, i: (i, 0))],
        out_specs=[pl.BlockSpec((tb, LANES), lambda p, i: (i * p, 0)),
                   pl.BlockSpec((8, LANES), lambda p, i: (0, 0))],
        out_shape=[jax.ShapeDtypeStruct((t, LANES), F32),
                   jax.ShapeDtypeStruct((8, LANES), F32)],
        scratch_shapes=[pltpu.VMEM((1, LANES), F32), pltpu.VMEM((1, LANES), F32)],
        compiler_params=_cparams(("arbitrary", "arbitrary")),
        name="rank",
    )(meta)


def _dispatch_kernel(dst_sm, u2_ref, xs_in_ref, xs_ref, sem, *, tb):
    del xs_in_ref
    base = pl.program_id(0) * (tb * TOP_K)

    def row_copy(r, d):
        return pltpu.make_async_copy(u2_ref.at[pl.ds(r, 1)], xs_ref.at[pl.ds(d, 1)], sem)

    def issue(r, carry):
        for kk in range(TOP_K):
            row_copy(r, dst_sm[base + TOP_K * r + kk]).start()
        return carry

    lax.fori_loop(0, tb, issue, 0, unroll=8)
    for kk in range(TOP_K):
        pltpu.make_async_copy(u2_ref, xs_ref.at[pl.ds(0, tb)], sem).wait()


def _dispatch(dst_flat, u2, *, p_rows):
    t, d_model = u2.shape
    tb = TB_ROWS
    xs0 = jnp.zeros((p_rows, d_model), F32)
    kern = functools.partial(_dispatch_kernel, tb=tb)
    return pl.pallas_call(
        kern,
        grid_spec=pltpu.PrefetchScalarGridSpec(
            num_scalar_prefetch=1,
            grid=(t // tb,),
            in_specs=[pl.BlockSpec((tb, d_model), lambda i, dst: (i, 0)),
                      pl.BlockSpec(memory_space=pl.ANY)],
            out_specs=pl.BlockSpec(memory_space=pl.ANY),
            scratch_shapes=[pltpu.SemaphoreType.DMA]),
        out_shape=jax.ShapeDtypeStruct((p_rows, d_model), F32),
        input_output_aliases={2: 0},
        compiler_params=_cparams(("arbitrary",)),
        name="dispatch",
    )(dst_flat, u2, xs0)


def _experts_kernel(be_sm, nused_sm, xs_ref, w1_ref, w3_ref, w2_ref, ys_ref,
                    w13b_ref, w2b_ref, *, d_expert):
    b = pl.program_id(0)
    e = be_sm[b]
    e_prev = be_sm[jnp.maximum(b - 1, 0)]
    used = b < nused_sm[0]

    @pl.when((b == 0) | (e != e_prev))
    def _():
        w13b_ref[:, :d_expert] = w1_ref[...].astype(BF16)
        w13b_ref[:, d_expert:] = w3_ref[...].astype(BF16)
        w2b_ref[...] = w2_ref[...].astype(BF16)

    @pl.when(used)
    def _():
        xb = xs_ref[...].astype(BF16)
        h = jnp.dot(xb, w13b_ref[...], preferred_element_type=F32)
        h1 = h[:, :d_expert]
        h3 = h[:, d_expert:]
        a = (h1 * _sigmoid(h1) * h3).astype(BF16)
        ys_ref[...] = jnp.dot(a, w2b_ref[...], preferred_element_type=F32)

    @pl.when(jnp.logical_not(used))
    def _():
        ys_ref[...] = jnp.zeros(ys_ref.shape, F32)


def _experts(blk_expert, nused, xs, w1, w3, w2, *, blk):
    p_rows, d_model = xs.shape
    d_expert = w1.shape[-1]
    kern = functools.partial(_experts_kernel, d_expert=d_expert)
    return pl.pallas_call(
        kern,
        grid_spec=pltpu.PrefetchScalarGridSpec(
            num_scalar_prefetch=2,
            grid=(p_rows // blk,),
            in_specs=[pl.BlockSpec((blk, d_model), lambda b, be, nu: (b, 0)),
                      pl.BlockSpec((None, d_model, d_expert),
                                   lambda b, be, nu: (be[b], 0, 0)),
                      pl.BlockSpec((None, d_model, d_expert),
                                   lambda b, be, nu: (be[b], 0, 0)),
                      pl.BlockSpec((None, d_expert, d_model),
                                   lambda b, be, nu: (be[b], 0, 0))],
            out_specs=pl.BlockSpec((blk, d_model), lambda b, be, nu: (b, 0)),
            scratch_shapes=[pltpu.VMEM((d_model, 2 * d_expert), BF16),
                            pltpu.VMEM((d_expert, d_model), BF16)]),
        out_shape=jax.ShapeDtypeStruct((p_rows, d_model), F32),
        compiler_params=_cparams(("arbitrary",)),
        name="experts",
    )(blk_expert, nused, xs, w1, w3, w2)


def _combine_kernel(dst_sm, x1_ref, meta_ref, ys_ref, out_ref, buf_ref, sem, *, tb):
    base = pl.program_id(0) * (tb * TOP_K)

    def issue(r, carry):
        for kk in range(TOP_K):
            d = dst_sm[base + TOP_K * r + kk]
            pltpu.make_async_copy(ys_ref.at[pl.ds(d, 1)],
                                  buf_ref.at[kk, pl.ds(r, 1)], sem).start()
        return carry

    lax.fori_loop(0, tb, issue, 0, unroll=8)
    for kk in range(TOP_K):
        pltpu.make_async_copy(ys_ref.at[pl.ds(0, tb)], buf_ref.at[kk], sem).wait()
    meta = meta_ref[...]
    moe = buf_ref[0] * meta[:, 2:3] + buf_ref[1] * meta[:, 3:4]
    out_ref[...] = x1_ref[...] + moe


def _combine(dst_flat, x1, meta, ys):
    t, d_model = x1.shape
    tb = TB_ROWS
    kern = functools.partial(_combine_kernel, tb=tb)
    return pl.pallas_call(
        kern,
        grid_spec=pltpu.PrefetchScalarGridSpec(
            num_scalar_prefetch=1,
            grid=(t // tb,),
            in_specs=[pl.BlockSpec((tb, d_model), lambda i, dst: (i, 0)),
                      pl.BlockSpec((tb, LANES), lambda i, dst: (i, 0)),
                      pl.BlockSpec(memory_space=pl.ANY)],
            out_specs=pl.BlockSpec((tb, d_model), lambda i, dst: (i, 0)),
            scratch_shapes=[pltpu.VMEM((TOP_K, tb, d_model), F32),
                            pltpu.SemaphoreType.DMA]),
        out_shape=jax.ShapeDtypeStruct((t, d_model), F32),
        compiler_params=_cparams(("arbitrary",)),
        name="combine",
    )(dst_flat, x1, meta, ys)


def _layer(h2, *, batch, seq, norm1_g, w_in, conv_w, q_norm_g, k_norm_g,
           w_conv_out, w_attn_out, w_o, norm2_g, w_group, w_router, w1, w3, w2):
    t, d_model = h2.shape
    c, sa, sb = _rope_tables(seq)
    scale = HEAD_DIM ** -0.5 * LOG2_E
    tables_q = (c * scale, sa * scale, sb * scale)
    tables_k = (c, sa, sb)

    cb, z, q, k, v, sgc, sga = _inproj(
        h2, norm1_g[None, :], w_in.astype(BF16), q_norm_g[None, :], k_norm_g[None, :],
        tables_q, tables_k, seq=seq)
    o = _attention(q, k, v, batch=batch, seq=seq)

    n_route = N_GROUPS + N_EXPERTS
    wr = jnp.concatenate(
        [w_group, w_router, jnp.zeros((d_model, LANES - n_route), F32)], axis=1)
    x1, u2, meta = _post(h2, cb, z, o, sgc, sga, conv_w,
                         w_conv_out.astype(BF16), w_attn_out.astype(BF16),
                         w_o.astype(BF16), norm2_g[None, :], wr, seq=seq)

    blk = MOE_ROWS
    dst, cnt = _rank(meta, blk=blk)
    dst_flat = dst[:, :TOP_K].astype(jnp.int32).reshape(-1)
    counts = cnt[0, :N_EXPERTS].astype(jnp.int32)
    p_rows = t * TOP_K + N_EXPERTS * blk
    nb = p_rows // blk
    pend = jnp.cumsum((counts + blk - 1) // blk)
    nused = pend[-1]
    bidx = jnp.minimum(jnp.arange(nb, dtype=jnp.int32), nused - 1)
    blk_expert = jnp.minimum(jnp.searchsorted(pend, bidx, side='right'),
                             N_EXPERTS - 1).astype(jnp.int32)

    xs = _dispatch(dst_flat, u2, p_rows=p_rows)
    ys = _experts(blk_expert, nused.reshape(1).astype(jnp.int32), xs, w1, w3, w2, blk=blk)
    return _combine(dst_flat, x1, meta, ys)


def kernel(x, norm1_g, w_in, conv_w, q_norm_g, k_norm_g, w_conv_out, w_attn_out, w_o,
           norm2_g, w_group, w_router, w1, w3, w2):
    batch, seq, d_model = x.shape
    h2 = x.reshape(batch * seq, d_model)
    for l in range(norm1_g.shape[0]):
        h2 = _layer(h2, batch=batch, seq=seq, norm1_g=norm1_g[l], w_in=w_in[l],
                    conv_w=conv_w[l], q_norm_g=q_norm_g[l], k_norm_g=k_norm_g[l],
                    w_conv_out=w_conv_out[l], w_attn_out=w_attn_out[l], w_o=w_o[l],
                    norm2_g=norm2_g[l], w_group=w_group[l], w_router=w_router[l],
                    w1=w1[l], w3=w3[l], w2=w2[l])
    return h2.reshape(batch, seq, d_model)
```

```python
import functools

import jax
import jax.numpy as jnp
from jax import lax
from jax.experimental import pallas as pl
from jax.experimental.pallas import tpu as pltpu

F32 = jnp.float32
BF16 = jnp.bfloat16

GRID_W = 64
EPS = 1e-6
N_HEADS = 8
N_KV_HEADS = 2
HEAD_DIM = 128
ROPE_THETA = 10000.0
N_GROUPS = 8
EXPERTS_PER_GROUP = 8
N_EXPERTS = N_GROUPS * EXPERTS_PER_GROUP
TOP_K = 2
LOG2_E = 1.4426950408889634

LANES = 128
V7X_VMEM_LIMIT_BYTES = 56000 * 1024

TM_PROJ = 512
TQ_ATTN = 256
TK_ATTN = 512
TB_RANK = 1024
TB_ROWS = 256
MOE_ROWS = 256
BF16_SUBLANES = 16
F32_SUBLANES = 8


def _cparams(sem):
    return pltpu.CompilerParams(dimension_semantics=sem,
                                vmem_limit_bytes=V7X_VMEM_LIMIT_BYTES)


def _resident(shape):
    nd = len(shape)
    return pl.BlockSpec(shape, lambda *_: (0,) * nd, pipeline_mode=pl.Buffered(1))


def _head_norm_rope(xh, g, c, sa, sb):
    ms = jnp.mean(xh * xh, axis=-1, keepdims=True)
    y = xh * lax.rsqrt(ms + EPS) * g
    y_next = pltpu.roll(y, HEAD_DIM - 1, axis=1)
    y_prev = pltpu.roll(y, 1, axis=1)
    return y * c + y_next * sa + y_prev * sb


def _sigmoid(x):
    return 1.0 / (1.0 + jnp.exp(-x))


def _inproj_kernel(x_ref, g1_ref, w_ref, gq_ref, gk_ref,
                   cq_ref, saq_ref, sbq_ref, ck_ref, sak_ref, sbk_ref,
                   cb_ref, z_ref, q_ref, k_ref, v_ref, sgc_ref, sga_ref,
                   *, d_conv, d_q, d_kv, d_model):
    x = x_ref[...]
    ms = jnp.mean(x * x, axis=-1, keepdims=True)
    u = (x * lax.rsqrt(ms + EPS) * g1_ref[...]).astype(BF16)

    def proj(lo, width):
        return jnp.dot(u, w_ref[:, lo:lo + width], preferred_element_type=F32)

    off = 0
    cb_ref[...] = proj(off, d_conv).astype(BF16)
    off += d_conv
    cc = proj(off, d_conv)
    off += d_conv
    cx = proj(off, d_conv)
    off += d_conv
    z_ref[...] = (cc * cx).astype(BF16)

    q = proj(off, d_q)
    off += d_q
    gq = gq_ref[...]
    cq, saq, sbq = cq_ref[...], saq_ref[...], sbq_ref[...]
    for h in range(d_q // HEAD_DIM):
        sl = slice(h * HEAD_DIM, (h + 1) * HEAD_DIM)
        q_ref[:, sl] = _head_norm_rope(q[:, sl], gq, cq, saq, sbq).astype(BF16)

    k = proj(off, d_kv)
    off += d_kv
    gk = gk_ref[...]
    ck, sak, sbk = ck_ref[...], sak_ref[...], sbk_ref[...]
    for h in range(d_kv // HEAD_DIM):
        sl = slice(h * HEAD_DIM, (h + 1) * HEAD_DIM)
        k_ref[:, sl] = _head_norm_rope(k[:, sl], gk, ck, sak, sbk).astype(BF16)

    v_ref[...] = proj(off, d_kv).astype(BF16)
    off += d_kv
    sgc_ref[...] = _sigmoid(proj(off, d_model)).astype(BF16)
    off += d_model
    sga_ref[...] = _sigmoid(proj(off, d_model)).astype(BF16)


def _rope_tables(seq):
    rows = seq // GRID_W
    axis_dim = HEAD_DIM // 2
    row = jnp.repeat(jnp.arange(rows, dtype=F32), GRID_W)
    col = jnp.tile(jnp.arange(GRID_W, dtype=F32), rows)
    inv = ROPE_THETA ** (-jnp.arange(0, axis_dim, 2, dtype=F32) / axis_dim)
    ang = jnp.concatenate([row[:, None] * inv, col[:, None] * inv], axis=-1)
    cos, sin = jnp.cos(ang), jnp.sin(ang)
    zero = jnp.zeros_like(sin)
    c = jnp.repeat(cos, 2, axis=-1)
    sa = jnp.stack([-sin, zero], axis=-1).reshape(seq, HEAD_DIM)
    sb = jnp.stack([zero, sin], axis=-1).reshape(seq, HEAD_DIM)
    return c, sa, sb


def _inproj(x2, g1, w_in_bf, gq, gk, tables_q, tables_k, *, seq):
    t, d_model = x2.shape
    d_q = N_HEADS * HEAD_DIM
    d_kv = N_KV_HEADS * HEAD_DIM
    d_in = w_in_bf.shape[1]
    d_conv = (d_in - d_q - 2 * d_kv - 2 * d_model) // 3
    tm = TM_PROJ
    nseq = seq // tm

    def row(width):
        return pl.BlockSpec((tm, width), lambda i: (i, 0))

    table = pl.BlockSpec((tm, HEAD_DIM), lambda i: (i % nseq, 0))
    kern = functools.partial(_inproj_kernel, d_conv=d_conv, d_q=d_q, d_kv=d_kv,
                             d_model=d_model)
    out_shape = [jax.ShapeDtypeStruct((t, w), BF16)
                 for w in (d_conv, d_conv, d_q, d_kv, d_kv, d_model, d_model)]
    return pl.pallas_call(
        kern,
        grid=(t // tm,),
        in_specs=[row(d_model), _resident((1, d_model)), _resident((d_model, d_in)),
                  _resident((1, HEAD_DIM)), _resident((1, HEAD_DIM)),
                  table, table, table, table, table, table],
        out_specs=[row(d_conv), row(d_conv), row(d_q), row(d_kv), row(d_kv),
                   row(d_model), row(d_model)],
        out_shape=out_shape,
        compiler_params=_cparams(("arbitrary",)),
        name="inproj",
    )(x2, g1, w_in_bf, gq, gk, *tables_q, *tables_k)


def _attn_kernel(q_ref, k_ref, v_ref, o_ref, qs_ref, vext_ref, m_ref, acc_ref,
                 *, tq, tk, nk, group):
    @pl.when(pl.program_id(2) == 0)
    def _():
        vext_ref[:, :HEAD_DIM] = v_ref[...]
        vext_ref[:, HEAD_DIM:] = jnp.ones((vext_ref.shape[0], HEAD_DIM), BF16)

    for g in range(group):
        qs_ref[g * tq:(g + 1) * tq, :] = q_ref[:, g * HEAD_DIM:(g + 1) * HEAD_DIM]
    m_ref[...] = jnp.full(m_ref.shape, -jnp.inf, F32)
    acc_ref[...] = jnp.zeros(acc_ref.shape, F32)

    for j in range(nk):
        kc = k_ref[j * tk:(j + 1) * tk, :]
        s = lax.dot_general(qs_ref[...], kc, (((1,), (1,)), ((), ())),
                            preferred_element_type=F32)
        m_prev = m_ref[...]
        m_new = jnp.maximum(m_prev, jnp.max(s, axis=-1, keepdims=True))
        alpha = jnp.exp2(m_prev - m_new)
        p = jnp.concatenate(
            [jnp.exp2(s[:, c * LANES:(c + 1) * LANES] - m_new) for c in range(tk // LANES)],
            axis=1).astype(BF16)
        pv = jnp.dot(p, vext_ref[j * tk:(j + 1) * tk, :], preferred_element_type=F32)
        acc_ref[...] = jnp.concatenate([alpha, alpha], axis=1) * acc_ref[...] + pv
        m_ref[...] = m_new

    out = acc_ref[:, :HEAD_DIM] / acc_ref[:, HEAD_DIM:]
    for g in range(group):
        o_ref[:, g * HEAD_DIM:(g + 1) * HEAD_DIM] = out[g * tq:(g + 1) * tq].astype(BF16)


def _attention(q, k, v, *, batch, seq):
    t = q.shape[0]
    group = N_HEADS // N_KV_HEADS
    tq, tk = TQ_ATTN, TK_ATTN
    nq = seq // tq
    gw = group * HEAD_DIM
    kern = functools.partial(_attn_kernel, tq=tq, tk=tk, nk=seq // tk, group=group)
    return pl.pallas_call(
        kern,
        grid=(batch, N_KV_HEADS, nq),
        in_specs=[pl.BlockSpec((tq, gw), lambda b, h, i: (b * nq + i, h)),
                  pl.BlockSpec((seq, HEAD_DIM), lambda b, h, i: (b, h)),
                  pl.BlockSpec((seq, HEAD_DIM), lambda b, h, i: (b, h))],
        out_specs=pl.BlockSpec((tq, gw), lambda b, h, i: (b * nq + i, h)),
        out_shape=jax.ShapeDtypeStruct((t, N_HEADS * HEAD_DIM), BF16),
        scratch_shapes=[pltpu.VMEM((group * tq, HEAD_DIM), BF16),
                        pltpu.VMEM((seq, 2 * HEAD_DIM), BF16),
                        pltpu.VMEM((group * tq, LANES), F32),
                        pltpu.VMEM((group * tq, 2 * HEAD_DIM), F32)],
        compiler_params=_cparams(("arbitrary", "arbitrary", "arbitrary")),
        name="attention",
    )(q, k, v)


def _post_kernel(x_ref, cb_ref, z_ref, zprev_ref, znext_ref, o_ref, sgc_ref, sga_ref,
                 cw_ref, wc_ref, wa_ref, wo_ref, g2_ref, wr_ref,
                 x1_ref, u2_ref, meta_ref, cnt_ref, *, tm, nseq):
    i = pl.program_id(0)
    at_start = (i % nseq) == 0
    at_end = (i % nseq) == nseq - 1

    z = z_ref[...].astype(F32)
    prev_row = zprev_ref[BF16_SUBLANES - 1:BF16_SUBLANES, :].astype(F32)
    next_row = znext_ref[0:1, :].astype(F32)
    prev_row = jnp.where(at_start, 0.0, prev_row)
    next_row = jnp.where(at_end, 0.0, next_row)
    rowid = lax.broadcasted_iota(jnp.int32, (tm, 1), 0)
    zp = jnp.where(rowid == 0, prev_row, pltpu.roll(z, 1, axis=0))
    zn = jnp.where(rowid == tm - 1, next_row, pltpu.roll(z, tm - 1, axis=0))
    cw = cw_ref[...]
    conv = cw[0:1, :] * zp + cw[1:2, :] * z + cw[2:3, :] * zn
    cbz = (cb_ref[...].astype(F32) * conv).astype(BF16)

    y_conv = jnp.dot(cbz, wc_ref[...], preferred_element_type=F32)
    y_attn = jnp.dot(o_ref[...], wa_ref[...], preferred_element_type=F32)
    merged = (sgc_ref[...].astype(F32) * y_conv
              + sga_ref[...].astype(F32) * y_attn).astype(BF16)
    x1 = x_ref[...] + jnp.dot(merged, wo_ref[...], preferred_element_type=F32)
    x1_ref[...] = x1

    ms = jnp.mean(x1 * x1, axis=-1, keepdims=True)
    u2 = x1 * lax.rsqrt(ms + EPS) * g2_ref[...]
    u2_ref[...] = u2

    logits = jnp.dot(u2, wr_ref[...], precision=lax.Precision.HIGHEST,
                     preferred_element_type=F32)
    lane = lax.broadcasted_iota(jnp.int32, (tm, LANES), 1).astype(F32)
    neg = -jnp.inf
    big = float(2 * LANES)
    is_group = lane < N_GROUPS
    gl = jnp.where(is_group, logits, neg)
    gmax = jnp.max(gl, axis=-1, keepdims=True)
    gidx = jnp.min(jnp.where(gl == gmax, lane, big), axis=-1, keepdims=True)
    gsum = jnp.sum(jnp.where(is_group, jnp.exp(logits - gmax), 0.0), axis=-1,
                   keepdims=True)
    pg = 1.0 / gsum
    lane_group = jnp.floor(lane * (1.0 / EXPERTS_PER_GROUP)) - 1.0
    mine = (lane_group == gidx) & (lane >= N_GROUPS) & (lane < N_GROUPS + N_EXPERTS)
    sel = jnp.where(mine, logits, neg)
    v1 = jnp.max(sel, axis=-1, keepdims=True)
    i1 = jnp.min(jnp.where(sel == v1, lane, big), axis=-1, keepdims=True)
    sel2 = jnp.where(lane == i1, neg, sel)
    v2 = jnp.max(sel2, axis=-1, keepdims=True)
    i2 = jnp.min(jnp.where(sel2 == v2, lane, big), axis=-1, keepdims=True)
    t2 = jnp.exp(v2 - v1)
    den = 1.0 + t2
    wgt1 = pg * (1.0 / den)
    wgt2 = pg * (t2 / den)
    meta = jnp.where(lane == 0, i1 - N_GROUPS,
                     jnp.where(lane == 1, i2 - N_GROUPS,
                               jnp.where(lane == 2, wgt1,
                                         jnp.where(lane == 3, wgt2, 0.0))))
    meta_ref[...] = meta

    picked = jnp.where((lane == i1 - N_GROUPS) | (lane == i2 - N_GROUPS), 1.0, 0.0)

    @pl.when(i == 0)
    def _():
        cnt_ref[...] = jnp.zeros(cnt_ref.shape, F32)

    cnt_ref[...] += jnp.sum(picked, axis=0, keepdims=True)


def _post(x2, cb, z, o, sgc, sga, conv_w, wc, wa, wo, g2, wr, *, seq):
    t, d_model = x2.shape
    tm = TM_PROJ
    nseq = seq // tm
    hb = tm // BF16_SUBLANES
    nhalo = t // BF16_SUBLANES
    d_conv = cb.shape[1]
    d_q = o.shape[1]

    def row(width):
        return pl.BlockSpec((tm, width), lambda i: (i, 0))

    kern = functools.partial(_post_kernel, tm=tm, nseq=nseq)
    return pl.pallas_call(
        kern,
        grid=(t // tm,),
        in_specs=[row(d_model), row(d_conv), row(d_conv),
                  pl.BlockSpec((BF16_SUBLANES, d_conv),
                               lambda i: (jnp.maximum(i * hb - 1, 0), 0)),
                  pl.BlockSpec((BF16_SUBLANES, d_conv),
                               lambda i: (jnp.minimum((i + 1) * hb, nhalo - 1), 0)),
                  row(d_q), row(d_model), row(d_model),
                  _resident(conv_w.shape), _resident(wc.shape), _resident(wa.shape),
                  _resident(wo.shape), _resident(g2.shape), _resident(wr.shape)],
        out_specs=[row(d_model), row(d_model), row(LANES),
                   pl.BlockSpec((1, LANES), lambda i: (0, 0))],
        out_shape=[jax.ShapeDtypeStruct((t, d_model), F32),
                   jax.ShapeDtypeStruct((t, d_model), F32),
                   jax.ShapeDtypeStruct((t, LANES), F32),
                   jax.ShapeDtypeStruct((1, LANES), F32)],
        compiler_params=_cparams(("arbitrary",)),
        name="post",
    )(x2, cb, z, z, z, o, sgc, sga, conv_w, wc, wa, wo, g2, wr)


def _rank_kernel(meta_ref, cnt_ref, dst_ref, carry_ref, pstart_ref, *, tb, blk):
    i = pl.program_id(0)
    lane = lax.broadcasted_iota(jnp.int32, (tb, LANES), 1).astype(F32)
    meta = meta_ref[...]
    oh1 = jnp.where(lane == meta[:, 0:1], 1.0, 0.0)
    oh2 = jnp.where(lane == meta[:, 1:2], 1.0, 0.0)
    c = oh1 + oh2

    @pl.when(i == 0)
    def _():
        cnt = jnp.broadcast_to(cnt_ref[...], (8, LANES))
        nblk = jnp.floor((cnt + (blk - 1)) * (1.0 / blk))
        r = lax.broadcasted_iota(jnp.int32, (LANES, LANES), 0)
        col = lax.broadcasted_iota(jnp.int32, (LANES, LANES), 1)
        upper = jnp.where(r < col, 1.0, 0.0).astype(BF16)
        pblk = jnp.dot(nblk.astype(BF16), upper, preferred_element_type=F32)
        pstart_ref[...] = pblk[0:1, :] * blk
        carry_ref[...] = jnp.zeros(carry_ref.shape, F32)

    r = lax.broadcasted_iota(jnp.int32, (tb, tb), 0)
    col = lax.broadcasted_iota(jnp.int32, (tb, tb), 1)
    lower = jnp.where(col < r, 1.0, 0.0).astype(BF16)
    prefix = jnp.dot(lower, c.astype(BF16), preferred_element_type=F32)
    base = prefix + carry_ref[...] + pstart_ref[...]
    d1 = jnp.sum(oh1 * base, axis=-1, keepdims=True)
    d2 = jnp.sum(oh2 * base, axis=-1, keepdims=True)
    dst_ref[...] = jnp.where(lane == 0, d1, jnp.where(lane == 1, d2, 0.0))
    carry_ref[...] += jnp.sum(c, axis=0, keepdims=True)


def _rank(meta, cnt, *, blk):
    t = meta.shape[0]
    tb = TB_RANK
    kern = functools.partial(_rank_kernel, tb=tb, blk=blk)
    return pl.pallas_call(
        kern,
        grid=(t // tb,),
        in_specs=[pl.BlockSpec((tb, LANES), lambda i: (i, 0)),
                  pl.BlockSpec((1, LANES), lambda i: (0, 0))],
        out_specs=pl.BlockSpec((tb, LANES), lambda i: (i, 0)),
        out_shape=jax.ShapeDtypeStruct((t, LANES), F32),
        scratch_shapes=[pltpu.VMEM((1, LANES), F32), pltpu.VMEM((1, LANES), F32)],
        compiler_params=_cparams(("arbitrary",)),
        name="rank",
    )(meta, cnt)


def _dispatch_kernel(dst_sm, pad_start_sm, pad_len_sm, nused_sm, u2_ref, xs_ref,
                     zero_ref, sem, zsem, *, tb, blk, nblocks):
    step = pl.program_id(0)
    base = step * (tb * TOP_K)
    nbits = blk.bit_length() - 1
    tile_bits = F32_SUBLANES.bit_length() - 1

    def pad_copies(visit):
        def per_expert(e, carry):
            start, length = pad_start_sm[e], pad_len_sm[e]
            head = jnp.minimum((-start) & (F32_SUBLANES - 1), length)
            for j in range(F32_SUBLANES - 1):
                @pl.when(j < head)
                def _():
                    visit(pltpu.make_async_copy(
                        zero_ref.at[pl.ds(0, 1)], xs_ref.at[pl.ds(start + j, 1)], zsem))

            body_start = start + head
            body = length - head
            for bit in range(tile_bits, nbits):
                size = 1 << bit
                higher = lax.shift_left(lax.shift_right_logical(body, bit + 1), bit + 1)

                @pl.when((lax.shift_right_logical(body, bit) & 1) == 1)
                def _():
                    off = pl.multiple_of(body_start + higher, F32_SUBLANES)
                    visit(pltpu.make_async_copy(
                        zero_ref.at[pl.ds(0, size)], xs_ref.at[pl.ds(off, size)], zsem))
            return carry

        lax.fori_loop(0, N_EXPERTS, per_expert, 0)

        def per_block(b, carry):
            visit(pltpu.make_async_copy(
                zero_ref, xs_ref.at[pl.ds(pl.multiple_of(b * blk, blk), blk)], zsem))
            return carry

        lax.fori_loop(nused_sm[0], nblocks, per_block, 0)

    @pl.when(step == 0)
    def _():
        zero_ref[...] = jnp.zeros(zero_ref.shape, F32)
        pad_copies(lambda c: c.start())

    def issue(r, carry):
        for kk in range(TOP_K):
            pltpu.make_async_copy(
                u2_ref.at[pl.ds(r, 1)],
                xs_ref.at[pl.ds(dst_sm[base + TOP_K * r + kk], 1)], sem).start(priority=kk)
        return carry

    lax.fori_loop(0, tb, issue, 0, unroll=8)
    for kk in range(TOP_K):
        pltpu.make_async_copy(u2_ref, xs_ref.at[pl.ds(0, tb)], sem).wait()

    @pl.when(step == pl.num_programs(0) - 1)
    def _():
        pad_copies(lambda c: c.wait())


def _dispatch(dst_flat, pad_start, pad_len, nused, u2, *, p_rows, blk):
    t, d_model = u2.shape
    tb = TB_ROWS
    kern = functools.partial(_dispatch_kernel, tb=tb, blk=blk, nblocks=p_rows // blk)
    return pl.pallas_call(
        kern,
        grid_spec=pltpu.PrefetchScalarGridSpec(
            num_scalar_prefetch=4,
            grid=(t // tb,),
            in_specs=[pl.BlockSpec((tb, d_model), lambda i, *_: (i, 0))],
            out_specs=pl.BlockSpec(memory_space=pl.ANY),
            scratch_shapes=[pltpu.VMEM((blk, d_model), F32),
                            pltpu.SemaphoreType.DMA, pltpu.SemaphoreType.DMA]),
        out_shape=jax.ShapeDtypeStruct((p_rows, d_model), F32),
        compiler_params=_cparams(("arbitrary",)),
        name="dispatch",
    )(dst_flat, pad_start, pad_len, nused, u2)


def _experts_kernel(be_sm, nused_sm, xs_ref, w1_ref, w3_ref, w2_ref, ys_ref,
                    w13b_ref, w2b_ref, *, d_expert):
    b = pl.program_id(0)
    e = be_sm[b]
    e_prev = be_sm[jnp.maximum(b - 1, 0)]
    used = b < nused_sm[0]

    @pl.when((b == 0) | (e != e_prev))
    def _():
        w13b_ref[:, :d_expert] = w1_ref[...].astype(BF16)
        w13b_ref[:, d_expert:] = w3_ref[...].astype(BF16)
        w2b_ref[...] = w2_ref[...].astype(BF16)

    @pl.when(used)
    def _():
        xb = xs_ref[...].astype(BF16)
        h = jnp.dot(xb, w13b_ref[...], preferred_element_type=F32)
        h1 = h[:, :d_expert]
        h3 = h[:, d_expert:]
        a = (h1 * _sigmoid(h1) * h3).astype(BF16)
        ys_ref[...] = jnp.dot(a, w2b_ref[...], preferred_element_type=F32)

    @pl.when(jnp.logical_not(used))
    def _():
        ys_ref[...] = jnp.zeros(ys_ref.shape, F32)


def _experts(blk_expert, nused, xs, w1, w3, w2, *, blk):
    p_rows, d_model = xs.shape
    d_expert = w1.shape[-1]
    kern = functools.partial(_experts_kernel, d_expert=d_expert)

    def rows_in(b, be, nu):
        return (jnp.minimum(b, nu[0] - 1), 0)

    def weights(b, be, nu):
        return (be[b], 0, 0)

    return pl.pallas_call(
        kern,
        grid_spec=pltpu.PrefetchScalarGridSpec(
            num_scalar_prefetch=2,
            grid=(p_rows // blk,),
            in_specs=[pl.BlockSpec((blk, d_model), rows_in),
                      pl.BlockSpec((None, d_model, d_expert), weights),
                      pl.BlockSpec((None, d_model, d_expert), weights),
                      pl.BlockSpec((None, d_expert, d_model), weights)],
            out_specs=pl.BlockSpec((blk, d_model), lambda b, be, nu: (b, 0)),
            scratch_shapes=[pltpu.VMEM((d_model, 2 * d_expert), BF16),
                            pltpu.VMEM((d_expert, d_model), BF16)]),
        out_shape=jax.ShapeDtypeStruct((p_rows, d_model), F32),
        compiler_params=_cparams(("arbitrary",)),
        name="experts",
    )(blk_expert, nused, xs, w1, w3, w2)


def _combine_kernel(dst_sm, x1_ref, meta_ref, ys_ref, out_ref, buf_ref, sem, *, tb):
    base = pl.program_id(0) * (tb * TOP_K)

    def issue(r, carry):
        for kk in range(TOP_K):
            d = dst_sm[base + TOP_K * r + kk]
            pltpu.make_async_copy(ys_ref.at[pl.ds(d, 1)],
                                  buf_ref.at[kk, pl.ds(r, 1)], sem).start(priority=kk)
        return carry

    lax.fori_loop(0, tb, issue, 0, unroll=8)
    for kk in range(TOP_K):
        pltpu.make_async_copy(ys_ref.at[pl.ds(0, tb)], buf_ref.at[kk], sem).wait()
    meta = meta_ref[...]
    moe = buf_ref[0] * meta[:, 2:3] + buf_ref[1] * meta[:, 3:4]
    out_ref[...] = x1_ref[...] + moe


def _combine(dst_flat, x1, meta, ys):
    t, d_model = x1.shape
    tb = TB_ROWS
    kern = functools.partial(_combine_kernel, tb=tb)
    return pl.pallas_call(
        kern,
        grid_spec=pltpu.PrefetchScalarGridSpec(
            num_scalar_prefetch=1,
            grid=(t // tb,),
            in_specs=[pl.BlockSpec((tb, d_model), lambda i, dst: (i, 0)),
                      pl.BlockSpec((tb, LANES), lambda i, dst: (i, 0)),
                      pl.BlockSpec(memory_space=pl.ANY)],
            out_specs=pl.BlockSpec((tb, d_model), lambda i, dst: (i, 0)),
            scratch_shapes=[pltpu.VMEM((TOP_K, tb, d_model), F32),
                            pltpu.SemaphoreType.DMA]),
        out_shape=jax.ShapeDtypeStruct((t, d_model), F32),
        compiler_params=_cparams(("arbitrary",)),
        name="combine",
    )(dst_flat, x1, meta, ys)


def _layer(h2, *, batch, seq, norm1_g, w_in, conv_w, q_norm_g, k_norm_g,
           w_conv_out, w_attn_out, w_o, norm2_g, w_group, w_router, w1, w3, w2):
    t, d_model = h2.shape
    c, sa, sb = _rope_tables(seq)
    scale = HEAD_DIM ** -0.5 * LOG2_E
    tables_q = (c * scale, sa * scale, sb * scale)
    tables_k = (c, sa, sb)

    cb, z, q, k, v, sgc, sga = _inproj(
        h2, norm1_g[None, :], w_in.astype(BF16), q_norm_g[None, :], k_norm_g[None, :],
        tables_q, tables_k, seq=seq)
    o = _attention(q, k, v, batch=batch, seq=seq)

    n_route = N_GROUPS + N_EXPERTS
    wr = jnp.concatenate(
        [w_group, w_router, jnp.zeros((d_model, LANES - n_route), F32)], axis=1)
    x1, u2, meta, cnt = _post(h2, cb, z, o, sgc, sga, conv_w,
                              w_conv_out.astype(BF16), w_attn_out.astype(BF16),
                              w_o.astype(BF16), norm2_g[None, :], wr, seq=seq)

    blk = MOE_ROWS
    dst = _rank(meta, cnt, blk=blk)
    dst_flat = dst[:, :TOP_K].astype(jnp.int32).reshape(-1)
    counts = cnt[0, :N_EXPERTS].astype(jnp.int32)
    p_rows = t * TOP_K + N_EXPERTS * blk
    nb = p_rows // blk
    nblk = (counts + blk - 1) // blk
    pend = jnp.cumsum(nblk)
    nused = pend[-1]
    bidx = jnp.minimum(jnp.arange(nb, dtype=jnp.int32), nused - 1)
    blk_expert = jnp.minimum(jnp.searchsorted(pend, bidx, side='right'),
                             N_EXPERTS - 1).astype(jnp.int32)
    pad_start = ((pend - nblk) * blk + counts).astype(jnp.int32)
    pad_len = (nblk * blk - counts).astype(jnp.int32)
    nused1 = nused.reshape(1).astype(jnp.int32)

    xs = _dispatch(dst_flat, pad_start, pad_len, nused1, u2, p_rows=p_rows, blk=blk)
    ys = _experts(blk_expert, nused1, xs, w1, w3, w2, blk=blk)
    return _combine(dst_flat, x1, meta, ys)


def kernel(x, norm1_g, w_in, conv_w, q_norm_g, k_norm_g, w_conv_out, w_attn_out, w_o,
           norm2_g, w_group, w_router, w1, w3, w2):
    batch, seq, d_model = x.shape
    h2 = x.reshape(batch * seq, d_model)
    for l in range(norm1_g.shape[0]):
        h2 = _layer(h2, batch=batch, seq=seq, norm1_g=norm1_g[l], w_in=w_in[l],
                    conv_w=conv_w[l], q_norm_g=q_norm_g[l], k_norm_g=k_norm_g[l],
                    w_conv_out=w_conv_out[l], w_attn_out=w_attn_out[l], w_o=w_o[l],
                    norm2_g=norm2_g[l], w_group=w_group[l], w_router=w_router[l],
                    w1=w1[l], w3=w3[l], w2=w2[l])
    return h2.reshape(batch, seq, d_model)
```

```python
import functools

import jax
import jax.numpy as jnp
from jax import lax
from jax.experimental import pallas as pl
from jax.experimental.pallas import tpu as pltpu

F32 = jnp.float32
BF16 = jnp.bfloat16

GRID_W = 64
EPS = 1e-6
N_HEADS = 8
N_KV_HEADS = 2
HEAD_DIM = 128
ROPE_THETA = 10000.0
N_GROUPS = 8
EXPERTS_PER_GROUP = 8
N_EXPERTS = N_GROUPS * EXPERTS_PER_GROUP
TOP_K = 2
LOG2_E = 1.4426950408889634

LANES = 128
V7X_VMEM_LIMIT_BYTES = 56000 * 1024

TM_PROJ = 512
TQ_ATTN = 256
TK_ATTN = 512
TB_RANK = 1024
TB_ROWS = 256
MOE_ROWS = 256
BF16_SUBLANES = 16
F32_SUBLANES = 8


def _cparams(sem):
    return pltpu.CompilerParams(dimension_semantics=sem,
                                vmem_limit_bytes=V7X_VMEM_LIMIT_BYTES)


ROW_TILE = 8


def _row_tiles(shape2d):
    rows, width = shape2d
    assert width == ROW_TILE * LANES
    return (rows * ROW_TILE, LANES)


def _rows(ref, r, n=1):
    return ref.at[pl.ds(pl.multiple_of(r * ROW_TILE, ROW_TILE), n * ROW_TILE)]


def _load_row_tiles(ref):
    rows = ref.shape[0] // ROW_TILE
    return jnp.concatenate(
        [ref[pl.ds(s, rows, stride=ROW_TILE), :] for s in range(ROW_TILE)], axis=1)


def _store_row_tiles(ref, value):
    rows = ref.shape[0] // ROW_TILE
    for s in range(ROW_TILE):
        ref[pl.ds(s, rows, stride=ROW_TILE), :] = value[:, s * LANES:(s + 1) * LANES]


def _resident(shape):
    nd = len(shape)
    return pl.BlockSpec(shape, lambda *_: (0,) * nd, pipeline_mode=pl.Buffered(1))


def _head_norm_rope(xh, g, c, sa, sb):
    ms = jnp.mean(xh * xh, axis=-1, keepdims=True)
    y = xh * lax.rsqrt(ms + EPS) * g
    y_next = pltpu.roll(y, HEAD_DIM - 1, axis=1)
    y_prev = pltpu.roll(y, 1, axis=1)
    return y * c + y_next * sa + y_prev * sb


def _sigmoid(x):
    return 1.0 / (1.0 + jnp.exp(-x))


def _inproj_kernel(x_ref, g1_ref, w_ref, gq_ref, gk_ref,
                   cq_ref, saq_ref, sbq_ref, ck_ref, sak_ref, sbk_ref,
                   cb_ref, z_ref, q_ref, k_ref, v_ref, sgc_ref, sga_ref,
                   *, d_conv, d_q, d_kv, d_model):
    x = x_ref[...]
    ms = jnp.mean(x * x, axis=-1, keepdims=True)
    u = (x * lax.rsqrt(ms + EPS) * g1_ref[...]).astype(BF16)

    def proj(lo, width):
        return jnp.dot(u, w_ref[:, lo:lo + width], preferred_element_type=F32)

    off = 0
    cb_ref[...] = proj(off, d_conv).astype(BF16)
    off += d_conv
    cc = proj(off, d_conv)
    off += d_conv
    cx = proj(off, d_conv)
    off += d_conv
    z_ref[...] = (cc * cx).astype(BF16)

    q = proj(off, d_q)
    off += d_q
    gq = gq_ref[...]
    cq, saq, sbq = cq_ref[...], saq_ref[...], sbq_ref[...]
    for h in range(d_q // HEAD_DIM):
        sl = slice(h * HEAD_DIM, (h + 1) * HEAD_DIM)
        q_ref[:, sl] = _head_norm_rope(q[:, sl], gq, cq, saq, sbq).astype(BF16)

    k = proj(off, d_kv)
    off += d_kv
    gk = gk_ref[...]
    ck, sak, sbk = ck_ref[...], sak_ref[...], sbk_ref[...]
    for h in range(d_kv // HEAD_DIM):
        sl = slice(h * HEAD_DIM, (h + 1) * HEAD_DIM)
        k_ref[:, sl] = _head_norm_rope(k[:, sl], gk, ck, sak, sbk).astype(BF16)

    v_ref[...] = proj(off, d_kv).astype(BF16)
    off += d_kv
    sgc_ref[...] = _sigmoid(proj(off, d_model)).astype(BF16)
    off += d_model
    sga_ref[...] = _sigmoid(proj(off, d_model)).astype(BF16)


def _rope_tables(seq):
    rows = seq // GRID_W
    axis_dim = HEAD_DIM // 2
    row = jnp.repeat(jnp.arange(rows, dtype=F32), GRID_W)
    col = jnp.tile(jnp.arange(GRID_W, dtype=F32), rows)
    inv = ROPE_THETA ** (-jnp.arange(0, axis_dim, 2, dtype=F32) / axis_dim)
    ang = jnp.concatenate([row[:, None] * inv, col[:, None] * inv], axis=-1)
    cos, sin = jnp.cos(ang), jnp.sin(ang)
    zero = jnp.zeros_like(sin)
    c = jnp.repeat(cos, 2, axis=-1)
    sa = jnp.stack([-sin, zero], axis=-1).reshape(seq, HEAD_DIM)
    sb = jnp.stack([zero, sin], axis=-1).reshape(seq, HEAD_DIM)
    return c, sa, sb


def _inproj(x2, g1, w_in_bf, gq, gk, tables_q, tables_k, *, seq):
    t, d_model = x2.shape
    d_q = N_HEADS * HEAD_DIM
    d_kv = N_KV_HEADS * HEAD_DIM
    d_in = w_in_bf.shape[1]
    d_conv = (d_in - d_q - 2 * d_kv - 2 * d_model) // 3
    tm = TM_PROJ
    nseq = seq // tm

    def row(width):
        return pl.BlockSpec((tm, width), lambda i: (i, 0))

    table = pl.BlockSpec((tm, HEAD_DIM), lambda i: (i % nseq, 0))
    kern = functools.partial(_inproj_kernel, d_conv=d_conv, d_q=d_q, d_kv=d_kv,
                             d_model=d_model)
    out_shape = [jax.ShapeDtypeStruct((t, w), BF16)
                 for w in (d_conv, d_conv, d_q, d_kv, d_kv, d_model, d_model)]
    return pl.pallas_call(
        kern,
        grid=(t // tm,),
        in_specs=[row(d_model), _resident((1, d_model)), _resident((d_model, d_in)),
                  _resident((1, HEAD_DIM)), _resident((1, HEAD_DIM)),
                  table, table, table, table, table, table],
        out_specs=[row(d_conv), row(d_conv), row(d_q), row(d_kv), row(d_kv),
                   row(d_model), row(d_model)],
        out_shape=out_shape,
        compiler_params=_cparams(("arbitrary",)),
        name="inproj",
    )(x2, g1, w_in_bf, gq, gk, *tables_q, *tables_k)


def _attn_kernel(q_ref, k_ref, v_ref, o_ref, qs_ref, vext_ref, m_ref, acc_ref,
                 *, tq, tk, nk, group):
    @pl.when(pl.program_id(2) == 0)
    def _():
        vext_ref[:, :HEAD_DIM] = v_ref[...]
        vext_ref[:, HEAD_DIM:] = jnp.ones((vext_ref.shape[0], HEAD_DIM), BF16)

    for g in range(group):
        qs_ref[g * tq:(g + 1) * tq, :] = q_ref[:, g * HEAD_DIM:(g + 1) * HEAD_DIM]
    m_ref[...] = jnp.full(m_ref.shape, -jnp.inf, F32)
    acc_ref[...] = jnp.zeros(acc_ref.shape, F32)

    for j in range(nk):
        kc = k_ref[j * tk:(j + 1) * tk, :]
        s = lax.dot_general(qs_ref[...], kc, (((1,), (1,)), ((), ())),
                            preferred_element_type=F32)
        m_prev = m_ref[...]
        m_new = jnp.maximum(m_prev, jnp.max(s, axis=-1, keepdims=True))
        alpha = jnp.exp2(m_prev - m_new)
        p = jnp.concatenate(
            [jnp.exp2(s[:, c * LANES:(c + 1) * LANES] - m_new) for c in range(tk // LANES)],
            axis=1).astype(BF16)
        pv = jnp.dot(p, vext_ref[j * tk:(j + 1) * tk, :], preferred_element_type=F32)
        acc_ref[...] = jnp.concatenate([alpha, alpha], axis=1) * acc_ref[...] + pv
        m_ref[...] = m_new

    out = acc_ref[:, :HEAD_DIM] / acc_ref[:, HEAD_DIM:]
    for g in range(group):
        o_ref[:, g * HEAD_DIM:(g + 1) * HEAD_DIM] = out[g * tq:(g + 1) * tq].astype(BF16)


def _attention(q, k, v, *, batch, seq):
    t = q.shape[0]
    group = N_HEADS // N_KV_HEADS
    tq, tk = TQ_ATTN, TK_ATTN
    nq = seq // tq
    gw = group * HEAD_DIM
    kern = functools.partial(_attn_kernel, tq=tq, tk=tk, nk=seq // tk, group=group)
    return pl.pallas_call(
        kern,
        grid=(batch, N_KV_HEADS, nq),
        in_specs=[pl.BlockSpec((tq, gw), lambda b, h, i: (b * nq + i, h)),
                  pl.BlockSpec((seq, HEAD_DIM), lambda b, h, i: (b, h)),
                  pl.BlockSpec((seq, HEAD_DIM), lambda b, h, i: (b, h))],
        out_specs=pl.BlockSpec((tq, gw), lambda b, h, i: (b * nq + i, h)),
        out_shape=jax.ShapeDtypeStruct((t, N_HEADS * HEAD_DIM), BF16),
        scratch_shapes=[pltpu.VMEM((group * tq, HEAD_DIM), BF16),
                        pltpu.VMEM((seq, 2 * HEAD_DIM), BF16),
                        pltpu.VMEM((group * tq, LANES), F32),
                        pltpu.VMEM((group * tq, 2 * HEAD_DIM), F32)],
        compiler_params=_cparams(("arbitrary", "arbitrary", "arbitrary")),
        name="attention",
    )(q, k, v)


def _post_kernel(x_ref, cb_ref, z_ref, zprev_ref, znext_ref, o_ref, sgc_ref, sga_ref,
                 cw_ref, wc_ref, wa_ref, wo_ref, g2_ref, wr_ref,
                 x1_ref, u2_ref, meta_ref, cnt_ref, *, tm, nseq):
    i = pl.program_id(0)
    at_start = (i % nseq) == 0
    at_end = (i % nseq) == nseq - 1

    z = z_ref[...].astype(F32)
    prev_row = zprev_ref[BF16_SUBLANES - 1:BF16_SUBLANES, :].astype(F32)
    next_row = znext_ref[0:1, :].astype(F32)
    prev_row = jnp.where(at_start, 0.0, prev_row)
    next_row = jnp.where(at_end, 0.0, next_row)
    rowid = lax.broadcasted_iota(jnp.int32, (tm, 1), 0)
    zp = jnp.where(rowid == 0, prev_row, pltpu.roll(z, 1, axis=0))
    zn = jnp.where(rowid == tm - 1, next_row, pltpu.roll(z, tm - 1, axis=0))
    cw = cw_ref[...]
    conv = cw[0:1, :] * zp + cw[1:2, :] * z + cw[2:3, :] * zn
    cbz = (cb_ref[...].astype(F32) * conv).astype(BF16)

    y_conv = jnp.dot(cbz, wc_ref[...], preferred_element_type=F32)
    y_attn = jnp.dot(o_ref[...], wa_ref[...], preferred_element_type=F32)
    merged = (sgc_ref[...].astype(F32) * y_conv
              + sga_ref[...].astype(F32) * y_attn).astype(BF16)
    x1 = x_ref[...] + jnp.dot(merged, wo_ref[...], preferred_element_type=F32)
    x1_ref[...] = x1

    ms = jnp.mean(x1 * x1, axis=-1, keepdims=True)
    u2 = x1 * lax.rsqrt(ms + EPS) * g2_ref[...]
    _store_row_tiles(u2_ref, u2)

    logits = jnp.dot(u2, wr_ref[...], precision=lax.Precision.HIGHEST,
                     preferred_element_type=F32)
    lane = lax.broadcasted_iota(jnp.int32, (tm, LANES), 1).astype(F32)
    neg = -jnp.inf
    big = float(2 * LANES)
    is_group = lane < N_GROUPS
    gl = jnp.where(is_group, logits, neg)
    gmax = jnp.max(gl, axis=-1, keepdims=True)
    gidx = jnp.min(jnp.where(gl == gmax, lane, big), axis=-1, keepdims=True)
    gsum = jnp.sum(jnp.where(is_group, jnp.exp(logits - gmax), 0.0), axis=-1,
                   keepdims=True)
    pg = 1.0 / gsum
    lane_group = jnp.floor(lane * (1.0 / EXPERTS_PER_GROUP)) - 1.0
    mine = (lane_group == gidx) & (lane >= N_GROUPS) & (lane < N_GROUPS + N_EXPERTS)
    sel = jnp.where(mine, logits, neg)
    v1 = jnp.max(sel, axis=-1, keepdims=True)
    i1 = jnp.min(jnp.where(sel == v1, lane, big), axis=-1, keepdims=True)
    sel2 = jnp.where(lane == i1, neg, sel)
    v2 = jnp.max(sel2, axis=-1, keepdims=True)
    i2 = jnp.min(jnp.where(sel2 == v2, lane, big), axis=-1, keepdims=True)
    t2 = jnp.exp(v2 - v1)
    den = 1.0 + t2
    wgt1 = pg * (1.0 / den)
    wgt2 = pg * (t2 / den)
    meta = jnp.where(lane == 0, i1 - N_GROUPS,
                     jnp.where(lane == 1, i2 - N_GROUPS,
                               jnp.where(lane == 2, wgt1,
                                         jnp.where(lane == 3, wgt2, 0.0))))
    meta_ref[...] = meta

    picked = jnp.where((lane == i1 - N_GROUPS) | (lane == i2 - N_GROUPS), 1.0, 0.0)

    @pl.when(i == 0)
    def _():
        cnt_ref[...] = jnp.zeros(cnt_ref.shape, F32)

    cnt_ref[...] += jnp.sum(picked, axis=0, keepdims=True)


def _post(x2, cb, z, o, sgc, sga, conv_w, wc, wa, wo, g2, wr, *, seq):
    t, d_model = x2.shape
    tm = TM_PROJ
    nseq = seq // tm
    hb = tm // BF16_SUBLANES
    nhalo = t // BF16_SUBLANES
    d_conv = cb.shape[1]
    d_q = o.shape[1]

    def row(width):
        return pl.BlockSpec((tm, width), lambda i: (i, 0))

    kern = functools.partial(_post_kernel, tm=tm, nseq=nseq)
    return pl.pallas_call(
        kern,
        grid=(t // tm,),
        in_specs=[row(d_model), row(d_conv), row(d_conv),
                  pl.BlockSpec((BF16_SUBLANES, d_conv),
                               lambda i: (jnp.maximum(i * hb - 1, 0), 0)),
                  pl.BlockSpec((BF16_SUBLANES, d_conv),
                               lambda i: (jnp.minimum((i + 1) * hb, nhalo - 1), 0)),
                  row(d_q), row(d_model), row(d_model),
                  _resident(conv_w.shape), _resident(wc.shape), _resident(wa.shape),
                  _resident(wo.shape), _resident(g2.shape), _resident(wr.shape)],
        out_specs=[row(d_model),
                   pl.BlockSpec(_row_tiles((tm, d_model)), lambda i: (i, 0)),
                   row(LANES),
                   pl.BlockSpec((1, LANES), lambda i: (0, 0))],
        out_shape=[jax.ShapeDtypeStruct((t, d_model), F32),
                   jax.ShapeDtypeStruct(_row_tiles((t, d_model)), F32),
                   jax.ShapeDtypeStruct((t, LANES), F32),
                   jax.ShapeDtypeStruct((1, LANES), F32)],
        compiler_params=_cparams(("arbitrary",)),
        name="post",
    )(x2, cb, z, z, z, o, sgc, sga, conv_w, wc, wa, wo, g2, wr)


def _rank_kernel(meta_ref, cnt_ref, dst_ref, carry_ref, pstart_ref, *, tb, blk):
    i = pl.program_id(0)
    lane = lax.broadcasted_iota(jnp.int32, (tb, LANES), 1).astype(F32)
    meta = meta_ref[...]
    oh1 = jnp.where(lane == meta[:, 0:1], 1.0, 0.0)
    oh2 = jnp.where(lane == meta[:, 1:2], 1.0, 0.0)
    c = oh1 + oh2

    @pl.when(i == 0)
    def _():
        cnt = jnp.broadcast_to(cnt_ref[...], (8, LANES))
        nblk = jnp.floor((cnt + (blk - 1)) * (1.0 / blk))
        r = lax.broadcasted_iota(jnp.int32, (LANES, LANES), 0)
        col = lax.broadcasted_iota(jnp.int32, (LANES, LANES), 1)
        upper = jnp.where(r < col, 1.0, 0.0).astype(BF16)
        pblk = jnp.dot(nblk.astype(BF16), upper, preferred_element_type=F32)
        pstart_ref[...] = pblk[0:1, :] * blk
        carry_ref[...] = jnp.zeros(carry_ref.shape, F32)

    r = lax.broadcasted_iota(jnp.int32, (tb, tb), 0)
    col = lax.broadcasted_iota(jnp.int32, (tb, tb), 1)
    lower = jnp.where(col < r, 1.0, 0.0).astype(BF16)
    prefix = jnp.dot(lower, c.astype(BF16), preferred_element_type=F32)
    base = prefix + carry_ref[...] + pstart_ref[...]
    d1 = jnp.sum(oh1 * base, axis=-1, keepdims=True)
    d2 = jnp.sum(oh2 * base, axis=-1, keepdims=True)
    dst_ref[...] = jnp.where(lane == 0, d1, jnp.where(lane == 1, d2, 0.0))
    carry_ref[...] += jnp.sum(c, axis=0, keepdims=True)


def _rank(meta, cnt, *, blk):
    t = meta.shape[0]
    tb = TB_RANK
    kern = functools.partial(_rank_kernel, tb=tb, blk=blk)
    return pl.pallas_call(
        kern,
        grid=(t // tb,),
        in_specs=[pl.BlockSpec((tb, LANES), lambda i: (i, 0)),
                  pl.BlockSpec((1, LANES), lambda i: (0, 0))],
        out_specs=pl.BlockSpec((tb, LANES), lambda i: (i, 0)),
        out_shape=jax.ShapeDtypeStruct((t, LANES), F32),
        scratch_shapes=[pltpu.VMEM((1, LANES), F32), pltpu.VMEM((1, LANES), F32)],
        compiler_params=_cparams(("arbitrary",)),
        name="rank",
    )(meta, cnt)


def _dispatch_kernel(dst_sm, pad_start_sm, pad_len_sm, nused_sm, u2_ref, xs_ref,
                     zero_ref, sem, zsem, *, tb, blk, nblocks):
    step = pl.program_id(0)
    base = step * (tb * TOP_K)
    nbits = blk.bit_length() - 1

    def pad_copies(visit):
        def per_expert(e, carry):
            start, length = pad_start_sm[e], pad_len_sm[e]
            for bit in range(nbits):
                size = 1 << bit
                higher = lax.shift_left(lax.shift_right_logical(length, bit + 1), bit + 1)

                @pl.when((lax.shift_right_logical(length, bit) & 1) == 1)
                def _():
                    visit(pltpu.make_async_copy(
                        zero_ref.at[pl.ds(0, size * ROW_TILE)],
                        _rows(xs_ref, start + higher, size), zsem))
            return carry

        lax.fori_loop(0, N_EXPERTS, per_expert, 0)

        def per_block(b, carry):
            visit(pltpu.make_async_copy(zero_ref, _rows(xs_ref, b * blk, blk), zsem))
            return carry

        lax.fori_loop(nused_sm[0], nblocks, per_block, 0)

    @pl.when(step == 0)
    def _():
        zero_ref[...] = jnp.zeros(zero_ref.shape, F32)
        pad_copies(lambda c: c.start())

    def issue(r, carry):
        for kk in range(TOP_K):
            pltpu.make_async_copy(
                _rows(u2_ref, r), _rows(xs_ref, dst_sm[base + TOP_K * r + kk]), sem
            ).start(priority=kk)
        return carry

    lax.fori_loop(0, tb, issue, 0, unroll=8)
    for kk in range(TOP_K):
        pltpu.make_async_copy(u2_ref, _rows(xs_ref, 0, tb), sem).wait()

    @pl.when(step == pl.num_programs(0) - 1)
    def _():
        pad_copies(lambda c: c.wait())


def _dispatch(dst_flat, pad_start, pad_len, nused, u2, *, p_rows, blk):
    t = u2.shape[0] // ROW_TILE
    tb = TB_ROWS
    kern = functools.partial(_dispatch_kernel, tb=tb, blk=blk, nblocks=p_rows // blk)
    return pl.pallas_call(
        kern,
        grid_spec=pltpu.PrefetchScalarGridSpec(
            num_scalar_prefetch=4,
            grid=(t // tb,),
            in_specs=[pl.BlockSpec((tb * ROW_TILE, LANES), lambda i, *_: (i, 0))],
            out_specs=pl.BlockSpec(memory_space=pl.ANY),
            scratch_shapes=[pltpu.VMEM((blk * ROW_TILE, LANES), F32),
                            pltpu.SemaphoreType.DMA, pltpu.SemaphoreType.DMA]),
        out_shape=jax.ShapeDtypeStruct((p_rows * ROW_TILE, LANES), F32),
        compiler_params=_cparams(("arbitrary",)),
        name="dispatch",
    )(dst_flat, pad_start, pad_len, nused, u2)


def _experts_kernel(be_sm, nused_sm, xs_ref, w1_ref, w3_ref, w2_ref, ys_ref,
                    w13b_ref, w2b_ref, *, d_expert):
    b = pl.program_id(0)
    e = be_sm[b]
    e_prev = be_sm[jnp.maximum(b - 1, 0)]
    used = b < nused_sm[0]

    @pl.when((b == 0) | (e != e_prev))
    def _():
        w13b_ref[:, :d_expert] = w1_ref[...].astype(BF16)
        w13b_ref[:, d_expert:] = w3_ref[...].astype(BF16)
        w2b_ref[...] = w2_ref[...].astype(BF16)

    @pl.when(used)
    def _():
        xb = _load_row_tiles(xs_ref).astype(BF16)
        h = jnp.dot(xb, w13b_ref[...], preferred_element_type=F32)
        h1 = h[:, :d_expert]
        h3 = h[:, d_expert:]
        a = (h1 * _sigmoid(h1) * h3).astype(BF16)
        _store_row_tiles(ys_ref, jnp.dot(a, w2b_ref[...], preferred_element_type=F32))

    @pl.when(jnp.logical_not(used))
    def _():
        ys_ref[...] = jnp.zeros(ys_ref.shape, F32)


def _experts(blk_expert, nused, xs, w1, w3, w2, *, blk):
    p_rows = xs.shape[0] // ROW_TILE
    d_model, d_expert = w1.shape[-2:]
    kern = functools.partial(_experts_kernel, d_expert=d_expert)
    rows_blk = (blk * ROW_TILE, LANES)

    def rows_in(b, be, nu):
        return (jnp.maximum(jnp.minimum(b, nu[0] - 1), 0), 0)

    def weights(b, be, nu):
        return (be[b], 0, 0)

    return pl.pallas_call(
        kern,
        grid_spec=pltpu.PrefetchScalarGridSpec(
            num_scalar_prefetch=2,
            grid=(p_rows // blk,),
            in_specs=[pl.BlockSpec(rows_blk, rows_in),
                      pl.BlockSpec((None, d_model, d_expert), weights),
                      pl.BlockSpec((None, d_model, d_expert), weights),
                      pl.BlockSpec((None, d_expert, d_model), weights)],
            out_specs=pl.BlockSpec(rows_blk, lambda b, be, nu: (b, 0)),
            scratch_shapes=[pltpu.VMEM((d_model, 2 * d_expert), BF16),
                            pltpu.VMEM((d_expert, d_model), BF16)]),
        out_shape=jax.ShapeDtypeStruct((p_rows * ROW_TILE, LANES), F32),
        compiler_params=_cparams(("arbitrary",)),
        name="experts",
    )(blk_expert, nused, xs, w1, w3, w2)


def _combine_kernel(dst_sm, x1_ref, meta_ref, ys_ref, out_ref, buf0_ref, buf1_ref, sem,
                    *, tb):
    base = pl.program_id(0) * (tb * TOP_K)
    bufs = (buf0_ref, buf1_ref)

    def issue(r, carry):
        for kk in range(TOP_K):
            pltpu.make_async_copy(_rows(ys_ref, dst_sm[base + TOP_K * r + kk]),
                                  _rows(bufs[kk], r), sem).start(priority=kk)
        return carry

    lax.fori_loop(0, tb, issue, 0, unroll=8)
    for kk in range(TOP_K):
        pltpu.make_async_copy(_rows(ys_ref, 0, tb), bufs[kk], sem).wait()
    meta = meta_ref[...]
    moe = (_load_row_tiles(buf0_ref) * meta[:, 2:3]
           + _load_row_tiles(buf1_ref) * meta[:, 3:4])
    out_ref[...] = x1_ref[...] + moe


def _combine(dst_flat, x1, meta, ys):
    t, d_model = x1.shape
    tb = TB_ROWS
    kern = functools.partial(_combine_kernel, tb=tb)
    return pl.pallas_call(
        kern,
        grid_spec=pltpu.PrefetchScalarGridSpec(
            num_scalar_prefetch=1,
            grid=(t // tb,),
            in_specs=[pl.BlockSpec((tb, d_model), lambda i, dst: (i, 0)),
                      pl.BlockSpec((tb, LANES), lambda i, dst: (i, 0)),
                      pl.BlockSpec(memory_space=pl.ANY)],
            out_specs=pl.BlockSpec((tb, d_model), lambda i, dst: (i, 0)),
            scratch_shapes=[pltpu.VMEM(_row_tiles((tb, d_model)), F32),
                            pltpu.VMEM(_row_tiles((tb, d_model)), F32),
                            pltpu.SemaphoreType.DMA]),
        out_shape=jax.ShapeDtypeStruct((t, d_model), F32),
        compiler_params=_cparams(("arbitrary",)),
        name="combine",
    )(dst_flat, x1, meta, ys)


def _layer(h2, *, batch, seq, norm1_g, w_in, conv_w, q_norm_g, k_norm_g,
           w_conv_out, w_attn_out, w_o, norm2_g, w_group, w_router, w1, w3, w2):
    t, d_model = h2.shape
    c, sa, sb = _rope_tables(seq)
    scale = HEAD_DIM ** -0.5 * LOG2_E
    tables_q = (c * scale, sa * scale, sb * scale)
    tables_k = (c, sa, sb)

    cb, z, q, k, v, sgc, sga = _inproj(
        h2, norm1_g[None, :], w_in.astype(BF16), q_norm_g[None, :], k_norm_g[None, :],
        tables_q, tables_k, seq=seq)
    o = _attention(q, k, v, batch=batch, seq=seq)

    n_route = N_GROUPS + N_EXPERTS
    wr = jnp.concatenate(
        [w_group, w_router, jnp.zeros((d_model, LANES - n_route), F32)], axis=1)
    x1, u2, meta, cnt = _post(h2, cb, z, o, sgc, sga, conv_w,
                              w_conv_out.astype(BF16), w_attn_out.astype(BF16),
                              w_o.astype(BF16), norm2_g[None, :], wr, seq=seq)

    blk = MOE_ROWS
    dst = _rank(meta, cnt, blk=blk)
    dst_flat = dst[:, :TOP_K].astype(jnp.int32).reshape(-1)
    counts = cnt[0, :N_EXPERTS].astype(jnp.int32)
    p_rows = t * TOP_K + N_EXPERTS * blk
    nb = p_rows // blk
    nblk = (counts + blk - 1) // blk
    pend = jnp.cumsum(nblk)
    nused = pend[-1]
    bidx = jnp.minimum(jnp.arange(nb, dtype=jnp.int32), nused - 1)
    blk_expert = jnp.minimum(jnp.searchsorted(pend, bidx, side='right'),
                             N_EXPERTS - 1).astype(jnp.int32)
    pad_start = ((pend - nblk) * blk + counts).astype(jnp.int32)
    pad_len = (nblk * blk - counts).astype(jnp.int32)
    nused1 = nused.reshape(1).astype(jnp.int32)

    xs = _dispatch(dst_flat, pad_start, pad_len, nused1, u2, p_rows=p_rows, blk=blk)
    ys = _experts(blk_expert, nused1, xs, w1, w3, w2, blk=blk)
    return _combine(dst_flat, x1, meta, ys)


def kernel(x, norm1_g, w_in, conv_w, q_norm_g, k_norm_g, w_conv_out, w_attn_out, w_o,
           norm2_g, w_group, w_router, w1, w3, w2):
    batch, seq, d_model = x.shape
    h2 = x.reshape(batch * seq, d_model)
    for l in range(norm1_g.shape[0]):
        h2 = _layer(h2, batch=batch, seq=seq, norm1_g=norm1_g[l], w_in=w_in[l],
                    conv_w=conv_w[l], q_norm_g=q_norm_g[l], k_norm_g=k_norm_g[l],
                    w_conv_out=w_conv_out[l], w_attn_out=w_attn_out[l], w_o=w_o[l],
                    norm2_g=norm2_g[l], w_group=w_group[l], w_router=w_router[l],
                    w1=w1[l], w3=w3[l], w2=w2[l])
    return h2.reshape(batch, seq, d_model)
```

```python
import functools

import jax
import jax.numpy as jnp
from jax import lax
from jax.experimental import pallas as pl
from jax.experimental.pallas import tpu as pltpu

F32 = jnp.float32
BF16 = jnp.bfloat16

GRID_W = 64
EPS = 1e-6
N_HEADS = 8
N_KV_HEADS = 2
HEAD_DIM = 128
ROPE_THETA = 10000.0
N_GROUPS = 8
EXPERTS_PER_GROUP = 8
N_EXPERTS = N_GROUPS * EXPERTS_PER_GROUP
TOP_K = 2
LOG2_E = 1.4426950408889634

LANES = 128
V7X_VMEM_LIMIT_BYTES = 56000 * 1024

TM_PROJ = 512
TQ_ATTN = 256
TK_ATTN = 512
TB_RANK = 1024
TB_ROWS = 256
MOE_ROWS = 256
BF16_SUBLANES = 16
F32_SUBLANES = 8


def _cparams(sem):
    return pltpu.CompilerParams(dimension_semantics=sem,
                                vmem_limit_bytes=V7X_VMEM_LIMIT_BYTES)


ROW_TILE = 8


def _row_tiles(shape2d):
    rows, width = shape2d
    assert width == ROW_TILE * LANES
    return (rows * ROW_TILE, LANES)


def _rows(ref, r, n=1):
    return ref.at[pl.ds(pl.multiple_of(r * ROW_TILE, ROW_TILE), n * ROW_TILE)]


def _load_row_tiles(ref):
    rows = ref.shape[0] // ROW_TILE
    return jnp.concatenate(
        [ref[pl.ds(s, rows, stride=ROW_TILE), :] for s in range(ROW_TILE)], axis=1)


def _store_row_tiles(ref, value):
    rows = ref.shape[0] // ROW_TILE
    for s in range(ROW_TILE):
        ref[pl.ds(s, rows, stride=ROW_TILE), :] = value[:, s * LANES:(s + 1) * LANES]


def _resident(shape):
    nd = len(shape)
    return pl.BlockSpec(shape, lambda *_: (0,) * nd, pipeline_mode=pl.Buffered(1))


def _head_norm_rope(xh, g, c, sa, sb):
    ms = jnp.mean(xh * xh, axis=-1, keepdims=True)
    y = xh * lax.rsqrt(ms + EPS) * g
    y_next = pltpu.roll(y, HEAD_DIM - 1, axis=1)
    y_prev = pltpu.roll(y, 1, axis=1)
    return y * c + y_next * sa + y_prev * sb


def _sigmoid(x):
    return 1.0 / (1.0 + jnp.exp(-x))


def _inproj_kernel(x_ref, g1_ref, w_ref, gq_ref, gk_ref,
                   cq_ref, saq_ref, sbq_ref, ck_ref, sak_ref, sbk_ref,
                   cb_ref, z_ref, q_ref, k_ref, v_ref, sgc_ref, sga_ref,
                   *, d_conv, d_q, d_kv, d_model):
    x = x_ref[...]
    ms = jnp.mean(x * x, axis=-1, keepdims=True)
    u = (x * lax.rsqrt(ms + EPS) * g1_ref[...]).astype(BF16)

    def proj(lo, width):
        return jnp.dot(u, w_ref[:, lo:lo + width], preferred_element_type=F32)

    off = 0
    cb_ref[...] = proj(off, d_conv).astype(BF16)
    off += d_conv
    cc = proj(off, d_conv)
    off += d_conv
    cx = proj(off, d_conv)
    off += d_conv
    z_ref[...] = (cc * cx).astype(BF16)

    q = proj(off, d_q)
    off += d_q
    gq = gq_ref[...]
    cq, saq, sbq = cq_ref[...], saq_ref[...], sbq_ref[...]
    for h in range(d_q // HEAD_DIM):
        sl = slice(h * HEAD_DIM, (h + 1) * HEAD_DIM)
        q_ref[:, sl] = _head_norm_rope(q[:, sl], gq, cq, saq, sbq).astype(BF16)

    k = proj(off, d_kv)
    off += d_kv
    gk = gk_ref[...]
    ck, sak, sbk = ck_ref[...], sak_ref[...], sbk_ref[...]
    for h in range(d_kv // HEAD_DIM):
        sl = slice(h * HEAD_DIM, (h + 1) * HEAD_DIM)
        k_ref[:, sl] = _head_norm_rope(k[:, sl], gk, ck, sak, sbk).astype(BF16)

    v_ref[...] = proj(off, d_kv).astype(BF16)
    off += d_kv
    sgc_ref[...] = _sigmoid(proj(off, d_model)).astype(BF16)
    off += d_model
    sga_ref[...] = _sigmoid(proj(off, d_model)).astype(BF16)


def _rope_tables(seq):
    rows = seq // GRID_W
    axis_dim = HEAD_DIM // 2
    row = jnp.repeat(jnp.arange(rows, dtype=F32), GRID_W)
    col = jnp.tile(jnp.arange(GRID_W, dtype=F32), rows)
    inv = ROPE_THETA ** (-jnp.arange(0, axis_dim, 2, dtype=F32) / axis_dim)
    ang = jnp.concatenate([row[:, None] * inv, col[:, None] * inv], axis=-1)
    cos, sin = jnp.cos(ang), jnp.sin(ang)
    zero = jnp.zeros_like(sin)
    c = jnp.repeat(cos, 2, axis=-1)
    sa = jnp.stack([-sin, zero], axis=-1).reshape(seq, HEAD_DIM)
    sb = jnp.stack([zero, sin], axis=-1).reshape(seq, HEAD_DIM)
    return c, sa, sb


def _inproj(x2, g1, w_in_bf, gq, gk, tables_q, tables_k, *, seq):
    t, d_model = x2.shape
    d_q = N_HEADS * HEAD_DIM
    d_kv = N_KV_HEADS * HEAD_DIM
    d_in = w_in_bf.shape[1]
    d_conv = (d_in - d_q - 2 * d_kv - 2 * d_model) // 3
    tm = TM_PROJ
    nseq = seq // tm

    def row(width):
        return pl.BlockSpec((tm, width), lambda i: (i, 0))

    table = pl.BlockSpec((tm, HEAD_DIM), lambda i: (i % nseq, 0))
    kern = functools.partial(_inproj_kernel, d_conv=d_conv, d_q=d_q, d_kv=d_kv,
                             d_model=d_model)
    out_shape = [jax.ShapeDtypeStruct((t, w), BF16)
                 for w in (d_conv, d_conv, d_q, d_kv, d_kv, d_model, d_model)]
    return pl.pallas_call(
        kern,
        grid=(t // tm,),
        in_specs=[row(d_model), _resident((1, d_model)), _resident((d_model, d_in)),
                  _resident((1, HEAD_DIM)), _resident((1, HEAD_DIM)),
                  table, table, table, table, table, table],
        out_specs=[row(d_conv), row(d_conv), row(d_q), row(d_kv), row(d_kv),
                   row(d_model), row(d_model)],
        out_shape=out_shape,
        compiler_params=_cparams(("arbitrary",)),
        name="inproj",
    )(x2, g1, w_in_bf, gq, gk, *tables_q, *tables_k)


def _attn_kernel(q_ref, k_ref, v_ref, o_ref, qs_ref, vext_ref, m_ref, acc_ref,
                 *, tq, tk, nk, group):
    @pl.when(pl.program_id(2) == 0)
    def _():
        vext_ref[:, :HEAD_DIM] = v_ref[...]
        vext_ref[:, HEAD_DIM:] = jnp.ones((vext_ref.shape[0], HEAD_DIM), BF16)

    for g in range(group):
        qs_ref[g * tq:(g + 1) * tq, :] = q_ref[:, g * HEAD_DIM:(g + 1) * HEAD_DIM]
    m_ref[...] = jnp.full(m_ref.shape, -jnp.inf, F32)
    acc_ref[...] = jnp.zeros(acc_ref.shape, F32)

    for j in range(nk):
        kc = k_ref[j * tk:(j + 1) * tk, :]
        s = lax.dot_general(qs_ref[...], kc, (((1,), (1,)), ((), ())),
                            preferred_element_type=F32)
        m_prev = m_ref[...]
        m_new = jnp.maximum(m_prev, jnp.max(s, axis=-1, keepdims=True))
        alpha = jnp.exp2(m_prev - m_new)
        p = jnp.concatenate(
            [jnp.exp2(s[:, c * LANES:(c + 1) * LANES] - m_new) for c in range(tk // LANES)],
            axis=1).astype(BF16)
        pv = jnp.dot(p, vext_ref[j * tk:(j + 1) * tk, :], preferred_element_type=F32)
        acc_ref[...] = jnp.concatenate([alpha, alpha], axis=1) * acc_ref[...] + pv
        m_ref[...] = m_new

    out = acc_ref[:, :HEAD_DIM] / acc_ref[:, HEAD_DIM:]
    for g in range(group):
        o_ref[:, g * HEAD_DIM:(g + 1) * HEAD_DIM] = out[g * tq:(g + 1) * tq].astype(BF16)


def _attention(q, k, v, *, batch, seq):
    t = q.shape[0]
    group = N_HEADS // N_KV_HEADS
    tq, tk = TQ_ATTN, TK_ATTN
    nq = seq // tq
    gw = group * HEAD_DIM
    kern = functools.partial(_attn_kernel, tq=tq, tk=tk, nk=seq // tk, group=group)
    return pl.pallas_call(
        kern,
        grid=(batch, N_KV_HEADS, nq),
        in_specs=[pl.BlockSpec((tq, gw), lambda b, h, i: (b * nq + i, h)),
                  pl.BlockSpec((seq, HEAD_DIM), lambda b, h, i: (b, h)),
                  pl.BlockSpec((seq, HEAD_DIM), lambda b, h, i: (b, h))],
        out_specs=pl.BlockSpec((tq, gw), lambda b, h, i: (b * nq + i, h)),
        out_shape=jax.ShapeDtypeStruct((t, N_HEADS * HEAD_DIM), BF16),
        scratch_shapes=[pltpu.VMEM((group * tq, HEAD_DIM), BF16),
                        pltpu.VMEM((seq, 2 * HEAD_DIM), BF16),
                        pltpu.VMEM((group * tq, LANES), F32),
                        pltpu.VMEM((group * tq, 2 * HEAD_DIM), F32)],
        compiler_params=_cparams(("arbitrary", "arbitrary", "arbitrary")),
        name="attention",
    )(q, k, v)


def _post_kernel(x_ref, cb_ref, z_ref, zprev_ref, znext_ref, o_ref, sgc_ref, sga_ref,
                 cw_ref, wc_ref, wa_ref, wo_ref, g2_ref, wr_ref,
                 x1_ref, u2_ref, meta_ref, cnt_ref, *, tm, nseq):
    i = pl.program_id(0)
    at_start = (i % nseq) == 0
    at_end = (i % nseq) == nseq - 1

    z = z_ref[...].astype(F32)
    prev_row = zprev_ref[BF16_SUBLANES - 1:BF16_SUBLANES, :].astype(F32)
    next_row = znext_ref[0:1, :].astype(F32)
    prev_row = jnp.where(at_start, 0.0, prev_row)
    next_row = jnp.where(at_end, 0.0, next_row)
    rowid = lax.broadcasted_iota(jnp.int32, (tm, 1), 0)
    zp = jnp.where(rowid == 0, prev_row, pltpu.roll(z, 1, axis=0))
    zn = jnp.where(rowid == tm - 1, next_row, pltpu.roll(z, tm - 1, axis=0))
    cw = cw_ref[...]
    conv = cw[0:1, :] * zp + cw[1:2, :] * z + cw[2:3, :] * zn
    cbz = (cb_ref[...].astype(F32) * conv).astype(BF16)

    y_conv = jnp.dot(cbz, wc_ref[...], preferred_element_type=F32)
    y_attn = jnp.dot(o_ref[...], wa_ref[...], preferred_element_type=F32)
    merged = (sgc_ref[...].astype(F32) * y_conv
              + sga_ref[...].astype(F32) * y_attn).astype(BF16)
    x1 = x_ref[...] + jnp.dot(merged, wo_ref[...], preferred_element_type=F32)
    x1_ref[...] = x1

    ms = jnp.mean(x1 * x1, axis=-1, keepdims=True)
    u2 = x1 * lax.rsqrt(ms + EPS) * g2_ref[...]
    _store_row_tiles(u2_ref, u2)

    u2_hi = u2.astype(BF16)
    u2_lo = (u2 - u2_hi.astype(F32)).astype(BF16)
    hi_part = jnp.dot(u2_hi, wr_ref[...], preferred_element_type=F32)
    lo_part = jnp.dot(u2_lo, wr_ref[:, :LANES], preferred_element_type=F32)
    logits = hi_part[:, :LANES] + (hi_part[:, LANES:] + lo_part)
    lane = lax.broadcasted_iota(jnp.int32, (tm, LANES), 1).astype(F32)
    neg = -jnp.inf
    big = float(2 * LANES)
    is_group = lane < N_GROUPS
    gl = jnp.where(is_group, logits, neg)
    gmax = jnp.max(gl, axis=-1, keepdims=True)
    gidx = jnp.min(jnp.where(gl == gmax, lane, big), axis=-1, keepdims=True)
    gsum = jnp.sum(jnp.where(is_group, jnp.exp(logits - gmax), 0.0), axis=-1,
                   keepdims=True)
    pg = 1.0 / gsum
    lane_group = jnp.floor(lane * (1.0 / EXPERTS_PER_GROUP)) - 1.0
    mine = (lane_group == gidx) & (lane >= N_GROUPS) & (lane < N_GROUPS + N_EXPERTS)
    sel = jnp.where(mine, logits, neg)
    v1 = jnp.max(sel, axis=-1, keepdims=True)
    i1 = jnp.min(jnp.where(sel == v1, lane, big), axis=-1, keepdims=True)
    sel2 = jnp.where(lane == i1, neg, sel)
    v2 = jnp.max(sel2, axis=-1, keepdims=True)
    i2 = jnp.min(jnp.where(sel2 == v2, lane, big), axis=-1, keepdims=True)
    t2 = jnp.exp(v2 - v1)
    den = 1.0 + t2
    wgt1 = pg * (1.0 / den)
    wgt2 = pg * (t2 / den)
    meta = jnp.where(lane == 0, i1 - N_GROUPS,
                     jnp.where(lane == 1, i2 - N_GROUPS,
                               jnp.where(lane == 2, wgt1,
                                         jnp.where(lane == 3, wgt2, 0.0))))
    meta_ref[...] = meta

    picked = jnp.where((lane == i1 - N_GROUPS) | (lane == i2 - N_GROUPS), 1.0, 0.0)

    @pl.when(i == 0)
    def _():
        cnt_ref[...] = jnp.zeros(cnt_ref.shape, F32)

    cnt_ref[...] += jnp.sum(picked, axis=0, keepdims=True)


def _post(x2, cb, z, o, sgc, sga, conv_w, wc, wa, wo, g2, wr, *, seq):
    t, d_model = x2.shape
    tm = TM_PROJ
    nseq = seq // tm
    hb = tm // BF16_SUBLANES
    nhalo = t // BF16_SUBLANES
    d_conv = cb.shape[1]
    d_q = o.shape[1]

    def row(width):
        return pl.BlockSpec((tm, width), lambda i: (i, 0))

    kern = functools.partial(_post_kernel, tm=tm, nseq=nseq)
    return pl.pallas_call(
        kern,
        grid=(t // tm,),
        in_specs=[row(d_model), row(d_conv), row(d_conv),
                  pl.BlockSpec((BF16_SUBLANES, d_conv),
                               lambda i: (jnp.maximum(i * hb - 1, 0), 0)),
                  pl.BlockSpec((BF16_SUBLANES, d_conv),
                               lambda i: (jnp.minimum((i + 1) * hb, nhalo - 1), 0)),
                  row(d_q), row(d_model), row(d_model),
                  _resident(conv_w.shape), _resident(wc.shape), _resident(wa.shape),
                  _resident(wo.shape), _resident(g2.shape), _resident(wr.shape)],
        out_specs=[row(d_model),
                   pl.BlockSpec(_row_tiles((tm, d_model)), lambda i: (i, 0)),
                   row(LANES),
                   pl.BlockSpec((1, LANES), lambda i: (0, 0))],
        out_shape=[jax.ShapeDtypeStruct((t, d_model), F32),
                   jax.ShapeDtypeStruct(_row_tiles((t, d_model)), F32),
                   jax.ShapeDtypeStruct((t, LANES), F32),
                   jax.ShapeDtypeStruct((1, LANES), F32)],
        compiler_params=_cparams(("arbitrary",)),
        name="post",
    )(x2, cb, z, z, z, o, sgc, sga, conv_w, wc, wa, wo, g2, wr)


def _rank_kernel(meta_ref, cnt_ref, dst_ref, carry_ref, pstart_ref, *, tb, blk):
    i = pl.program_id(0)
    lane = lax.broadcasted_iota(jnp.int32, (tb, LANES), 1).astype(F32)
    meta = meta_ref[...]
    oh1 = jnp.where(lane == meta[:, 0:1], 1.0, 0.0)
    oh2 = jnp.where(lane == meta[:, 1:2], 1.0, 0.0)
    c = oh1 + oh2

    @pl.when(i == 0)
    def _():
        cnt = jnp.broadcast_to(cnt_ref[...], (8, LANES))
        nblk = jnp.floor((cnt + (blk - 1)) * (1.0 / blk))
        r = lax.broadcasted_iota(jnp.int32, (LANES, LANES), 0)
        col = lax.broadcasted_iota(jnp.int32, (LANES, LANES), 1)
        upper = jnp.where(r < col, 1.0, 0.0).astype(BF16)
        pblk = jnp.dot(nblk.astype(BF16), upper, preferred_element_type=F32)
        pstart_ref[...] = pblk[0:1, :] * blk
        carry_ref[...] = jnp.zeros(carry_ref.shape, F32)

    r = lax.broadcasted_iota(jnp.int32, (tb, tb), 0)
    col = lax.broadcasted_iota(jnp.int32, (tb, tb), 1)
    lower = jnp.where(col < r, 1.0, 0.0).astype(BF16)
    prefix = jnp.dot(lower, c.astype(BF16), preferred_element_type=F32)
    base = prefix + carry_ref[...] + pstart_ref[...]
    d1 = jnp.sum(oh1 * base, axis=-1, keepdims=True)
    d2 = jnp.sum(oh2 * base, axis=-1, keepdims=True)
    dst_ref[...] = jnp.where(lane == 0, d1, jnp.where(lane == 1, d2, 0.0))
    carry_ref[...] += jnp.sum(c, axis=0, keepdims=True)


def _rank(meta, cnt, *, blk):
    t = meta.shape[0]
    tb = TB_RANK
    kern = functools.partial(_rank_kernel, tb=tb, blk=blk)
    return pl.pallas_call(
        kern,
        grid=(t // tb,),
        in_specs=[pl.BlockSpec((tb, LANES), lambda i: (i, 0)),
                  pl.BlockSpec((1, LANES), lambda i: (0, 0))],
        out_specs=pl.BlockSpec((tb, LANES), lambda i: (i, 0)),
        out_shape=jax.ShapeDtypeStruct((t, LANES), F32),
        scratch_shapes=[pltpu.VMEM((1, LANES), F32), pltpu.VMEM((1, LANES), F32)],
        compiler_params=_cparams(("arbitrary",)),
        name="rank",
    )(meta, cnt)


def _dispatch_kernel(dst_sm, pad_start_sm, pad_len_sm, nused_sm, u2_ref, xs_ref,
                     zero_ref, sem, zsem, *, tb, blk, nblocks):
    step = pl.program_id(0)
    base = step * (tb * TOP_K)
    nbits = blk.bit_length() - 1

    def pad_copies(visit):
        def per_expert(e, carry):
            start, length = pad_start_sm[e], pad_len_sm[e]
            for bit in range(nbits):
                size = 1 << bit
                higher = lax.shift_left(lax.shift_right_logical(length, bit + 1), bit + 1)

                @pl.when((lax.shift_right_logical(length, bit) & 1) == 1)
                def _():
                    visit(pltpu.make_async_copy(
                        zero_ref.at[pl.ds(0, size * ROW_TILE)],
                        _rows(xs_ref, start + higher, size), zsem))
            return carry

        lax.fori_loop(0, N_EXPERTS, per_expert, 0)

        def per_block(b, carry):
            visit(pltpu.make_async_copy(zero_ref, _rows(xs_ref, b * blk, blk), zsem))
            return carry

        lax.fori_loop(nused_sm[0], nblocks, per_block, 0)

    @pl.when(step == 0)
    def _():
        zero_ref[...] = jnp.zeros(zero_ref.shape, F32)
        pad_copies(lambda c: c.start())

    def issue(r, carry):
        for kk in range(TOP_K):
            pltpu.make_async_copy(
                _rows(u2_ref, r), _rows(xs_ref, dst_sm[base + TOP_K * r + kk]), sem
            ).start(priority=kk)
        return carry

    lax.fori_loop(0, tb, issue, 0, unroll=8)
    for kk in range(TOP_K):
        pltpu.make_async_copy(u2_ref, _rows(xs_ref, 0, tb), sem).wait()

    @pl.when(step == pl.num_programs(0) - 1)
    def _():
        pad_copies(lambda c: c.wait())


def _dispatch(dst_flat, pad_start, pad_len, nused, u2, *, p_rows, blk):
    t = u2.shape[0] // ROW_TILE
    tb = TB_ROWS
    kern = functools.partial(_dispatch_kernel, tb=tb, blk=blk, nblocks=p_rows // blk)
    return pl.pallas_call(
        kern,
        grid_spec=pltpu.PrefetchScalarGridSpec(
            num_scalar_prefetch=4,
            grid=(t // tb,),
            in_specs=[pl.BlockSpec((tb * ROW_TILE, LANES), lambda i, *_: (i, 0))],
            out_specs=pl.BlockSpec(memory_space=pl.ANY),
            scratch_shapes=[pltpu.VMEM((blk * ROW_TILE, LANES), F32),
                            pltpu.SemaphoreType.DMA, pltpu.SemaphoreType.DMA]),
        out_shape=jax.ShapeDtypeStruct((p_rows * ROW_TILE, LANES), F32),
        compiler_params=_cparams(("arbitrary",)),
        name="dispatch",
    )(dst_flat, pad_start, pad_len, nused, u2)


def _experts_kernel(be_sm, first_sm, next_sm, slot_sm, nused_sm,
                    xs_ref, w1_hbm, w3_hbm, w2_hbm, ys_ref,
                    w1f_ref, w3f_ref, w2f_ref, w13b_ref, w2b_ref, wsem, *, d_expert):
    b = pl.program_id(0)
    used = b < nused_sm[0]

    def weight_copies(expert, slot):
        return [pltpu.make_async_copy(src.at[expert], dst.at[slot], wsem.at[slot])
                for src, dst in ((w1_hbm, w1f_ref), (w3_hbm, w3f_ref), (w2_hbm, w2f_ref))]

    @pl.when(b == 0)
    def _():
        for c in weight_copies(be_sm[0], slot_sm[0]):
            c.start()

    @pl.when(first_sm[b] == 1)
    def _():
        slot = slot_sm[b]
        for c in weight_copies(be_sm[b], slot):
            c.wait()

        @pl.when(next_sm[b] >= 0)
        def _():
            for c in weight_copies(next_sm[b], 1 - slot):
                c.start()

        w13b_ref[:, :d_expert] = w1f_ref[slot].astype(BF16)
        w13b_ref[:, d_expert:] = w3f_ref[slot].astype(BF16)
        w2b_ref[...] = w2f_ref[slot].astype(BF16)

    @pl.when(used)
    def _():
        xb = _load_row_tiles(xs_ref).astype(BF16)
        h = jnp.dot(xb, w13b_ref[...], preferred_element_type=F32)
        h1 = h[:, :d_expert]
        h3 = h[:, d_expert:]
        a = (h1 * _sigmoid(h1) * h3).astype(BF16)
        _store_row_tiles(ys_ref, jnp.dot(a, w2b_ref[...], preferred_element_type=F32))

    @pl.when(jnp.logical_not(used))
    def _():
        ys_ref[...] = jnp.zeros(ys_ref.shape, F32)


def _experts(blk_expert, blk_first, blk_next, blk_slot, nused, xs, w1, w3, w2, *, blk):
    p_rows = xs.shape[0] // ROW_TILE
    d_model, d_expert = w1.shape[-2:]
    kern = functools.partial(_experts_kernel, d_expert=d_expert)
    rows_blk = (blk * ROW_TILE, LANES)

    def rows_in(b, be, first, nxt, slot, nu):
        return (jnp.maximum(jnp.minimum(b, nu[0] - 1), 0), 0)

    hbm = pl.BlockSpec(memory_space=pl.ANY)
    return pl.pallas_call(
        kern,
        grid_spec=pltpu.PrefetchScalarGridSpec(
            num_scalar_prefetch=5,
            grid=(p_rows // blk,),
            in_specs=[pl.BlockSpec(rows_blk, rows_in), hbm, hbm, hbm],
            out_specs=pl.BlockSpec(rows_blk, lambda b, *_: (b, 0)),
            scratch_shapes=[pltpu.VMEM((2, d_model, d_expert), F32),
                            pltpu.VMEM((2, d_model, d_expert), F32),
                            pltpu.VMEM((2, d_expert, d_model), F32),
                            pltpu.VMEM((d_model, 2 * d_expert), BF16),
                            pltpu.VMEM((d_expert, d_model), BF16),
                            pltpu.SemaphoreType.DMA((2,))]),
        out_shape=jax.ShapeDtypeStruct((p_rows * ROW_TILE, LANES), F32),
        compiler_params=_cparams(("arbitrary",)),
        name="experts",
    )(blk_expert, blk_first, blk_next, blk_slot, nused, xs, w1, w3, w2)


def _combine_kernel(dst_sm, x1_ref, meta_ref, ys_ref, out_ref, buf0_ref, buf1_ref, sem,
                    *, tb):
    base = pl.program_id(0) * (tb * TOP_K)
    bufs = (buf0_ref, buf1_ref)

    def issue(r, carry):
        for kk in range(TOP_K):
            pltpu.make_async_copy(_rows(ys_ref, dst_sm[base + TOP_K * r + kk]),
                                  _rows(bufs[kk], r), sem).start(priority=kk)
        return carry

    lax.fori_loop(0, tb, issue, 0, unroll=8)
    for kk in range(TOP_K):
        pltpu.make_async_copy(_rows(ys_ref, 0, tb), bufs[kk], sem).wait()
    meta = meta_ref[...]
    moe = (_load_row_tiles(buf0_ref) * meta[:, 2:3]
           + _load_row_tiles(buf1_ref) * meta[:, 3:4])
    out_ref[...] = x1_ref[...] + moe


def _combine(dst_flat, x1, meta, ys):
    t, d_model = x1.shape
    tb = TB_ROWS
    kern = functools.partial(_combine_kernel, tb=tb)
    return pl.pallas_call(
        kern,
        grid_spec=pltpu.PrefetchScalarGridSpec(
            num_scalar_prefetch=1,
            grid=(t // tb,),
            in_specs=[pl.BlockSpec((tb, d_model), lambda i, dst: (i, 0)),
                      pl.BlockSpec((tb, LANES), lambda i, dst: (i, 0)),
                      pl.BlockSpec(memory_space=pl.ANY)],
            out_specs=pl.BlockSpec((tb, d_model), lambda i, dst: (i, 0)),
            scratch_shapes=[pltpu.VMEM(_row_tiles((tb, d_model)), F32),
                            pltpu.VMEM(_row_tiles((tb, d_model)), F32),
                            pltpu.SemaphoreType.DMA]),
        out_shape=jax.ShapeDtypeStruct((t, d_model), F32),
        compiler_params=_cparams(("arbitrary",)),
        name="combine",
    )(dst_flat, x1, meta, ys)


def _layer(h2, *, batch, seq, norm1_g, w_in, conv_w, q_norm_g, k_norm_g,
           w_conv_out, w_attn_out, w_o, norm2_g, w_group, w_router, w1, w3, w2):
    t, d_model = h2.shape
    c, sa, sb = _rope_tables(seq)
    scale = HEAD_DIM ** -0.5 * LOG2_E
    tables_q = (c * scale, sa * scale, sb * scale)
    tables_k = (c, sa, sb)

    cb, z, q, k, v, sgc, sga = _inproj(
        h2, norm1_g[None, :], w_in.astype(BF16), q_norm_g[None, :], k_norm_g[None, :],
        tables_q, tables_k, seq=seq)
    o = _attention(q, k, v, batch=batch, seq=seq)

    n_route = N_GROUPS + N_EXPERTS
    wr = jnp.concatenate(
        [w_group, w_router, jnp.zeros((d_model, LANES - n_route), F32)], axis=1)
    wr_hi = wr.astype(BF16)
    wr = jnp.concatenate([wr_hi, (wr - wr_hi.astype(F32)).astype(BF16)], axis=1)
    x1, u2, meta, cnt = _post(h2, cb, z, o, sgc, sga, conv_w,
                              w_conv_out.astype(BF16), w_attn_out.astype(BF16),
                              w_o.astype(BF16), norm2_g[None, :], wr, seq=seq)

    blk = MOE_ROWS
    dst = _rank(meta, cnt, blk=blk)
    dst_flat = dst[:, :TOP_K].astype(jnp.int32).reshape(-1)
    counts = cnt[0, :N_EXPERTS].astype(jnp.int32)
    p_rows = t * TOP_K + N_EXPERTS * blk
    nb = p_rows // blk
    nblk = (counts + blk - 1) // blk
    pend = jnp.cumsum(nblk)
    nused = pend[-1]
    barange = jnp.arange(nb, dtype=jnp.int32)
    bidx = jnp.minimum(barange, nused - 1)
    blk_expert = jnp.minimum(jnp.searchsorted(pend, bidx, side='right'),
                             N_EXPERTS - 1).astype(jnp.int32)
    live = barange < nused
    blk_first = (live & (barange == (pend - nblk)[blk_expert])).astype(jnp.int32)
    after = pend[blk_expert]
    blk_next = jnp.where(live & (after < nused),
                         blk_expert[jnp.minimum(after, nb - 1)], -1).astype(jnp.int32)
    blk_slot = ((jnp.cumsum(nblk > 0) - 1)[blk_expert] % 2).astype(jnp.int32)
    pad_start = ((pend - nblk) * blk + counts).astype(jnp.int32)
    pad_len = (nblk * blk - counts).astype(jnp.int32)
    nused1 = nused.reshape(1).astype(jnp.int32)

    xs = _dispatch(dst_flat, pad_start, pad_len, nused1, u2, p_rows=p_rows, blk=blk)
    ys = _experts(blk_expert, blk_first, blk_next, blk_slot, nused1, xs, w1, w3, w2,
                  blk=blk)
    return _combine(dst_flat, x1, meta, ys)


def kernel(x, norm1_g, w_in, conv_w, q_norm_g, k_norm_g, w_conv_out, w_attn_out, w_o,
           norm2_g, w_group, w_router, w1, w3, w2):
    batch, seq, d_model = x.shape
    h2 = x.reshape(batch * seq, d_model)
    for l in range(norm1_g.shape[0]):
        h2 = _layer(h2, batch=batch, seq=seq, norm1_g=norm1_g[l], w_in=w_in[l],
                    conv_w=conv_w[l], q_norm_g=q_norm_g[l], k_norm_g=k_norm_g[l],
                    w_conv_out=w_conv_out[l], w_attn_out=w_attn_out[l], w_o=w_o[l],
                    norm2_g=norm2_g[l], w_group=w_group[l], w_router=w_router[l],
                    w1=w1[l], w3=w3[l], w2=w2[l])
    return h2.reshape(batch, seq, d_model)
```

```python
import functools

import jax
import jax.numpy as jnp
import numpy as np
from jax import lax
from jax.experimental import pallas as pl
from jax.experimental.pallas import tpu as pltpu

F32 = jnp.float32
BF16 = jnp.bfloat16

GRID_W = 64
EPS = 1e-6
N_HEADS = 8
N_KV_HEADS = 2
HEAD_DIM = 128
ROPE_THETA = 10000.0
N_GROUPS = 8
EXPERTS_PER_GROUP = 8
N_EXPERTS = N_GROUPS * EXPERTS_PER_GROUP
TOP_K = 2
LOG2_E = 1.4426950408889634

LANES = 128
MXU_DIM = 256
V7X_VMEM_LIMIT_BYTES = 56000 * 1024

TM_PROJ = 512
POST_SPLIT = 1
TQ_ATTN = 512
TK_ATTN = 512
TB_RANK = 1024
TB_ROWS = 256
MOE_ROWS = 256
BF16_SUBLANES = 16
F32_SUBLANES = 8


def _cparams(sem):
    return pltpu.CompilerParams(dimension_semantics=sem,
                                vmem_limit_bytes=V7X_VMEM_LIMIT_BYTES)


ROW_TILE = 8


def _row_tiles(shape2d):
    rows, width = shape2d
    assert width == ROW_TILE * LANES
    return (rows * ROW_TILE, LANES)


def _rows(ref, r, n=1):
    return ref.at[pl.ds(pl.multiple_of(r * ROW_TILE, ROW_TILE), n * ROW_TILE)]


def _load_row_tiles(ref):
    rows = ref.shape[0] // ROW_TILE
    return jnp.concatenate(
        [ref[pl.ds(s, rows, stride=ROW_TILE), :] for s in range(ROW_TILE)], axis=1)


def _store_row_tiles(ref, value, row0=0):
    rows = value.shape[0]
    for s in range(ROW_TILE):
        ref[pl.ds(row0 * ROW_TILE + s, rows, stride=ROW_TILE), :] = (
            value[:, s * LANES:(s + 1) * LANES])


def _resident(shape):
    nd = len(shape)
    return pl.BlockSpec(shape, lambda *_: (0,) * nd, pipeline_mode=pl.Buffered(1))


def _head_norm_rope(xh, g, c, sa, sb):
    ms = jnp.mean(xh * xh, axis=-1, keepdims=True)
    y = xh * lax.rsqrt(ms + EPS) * g
    y_next = pltpu.roll(y, HEAD_DIM - 1, axis=1)
    y_prev = pltpu.roll(y, 1, axis=1)
    return y * c + y_next * sa + y_prev * sb


def _sigmoid(x):
    return 0.5 * jnp.tanh(0.5 * x) + 0.5


def _inproj_kernel(x_ref, g1_ref, w_ref, gq_ref, gk_ref,
                   cq_ref, saq_ref, sbq_ref, ck_ref, sak_ref, sbk_ref,
                   cb_ref, z_ref, q_ref, k_ref, v_ref, sgc_ref, sga_ref,
                   *, d_conv, d_q, d_kv, d_model):
    x = x_ref[...]
    ms = jnp.mean(x * x, axis=-1, keepdims=True)
    u = (x * lax.rsqrt(ms + EPS) * g1_ref[...]).astype(BF16)

    def proj(lo, width):
        return jnp.dot(u, w_ref[:, lo:lo + width], preferred_element_type=F32)

    o_cb, o_cc, o_cx = 0, d_conv, 2 * d_conv
    o_q = 3 * d_conv
    o_k = o_q + d_q
    o_v = o_k + d_kv
    o_gc = o_v + d_kv
    o_ga = o_gc + d_model

    sgc_ref[...] = _sigmoid(proj(o_gc, d_model)).astype(BF16)
    sga_ref[...] = _sigmoid(proj(o_ga, d_model)).astype(BF16)

    q = proj(o_q, d_q)
    gq = gq_ref[...]
    cq, saq, sbq = cq_ref[...], saq_ref[...], sbq_ref[...]
    for h in range(d_q // HEAD_DIM):
        sl = slice(h * HEAD_DIM, (h + 1) * HEAD_DIM)
        q_ref[:, sl] = _head_norm_rope(q[:, sl], gq, cq, saq, sbq).astype(BF16)

    k = proj(o_k, d_kv)
    gk = gk_ref[...]
    ck, sak, sbk = ck_ref[...], sak_ref[...], sbk_ref[...]
    for h in range(d_kv // HEAD_DIM):
        sl = slice(h * HEAD_DIM, (h + 1) * HEAD_DIM)
        k_ref[:, sl] = _head_norm_rope(k[:, sl], gk, ck, sak, sbk).astype(BF16)

    z_ref[...] = (proj(o_cc, d_conv) * proj(o_cx, d_conv)).astype(BF16)
    v_ref[...] = proj(o_v, d_kv).astype(BF16)
    cb_ref[...] = proj(o_cb, d_conv).astype(BF16)


def _rope_tables(seq):
    rows = seq // GRID_W
    axis_dim = HEAD_DIM // 2
    row = np.repeat(np.arange(rows, dtype=np.float32), GRID_W)
    col = np.tile(np.arange(GRID_W, dtype=np.float32), rows)
    inv = (np.float32(ROPE_THETA)
           ** (-np.arange(0, axis_dim, 2, dtype=np.float32) / np.float32(axis_dim)))
    ang = np.concatenate([row[:, None] * inv, col[:, None] * inv], axis=-1)
    ang = ang.astype(np.float32)
    cos, sin = np.cos(ang), np.sin(ang)
    zero = np.zeros_like(sin)
    c = np.repeat(cos, 2, axis=-1)
    sa = np.stack([-sin, zero], axis=-1).reshape(seq, HEAD_DIM)
    sb = np.stack([zero, sin], axis=-1).reshape(seq, HEAD_DIM)
    return c, sa, sb


def _inproj(x2, g1, w_in_bf, gq, gk, tables_q, tables_k, *, seq):
    t, d_model = x2.shape
    d_q = N_HEADS * HEAD_DIM
    d_kv = N_KV_HEADS * HEAD_DIM
    d_in = w_in_bf.shape[1]
    d_conv = (d_in - d_q - 2 * d_kv - 2 * d_model) // 3
    tm = TM_PROJ
    nseq = seq // tm

    def row(width):
        return pl.BlockSpec((tm, width), lambda i: (i, 0))

    table = pl.BlockSpec((tm, HEAD_DIM), lambda i: (i % nseq, 0))
    kern = functools.partial(_inproj_kernel, d_conv=d_conv, d_q=d_q, d_kv=d_kv,
                             d_model=d_model)
    out_shape = [jax.ShapeDtypeStruct((t, w), BF16)
                 for w in (d_conv, d_conv, d_q, d_kv, d_kv, d_model, d_model)]
    return pl.pallas_call(
        kern,
        grid=(t // tm,),
        in_specs=[row(d_model), _resident((1, d_model)), _resident((d_model, d_in)),
                  _resident((1, HEAD_DIM)), _resident((1, HEAD_DIM)),
                  table, table, table, table, table, table],
        out_specs=[row(d_conv), row(d_conv), row(d_q), row(d_kv), row(d_kv),
                   row(d_model), row(d_model)],
        out_shape=out_shape,
        compiler_params=_cparams(("arbitrary",)),
        name="inproj",
    )(x2, g1, w_in_bf, gq, gk, *tables_q, *tables_k)


def _attn_kernel(q_ref, k_ref, v_ref, o_ref, qs_ref, vext_ref, m_ref, acc_ref,
                 *, tq, tk, nk, group):
    @pl.when(pl.program_id(2) == 0)
    def _():
        vext_ref[:, :HEAD_DIM] = v_ref[...]
        vext_ref[:, HEAD_DIM:] = jnp.ones((vext_ref.shape[0], HEAD_DIM), BF16)

    for g in range(group):
        qs_ref[g * tq:(g + 1) * tq, :] = q_ref[:, g * HEAD_DIM:(g + 1) * HEAD_DIM]
    m_ref[...] = jnp.full(m_ref.shape, -jnp.inf, F32)
    acc_ref[...] = jnp.zeros(acc_ref.shape, F32)

    for j in range(nk):
        kc = k_ref[j * tk:(j + 1) * tk, :]
        s = lax.dot_general(qs_ref[...], kc, (((1,), (1,)), ((), ())),
                            preferred_element_type=F32)
        m_prev = m_ref[...]
        m_new = jnp.maximum(m_prev, jnp.max(s, axis=-1, keepdims=True))
        alpha = jnp.exp2(m_prev - m_new)
        p = jnp.concatenate(
            [jnp.exp2(s[:, c * LANES:(c + 1) * LANES] - m_new) for c in range(tk // LANES)],
            axis=1).astype(BF16)
        pv = jnp.dot(p, vext_ref[j * tk:(j + 1) * tk, :], preferred_element_type=F32)
        acc_ref[...] = jnp.concatenate([alpha, alpha], axis=1) * acc_ref[...] + pv
        m_ref[...] = m_new

    out = acc_ref[:, :HEAD_DIM] / acc_ref[:, HEAD_DIM:]
    for g in range(group):
        o_ref[:, g * HEAD_DIM:(g + 1) * HEAD_DIM] = out[g * tq:(g + 1) * tq].astype(BF16)


def _attention(q, k, v, *, batch, seq):
    t = q.shape[0]
    group = N_HEADS // N_KV_HEADS
    tq, tk = TQ_ATTN, TK_ATTN
    nq = seq // tq
    gw = group * HEAD_DIM
    kern = functools.partial(_attn_kernel, tq=tq, tk=tk, nk=seq // tk, group=group)
    return pl.pallas_call(
        kern,
        grid=(batch, N_KV_HEADS, nq),
        in_specs=[pl.BlockSpec((tq, gw), lambda b, h, i: (b * nq + i, h)),
                  pl.BlockSpec((seq, HEAD_DIM), lambda b, h, i: (b, h)),
                  pl.BlockSpec((seq, HEAD_DIM), lambda b, h, i: (b, h))],
        out_specs=pl.BlockSpec((tq, gw), lambda b, h, i: (b * nq + i, h)),
        out_shape=jax.ShapeDtypeStruct((t, N_HEADS * HEAD_DIM), BF16),
        scratch_shapes=[pltpu.VMEM((group * tq, HEAD_DIM), BF16),
                        pltpu.VMEM((seq, 2 * HEAD_DIM), BF16),
                        pltpu.VMEM((group * tq, LANES), F32),
                        pltpu.VMEM((group * tq, 2 * HEAD_DIM), F32)],
        compiler_params=_cparams(("arbitrary", "arbitrary", "arbitrary")),
        name="attention",
    )(q, k, v)


def _route(logits):
    rows = logits.shape[0]
    lane = lax.broadcasted_iota(jnp.int32, (rows, LANES), 1).astype(F32)
    neg = -jnp.inf
    big = float(2 * LANES)
    is_group = lane < N_GROUPS
    gl = jnp.where(is_group, logits, neg)
    gmax = jnp.max(gl, axis=-1, keepdims=True)
    gidx = jnp.min(jnp.where(gl == gmax, lane, big), axis=-1, keepdims=True)
    gsum = jnp.sum(jnp.where(is_group, jnp.exp(logits - gmax), 0.0), axis=-1,
                   keepdims=True)
    pg = 1.0 / gsum
    lane_group = jnp.floor(lane * (1.0 / EXPERTS_PER_GROUP)) - 1.0
    mine = (lane_group == gidx) & (lane >= N_GROUPS) & (lane < N_GROUPS + N_EXPERTS)
    sel = jnp.where(mine, logits, neg)
    v1 = jnp.max(sel, axis=-1, keepdims=True)
    i1 = jnp.min(jnp.where(sel == v1, lane, big), axis=-1, keepdims=True)
    sel2 = jnp.where(lane == i1, neg, sel)
    v2 = jnp.max(sel2, axis=-1, keepdims=True)
    i2 = jnp.min(jnp.where(sel2 == v2, lane, big), axis=-1, keepdims=True)
    t2 = jnp.exp(v2 - v1)
    den = 1.0 + t2
    wgt1 = pg * (1.0 / den)
    wgt2 = pg * (t2 / den)
    e1 = i1 - N_GROUPS
    e2 = i2 - N_GROUPS
    meta = jnp.where(lane == 0, e1,
                     jnp.where(lane == 1, e2,
                               jnp.where(lane == 2, wgt1,
                                         jnp.where(lane == 3, wgt2, 0.0))))
    picked = jnp.where((lane == e1) | (lane == e2), 1.0, 0.0)
    return meta, jnp.sum(picked, axis=0, keepdims=True)


def _post_kernel(x_ref, cb_ref, z_ref, zprev_ref, znext_ref, o_ref, sgc_ref, sga_ref,
                 cw_ref, wc_ref, wa_ref, wo_ref, g2_ref, wr_ref,
                 x1_ref, u2_ref, meta_ref, cnt_ref, *, tm, nseq, nsplit):
    i = pl.program_id(0)
    at_start = (i % nseq) == 0
    at_end = (i % nseq) == nseq - 1
    rows = tm // nsplit
    rowid = lax.broadcasted_iota(jnp.int32, (rows, 1), 0)

    @pl.when(i == 0)
    def _():
        cnt_ref[...] = jnp.zeros(cnt_ref.shape, F32)

    for part_id in range(nsplit):
        r0 = part_id * rows
        rs = slice(r0, r0 + rows)
        y_attn = jnp.dot(o_ref[rs, :], wa_ref[...], preferred_element_type=F32)

        y_conv = None
        for c in range(z_ref.shape[1] // MXU_DIM):
            sl = slice(c * MXU_DIM, (c + 1) * MXU_DIM)
            z = z_ref[rs, sl].astype(F32)
            if part_id == 0:
                prev_row = zprev_ref[BF16_SUBLANES - 1:BF16_SUBLANES, sl].astype(F32)
                prev_row = jnp.where(at_start, 0.0, prev_row)
            else:
                prev_row = z_ref[r0 - 1:r0, sl].astype(F32)
            if part_id == nsplit - 1:
                next_row = znext_ref[0:1, sl].astype(F32)
                next_row = jnp.where(at_end, 0.0, next_row)
            else:
                next_row = z_ref[r0 + rows:r0 + rows + 1, sl].astype(F32)
            zp = jnp.where(rowid == 0, prev_row, pltpu.roll(z, 1, axis=0))
            zn = jnp.where(rowid == rows - 1, next_row, pltpu.roll(z, rows - 1, axis=0))
            conv = cw_ref[0:1, sl] * zp + cw_ref[1:2, sl] * z + cw_ref[2:3, sl] * zn
            cbz = (cb_ref[rs, sl].astype(F32) * conv).astype(BF16)
            part = jnp.dot(cbz, wc_ref[sl, :], preferred_element_type=F32)
            y_conv = part if y_conv is None else y_conv + part
        merged = (sgc_ref[rs, :].astype(F32) * y_conv
                  + sga_ref[rs, :].astype(F32) * y_attn).astype(BF16)
        x1 = x_ref[rs, :] + jnp.dot(merged, wo_ref[...], preferred_element_type=F32)
        x1_ref[rs, :] = x1

        ms = jnp.mean(x1 * x1, axis=-1, keepdims=True)
        u2 = x1 * lax.rsqrt(ms + EPS) * g2_ref[...]
        _store_row_tiles(u2_ref, u2, r0)

        u2_hi = u2.astype(BF16)
        u2_lo = (u2 - u2_hi.astype(F32)).astype(BF16)
        hi_part = jnp.dot(u2_hi, wr_ref[...], preferred_element_type=F32)
        lo_part = jnp.dot(u2_lo, wr_ref[:, :LANES], preferred_element_type=F32)
        logits = hi_part[:, :LANES] + (hi_part[:, LANES:] + lo_part)
        meta, picked = _route(logits)
        meta_ref[rs, :] = meta
        cnt_ref[...] += picked


def _post(x2, cb, z, o, sgc, sga, conv_w, wc, wa, wo, g2, wr, *, seq):
    t, d_model = x2.shape
    tm = TM_PROJ
    nseq = seq // tm
    hb = tm // BF16_SUBLANES
    nhalo = t // BF16_SUBLANES
    d_conv = cb.shape[1]
    d_q = o.shape[1]

    def row(width):
        return pl.BlockSpec((tm, width), lambda i: (i, 0))

    kern = functools.partial(_post_kernel, tm=tm, nseq=nseq, nsplit=POST_SPLIT)
    return pl.pallas_call(
        kern,
        grid=(t // tm,),
        in_specs=[row(d_model), row(d_conv), row(d_conv),
                  pl.BlockSpec((BF16_SUBLANES, d_conv),
                               lambda i: (jnp.maximum(i * hb - 1, 0), 0)),
                  pl.BlockSpec((BF16_SUBLANES, d_conv),
                               lambda i: (jnp.minimum((i + 1) * hb, nhalo - 1), 0)),
                  row(d_q), row(d_model), row(d_model),
                  _resident(conv_w.shape), _resident(wc.shape), _resident(wa.shape),
                  _resident(wo.shape), _resident(g2.shape), _resident(wr.shape)],
        out_specs=[row(d_model),
                   pl.BlockSpec(_row_tiles((tm, d_model)), lambda i: (i, 0)),
                   row(LANES),
                   pl.BlockSpec((1, LANES), lambda i: (0, 0))],
        out_shape=[jax.ShapeDtypeStruct((t, d_model), F32),
                   jax.ShapeDtypeStruct(_row_tiles((t, d_model)), F32),
                   jax.ShapeDtypeStruct((t, LANES), F32),
                   jax.ShapeDtypeStruct((1, LANES), F32)],
        compiler_params=_cparams(("arbitrary",)),
        name="post",
    )(x2, cb, z, z, z, o, sgc, sga, conv_w, wc, wa, wo, g2, wr)


def _rank_kernel(meta_ref, cnt_ref, dst_ref, carry_ref, pstart_ref, *, tb, blk):
    i = pl.program_id(0)
    lane = lax.broadcasted_iota(jnp.int32, (tb, LANES), 1).astype(F32)
    meta = meta_ref[...]
    oh1 = jnp.where(lane == meta[:, 0:1], 1.0, 0.0)
    oh2 = jnp.where(lane == meta[:, 1:2], 1.0, 0.0)
    c = oh1 + oh2

    @pl.when(i == 0)
    def _():
        cnt = jnp.broadcast_to(cnt_ref[...], (8, LANES))
        nblk = jnp.floor((cnt + (blk - 1)) * (1.0 / blk))
        r = lax.broadcasted_iota(jnp.int32, (LANES, LANES), 0)
        col = lax.broadcasted_iota(jnp.int32, (LANES, LANES), 1)
        upper = jnp.where(r < col, 1.0, 0.0).astype(BF16)
        pblk = jnp.dot(nblk.astype(BF16), upper, preferred_element_type=F32)
        pstart_ref[...] = pblk[0:1, :] * blk
        carry_ref[...] = jnp.zeros(carry_ref.shape, F32)

    r = lax.broadcasted_iota(jnp.int32, (tb, tb), 0)
    col = lax.broadcasted_iota(jnp.int32, (tb, tb), 1)
    lower = jnp.where(col < r, 1.0, 0.0).astype(BF16)
    prefix = jnp.dot(lower, c.astype(BF16), preferred_element_type=F32)
    base = prefix + carry_ref[...] + pstart_ref[...]
    d1 = jnp.sum(oh1 * base, axis=-1, keepdims=True)
    d2 = jnp.sum(oh2 * base, axis=-1, keepdims=True)
    dst_ref[...] = jnp.where(lane == 0, d1, jnp.where(lane == 1, d2, 0.0))
    carry_ref[...] += jnp.sum(c, axis=0, keepdims=True)


def _rank(meta, cnt, *, blk):
    t = meta.shape[0]
    tb = TB_RANK
    kern = functools.partial(_rank_kernel, tb=tb, blk=blk)
    return pl.pallas_call(
        kern,
        grid=(t // tb,),
        in_specs=[pl.BlockSpec((tb, LANES), lambda i: (i, 0)),
                  pl.BlockSpec((1, LANES), lambda i: (0, 0))],
        out_specs=pl.BlockSpec((tb, LANES), lambda i: (i, 0)),
        out_shape=jax.ShapeDtypeStruct((t, LANES), F32),
        scratch_shapes=[pltpu.VMEM((1, LANES), F32), pltpu.VMEM((1, LANES), F32)],
        compiler_params=_cparams(("arbitrary",)),
        name="rank",
    )(meta, cnt)


def _dispatch_kernel(dst_sm, pad_start_sm, pad_len_sm, nused_sm, u2_ref, xs_ref,
                     zero_ref, sem, zsem, *, tb, blk, nblocks):
    step = pl.program_id(0)
    base = step * (tb * TOP_K)
    nbits = blk.bit_length() - 1

    def pad_copies(visit):
        def per_expert(e, carry):
            start, length = pad_start_sm[e], pad_len_sm[e]
            for bit in range(nbits):
                size = 1 << bit
                higher = lax.shift_left(lax.shift_right_logical(length, bit + 1), bit + 1)

                @pl.when((lax.shift_right_logical(length, bit) & 1) == 1)
                def _():
                    visit(pltpu.make_async_copy(
                        zero_ref.at[pl.ds(0, size * ROW_TILE)],
                        _rows(xs_ref, start + higher, size), zsem))
            return carry

        lax.fori_loop(0, N_EXPERTS, per_expert, 0)

        def per_block(b, carry):
            visit(pltpu.make_async_copy(zero_ref, _rows(xs_ref, b * blk, blk), zsem))
            return carry

        lax.fori_loop(nused_sm[0], nblocks, per_block, 0)

    @pl.when(step == 0)
    def _():
        zero_ref[...] = jnp.zeros(zero_ref.shape, F32)
        pad_copies(lambda c: c.start())

    def issue(r, carry):
        for kk in range(TOP_K):
            pltpu.make_async_copy(
                _rows(u2_ref, r), _rows(xs_ref, dst_sm[base + TOP_K * r + kk]), sem
            ).start(priority=kk)
        return carry

    lax.fori_loop(0, tb, issue, 0, unroll=8)
    for kk in range(TOP_K):
        pltpu.make_async_copy(u2_ref, _rows(xs_ref, 0, tb), sem).wait()

    @pl.when(step == pl.num_programs(0) - 1)
    def _():
        pad_copies(lambda c: c.wait())


def _dispatch(dst_flat, pad_start, pad_len, nused, u2, *, p_rows, blk):
    t = u2.shape[0] // ROW_TILE
    tb = TB_ROWS
    kern = functools.partial(_dispatch_kernel, tb=tb, blk=blk, nblocks=p_rows // blk)
    return pl.pallas_call(
        kern,
        grid_spec=pltpu.PrefetchScalarGridSpec(
            num_scalar_prefetch=4,
            grid=(t // tb,),
            in_specs=[pl.BlockSpec((tb * ROW_TILE, LANES), lambda i, *_: (i, 0))],
            out_specs=pl.BlockSpec(memory_space=pl.ANY),
            scratch_shapes=[pltpu.VMEM((blk * ROW_TILE, LANES), F32),
                            pltpu.SemaphoreType.DMA, pltpu.SemaphoreType.DMA]),
        out_shape=jax.ShapeDtypeStruct((p_rows * ROW_TILE, LANES), F32),
        compiler_params=_cparams(("arbitrary",)),
        name="dispatch",
    )(dst_flat, pad_start, pad_len, nused, u2)


def _experts_kernel(be_sm, first_sm, next_sm, slot_sm, nused_sm,
                    xs_ref, w1_hbm, w3_hbm, w2_hbm, ys_ref,
                    w1f_ref, w3f_ref, w2f_ref, w13b_ref, w2b_ref, wsem, *, d_expert):
    b = pl.program_id(0)
    used = b < nused_sm[0]

    def weight_copies(expert, slot):
        return [pltpu.make_async_copy(src.at[expert], dst.at[slot], wsem.at[slot])
                for src, dst in ((w1_hbm, w1f_ref), (w3_hbm, w3f_ref), (w2_hbm, w2f_ref))]

    @pl.when(b == 0)
    def _():
        for c in weight_copies(be_sm[0], slot_sm[0]):
            c.start()

    @pl.when(first_sm[b] == 1)
    def _():
        slot = slot_sm[b]
        for c in weight_copies(be_sm[b], slot):
            c.wait()

        @pl.when(next_sm[b] >= 0)
        def _():
            for c in weight_copies(next_sm[b], 1 - slot):
                c.start()

        w13b_ref[:, :d_expert] = w1f_ref[slot].astype(BF16)
        w13b_ref[:, d_expert:] = w3f_ref[slot].astype(BF16)
        w2b_ref[...] = w2f_ref[slot].astype(BF16)

    @pl.when(used)
    def _():
        xb = _load_row_tiles(xs_ref).astype(BF16)
        h = jnp.dot(xb, w13b_ref[...], preferred_element_type=F32)
        h1 = h[:, :d_expert]
        h3 = h[:, d_expert:]
        a = (h1 * _sigmoid(h1) * h3).astype(BF16)
        _store_row_tiles(ys_ref, jnp.dot(a, w2b_ref[...], preferred_element_type=F32))

    @pl.when(jnp.logical_not(used))
    def _():
        ys_ref[...] = jnp.zeros(ys_ref.shape, F32)


def _experts(blk_expert, blk_first, blk_next, blk_slot, nused, xs, w1, w3, w2, *, blk):
    p_rows = xs.shape[0] // ROW_TILE
    d_model, d_expert = w1.shape[-2:]
    kern = functools.partial(_experts_kernel, d_expert=d_expert)
    rows_blk = (blk * ROW_TILE, LANES)

    def rows_in(b, be, first, nxt, slot, nu):
        return (jnp.maximum(jnp.minimum(b, nu[0] - 1), 0), 0)

    hbm = pl.BlockSpec(memory_space=pl.ANY)
    return pl.pallas_call(
        kern,
        grid_spec=pltpu.PrefetchScalarGridSpec(
            num_scalar_prefetch=5,
            grid=(p_rows // blk,),
            in_specs=[pl.BlockSpec(rows_blk, rows_in), hbm, hbm, hbm],
            out_specs=pl.BlockSpec(rows_blk, lambda b, *_: (b, 0)),
            scratch_shapes=[pltpu.VMEM((2, d_model, d_expert), F32),
                            pltpu.VMEM((2, d_model, d_expert), F32),
                            pltpu.VMEM((2, d_expert, d_model), F32),
                            pltpu.VMEM((d_model, 2 * d_expert), BF16),
                            pltpu.VMEM((d_expert, d_model), BF16),
                            pltpu.SemaphoreType.DMA((2,))]),
        out_shape=jax.ShapeDtypeStruct((p_rows * ROW_TILE, LANES), F32),
        compiler_params=_cparams(("arbitrary",)),
        name="experts",
    )(blk_expert, blk_first, blk_next, blk_slot, nused, xs, w1, w3, w2)


def _combine_kernel(dst_sm, x1_ref, meta_ref, ys_ref, out_ref, buf0_ref, buf1_ref, sem,
                    *, tb):
    base = pl.program_id(0) * (tb * TOP_K)
    bufs = (buf0_ref, buf1_ref)

    def issue(r, carry):
        for kk in range(TOP_K):
            pltpu.make_async_copy(_rows(ys_ref, dst_sm[base + TOP_K * r + kk]),
                                  _rows(bufs[kk], r), sem).start(priority=kk)
        return carry

    lax.fori_loop(0, tb, issue, 0, unroll=8)
    for kk in range(TOP_K):
        pltpu.make_async_copy(_rows(ys_ref, 0, tb), bufs[kk], sem).wait()
    meta = meta_ref[...]
    moe = (_load_row_tiles(buf0_ref) * meta[:, 2:3]
           + _load_row_tiles(buf1_ref) * meta[:, 3:4])
    out_ref[...] = x1_ref[...] + moe


def _combine(dst_flat, x1, meta, ys):
    t, d_model = x1.shape
    tb = TB_ROWS
    kern = functools.partial(_combine_kernel, tb=tb)
    return pl.pallas_call(
        kern,
        grid_spec=pltpu.PrefetchScalarGridSpec(
            num_scalar_prefetch=1,
            grid=(t // tb,),
            in_specs=[pl.BlockSpec((tb, d_model), lambda i, dst: (i, 0)),
                      pl.BlockSpec((tb, LANES), lambda i, dst: (i, 0)),
                      pl.BlockSpec(memory_space=pl.ANY)],
            out_specs=pl.BlockSpec((tb, d_model), lambda i, dst: (i, 0)),
            scratch_shapes=[pltpu.VMEM(_row_tiles((tb, d_model)), F32),
                            pltpu.VMEM(_row_tiles((tb, d_model)), F32),
                            pltpu.SemaphoreType.DMA]),
        out_shape=jax.ShapeDtypeStruct((t, d_model), F32),
        compiler_params=_cparams(("arbitrary",)),
        name="combine",
    )(dst_flat, x1, meta, ys)


def _layer(h2, *, batch, seq, norm1_g, w_in, conv_w, q_norm_g, k_norm_g,
           w_conv_out, w_attn_out, w_o, norm2_g, w_group, w_router, w1, w3, w2):
    t, d_model = h2.shape
    c, sa, sb = _rope_tables(seq)
    scale = HEAD_DIM ** -0.5 * LOG2_E
    tables_q = tuple(jnp.asarray(tab * np.float32(scale)) for tab in (c, sa, sb))
    tables_k = tuple(jnp.asarray(tab) for tab in (c, sa, sb))

    cb, z, q, k, v, sgc, sga = _inproj(
        h2, norm1_g[None, :], w_in.astype(BF16), q_norm_g[None, :], k_norm_g[None, :],
        tables_q, tables_k, seq=seq)
    o = _attention(q, k, v, batch=batch, seq=seq)

    n_route = N_GROUPS + N_EXPERTS
    wr = jnp.concatenate(
        [w_group, w_router, jnp.zeros((d_model, LANES - n_route), F32)], axis=1)
    wr_hi = wr.astype(BF16)
    wr = jnp.concatenate([wr_hi, (wr - wr_hi.astype(F32)).astype(BF16)], axis=1)
    x1, u2, meta, cnt = _post(h2, cb, z, o, sgc, sga, conv_w,
                              w_conv_out.astype(BF16), w_attn_out.astype(BF16),
                              w_o.astype(BF16), norm2_g[None, :], wr, seq=seq)

    blk = MOE_ROWS
    dst = _rank(meta, cnt, blk=blk)
    dst_flat = dst[:, :TOP_K].astype(jnp.int32).reshape(-1)
    counts = cnt[0, :N_EXPERTS].astype(jnp.int32)
    p_rows = t * TOP_K + N_EXPERTS * blk
    nb = p_rows // blk
    nblk = (counts + blk - 1) // blk
    pend = jnp.cumsum(nblk)
    nused = pend[-1]
    barange = jnp.arange(nb, dtype=jnp.int32)
    bidx = jnp.minimum(barange, nused - 1)
    blk_expert = jnp.minimum(jnp.sum(pend[None, :] <= bidx[:, None], axis=1),
                             N_EXPERTS - 1).astype(jnp.int32)
    live = barange < nused
    blk_first = (live & (barange == (pend - nblk)[blk_expert])).astype(jnp.int32)
    after = pend[blk_expert]
    blk_next = jnp.where(live & (after < nused),
                         blk_expert[jnp.minimum(after, nb - 1)], -1).astype(jnp.int32)
    blk_slot = ((jnp.cumsum(nblk > 0) - 1)[blk_expert] % 2).astype(jnp.int32)
    pad_start = ((pend - nblk) * blk + counts).astype(jnp.int32)
    pad_len = (nblk * blk - counts).astype(jnp.int32)
    nused1 = nused.reshape(1).astype(jnp.int32)

    xs = _dispatch(dst_flat, pad_start, pad_len, nused1, u2, p_rows=p_rows, blk=blk)
    ys = _experts(blk_expert, blk_first, blk_next, blk_slot, nused1, xs, w1, w3, w2,
                  blk=blk)
    return _combine(dst_flat, x1, meta, ys)


def kernel(x, norm1_g, w_in, conv_w, q_norm_g, k_norm_g, w_conv_out, w_attn_out, w_o,
           norm2_g, w_group, w_router, w1, w3, w2):
    batch, seq, d_model = x.shape
    h2 = x.reshape(batch * seq, d_model)
    for l in range(norm1_g.shape[0]):
        h2 = _layer(h2, batch=batch, seq=seq, norm1_g=norm1_g[l], w_in=w_in[l],
                    conv_w=conv_w[l], q_norm_g=q_norm_g[l], k_norm_g=k_norm_g[l],
                    w_conv_out=w_conv_out[l], w_attn_out=w_attn_out[l], w_o=w_o[l],
                    norm2_g=norm2_g[l], w_group=w_group[l], w_router=w_router[l],
                    w1=w1[l], w3=w3[l], w2=w2[l])
    return h2.reshape(batch, seq, d_model)
```

```python
import functools

import jax
import jax.numpy as jnp
import numpy as np
from jax import lax
from jax.experimental import pallas as pl
from jax.experimental.pallas import tpu as pltpu

F32 = jnp.float32
BF16 = jnp.bfloat16

GRID_W = 64
EPS = 1e-6
N_HEADS = 8
N_KV_HEADS = 2
HEAD_DIM = 128
ROPE_THETA = 10000.0
N_GROUPS = 8
EXPERTS_PER_GROUP = 8
N_EXPERTS = N_GROUPS * EXPERTS_PER_GROUP
TOP_K = 2
LOG2_E = 1.4426950408889634

LANES = 128
MXU_DIM = 256
V7X_VMEM_LIMIT_BYTES = 56000 * 1024

TM_PROJ = 512
POST_SPLIT = 1
TQ_ATTN = 512
TK_ATTN = 512
TB_RANK = 512
TB_ROWS = 256
RUN_CHUNK = 8
MOE_ROWS = 256
BF16_SUBLANES = 16
F32_SUBLANES = 8


def _cparams(sem):
    return pltpu.CompilerParams(dimension_semantics=sem,
                                vmem_limit_bytes=V7X_VMEM_LIMIT_BYTES)


ROW_TILE = 8


def _row_tiles(shape2d):
    rows, width = shape2d
    assert width == ROW_TILE * LANES
    return (rows * ROW_TILE, LANES)


def _rows(ref, r, n=1):
    return ref.at[pl.ds(pl.multiple_of(r * ROW_TILE, ROW_TILE), n * ROW_TILE)]


def _load_row_tiles(ref):
    rows = ref.shape[0] // ROW_TILE
    return jnp.concatenate(
        [ref[pl.ds(s, rows, stride=ROW_TILE), :] for s in range(ROW_TILE)], axis=1)


def _store_row_tiles(ref, value, row0=0):
    rows = value.shape[0]
    for s in range(ROW_TILE):
        ref[pl.ds(row0 * ROW_TILE + s, rows, stride=ROW_TILE), :] = (
            value[:, s * LANES:(s + 1) * LANES])


def _resident(shape):
    nd = len(shape)
    return pl.BlockSpec(shape, lambda *_: (0,) * nd, pipeline_mode=pl.Buffered(1))


def _head_norm_rope(xh, g, c, sa, sb):
    ms = jnp.mean(xh * xh, axis=-1, keepdims=True)
    y = xh * lax.rsqrt(ms + EPS) * g
    y_next = pltpu.roll(y, HEAD_DIM - 1, axis=1)
    y_prev = pltpu.roll(y, 1, axis=1)
    return y * c + y_next * sa + y_prev * sb


def _sigmoid(x):
    return 0.5 * jnp.tanh(0.5 * x) + 0.5


def _inproj_kernel(x_ref, g1_ref, w_ref, gq_ref, gk_ref,
                   cq_ref, saq_ref, sbq_ref, ck_ref, sak_ref, sbk_ref,
                   cb_ref, z_ref, q_ref, k_ref, v_ref, sgc_ref, sga_ref,
                   *, d_conv, d_q, d_kv, d_model):
    x = x_ref[...]
    ms = jnp.mean(x * x, axis=-1, keepdims=True)
    u = (x * lax.rsqrt(ms + EPS) * g1_ref[...]).astype(BF16)

    def proj(lo, width):
        return jnp.dot(u, w_ref[:, lo:lo + width], preferred_element_type=F32)

    o_cb, o_cc, o_cx = 0, d_conv, 2 * d_conv
    o_q = 3 * d_conv
    o_k = o_q + d_q
    o_v = o_k + d_kv
    o_gc = o_v + d_kv
    o_ga = o_gc + d_model

    sgc_ref[...] = _sigmoid(proj(o_gc, d_model)).astype(BF16)
    sga_ref[...] = _sigmoid(proj(o_ga, d_model)).astype(BF16)

    q = proj(o_q, d_q)
    gq = gq_ref[...]
    cq, saq, sbq = cq_ref[...], saq_ref[...], sbq_ref[...]
    for h in range(d_q // HEAD_DIM):
        sl = slice(h * HEAD_DIM, (h + 1) * HEAD_DIM)
        q_ref[:, sl] = _head_norm_rope(q[:, sl], gq, cq, saq, sbq).astype(BF16)

    k = proj(o_k, d_kv)
    gk = gk_ref[...]
    ck, sak, sbk = ck_ref[...], sak_ref[...], sbk_ref[...]
    for h in range(d_kv // HEAD_DIM):
        sl = slice(h * HEAD_DIM, (h + 1) * HEAD_DIM)
        k_ref[:, sl] = _head_norm_rope(k[:, sl], gk, ck, sak, sbk).astype(BF16)

    z_ref[...] = (proj(o_cc, d_conv) * proj(o_cx, d_conv)).astype(BF16)
    v_ref[...] = proj(o_v, d_kv).astype(BF16)
    cb_ref[...] = proj(o_cb, d_conv).astype(BF16)


def _rope_tables(seq):
    rows = seq // GRID_W
    axis_dim = HEAD_DIM // 2
    row = np.repeat(np.arange(rows, dtype=np.float32), GRID_W)
    col = np.tile(np.arange(GRID_W, dtype=np.float32), rows)
    inv = (np.float32(ROPE_THETA)
           ** (-np.arange(0, axis_dim, 2, dtype=np.float32) / np.float32(axis_dim)))
    ang = np.concatenate([row[:, None] * inv, col[:, None] * inv], axis=-1)
    ang = ang.astype(np.float32)
    cos, sin = np.cos(ang), np.sin(ang)
    zero = np.zeros_like(sin)
    c = np.repeat(cos, 2, axis=-1)
    sa = np.stack([-sin, zero], axis=-1).reshape(seq, HEAD_DIM)
    sb = np.stack([zero, sin], axis=-1).reshape(seq, HEAD_DIM)
    return c, sa, sb


def _inproj(x2, g1, w_in_bf, gq, gk, tables_q, tables_k, *, seq):
    t, d_model = x2.shape
    d_q = N_HEADS * HEAD_DIM
    d_kv = N_KV_HEADS * HEAD_DIM
    d_in = w_in_bf.shape[1]
    d_conv = (d_in - d_q - 2 * d_kv - 2 * d_model) // 3
    tm = TM_PROJ
    nseq = seq // tm

    def row(width):
        return pl.BlockSpec((tm, width), lambda i: (i, 0))

    table = pl.BlockSpec((tm, HEAD_DIM), lambda i: (i % nseq, 0))
    kern = functools.partial(_inproj_kernel, d_conv=d_conv, d_q=d_q, d_kv=d_kv,
                             d_model=d_model)
    out_shape = [jax.ShapeDtypeStruct((t, w), BF16)
                 for w in (d_conv, d_conv, d_q, d_kv, d_kv, d_model, d_model)]
    return pl.pallas_call(
        kern,
        grid=(t // tm,),
        in_specs=[row(d_model), _resident((1, d_model)), _resident((d_model, d_in)),
                  _resident((1, HEAD_DIM)), _resident((1, HEAD_DIM)),
                  table, table, table, table, table, table],
        out_specs=[row(d_conv), row(d_conv), row(d_q), row(d_kv), row(d_kv),
                   row(d_model), row(d_model)],
        out_shape=out_shape,
        compiler_params=_cparams(("arbitrary",)),
        name="inproj",
    )(x2, g1, w_in_bf, gq, gk, *tables_q, *tables_k)


def _attn_kernel(q_ref, k_ref, v_ref, o_ref, qs_ref, vext_ref, m_ref, acc_ref,
                 *, tq, tk, nk, group):
    @pl.when(pl.program_id(2) == 0)
    def _():
        vext_ref[:, :HEAD_DIM] = v_ref[...]
        vext_ref[:, HEAD_DIM:] = jnp.ones((vext_ref.shape[0], HEAD_DIM), BF16)

    for g in range(group):
        qs_ref[g * tq:(g + 1) * tq, :] = q_ref[:, g * HEAD_DIM:(g + 1) * HEAD_DIM]
    m_ref[...] = jnp.full(m_ref.shape, -jnp.inf, F32)
    acc_ref[...] = jnp.zeros(acc_ref.shape, F32)

    for j in range(nk):
        kc = k_ref[j * tk:(j + 1) * tk, :]
        s = lax.dot_general(qs_ref[...], kc, (((1,), (1,)), ((), ())),
                            preferred_element_type=F32)
        m_prev = m_ref[...]
        m_new = jnp.maximum(m_prev, jnp.max(s, axis=-1, keepdims=True))
        alpha = jnp.exp2(m_prev - m_new)
        p = jnp.concatenate(
            [jnp.exp2(s[:, c * LANES:(c + 1) * LANES] - m_new) for c in range(tk // LANES)],
            axis=1).astype(BF16)
        pv = jnp.dot(p, vext_ref[j * tk:(j + 1) * tk, :], preferred_element_type=F32)
        acc_ref[...] = jnp.concatenate([alpha, alpha], axis=1) * acc_ref[...] + pv
        m_ref[...] = m_new

    out = acc_ref[:, :HEAD_DIM] / acc_ref[:, HEAD_DIM:]
    for g in range(group):
        o_ref[:, g * HEAD_DIM:(g + 1) * HEAD_DIM] = out[g * tq:(g + 1) * tq].astype(BF16)


def _attention(q, k, v, *, batch, seq):
    t = q.shape[0]
    group = N_HEADS // N_KV_HEADS
    tq, tk = TQ_ATTN, TK_ATTN
    nq = seq // tq
    gw = group * HEAD_DIM
    kern = functools.partial(_attn_kernel, tq=tq, tk=tk, nk=seq // tk, group=group)
    return pl.pallas_call(
        kern,
        grid=(batch, N_KV_HEADS, nq),
        in_specs=[pl.BlockSpec((tq, gw), lambda b, h, i: (b * nq + i, h)),
                  pl.BlockSpec((seq, HEAD_DIM), lambda b, h, i: (b, h)),
                  pl.BlockSpec((seq, HEAD_DIM), lambda b, h, i: (b, h))],
        out_specs=pl.BlockSpec((tq, gw), lambda b, h, i: (b * nq + i, h)),
        out_shape=jax.ShapeDtypeStruct((t, N_HEADS * HEAD_DIM), BF16),
        scratch_shapes=[pltpu.VMEM((group * tq, HEAD_DIM), BF16),
                        pltpu.VMEM((seq, 2 * HEAD_DIM), BF16),
                        pltpu.VMEM((group * tq, LANES), F32),
                        pltpu.VMEM((group * tq, 2 * HEAD_DIM), F32)],
        compiler_params=_cparams(("arbitrary", "arbitrary", "arbitrary")),
        name="attention",
    )(q, k, v)


def _route(logits):
    rows = logits.shape[0]
    lane = lax.broadcasted_iota(jnp.int32, (rows, LANES), 1).astype(F32)
    neg = -jnp.inf
    big = float(2 * LANES)
    is_group = lane < N_GROUPS
    gl = jnp.where(is_group, logits, neg)
    gmax = jnp.max(gl, axis=-1, keepdims=True)
    gidx = jnp.min(jnp.where(gl == gmax, lane, big), axis=-1, keepdims=True)
    gsum = jnp.sum(jnp.where(is_group, jnp.exp(logits - gmax), 0.0), axis=-1,
                   keepdims=True)
    pg = 1.0 / gsum
    lane_group = jnp.floor(lane * (1.0 / EXPERTS_PER_GROUP)) - 1.0
    mine = (lane_group == gidx) & (lane >= N_GROUPS) & (lane < N_GROUPS + N_EXPERTS)
    sel = jnp.where(mine, logits, neg)
    v1 = jnp.max(sel, axis=-1, keepdims=True)
    i1 = jnp.min(jnp.where(sel == v1, lane, big), axis=-1, keepdims=True)
    sel2 = jnp.where(lane == i1, neg, sel)
    v2 = jnp.max(sel2, axis=-1, keepdims=True)
    i2 = jnp.min(jnp.where(sel2 == v2, lane, big), axis=-1, keepdims=True)
    t2 = jnp.exp(v2 - v1)
    den = 1.0 + t2
    wgt1 = pg * (1.0 / den)
    wgt2 = pg * (t2 / den)
    e1 = i1 - N_GROUPS
    e2 = i2 - N_GROUPS
    meta = jnp.where(lane == 0, e1,
                     jnp.where(lane == 1, e2,
                               jnp.where(lane == 2, wgt1,
                                         jnp.where(lane == 3, wgt2, 0.0))))
    picked = jnp.where((lane == e1) | (lane == e2), 1.0, 0.0)
    return meta, jnp.sum(picked, axis=0, keepdims=True)


def _post_kernel(x_ref, cb_ref, z_ref, zprev_ref, znext_ref, o_ref, sgc_ref, sga_ref,
                 cw_ref, wc_ref, wa_ref, wo_ref, g2_ref, wr_ref,
                 x1_ref, u2_ref, meta_ref, cnt_ref, *, tm, nseq, nsplit):
    i = pl.program_id(0)
    at_start = (i % nseq) == 0
    at_end = (i % nseq) == nseq - 1
    rows = tm // nsplit
    rowid = lax.broadcasted_iota(jnp.int32, (rows, 1), 0)

    @pl.when(i == 0)
    def _():
        cnt_ref[...] = jnp.zeros(cnt_ref.shape, F32)

    for part_id in range(nsplit):
        r0 = part_id * rows
        rs = slice(r0, r0 + rows)
        y_attn = jnp.dot(o_ref[rs, :], wa_ref[...], preferred_element_type=F32)

        y_conv = None
        for c in range(z_ref.shape[1] // MXU_DIM):
            sl = slice(c * MXU_DIM, (c + 1) * MXU_DIM)
            z = z_ref[rs, sl].astype(F32)
            if part_id == 0:
                prev_row = zprev_ref[BF16_SUBLANES - 1:BF16_SUBLANES, sl].astype(F32)
                prev_row = jnp.where(at_start, 0.0, prev_row)
            else:
                prev_row = z_ref[r0 - 1:r0, sl].astype(F32)
            if part_id == nsplit - 1:
                next_row = znext_ref[0:1, sl].astype(F32)
                next_row = jnp.where(at_end, 0.0, next_row)
            else:
                next_row = z_ref[r0 + rows:r0 + rows + 1, sl].astype(F32)
            zp = jnp.where(rowid == 0, prev_row, pltpu.roll(z, 1, axis=0))
            zn = jnp.where(rowid == rows - 1, next_row, pltpu.roll(z, rows - 1, axis=0))
            conv = cw_ref[0:1, sl] * zp + cw_ref[1:2, sl] * z + cw_ref[2:3, sl] * zn
            cbz = (cb_ref[rs, sl].astype(F32) * conv).astype(BF16)
            part = jnp.dot(cbz, wc_ref[sl, :], preferred_element_type=F32)
            y_conv = part if y_conv is None else y_conv + part
        merged = (sgc_ref[rs, :].astype(F32) * y_conv
                  + sga_ref[rs, :].astype(F32) * y_attn).astype(BF16)
        x1 = x_ref[rs, :] + jnp.dot(merged, wo_ref[...], preferred_element_type=F32)
        x1_ref[rs, :] = x1

        ms = jnp.mean(x1 * x1, axis=-1, keepdims=True)
        u2 = x1 * lax.rsqrt(ms + EPS) * g2_ref[...]
        _store_row_tiles(u2_ref, u2, r0)

        u2_hi = u2.astype(BF16)
        u2_lo = (u2 - u2_hi.astype(F32)).astype(BF16)
        hi_part = jnp.dot(u2_hi, wr_ref[...], preferred_element_type=F32)
        lo_part = jnp.dot(u2_lo, wr_ref[:, :LANES], preferred_element_type=F32)
        logits = hi_part[:, :LANES] + (hi_part[:, LANES:] + lo_part)
        meta, picked = _route(logits)
        meta_ref[rs, :] = meta
        cnt_ref[...] += picked


def _post(x2, cb, z, o, sgc, sga, conv_w, wc, wa, wo, g2, wr, *, seq):
    t, d_model = x2.shape
    tm = TM_PROJ
    nseq = seq // tm
    hb = tm // BF16_SUBLANES
    nhalo = t // BF16_SUBLANES
    d_conv = cb.shape[1]
    d_q = o.shape[1]

    def row(width):
        return pl.BlockSpec((tm, width), lambda i: (i, 0))

    kern = functools.partial(_post_kernel, tm=tm, nseq=nseq, nsplit=POST_SPLIT)
    return pl.pallas_call(
        kern,
        grid=(t // tm,),
        in_specs=[row(d_model), row(d_conv), row(d_conv),
                  pl.BlockSpec((BF16_SUBLANES, d_conv),
                               lambda i: (jnp.maximum(i * hb - 1, 0), 0)),
                  pl.BlockSpec((BF16_SUBLANES, d_conv),
                               lambda i: (jnp.minimum((i + 1) * hb, nhalo - 1), 0)),
                  row(d_q), row(d_model), row(d_model),
                  _resident(conv_w.shape), _resident(wc.shape), _resident(wa.shape),
                  _resident(wo.shape), _resident(g2.shape), _resident(wr.shape)],
        out_specs=[row(d_model),
                   pl.BlockSpec(_row_tiles((tm, d_model)), lambda i: (i, 0)),
                   row(LANES),
                   pl.BlockSpec((1, LANES), lambda i: (0, 0))],
        out_shape=[jax.ShapeDtypeStruct((t, d_model), F32),
                   jax.ShapeDtypeStruct(_row_tiles((t, d_model)), F32),
                   jax.ShapeDtypeStruct((t, LANES), F32),
                   jax.ShapeDtypeStruct((1, LANES), F32)],
        compiler_params=_cparams(("arbitrary",)),
        name="post",
    )(x2, cb, z, z, z, o, sgc, sga, conv_w, wc, wa, wo, g2, wr)


def _rank_kernel(meta_ref, cnt_ref, dst_ref, runs_ref, carry_ref, pstart_ref, *, tb, blk):
    i = pl.program_id(0)
    lane = lax.broadcasted_iota(jnp.int32, (tb, LANES), 1).astype(F32)
    meta = meta_ref[...]
    oh1 = jnp.where(lane == meta[:, 0:1], 1.0, 0.0)
    oh2 = jnp.where(lane == meta[:, 1:2], 1.0, 0.0)
    c = oh1 + oh2

    @pl.when(i == 0)
    def _():
        cnt = jnp.broadcast_to(cnt_ref[...], (8, LANES))
        nblk = jnp.floor((cnt + (blk - 1)) * (1.0 / blk))
        r = lax.broadcasted_iota(jnp.int32, (LANES, LANES), 0)
        col = lax.broadcasted_iota(jnp.int32, (LANES, LANES), 1)
        upper = jnp.where(r < col, 1.0, 0.0).astype(BF16)
        pblk = jnp.dot(nblk.astype(BF16), upper, preferred_element_type=F32)
        pstart_ref[...] = pblk[0:1, :] * blk
        carry_ref[...] = jnp.zeros(carry_ref.shape, F32)

    r = lax.broadcasted_iota(jnp.int32, (tb, tb), 0)
    col = lax.broadcasted_iota(jnp.int32, (tb, tb), 1)
    lower = jnp.where(col < r, 1.0, 0.0).astype(BF16)
    prefix = jnp.dot(lower, c.astype(BF16), preferred_element_type=F32)
    run_start = carry_ref[...] + pstart_ref[...]
    run_len = jnp.sum(c, axis=0, keepdims=True)
    base = prefix + run_start
    d1 = jnp.sum(oh1 * base, axis=-1, keepdims=True)
    d2 = jnp.sum(oh2 * base, axis=-1, keepdims=True)

    nchunk = jnp.floor((run_len + (RUN_CHUNK - 1)) * (1.0 / RUN_CHUNK))
    r = lax.broadcasted_iota(jnp.int32, (LANES, LANES), 0)
    col = lax.broadcasted_iota(jnp.int32, (LANES, LANES), 1)
    upper = jnp.where(r < col, 1.0, 0.0).astype(BF16)
    local_start = jnp.dot(jnp.broadcast_to(nchunk, (8, LANES)).astype(BF16), upper,
                          preferred_element_type=F32)[0:1, :] * RUN_CHUNK
    local = prefix + local_start
    l1 = jnp.sum(oh1 * local, axis=-1, keepdims=True)
    l2 = jnp.sum(oh2 * local, axis=-1, keepdims=True)
    dst_ref[...] = jnp.where(lane == 0, d1,
                             jnp.where(lane == 1, d2,
                                       jnp.where(lane == 2, l1,
                                                 jnp.where(lane == 3, l2, 0.0))))
    row = lax.broadcasted_iota(jnp.int32, (8, LANES), 0)
    runs_ref[...] = jnp.where(row == 0, run_start, jnp.where(row == 1, run_len, 0.0))
    carry_ref[...] += run_len


def _rank(meta, cnt, *, blk):
    t = meta.shape[0]
    tb = TB_RANK
    kern = functools.partial(_rank_kernel, tb=tb, blk=blk)
    return pl.pallas_call(
        kern,
        grid=(t // tb,),
        in_specs=[pl.BlockSpec((tb, LANES), lambda i: (i, 0)),
                  pl.BlockSpec((1, LANES), lambda i: (0, 0))],
        out_specs=[pl.BlockSpec((tb, LANES), lambda i: (i, 0)),
                   pl.BlockSpec((8, LANES), lambda i: (i, 0))],
        out_shape=[jax.ShapeDtypeStruct((t, LANES), F32),
                   jax.ShapeDtypeStruct((t // tb * 8, LANES), F32)],
        scratch_shapes=[pltpu.VMEM((1, LANES), F32), pltpu.VMEM((1, LANES), F32)],
        compiler_params=_cparams(("arbitrary",)),
        name="rank",
    )(meta, cnt)


def _dispatch_kernel(dst_sm, pad_start_sm, pad_len_sm, nused_sm, u2_ref, xs_ref,
                     zero_ref, sem, zsem, *, tb, blk, nblocks):
    step = pl.program_id(0)
    base = step * (tb * TOP_K)
    nbits = blk.bit_length() - 1

    def pad_copies(visit):
        def per_expert(e, carry):
            start, length = pad_start_sm[e], pad_len_sm[e]
            for bit in range(nbits):
                size = 1 << bit
                higher = lax.shift_left(lax.shift_right_logical(length, bit + 1), bit + 1)

                @pl.when((lax.shift_right_logical(length, bit) & 1) == 1)
                def _():
                    visit(pltpu.make_async_copy(
                        zero_ref.at[pl.ds(0, size * ROW_TILE)],
                        _rows(xs_ref, start + higher, size), zsem))
            return carry

        lax.fori_loop(0, N_EXPERTS, per_expert, 0)

        def per_block(b, carry):
            visit(pltpu.make_async_copy(zero_ref, _rows(xs_ref, b * blk, blk), zsem))
            return carry

        lax.fori_loop(nused_sm[0], nblocks, per_block, 0)

    @pl.when(step == 0)
    def _():
        zero_ref[...] = jnp.zeros(zero_ref.shape, F32)
        pad_copies(lambda c: c.start())

    def issue(r, carry):
        for kk in range(TOP_K):
            pltpu.make_async_copy(
                _rows(u2_ref, r), _rows(xs_ref, dst_sm[base + TOP_K * r + kk]), sem
            ).start(priority=kk)
        return carry

    lax.fori_loop(0, tb, issue, 0, unroll=8)
    for kk in range(TOP_K):
        pltpu.make_async_copy(u2_ref, _rows(xs_ref, 0, tb), sem).wait()

    @pl.when(step == pl.num_programs(0) - 1)
    def _():
        pad_copies(lambda c: c.wait())


def _dispatch(dst_flat, pad_start, pad_len, nused, u2, *, p_rows, blk):
    t = u2.shape[0] // ROW_TILE
    tb = TB_ROWS
    kern = functools.partial(_dispatch_kernel, tb=tb, blk=blk, nblocks=p_rows // blk)
    return pl.pallas_call(
        kern,
        grid_spec=pltpu.PrefetchScalarGridSpec(
            num_scalar_prefetch=4,
            grid=(t // tb,),
            in_specs=[pl.BlockSpec((tb * ROW_TILE, LANES), lambda i, *_: (i, 0))],
            out_specs=pl.BlockSpec(memory_space=pl.ANY),
            scratch_shapes=[pltpu.VMEM((blk * ROW_TILE, LANES), F32),
                            pltpu.SemaphoreType.DMA, pltpu.SemaphoreType.DMA]),
        out_shape=jax.ShapeDtypeStruct((p_rows * ROW_TILE, LANES), F32),
        compiler_params=_cparams(("arbitrary",)),
        name="dispatch",
    )(dst_flat, pad_start, pad_len, nused, u2)


def _experts_kernel(be_sm, first_sm, next_sm, slot_sm, nused_sm,
                    xs_ref, w1_hbm, w3_hbm, w2_hbm, ys_ref,
                    w1f_ref, w3f_ref, w2f_ref, w13b_ref, w2b_ref, wsem, *, d_expert):
    b = pl.program_id(0)
    used = b < nused_sm[0]

    def weight_copies(expert, slot):
        return [pltpu.make_async_copy(src.at[expert], dst.at[slot], wsem.at[slot])
                for src, dst in ((w1_hbm, w1f_ref), (w3_hbm, w3f_ref), (w2_hbm, w2f_ref))]

    @pl.when(b == 0)
    def _():
        for c in weight_copies(be_sm[0], slot_sm[0]):
            c.start()

    @pl.when(first_sm[b] == 1)
    def _():
        slot = slot_sm[b]
        for c in weight_copies(be_sm[b], slot):
            c.wait()

        @pl.when(next_sm[b] >= 0)
        def _():
            for c in weight_copies(next_sm[b], 1 - slot):
                c.start()

        w13b_ref[:, :d_expert] = w1f_ref[slot].astype(BF16)
        w13b_ref[:, d_expert:] = w3f_ref[slot].astype(BF16)
        w2b_ref[...] = w2f_ref[slot].astype(BF16)

    @pl.when(used)
    def _():
        xb = _load_row_tiles(xs_ref).astype(BF16)
        h = jnp.dot(xb, w13b_ref[...], preferred_element_type=F32)
        h1 = h[:, :d_expert]
        h3 = h[:, d_expert:]
        a = (h1 * _sigmoid(h1) * h3).astype(BF16)
        _store_row_tiles(ys_ref, jnp.dot(a, w2b_ref[...], preferred_element_type=F32))

    @pl.when(jnp.logical_not(used))
    def _():
        ys_ref[...] = jnp.zeros(ys_ref.shape, F32)


def _experts(blk_expert, blk_first, blk_next, blk_slot, nused, xs, w1, w3, w2, *, blk):
    p_rows = xs.shape[0] // ROW_TILE
    d_model, d_expert = w1.shape[-2:]
    kern = functools.partial(_experts_kernel, d_expert=d_expert)
    rows_blk = (blk * ROW_TILE, LANES)

    def rows_in(b, be, first, nxt, slot, nu):
        return (jnp.maximum(jnp.minimum(b, nu[0] - 1), 0), 0)

    hbm = pl.BlockSpec(memory_space=pl.ANY)
    return pl.pallas_call(
        kern,
        grid_spec=pltpu.PrefetchScalarGridSpec(
            num_scalar_prefetch=5,
            grid=(p_rows // blk,),
            in_specs=[pl.BlockSpec(rows_blk, rows_in), hbm, hbm, hbm],
            out_specs=pl.BlockSpec(rows_blk, lambda b, *_: (b, 0)),
            scratch_shapes=[pltpu.VMEM((2, d_model, d_expert), F32),
                            pltpu.VMEM((2, d_model, d_expert), F32),
                            pltpu.VMEM((2, d_expert, d_model), F32),
                            pltpu.VMEM((d_model, 2 * d_expert), BF16),
                            pltpu.VMEM((d_expert, d_model), BF16),
                            pltpu.SemaphoreType.DMA((2,))]),
        out_shape=jax.ShapeDtypeStruct((p_rows * ROW_TILE, LANES), F32),
        compiler_params=_cparams(("arbitrary",)),
        name="experts",
    )(blk_expert, blk_first, blk_next, blk_slot, nused, xs, w1, w3, w2)


def _combine_kernel(local_sm, chunk_src_sm, nchunk_sm, x1_ref, meta_ref, ys_ref, out_ref,
                    buf_ref, g0_ref, g1_ref, sem, *, tb, max_chunks):
    step = pl.program_id(0)
    nsteps = pl.num_programs(0)

    def start_chunks(tile):
        slot = tile % 2

        def per_chunk(c, carry):
            pltpu.make_async_copy(
                _rows(ys_ref, chunk_src_sm[tile * max_chunks + c], RUN_CHUNK),
                _rows(buf_ref.at[slot], c * RUN_CHUNK, RUN_CHUNK),
                sem.at[slot]).start()
            return carry

        lax.fori_loop(0, nchunk_sm[tile], per_chunk, 0)

    def wait_chunks(tile):
        slot = tile % 2
        n = nchunk_sm[tile]
        for bit in range(max_chunks.bit_length()):
            @pl.when((lax.shift_right_logical(n, bit) & 1) == 1)
            def _():
                rows = RUN_CHUNK << bit
                pltpu.make_async_copy(_rows(ys_ref, 0, rows),
                                      _rows(buf_ref.at[slot], 0, rows),
                                      sem.at[slot]).wait()

    @pl.when(step == 0)
    def _():
        start_chunks(step)

    @pl.when(step + 1 < nsteps)
    def _():
        start_chunks(step + 1)

    wait_chunks(step)

    base = step * (tb * TOP_K)
    tile_buf = buf_ref.at[step % 2]

    def move(r, carry):
        for kk, g_ref in enumerate((g0_ref, g1_ref)):
            _rows(g_ref, r)[...] = _rows(tile_buf, local_sm[base + TOP_K * r + kk])[...]
        return carry

    lax.fori_loop(0, tb, move, 0, unroll=8)
    meta = meta_ref[...]
    moe = (_load_row_tiles(g0_ref) * meta[:, 2:3]
           + _load_row_tiles(g1_ref) * meta[:, 3:4])
    out_ref[...] = x1_ref[...] + moe


def _combine(local_flat, run_start, run_len, x1, meta, ys):
    t, d_model = x1.shape
    tb = TB_RANK
    ntiles = t // tb
    max_chunks = tb * TOP_K // RUN_CHUNK + N_EXPERTS
    nchunk = (run_len.reshape(ntiles, N_EXPERTS) + RUN_CHUNK - 1) // RUN_CHUNK
    cum = jnp.cumsum(nchunk, axis=1)
    c_idx = jnp.arange(max_chunks, dtype=jnp.int32)
    owner = jnp.minimum(jnp.sum(cum[:, None, :] <= c_idx[None, :, None], axis=2),
                        N_EXPERTS - 1)
    first = jnp.take_along_axis(cum - nchunk, owner, axis=1)
    start = jnp.take_along_axis(run_start.reshape(ntiles, N_EXPERTS), owner, axis=1)
    chunk_src = (start + (c_idx[None, :] - first) * RUN_CHUNK).astype(jnp.int32)
    tile_chunks = cum[:, -1].astype(jnp.int32)

    kern = functools.partial(_combine_kernel, tb=tb, max_chunks=max_chunks)
    buf_rows = max_chunks * RUN_CHUNK
    return pl.pallas_call(
        kern,
        grid_spec=pltpu.PrefetchScalarGridSpec(
            num_scalar_prefetch=3,
            grid=(t // tb,),
            in_specs=[pl.BlockSpec((tb, d_model), lambda i, *_: (i, 0)),
                      pl.BlockSpec((tb, LANES), lambda i, *_: (i, 0)),
                      pl.BlockSpec(memory_space=pl.ANY)],
            out_specs=pl.BlockSpec((tb, d_model), lambda i, *_: (i, 0)),
            scratch_shapes=[pltpu.VMEM((2,) + _row_tiles((buf_rows, d_model)), F32),
                            pltpu.VMEM(_row_tiles((tb, d_model)), F32),
                            pltpu.VMEM(_row_tiles((tb, d_model)), F32),
                            pltpu.SemaphoreType.DMA((2,))]),
        out_shape=jax.ShapeDtypeStruct((t, d_model), F32),
        compiler_params=_cparams(("arbitrary",)),
        name="combine",
    )(local_flat, chunk_src.reshape(-1), tile_chunks, x1, meta, ys)


def _layer(h2, *, batch, seq, norm1_g, w_in, conv_w, q_norm_g, k_norm_g,
           w_conv_out, w_attn_out, w_o, norm2_g, w_group, w_router, w1, w3, w2):
    t, d_model = h2.shape
    c, sa, sb = _rope_tables(seq)
    scale = HEAD_DIM ** -0.5 * LOG2_E
    tables_q = tuple(jnp.asarray(tab * np.float32(scale)) for tab in (c, sa, sb))
    tables_k = tuple(jnp.asarray(tab) for tab in (c, sa, sb))

    cb, z, q, k, v, sgc, sga = _inproj(
        h2, norm1_g[None, :], w_in.astype(BF16), q_norm_g[None, :], k_norm_g[None, :],
        tables_q, tables_k, seq=seq)
    o = _attention(q, k, v, batch=batch, seq=seq)

    n_route = N_GROUPS + N_EXPERTS
    wr = jnp.concatenate(
        [w_group, w_router, jnp.zeros((d_model, LANES - n_route), F32)], axis=1)
    wr_hi = wr.astype(BF16)
    wr = jnp.concatenate([wr_hi, (wr - wr_hi.astype(F32)).astype(BF16)], axis=1)
    x1, u2, meta, cnt = _post(h2, cb, z, o, sgc, sga, conv_w,
                              w_conv_out.astype(BF16), w_attn_out.astype(BF16),
                              w_o.astype(BF16), norm2_g[None, :], wr, seq=seq)

    blk = MOE_ROWS
    dst, runs = _rank(meta, cnt, blk=blk)
    dst_int = dst[:, :2 * TOP_K].astype(jnp.int32)
    dst_flat = dst_int[:, :TOP_K].reshape(-1)
    local_flat = dst_int[:, TOP_K:].reshape(-1)
    runs_int = runs.reshape(-1, 8, LANES)[:, :2, :N_EXPERTS].astype(jnp.int32)
    run_start = runs_int[:, 0, :].reshape(-1)
    run_len = runs_int[:, 1, :].reshape(-1)
    counts = cnt[0, :N_EXPERTS].astype(jnp.int32)
    p_rows = t * TOP_K + N_EXPERTS * blk
    nb = p_rows // blk
    nblk = (counts + blk - 1) // blk
    pend = jnp.cumsum(nblk)
    nused = pend[-1]
    barange = jnp.arange(nb, dtype=jnp.int32)
    bidx = jnp.minimum(barange, nused - 1)
    blk_expert = jnp.minimum(jnp.sum(pend[None, :] <= bidx[:, None], axis=1),
                             N_EXPERTS - 1).astype(jnp.int32)
    live = barange < nused
    blk_first = (live & (barange == (pend - nblk)[blk_expert])).astype(jnp.int32)
    after = pend[blk_expert]
    blk_next = jnp.where(live & (after < nused),
                         blk_expert[jnp.minimum(after, nb - 1)], -1).astype(jnp.int32)
    blk_slot = ((jnp.cumsum(nblk > 0) - 1)[blk_expert] % 2).astype(jnp.int32)
    pad_start = ((pend - nblk) * blk + counts).astype(jnp.int32)
    pad_len = (nblk * blk - counts).astype(jnp.int32)
    nused1 = nused.reshape(1).astype(jnp.int32)

    xs = _dispatch(dst_flat, pad_start, pad_len, nused1, u2, p_rows=p_rows, blk=blk)
    ys = _experts(blk_expert, blk_first, blk_next, blk_slot, nused1, xs, w1, w3, w2,
                  blk=blk)
    return _combine(local_flat, run_start, run_len, x1, meta, ys)


def kernel(x, norm1_g, w_in, conv_w, q_norm_g, k_norm_g, w_conv_out, w_attn_out, w_o,
           norm2_g, w_group, w_router, w1, w3, w2):
    batch, seq, d_model = x.shape
    h2 = x.reshape(batch * seq, d_model)
    for l in range(norm1_g.shape[0]):
        h2 = _layer(h2, batch=batch, seq=seq, norm1_g=norm1_g[l], w_in=w_in[l],
                    conv_w=conv_w[l], q_norm_g=q_norm_g[l], k_norm_g=k_norm_g[l],
                    w_conv_out=w_conv_out[l], w_attn_out=w_attn_out[l], w_o=w_o[l],
                    norm2_g=norm2_g[l], w_group=w_group[l], w_router=w_router[l],
                    w1=w1[l], w3=w3[l], w2=w2[l])
    return h2.reshape(batch, seq, d_model)
```

```python
import functools

import jax
import jax.numpy as jnp
import numpy as np
from jax import lax
from jax.experimental import pallas as pl
from jax.experimental.pallas import tpu as pltpu

F32 = jnp.float32
BF16 = jnp.bfloat16

GRID_W = 64
EPS = 1e-6
N_HEADS = 8
N_KV_HEADS = 2
HEAD_DIM = 128
ROPE_THETA = 10000.0
N_GROUPS = 8
EXPERTS_PER_GROUP = 8
N_EXPERTS = N_GROUPS * EXPERTS_PER_GROUP
TOP_K = 2
LOG2_E = 1.4426950408889634

LANES = 128
MXU_DIM = 256
V7X_VMEM_LIMIT_BYTES = 56000 * 1024

TM_PROJ = 512
POST_SPLIT = 1
TQ_ATTN = 512
TK_ATTN = 512
TB_RANK = 512
TB_ROWS = 256
RUN_CHUNK = 8
MOE_ROWS = 256
BF16_SUBLANES = 16
F32_SUBLANES = 8


def _cparams(sem):
    return pltpu.CompilerParams(dimension_semantics=sem,
                                vmem_limit_bytes=V7X_VMEM_LIMIT_BYTES)


ROW_TILE = 8


def _row_tiles(shape2d):
    rows, width = shape2d
    assert width == ROW_TILE * LANES
    return (rows * ROW_TILE, LANES)


def _rows(ref, r, n=1):
    return ref.at[pl.ds(pl.multiple_of(r * ROW_TILE, ROW_TILE), n * ROW_TILE)]


def _load_row_tiles(ref):
    rows = ref.shape[0] // ROW_TILE
    return jnp.concatenate(
        [ref[pl.ds(s, rows, stride=ROW_TILE), :] for s in range(ROW_TILE)], axis=1)


def _store_row_tiles(ref, value, row0=0):
    rows = value.shape[0]
    for s in range(ROW_TILE):
        ref[pl.ds(row0 * ROW_TILE + s, rows, stride=ROW_TILE), :] = (
            value[:, s * LANES:(s + 1) * LANES])


def _resident(shape):
    nd = len(shape)
    return pl.BlockSpec(shape, lambda *_: (0,) * nd, pipeline_mode=pl.Buffered(1))


def _head_norm_rope(xh, g, c, sa, sb):
    ms = jnp.mean(xh * xh, axis=-1, keepdims=True)
    y = xh * lax.rsqrt(ms + EPS) * g
    y_next = pltpu.roll(y, HEAD_DIM - 1, axis=1)
    y_prev = pltpu.roll(y, 1, axis=1)
    return y * c + y_next * sa + y_prev * sb


def _sigmoid(x):
    return 0.5 * jnp.tanh(0.5 * x) + 0.5


def _inproj_kernel(x_ref, g1_ref, w_ref, gq_ref, gk_ref,
                   cq_ref, saq_ref, sbq_ref, ck_ref, sak_ref, sbk_ref,
                   cb_ref, z_ref, q_ref, k_ref, v_ref, sgc_ref, sga_ref,
                   *, d_conv, d_q, d_kv, d_model):
    x = x_ref[...]
    ms = jnp.mean(x * x, axis=-1, keepdims=True)
    u = (x * lax.rsqrt(ms + EPS) * g1_ref[...]).astype(BF16)

    def proj(lo, width):
        return jnp.dot(u, w_ref[:, lo:lo + width], preferred_element_type=F32)

    o_cb, o_cc, o_cx = 0, d_conv, 2 * d_conv
    o_q = 3 * d_conv
    o_k = o_q + d_q
    o_v = o_k + d_kv
    o_gc = o_v + d_kv
    o_ga = o_gc + d_model

    sgc_ref[...] = _sigmoid(proj(o_gc, d_model)).astype(BF16)
    sga_ref[...] = _sigmoid(proj(o_ga, d_model)).astype(BF16)

    q = proj(o_q, d_q)
    gq = gq_ref[...]
    cq, saq, sbq = cq_ref[...], saq_ref[...], sbq_ref[...]
    for h in range(d_q // HEAD_DIM):
        sl = slice(h * HEAD_DIM, (h + 1) * HEAD_DIM)
        q_ref[:, sl] = _head_norm_rope(q[:, sl], gq, cq, saq, sbq).astype(BF16)

    k = proj(o_k, d_kv)
    gk = gk_ref[...]
    ck, sak, sbk = ck_ref[...], sak_ref[...], sbk_ref[...]
    for h in range(d_kv // HEAD_DIM):
        sl = slice(h * HEAD_DIM, (h + 1) * HEAD_DIM)
        k_ref[:, sl] = _head_norm_rope(k[:, sl], gk, ck, sak, sbk).astype(BF16)

    z_ref[...] = (proj(o_cc, d_conv) * proj(o_cx, d_conv)).astype(BF16)
    v_ref[...] = proj(o_v, d_kv).astype(BF16)
    cb_ref[...] = proj(o_cb, d_conv).astype(BF16)


def _rope_tables(seq):
    rows = seq // GRID_W
    axis_dim = HEAD_DIM // 2
    row = np.repeat(np.arange(rows, dtype=np.float32), GRID_W)
    col = np.tile(np.arange(GRID_W, dtype=np.float32), rows)
    inv = (np.float32(ROPE_THETA)
           ** (-np.arange(0, axis_dim, 2, dtype=np.float32) / np.float32(axis_dim)))
    ang = np.concatenate([row[:, None] * inv, col[:, None] * inv], axis=-1)
    ang = ang.astype(np.float32)
    cos, sin = np.cos(ang), np.sin(ang)
    zero = np.zeros_like(sin)
    c = np.repeat(cos, 2, axis=-1)
    sa = np.stack([-sin, zero], axis=-1).reshape(seq, HEAD_DIM)
    sb = np.stack([zero, sin], axis=-1).reshape(seq, HEAD_DIM)
    return c, sa, sb


def _inproj(x2, g1, w_in_bf, gq, gk, tables_q, tables_k, *, seq):
    t, d_model = x2.shape
    d_q = N_HEADS * HEAD_DIM
    d_kv = N_KV_HEADS * HEAD_DIM
    d_in = w_in_bf.shape[1]
    d_conv = (d_in - d_q - 2 * d_kv - 2 * d_model) // 3
    tm = TM_PROJ
    nseq = seq // tm

    def row(width):
        return pl.BlockSpec((tm, width), lambda i: (i, 0))

    table = pl.BlockSpec((tm, HEAD_DIM), lambda i: (i % nseq, 0))
    kern = functools.partial(_inproj_kernel, d_conv=d_conv, d_q=d_q, d_kv=d_kv,
                             d_model=d_model)
    out_shape = [jax.ShapeDtypeStruct((t, w), BF16)
                 for w in (d_conv, d_conv, d_q, d_kv, d_kv, d_model, d_model)]
    return pl.pallas_call(
        kern,
        grid=(t // tm,),
        in_specs=[row(d_model), _resident((1, d_model)), _resident((d_model, d_in)),
                  _resident((1, HEAD_DIM)), _resident((1, HEAD_DIM)),
                  table, table, table, table, table, table],
        out_specs=[row(d_conv), row(d_conv), row(d_q), row(d_kv), row(d_kv),
                   row(d_model), row(d_model)],
        out_shape=out_shape,
        compiler_params=_cparams(("arbitrary",)),
        name="inproj",
    )(x2, g1, w_in_bf, gq, gk, *tables_q, *tables_k)


def _attn_kernel(q_ref, k_ref, v_ref, o_ref, qs_ref, vext_ref, m_ref, acc_ref,
                 *, tq, tk, nk, group):
    @pl.when(pl.program_id(2) == 0)
    def _():
        vext_ref[:, :HEAD_DIM] = v_ref[...]
        vext_ref[:, HEAD_DIM:] = jnp.ones((vext_ref.shape[0], HEAD_DIM), BF16)

    for g in range(group):
        qs_ref[g * tq:(g + 1) * tq, :] = q_ref[:, g * HEAD_DIM:(g + 1) * HEAD_DIM]
    m_ref[...] = jnp.full(m_ref.shape, -jnp.inf, F32)
    acc_ref[...] = jnp.zeros(acc_ref.shape, F32)

    for j in range(nk):
        kc = k_ref[j * tk:(j + 1) * tk, :]
        s = lax.dot_general(qs_ref[...], kc, (((1,), (1,)), ((), ())),
                            preferred_element_type=F32)
        m_prev = m_ref[...]
        m_new = jnp.maximum(m_prev, jnp.max(s, axis=-1, keepdims=True))
        alpha = jnp.exp2(m_prev - m_new)
        p = jnp.concatenate(
            [jnp.exp2(s[:, c * LANES:(c + 1) * LANES] - m_new) for c in range(tk // LANES)],
            axis=1).astype(BF16)
        pv = jnp.dot(p, vext_ref[j * tk:(j + 1) * tk, :], preferred_element_type=F32)
        acc_ref[...] = jnp.concatenate([alpha, alpha], axis=1) * acc_ref[...] + pv
        m_ref[...] = m_new

    out = acc_ref[:, :HEAD_DIM] / acc_ref[:, HEAD_DIM:]
    for g in range(group):
        o_ref[:, g * HEAD_DIM:(g + 1) * HEAD_DIM] = out[g * tq:(g + 1) * tq].astype(BF16)


def _attention(q, k, v, *, batch, seq):
    t = q.shape[0]
    group = N_HEADS // N_KV_HEADS
    tq, tk = TQ_ATTN, TK_ATTN
    nq = seq // tq
    gw = group * HEAD_DIM
    kern = functools.partial(_attn_kernel, tq=tq, tk=tk, nk=seq // tk, group=group)
    return pl.pallas_call(
        kern,
        grid=(batch, N_KV_HEADS, nq),
        in_specs=[pl.BlockSpec((tq, gw), lambda b, h, i: (b * nq + i, h)),
                  pl.BlockSpec((seq, HEAD_DIM), lambda b, h, i: (b, h)),
                  pl.BlockSpec((seq, HEAD_DIM), lambda b, h, i: (b, h))],
        out_specs=pl.BlockSpec((tq, gw), lambda b, h, i: (b * nq + i, h)),
        out_shape=jax.ShapeDtypeStruct((t, N_HEADS * HEAD_DIM), BF16),
        scratch_shapes=[pltpu.VMEM((group * tq, HEAD_DIM), BF16),
                        pltpu.VMEM((seq, 2 * HEAD_DIM), BF16),
                        pltpu.VMEM((group * tq, LANES), F32),
                        pltpu.VMEM((group * tq, 2 * HEAD_DIM), F32)],
        compiler_params=_cparams(("arbitrary", "arbitrary", "arbitrary")),
        name="attention",
    )(q, k, v)


def _route(logits):
    rows = logits.shape[0]
    lane = lax.broadcasted_iota(jnp.int32, (rows, LANES), 1).astype(F32)
    neg = -jnp.inf
    big = float(2 * LANES)
    is_group = lane < N_GROUPS
    gl = jnp.where(is_group, logits, neg)
    gmax = jnp.max(gl, axis=-1, keepdims=True)
    gidx = jnp.min(jnp.where(gl == gmax, lane, big), axis=-1, keepdims=True)
    gsum = jnp.sum(jnp.where(is_group, jnp.exp(logits - gmax), 0.0), axis=-1,
                   keepdims=True)
    pg = 1.0 / gsum
    lane_group = jnp.floor(lane * (1.0 / EXPERTS_PER_GROUP)) - 1.0
    mine = (lane_group == gidx) & (lane >= N_GROUPS) & (lane < N_GROUPS + N_EXPERTS)
    sel = jnp.where(mine, logits, neg)
    v1 = jnp.max(sel, axis=-1, keepdims=True)
    i1 = jnp.min(jnp.where(sel == v1, lane, big), axis=-1, keepdims=True)
    sel2 = jnp.where(lane == i1, neg, sel)
    v2 = jnp.max(sel2, axis=-1, keepdims=True)
    i2 = jnp.min(jnp.where(sel2 == v2, lane, big), axis=-1, keepdims=True)
    t2 = jnp.exp(v2 - v1)
    den = 1.0 + t2
    wgt1 = pg * (1.0 / den)
    wgt2 = pg * (t2 / den)
    e1 = i1 - N_GROUPS
    e2 = i2 - N_GROUPS
    meta = jnp.where(lane == 0, e1,
                     jnp.where(lane == 1, e2,
                               jnp.where(lane == 2, wgt1,
                                         jnp.where(lane == 3, wgt2, 0.0))))
    picked = jnp.where((lane == e1) | (lane == e2), 1.0, 0.0)
    return meta, jnp.sum(picked, axis=0, keepdims=True)


def _post_kernel(x_ref, cb_ref, z_ref, zprev_ref, znext_ref, o_ref, sgc_ref, sga_ref,
                 cw_ref, wc_ref, wa_ref, wo_ref, g2_ref, wr_ref,
                 x1_ref, u2_ref, meta_ref, cnt_ref, *, tm, nseq, nsplit):
    i = pl.program_id(0)
    at_start = (i % nseq) == 0
    at_end = (i % nseq) == nseq - 1
    rows = tm // nsplit
    rowid = lax.broadcasted_iota(jnp.int32, (rows, 1), 0)

    @pl.when(i == 0)
    def _():
        cnt_ref[...] = jnp.zeros(cnt_ref.shape, F32)

    for part_id in range(nsplit):
        r0 = part_id * rows
        rs = slice(r0, r0 + rows)
        y_attn = jnp.dot(o_ref[rs, :], wa_ref[...], preferred_element_type=F32)

        y_conv = None
        for c in range(z_ref.shape[1] // MXU_DIM):
            sl = slice(c * MXU_DIM, (c + 1) * MXU_DIM)
            z = z_ref[rs, sl].astype(F32)
            if part_id == 0:
                prev_row = zprev_ref[BF16_SUBLANES - 1:BF16_SUBLANES, sl].astype(F32)
                prev_row = jnp.where(at_start, 0.0, prev_row)
            else:
                prev_row = z_ref[r0 - 1:r0, sl].astype(F32)
            if part_id == nsplit - 1:
                next_row = znext_ref[0:1, sl].astype(F32)
                next_row = jnp.where(at_end, 0.0, next_row)
            else:
                next_row = z_ref[r0 + rows:r0 + rows + 1, sl].astype(F32)
            zp = jnp.where(rowid == 0, prev_row, pltpu.roll(z, 1, axis=0))
            zn = jnp.where(rowid == rows - 1, next_row, pltpu.roll(z, rows - 1, axis=0))
            conv = cw_ref[0:1, sl] * zp + cw_ref[1:2, sl] * z + cw_ref[2:3, sl] * zn
            cbz = (cb_ref[rs, sl].astype(F32) * conv).astype(BF16)
            part = jnp.dot(cbz, wc_ref[sl, :], preferred_element_type=F32)
            y_conv = part if y_conv is None else y_conv + part
        merged = (sgc_ref[rs, :].astype(F32) * y_conv
                  + sga_ref[rs, :].astype(F32) * y_attn).astype(BF16)
        x1 = x_ref[rs, :] + jnp.dot(merged, wo_ref[...], preferred_element_type=F32)
        x1_ref[rs, :] = x1

        ms = jnp.mean(x1 * x1, axis=-1, keepdims=True)
        u2 = x1 * lax.rsqrt(ms + EPS) * g2_ref[...]
        _store_row_tiles(u2_ref, u2, r0)

        u2_hi = u2.astype(BF16)
        u2_lo = (u2 - u2_hi.astype(F32)).astype(BF16)
        hi_part = jnp.dot(u2_hi, wr_ref[...], preferred_element_type=F32)
        lo_part = jnp.dot(u2_lo, wr_ref[:, :LANES], preferred_element_type=F32)
        logits = hi_part[:, :LANES] + (hi_part[:, LANES:] + lo_part)
        meta, picked = _route(logits)
        meta_ref[rs, :] = meta
        cnt_ref[...] += picked


def _post(x2, cb, z, o, sgc, sga, conv_w, wc, wa, wo, g2, wr, *, seq):
    t, d_model = x2.shape
    tm = TM_PROJ
    nseq = seq // tm
    hb = tm // BF16_SUBLANES
    nhalo = t // BF16_SUBLANES
    d_conv = cb.shape[1]
    d_q = o.shape[1]

    def row(width):
        return pl.BlockSpec((tm, width), lambda i: (i, 0))

    kern = functools.partial(_post_kernel, tm=tm, nseq=nseq, nsplit=POST_SPLIT)
    return pl.pallas_call(
        kern,
        grid=(t // tm,),
        in_specs=[row(d_model), row(d_conv), row(d_conv),
                  pl.BlockSpec((BF16_SUBLANES, d_conv),
                               lambda i: (jnp.maximum(i * hb - 1, 0), 0)),
                  pl.BlockSpec((BF16_SUBLANES, d_conv),
                               lambda i: (jnp.minimum((i + 1) * hb, nhalo - 1), 0)),
                  row(d_q), row(d_model), row(d_model),
                  _resident(conv_w.shape), _resident(wc.shape), _resident(wa.shape),
                  _resident(wo.shape), _resident(g2.shape), _resident(wr.shape)],
        out_specs=[row(d_model),
                   pl.BlockSpec(_row_tiles((tm, d_model)), lambda i: (i, 0)),
                   row(LANES),
                   pl.BlockSpec((1, LANES), lambda i: (0, 0))],
        out_shape=[jax.ShapeDtypeStruct((t, d_model), F32),
                   jax.ShapeDtypeStruct(_row_tiles((t, d_model)), F32),
                   jax.ShapeDtypeStruct((t, LANES), F32),
                   jax.ShapeDtypeStruct((1, LANES), F32)],
        compiler_params=_cparams(("arbitrary",)),
        name="post",
    )(x2, cb, z, z, z, o, sgc, sga, conv_w, wc, wa, wo, g2, wr)


def _rank_kernel(meta_ref, cnt_ref, d0_ref, d1_ref, l0_ref, l1_ref, runs_ref,
                 carry_ref, pstart_ref, *, tb, blk):
    i = pl.program_id(0)
    lane = lax.broadcasted_iota(jnp.int32, (tb, LANES), 1).astype(F32)
    meta = meta_ref[...]
    oh1 = jnp.where(lane == meta[:, 0:1], 1.0, 0.0)
    oh2 = jnp.where(lane == meta[:, 1:2], 1.0, 0.0)
    c = oh1 + oh2

    @pl.when(i == 0)
    def _():
        cnt = jnp.broadcast_to(cnt_ref[...], (8, LANES))
        nblk = jnp.floor((cnt + (blk - 1)) * (1.0 / blk))
        r = lax.broadcasted_iota(jnp.int32, (LANES, LANES), 0)
        col = lax.broadcasted_iota(jnp.int32, (LANES, LANES), 1)
        upper = jnp.where(r < col, 1.0, 0.0).astype(BF16)
        pblk = jnp.dot(nblk.astype(BF16), upper, preferred_element_type=F32)
        pstart_ref[...] = pblk[0:1, :] * blk
        carry_ref[...] = jnp.zeros(carry_ref.shape, F32)

    r = lax.broadcasted_iota(jnp.int32, (tb, tb), 0)
    col = lax.broadcasted_iota(jnp.int32, (tb, tb), 1)
    lower = jnp.where(col < r, 1.0, 0.0).astype(BF16)
    prefix = jnp.dot(lower, c.astype(BF16), preferred_element_type=F32)
    run_start = carry_ref[...] + pstart_ref[...]
    run_len = jnp.sum(c, axis=0, keepdims=True)
    base = prefix + run_start

    nchunk = jnp.floor((run_len + (RUN_CHUNK - 1)) * (1.0 / RUN_CHUNK))
    r = lax.broadcasted_iota(jnp.int32, (LANES, LANES), 0)
    col = lax.broadcasted_iota(jnp.int32, (LANES, LANES), 1)
    upper = jnp.where(r < col, 1.0, 0.0).astype(BF16)
    local_start = jnp.dot(jnp.broadcast_to(nchunk, (8, LANES)).astype(BF16), upper,
                          preferred_element_type=F32)[0:1, :] * RUN_CHUNK
    local = prefix + local_start

    digits = jnp.zeros((tb, LANES), F32)
    for j, (onehot, values) in enumerate(((oh1, base), (oh2, base),
                                          (oh1, local), (oh2, local))):
        value = jnp.sum(onehot * values, axis=-1, keepdims=True)
        high = jnp.floor(value * (1.0 / 256.0))
        digits = jnp.where(lane == 2 * j, high,
                           jnp.where(lane == 2 * j + 1, value - high * 256.0, digits))
    r = lax.broadcasted_iota(jnp.int32, (tb, tb), 0)
    col = lax.broadcasted_iota(jnp.int32, (tb, tb), 1)
    eye = jnp.where(col == r, 1.0, 0.0).astype(BF16)
    rows = lax.dot_general(digits.astype(BF16), eye, (((0,), (0,)), ((), ())),
                           preferred_element_type=F32)
    for j, out_ref in enumerate((d0_ref, d1_ref, l0_ref, l1_ref)):
        out_ref[...] = (rows[2 * j:2 * j + 1, :] * 256.0
                        + rows[2 * j + 1:2 * j + 2, :]).astype(jnp.int32)
    row = lax.broadcasted_iota(jnp.int32, (8, LANES), 0)
    runs = jnp.where(row == 0, run_start, jnp.where(row == 1, run_len, 0.0))
    runs_ref[...] = runs.astype(jnp.int32)
    carry_ref[...] += run_len


def _rank(meta, cnt, *, blk):
    t = meta.shape[0]
    tb = TB_RANK
    kern = functools.partial(_rank_kernel, tb=tb, blk=blk)
    per_token = pl.BlockSpec((1, tb), lambda i: (0, i))
    return pl.pallas_call(
        kern,
        grid=(t // tb,),
        in_specs=[pl.BlockSpec((tb, LANES), lambda i: (i, 0)),
                  pl.BlockSpec((1, LANES), lambda i: (0, 0))],
        out_specs=[per_token] * 4 + [pl.BlockSpec((8, LANES), lambda i: (i, 0))],
        out_shape=[jax.ShapeDtypeStruct((1, t), jnp.int32)] * 4
                  + [jax.ShapeDtypeStruct((t // tb * 8, LANES), jnp.int32)],
        scratch_shapes=[pltpu.VMEM((1, LANES), F32), pltpu.VMEM((1, LANES), F32)],
        compiler_params=_cparams(("arbitrary",)),
        name="rank",
    )(meta, cnt)


def _dispatch_kernel(dst0_sm, dst1_sm, pad_start_sm, pad_len_sm, nused_sm, u2_ref, xs_ref,
                     zero_ref, sem, zsem, *, tb, blk, nblocks):
    step = pl.program_id(0)
    base = step * tb
    nbits = blk.bit_length() - 1

    def pad_copies(visit):
        def per_expert(e, carry):
            start, length = pad_start_sm[e], pad_len_sm[e]
            for bit in range(nbits):
                size = 1 << bit
                higher = lax.shift_left(lax.shift_right_logical(length, bit + 1), bit + 1)

                @pl.when((lax.shift_right_logical(length, bit) & 1) == 1)
                def _():
                    visit(pltpu.make_async_copy(
                        zero_ref.at[pl.ds(0, size * ROW_TILE)],
                        _rows(xs_ref, start + higher, size), zsem))
            return carry

        lax.fori_loop(0, N_EXPERTS, per_expert, 0)

        def per_block(b, carry):
            visit(pltpu.make_async_copy(zero_ref, _rows(xs_ref, b * blk, blk), zsem))
            return carry

        lax.fori_loop(nused_sm[0], nblocks, per_block, 0)

    @pl.when(step == 0)
    def _():
        zero_ref[...] = jnp.zeros(zero_ref.shape, F32)
        pad_copies(lambda c: c.start())

    def issue(r, carry):
        for kk, dst_sm in enumerate((dst0_sm, dst1_sm)):
            pltpu.make_async_copy(
                _rows(u2_ref, r), _rows(xs_ref, dst_sm[base + r]), sem
            ).start(priority=kk)
        return carry

    lax.fori_loop(0, tb, issue, 0, unroll=8)
    for kk in range(TOP_K):
        pltpu.make_async_copy(u2_ref, _rows(xs_ref, 0, tb), sem).wait()

    @pl.when(step == pl.num_programs(0) - 1)
    def _():
        pad_copies(lambda c: c.wait())


def _dispatch(dst0, dst1, pad_start, pad_len, nused, u2, *, p_rows, blk):
    t = u2.shape[0] // ROW_TILE
    tb = TB_ROWS
    kern = functools.partial(_dispatch_kernel, tb=tb, blk=blk, nblocks=p_rows // blk)
    return pl.pallas_call(
        kern,
        grid_spec=pltpu.PrefetchScalarGridSpec(
            num_scalar_prefetch=5,
            grid=(t // tb,),
            in_specs=[pl.BlockSpec((tb * ROW_TILE, LANES), lambda i, *_: (i, 0))],
            out_specs=pl.BlockSpec(memory_space=pl.ANY),
            scratch_shapes=[pltpu.VMEM((blk * ROW_TILE, LANES), F32),
                            pltpu.SemaphoreType.DMA, pltpu.SemaphoreType.DMA]),
        out_shape=jax.ShapeDtypeStruct((p_rows * ROW_TILE, LANES), F32),
        compiler_params=_cparams(("arbitrary",)),
        name="dispatch",
    )(dst0, dst1, pad_start, pad_len, nused, u2)


def _experts_kernel(be_sm, first_sm, next_sm, slot_sm, nused_sm,
                    xs_ref, w1_hbm, w3_hbm, w2_hbm, ys_ref,
                    w1f_ref, w3f_ref, w2f_ref, w13b_ref, w2b_ref, wsem, *, d_expert):
    b = pl.program_id(0)
    used = b < nused_sm[0]

    def weight_copies(expert, slot):
        return [pltpu.make_async_copy(src.at[expert], dst.at[slot], wsem.at[slot])
                for src, dst in ((w1_hbm, w1f_ref), (w3_hbm, w3f_ref), (w2_hbm, w2f_ref))]

    @pl.when(b == 0)
    def _():
        for c in weight_copies(be_sm[0], slot_sm[0]):
            c.start()

    @pl.when(first_sm[b] == 1)
    def _():
        slot = slot_sm[b]
        for c in weight_copies(be_sm[b], slot):
            c.wait()

        @pl.when(next_sm[b] >= 0)
        def _():
            for c in weight_copies(next_sm[b], 1 - slot):
                c.start()

        w13b_ref[:, :d_expert] = w1f_ref[slot].astype(BF16)
        w13b_ref[:, d_expert:] = w3f_ref[slot].astype(BF16)
        w2b_ref[...] = w2f_ref[slot].astype(BF16)

    @pl.when(used)
    def _():
        xb = _load_row_tiles(xs_ref).astype(BF16)
        h = jnp.dot(xb, w13b_ref[...], preferred_element_type=F32)
        h1 = h[:, :d_expert]
        h3 = h[:, d_expert:]
        a = (h1 * _sigmoid(h1) * h3).astype(BF16)
        _store_row_tiles(ys_ref, jnp.dot(a, w2b_ref[...], preferred_element_type=F32))

    @pl.when(jnp.logical_not(used))
    def _():
        ys_ref[...] = jnp.zeros(ys_ref.shape, F32)


def _experts(blk_expert, blk_first, blk_next, blk_slot, nused, xs, w1, w3, w2, *, blk):
    p_rows = xs.shape[0] // ROW_TILE
    d_model, d_expert = w1.shape[-2:]
    kern = functools.partial(_experts_kernel, d_expert=d_expert)
    rows_blk = (blk * ROW_TILE, LANES)

    def rows_in(b, be, first, nxt, slot, nu):
        return (jnp.maximum(jnp.minimum(b, nu[0] - 1), 0), 0)

    hbm = pl.BlockSpec(memory_space=pl.ANY)
    return pl.pallas_call(
        kern,
        grid_spec=pltpu.PrefetchScalarGridSpec(
            num_scalar_prefetch=5,
            grid=(p_rows // blk,),
            in_specs=[pl.BlockSpec(rows_blk, rows_in), hbm, hbm, hbm],
            out_specs=pl.BlockSpec(rows_blk, lambda b, *_: (b, 0)),
            scratch_shapes=[pltpu.VMEM((2, d_model, d_expert), F32),
                            pltpu.VMEM((2, d_model, d_expert), F32),
                            pltpu.VMEM((2, d_expert, d_model), F32),
                            pltpu.VMEM((d_model, 2 * d_expert), BF16),
                            pltpu.VMEM((d_expert, d_model), BF16),
                            pltpu.SemaphoreType.DMA((2,))]),
        out_shape=jax.ShapeDtypeStruct((p_rows * ROW_TILE, LANES), F32),
        compiler_params=_cparams(("arbitrary",)),
        name="experts",
    )(blk_expert, blk_first, blk_next, blk_slot, nused, xs, w1, w3, w2)


def _combine_kernel(local0_sm, local1_sm, chunk_src_sm, nchunk_sm,
                    x1_ref, meta_ref, ys_ref, out_ref,
                    buf_ref, g0_ref, g1_ref, sem, *, tb, max_chunks):
    step = pl.program_id(0)
    nsteps = pl.num_programs(0)

    def start_chunks(tile):
        slot = tile % 2

        def per_chunk(c, carry):
            pltpu.make_async_copy(
                _rows(ys_ref, chunk_src_sm[tile * max_chunks + c], RUN_CHUNK),
                _rows(buf_ref.at[slot], c * RUN_CHUNK, RUN_CHUNK),
                sem.at[slot]).start()
            return carry

        lax.fori_loop(0, nchunk_sm[tile], per_chunk, 0)

    def wait_chunks(tile):
        slot = tile % 2
        n = nchunk_sm[tile]
        for bit in range(max_chunks.bit_length()):
            @pl.when((lax.shift_right_logical(n, bit) & 1) == 1)
            def _():
                rows = RUN_CHUNK << bit
                pltpu.make_async_copy(_rows(ys_ref, 0, rows),
                                      _rows(buf_ref.at[slot], 0, rows),
                                      sem.at[slot]).wait()

    @pl.when(step == 0)
    def _():
        start_chunks(step)

    @pl.when(step + 1 < nsteps)
    def _():
        start_chunks(step + 1)

    wait_chunks(step)

    base = step * tb
    tile_buf = buf_ref.at[step % 2]

    def move(r, carry):
        for g_ref, local_sm in ((g0_ref, local0_sm), (g1_ref, local1_sm)):
            _rows(g_ref, r)[...] = _rows(tile_buf, local_sm[base + r])[...]
        return carry

    lax.fori_loop(0, tb, move, 0, unroll=8)
    meta = meta_ref[...]
    moe = (_load_row_tiles(g0_ref) * meta[:, 2:3]
           + _load_row_tiles(g1_ref) * meta[:, 3:4])
    out_ref[...] = x1_ref[...] + moe


def _combine(local0, local1, run_start, run_len, x1, meta, ys):
    t, d_model = x1.shape
    tb = TB_RANK
    max_chunks = tb * TOP_K // RUN_CHUNK + N_EXPERTS
    nchunk = (run_len + RUN_CHUNK - 1) // RUN_CHUNK
    cum = jnp.cumsum(nchunk, axis=1)
    first = (cum - nchunk)[:, None, :]
    c_idx = jnp.arange(max_chunks, dtype=jnp.int32)[None, :, None]
    owns = (first <= c_idx) & (c_idx < cum[:, None, :])
    src = run_start[:, None, :] + (c_idx - first) * RUN_CHUNK
    chunk_src = jnp.sum(jnp.where(owns, src, 0), axis=2).astype(jnp.int32)
    tile_chunks = cum[:, -1].astype(jnp.int32)

    kern = functools.partial(_combine_kernel, tb=tb, max_chunks=max_chunks)
    buf_rows = max_chunks * RUN_CHUNK
    return pl.pallas_call(
        kern,
        grid_spec=pltpu.PrefetchScalarGridSpec(
            num_scalar_prefetch=4,
            grid=(t // tb,),
            in_specs=[pl.BlockSpec((tb, d_model), lambda i, *_: (i, 0)),
                      pl.BlockSpec((tb, LANES), lambda i, *_: (i, 0)),
                      pl.BlockSpec(memory_space=pl.ANY)],
            out_specs=pl.BlockSpec((tb, d_model), lambda i, *_: (i, 0)),
            scratch_shapes=[pltpu.VMEM((2,) + _row_tiles((buf_rows, d_model)), F32),
                            pltpu.VMEM(_row_tiles((tb, d_model)), F32),
                            pltpu.VMEM(_row_tiles((tb, d_model)), F32),
                            pltpu.SemaphoreType.DMA((2,))]),
        out_shape=jax.ShapeDtypeStruct((t, d_model), F32),
        compiler_params=_cparams(("arbitrary",)),
        name="combine",
    )(local0, local1, chunk_src.reshape(-1), tile_chunks, x1, meta, ys)


def _layer(h2, *, batch, seq, norm1_g, w_in, conv_w, q_norm_g, k_norm_g,
           w_conv_out, w_attn_out, w_o, norm2_g, w_group, w_router, w1, w3, w2):
    t, d_model = h2.shape
    c, sa, sb = _rope_tables(seq)
    scale = HEAD_DIM ** -0.5 * LOG2_E
    tables_q = tuple(jnp.asarray(tab * np.float32(scale)) for tab in (c, sa, sb))
    tables_k = tuple(jnp.asarray(tab) for tab in (c, sa, sb))

    cb, z, q, k, v, sgc, sga = _inproj(
        h2, norm1_g[None, :], w_in.astype(BF16), q_norm_g[None, :], k_norm_g[None, :],
        tables_q, tables_k, seq=seq)
    o = _attention(q, k, v, batch=batch, seq=seq)

    n_route = N_GROUPS + N_EXPERTS
    wr = jnp.concatenate(
        [w_group, w_router, jnp.zeros((d_model, LANES - n_route), F32)], axis=1)
    wr_hi = wr.astype(BF16)
    wr = jnp.concatenate([wr_hi, (wr - wr_hi.astype(F32)).astype(BF16)], axis=1)
    x1, u2, meta, cnt = _post(h2, cb, z, o, sgc, sga, conv_w,
                              w_conv_out.astype(BF16), w_attn_out.astype(BF16),
                              w_o.astype(BF16), norm2_g[None, :], wr, seq=seq)

    blk = MOE_ROWS
    dst0, dst1, local0, local1, runs = _rank(meta, cnt, blk=blk)
    dst0, dst1, local0, local1 = (a.reshape(-1) for a in (dst0, dst1, local0, local1))
    runs = runs.reshape(-1, 8, LANES)
    run_start = runs[:, 0, :N_EXPERTS]
    run_len = runs[:, 1, :N_EXPERTS]
    counts = cnt[0, :N_EXPERTS].astype(jnp.int32)
    p_rows = t * TOP_K + N_EXPERTS * blk
    nb = p_rows // blk
    nblk = (counts + blk - 1) // blk
    pend = jnp.cumsum(nblk)
    nused = pend[-1]
    barange = jnp.arange(nb, dtype=jnp.int32)
    bidx = jnp.minimum(barange, nused - 1)
    blk_expert = jnp.minimum(jnp.sum(pend[None, :] <= bidx[:, None], axis=1),
                             N_EXPERTS - 1).astype(jnp.int32)
    live = barange < nused
    blk_first = (live & (barange == (pend - nblk)[blk_expert])).astype(jnp.int32)
    after = pend[blk_expert]
    blk_next = jnp.where(live & (after < nused),
                         blk_expert[jnp.minimum(after, nb - 1)], -1).astype(jnp.int32)
    blk_slot = ((jnp.cumsum(nblk > 0) - 1)[blk_expert] % 2).astype(jnp.int32)
    pad_start = ((pend - nblk) * blk + counts).astype(jnp.int32)
    pad_len = (nblk * blk - counts).astype(jnp.int32)
    nused1 = nused.reshape(1).astype(jnp.int32)

    xs = _dispatch(dst0, dst1, pad_start, pad_len, nused1, u2, p_rows=p_rows, blk=blk)
    ys = _experts(blk_expert, blk_first, blk_next, blk_slot, nused1, xs, w1, w3, w2,
                  blk=blk)
    return _combine(local0, local1, run_start, run_len, x1, meta, ys)


def kernel(x, norm1_g, w_in, conv_w, q_norm_g, k_norm_g, w_conv_out, w_attn_out, w_o,
           norm2_g, w_group, w_router, w1, w3, w2):
    batch, seq, d_model = x.shape
    h2 = x.reshape(batch * seq, d_model)
    for l in range(norm1_g.shape[0]):
        h2 = _layer(h2, batch=batch, seq=seq, norm1_g=norm1_g[l], w_in=w_in[l],
                    conv_w=conv_w[l], q_norm_g=q_norm_g[l], k_norm_g=k_norm_g[l],
                    w_conv_out=w_conv_out[l], w_attn_out=w_attn_out[l], w_o=w_o[l],
                    norm2_g=norm2_g[l], w_group=w_group[l], w_router=w_router[l],
                    w1=w1[l], w3=w3[l], w2=w2[l])
    return h2.reshape(batch, seq, d_model)
```

```python
import functools

import jax
import jax.numpy as jnp
import numpy as np
from jax import lax
from jax.experimental import pallas as pl
from jax.experimental.pallas import tpu as pltpu

F32 = jnp.float32
BF16 = jnp.bfloat16

GRID_W = 64
EPS = 1e-6
N_HEADS = 8
N_KV_HEADS = 2
HEAD_DIM = 128
ROPE_THETA = 10000.0
N_GROUPS = 8
EXPERTS_PER_GROUP = 8
N_EXPERTS = N_GROUPS * EXPERTS_PER_GROUP
TOP_K = 2
LOG2_E = 1.4426950408889634

LANES = 128
MXU_DIM = 256
V7X_VMEM_LIMIT_BYTES = 56000 * 1024

TM_PROJ = 512
POST_SPLIT = 1
TQ_ATTN = 512
TK_ATTN = 512
TB_RANK = 512
TB_ROWS = 256
RUN_CHUNK = 8
MOE_ROWS = 256
BF16_SUBLANES = 16
F32_SUBLANES = 8


def _cparams(sem):
    return pltpu.CompilerParams(dimension_semantics=sem,
                                vmem_limit_bytes=V7X_VMEM_LIMIT_BYTES)


ROW_TILE = 8


def _row_tiles(shape2d):
    rows, width = shape2d
    assert width == ROW_TILE * LANES
    return (rows * ROW_TILE, LANES)


def _rows(ref, r, n=1):
    return ref.at[pl.ds(pl.multiple_of(r * ROW_TILE, ROW_TILE), n * ROW_TILE)]


def _load_row_tiles(ref):
    rows = ref.shape[0] // ROW_TILE
    return jnp.concatenate(
        [ref[pl.ds(s, rows, stride=ROW_TILE), :] for s in range(ROW_TILE)], axis=1)


def _store_row_tiles(ref, value, row0=0):
    rows = value.shape[0]
    for s in range(ROW_TILE):
        ref[pl.ds(row0 * ROW_TILE + s, rows, stride=ROW_TILE), :] = (
            value[:, s * LANES:(s + 1) * LANES])


def _resident(shape):
    nd = len(shape)
    return pl.BlockSpec(shape, lambda *_: (0,) * nd, pipeline_mode=pl.Buffered(1))


def _head_norm_rope(xh, g, c, sa, sb):
    ms = jnp.mean(xh * xh, axis=-1, keepdims=True)
    y = xh * lax.rsqrt(ms + EPS) * g
    y_next = pltpu.roll(y, HEAD_DIM - 1, axis=1)
    y_prev = pltpu.roll(y, 1, axis=1)
    return y * c + y_next * sa + y_prev * sb


def _sigmoid(x):
    return 0.5 * jnp.tanh(0.5 * x) + 0.5


def _inproj_kernel(x_ref, g1_ref, w_ref, gq_ref, gk_ref,
                   cq_ref, saq_ref, sbq_ref, ck_ref, sak_ref, sbk_ref,
                   cb_ref, z_ref, q_ref, k_ref, v_ref, sgc_ref, sga_ref,
                   *, d_conv, d_q, d_kv, d_model):
    x = x_ref[...]
    ms = jnp.mean(x * x, axis=-1, keepdims=True)
    u = (x * lax.rsqrt(ms + EPS) * g1_ref[...]).astype(BF16)

    def proj(lo, width):
        return jnp.dot(u, w_ref[:, lo:lo + width], preferred_element_type=F32)

    o_cb, o_cc, o_cx = 0, d_conv, 2 * d_conv
    o_q = 3 * d_conv
    o_k = o_q + d_q
    o_v = o_k + d_kv
    o_gc = o_v + d_kv
    o_ga = o_gc + d_model

    sgc_ref[...] = _sigmoid(proj(o_gc, d_model)).astype(BF16)
    sga_ref[...] = _sigmoid(proj(o_ga, d_model)).astype(BF16)

    q = proj(o_q, d_q)
    gq = gq_ref[...]
    cq, saq, sbq = cq_ref[...], saq_ref[...], sbq_ref[...]
    for h in range(d_q // HEAD_DIM):
        sl = slice(h * HEAD_DIM, (h + 1) * HEAD_DIM)
        q_ref[:, sl] = _head_norm_rope(q[:, sl], gq, cq, saq, sbq).astype(BF16)

    k = proj(o_k, d_kv)
    gk = gk_ref[...]
    ck, sak, sbk = ck_ref[...], sak_ref[...], sbk_ref[...]
    for h in range(d_kv // HEAD_DIM):
        sl = slice(h * HEAD_DIM, (h + 1) * HEAD_DIM)
        k_ref[:, sl] = _head_norm_rope(k[:, sl], gk, ck, sak, sbk).astype(BF16)

    z_ref[...] = (proj(o_cc, d_conv) * proj(o_cx, d_conv)).astype(BF16)
    v_ref[...] = proj(o_v, d_kv).astype(BF16)
    cb_ref[...] = proj(o_cb, d_conv).astype(BF16)


def _rope_tables(seq):
    rows = seq // GRID_W
    axis_dim = HEAD_DIM // 2
    row = np.repeat(np.arange(rows, dtype=np.float32), GRID_W)
    col = np.tile(np.arange(GRID_W, dtype=np.float32), rows)
    inv = (np.float32(ROPE_THETA)
           ** (-np.arange(0, axis_dim, 2, dtype=np.float32) / np.float32(axis_dim)))
    ang = np.concatenate([row[:, None] * inv, col[:, None] * inv], axis=-1)
    ang = ang.astype(np.float32)
    cos, sin = np.cos(ang), np.sin(ang)
    zero = np.zeros_like(sin)
    c = np.repeat(cos, 2, axis=-1)
    sa = np.stack([-sin, zero], axis=-1).reshape(seq, HEAD_DIM)
    sb = np.stack([zero, sin], axis=-1).reshape(seq, HEAD_DIM)
    return c, sa, sb


def _inproj(x2, g1, w_in_bf, gq, gk, tables_q, tables_k, *, seq):
    t, d_model = x2.shape
    d_q = N_HEADS * HEAD_DIM
    d_kv = N_KV_HEADS * HEAD_DIM
    d_in = w_in_bf.shape[1]
    d_conv = (d_in - d_q - 2 * d_kv - 2 * d_model) // 3
    tm = TM_PROJ
    nseq = seq // tm

    def row(width):
        return pl.BlockSpec((tm, width), lambda i: (i, 0))

    table = pl.BlockSpec((tm, HEAD_DIM), lambda i: (i % nseq, 0))
    kern = functools.partial(_inproj_kernel, d_conv=d_conv, d_q=d_q, d_kv=d_kv,
                             d_model=d_model)
    out_shape = [jax.ShapeDtypeStruct((t, w), BF16)
                 for w in (d_conv, d_conv, d_q, d_kv, d_kv, d_model, d_model)]
    return pl.pallas_call(
        kern,
        grid=(t // tm,),
        in_specs=[row(d_model), _resident((1, d_model)), _resident((d_model, d_in)),
                  _resident((1, HEAD_DIM)), _resident((1, HEAD_DIM)),
                  table, table, table, table, table, table],
        out_specs=[row(d_conv), row(d_conv), row(d_q), row(d_kv), row(d_kv),
                   row(d_model), row(d_model)],
        out_shape=out_shape,
        compiler_params=_cparams(("arbitrary",)),
        name="inproj",
    )(x2, g1, w_in_bf, gq, gk, *tables_q, *tables_k)


def _attn_kernel(q_ref, k_ref, v_ref, o_ref, qs_ref, vext_ref, m_ref, acc_ref,
                 *, tq, tk, nk, group):
    @pl.when(pl.program_id(2) == 0)
    def _():
        vext_ref[:, :HEAD_DIM] = v_ref[...]
        vext_ref[:, HEAD_DIM:] = jnp.ones((vext_ref.shape[0], HEAD_DIM), BF16)

    for g in range(group):
        qs_ref[g * tq:(g + 1) * tq, :] = q_ref[:, g * HEAD_DIM:(g + 1) * HEAD_DIM]
    m_ref[...] = jnp.full(m_ref.shape, -jnp.inf, F32)
    acc_ref[...] = jnp.zeros(acc_ref.shape, F32)

    for j in range(nk):
        kc = k_ref[j * tk:(j + 1) * tk, :]
        s = lax.dot_general(qs_ref[...], kc, (((1,), (1,)), ((), ())),
                            preferred_element_type=F32)
        m_prev = m_ref[...]
        m_new = jnp.maximum(m_prev, jnp.max(s, axis=-1, keepdims=True))
        alpha = jnp.exp2(m_prev - m_new)
        p = jnp.concatenate(
            [jnp.exp2(s[:, c * LANES:(c + 1) * LANES] - m_new) for c in range(tk // LANES)],
            axis=1).astype(BF16)
        pv = jnp.dot(p, vext_ref[j * tk:(j + 1) * tk, :], preferred_element_type=F32)
        acc_ref[...] = jnp.concatenate([alpha, alpha], axis=1) * acc_ref[...] + pv
        m_ref[...] = m_new

    out = acc_ref[:, :HEAD_DIM] / acc_ref[:, HEAD_DIM:]
    for g in range(group):
        o_ref[:, g * HEAD_DIM:(g + 1) * HEAD_DIM] = out[g * tq:(g + 1) * tq].astype(BF16)


def _attention(q, k, v, *, batch, seq):
    t = q.shape[0]
    group = N_HEADS // N_KV_HEADS
    tq, tk = TQ_ATTN, TK_ATTN
    nq = seq // tq
    gw = group * HEAD_DIM
    kern = functools.partial(_attn_kernel, tq=tq, tk=tk, nk=seq // tk, group=group)
    return pl.pallas_call(
        kern,
        grid=(batch, N_KV_HEADS, nq),
        in_specs=[pl.BlockSpec((tq, gw), lambda b, h, i: (b * nq + i, h)),
                  pl.BlockSpec((seq, HEAD_DIM), lambda b, h, i: (b, h)),
                  pl.BlockSpec((seq, HEAD_DIM), lambda b, h, i: (b, h))],
        out_specs=pl.BlockSpec((tq, gw), lambda b, h, i: (b * nq + i, h)),
        out_shape=jax.ShapeDtypeStruct((t, N_HEADS * HEAD_DIM), BF16),
        scratch_shapes=[pltpu.VMEM((group * tq, HEAD_DIM), BF16),
                        pltpu.VMEM((seq, 2 * HEAD_DIM), BF16),
                        pltpu.VMEM((group * tq, LANES), F32),
                        pltpu.VMEM((group * tq, 2 * HEAD_DIM), F32)],
        compiler_params=_cparams(("arbitrary", "arbitrary", "arbitrary")),
        name="attention",
    )(q, k, v)


def _route(logits):
    rows = logits.shape[0]
    lane = lax.broadcasted_iota(jnp.int32, (rows, LANES), 1).astype(F32)
    neg = -jnp.inf
    big = float(2 * LANES)
    is_group = lane < N_GROUPS
    gl = jnp.where(is_group, logits, neg)
    gmax = jnp.max(gl, axis=-1, keepdims=True)
    gidx = jnp.min(jnp.where(gl == gmax, lane, big), axis=-1, keepdims=True)
    gsum = jnp.sum(jnp.where(is_group, jnp.exp(logits - gmax), 0.0), axis=-1,
                   keepdims=True)
    pg = 1.0 / gsum
    lane_group = jnp.floor(lane * (1.0 / EXPERTS_PER_GROUP)) - 1.0
    mine = (lane_group == gidx) & (lane >= N_GROUPS) & (lane < N_GROUPS + N_EXPERTS)
    sel = jnp.where(mine, logits, neg)
    v1 = jnp.max(sel, axis=-1, keepdims=True)
    i1 = jnp.min(jnp.where(sel == v1, lane, big), axis=-1, keepdims=True)
    sel2 = jnp.where(lane == i1, neg, sel)
    v2 = jnp.max(sel2, axis=-1, keepdims=True)
    i2 = jnp.min(jnp.where(sel2 == v2, lane, big), axis=-1, keepdims=True)
    t2 = jnp.exp(v2 - v1)
    den = 1.0 + t2
    wgt1 = pg * (1.0 / den)
    wgt2 = pg * (t2 / den)
    e1 = i1 - N_GROUPS
    e2 = i2 - N_GROUPS
    meta = jnp.where(lane == 0, e1,
                     jnp.where(lane == 1, e2,
                               jnp.where(lane == 2, wgt1,
                                         jnp.where(lane == 3, wgt2, 0.0))))
    picked = jnp.where((lane == e1) | (lane == e2), 1.0, 0.0)
    return meta, jnp.sum(picked, axis=0, keepdims=True)


def _post_kernel(x_ref, cb_ref, z_ref, zprev_ref, znext_ref, o_ref, sgc_ref, sga_ref,
                 cw_ref, wc_ref, wa_ref, wo_ref, g2_ref, wr_ref,
                 x1_ref, u2_ref, meta_ref, cnt_ref, *, tm, nseq, nsplit):
    i = pl.program_id(0)
    at_start = (i % nseq) == 0
    at_end = (i % nseq) == nseq - 1
    rows = tm // nsplit
    rowid = lax.broadcasted_iota(jnp.int32, (rows, 1), 0)

    @pl.when(i == 0)
    def _():
        cnt_ref[...] = jnp.zeros(cnt_ref.shape, F32)

    for part_id in range(nsplit):
        r0 = part_id * rows
        rs = slice(r0, r0 + rows)
        y_attn = jnp.dot(o_ref[rs, :], wa_ref[...], preferred_element_type=F32)

        y_conv = None
        for c in range(z_ref.shape[1] // MXU_DIM):
            sl = slice(c * MXU_DIM, (c + 1) * MXU_DIM)
            z = z_ref[rs, sl].astype(F32)
            if part_id == 0:
                prev_row = zprev_ref[BF16_SUBLANES - 1:BF16_SUBLANES, sl].astype(F32)
                prev_row = jnp.where(at_start, 0.0, prev_row)
            else:
                prev_row = z_ref[r0 - 1:r0, sl].astype(F32)
            if part_id == nsplit - 1:
                next_row = znext_ref[0:1, sl].astype(F32)
                next_row = jnp.where(at_end, 0.0, next_row)
            else:
                next_row = z_ref[r0 + rows:r0 + rows + 1, sl].astype(F32)
            zp = jnp.where(rowid == 0, prev_row, pltpu.roll(z, 1, axis=0))
            zn = jnp.where(rowid == rows - 1, next_row, pltpu.roll(z, rows - 1, axis=0))
            conv = cw_ref[0:1, sl] * zp + cw_ref[1:2, sl] * z + cw_ref[2:3, sl] * zn
            cbz = (cb_ref[rs, sl].astype(F32) * conv).astype(BF16)
            part = jnp.dot(cbz, wc_ref[sl, :], preferred_element_type=F32)
            y_conv = part if y_conv is None else y_conv + part
        merged = (sgc_ref[rs, :].astype(F32) * y_conv
                  + sga_ref[rs, :].astype(F32) * y_attn).astype(BF16)
        x1 = x_ref[rs, :] + jnp.dot(merged, wo_ref[...], preferred_element_type=F32)
        x1_ref[rs, :] = x1

        ms = jnp.mean(x1 * x1, axis=-1, keepdims=True)
        u2 = x1 * lax.rsqrt(ms + EPS) * g2_ref[...]
        _store_row_tiles(u2_ref, u2, r0)

        u2_hi = u2.astype(BF16)
        u2_lo = (u2 - u2_hi.astype(F32)).astype(BF16)
        hi_part = jnp.dot(u2_hi, wr_ref[...], preferred_element_type=F32)
        lo_part = jnp.dot(u2_lo, wr_ref[:, :LANES], preferred_element_type=F32)
        logits = hi_part[:, :LANES] + (hi_part[:, LANES:] + lo_part)
        meta, picked = _route(logits)
        meta_ref[rs, :] = meta
        cnt_ref[...] += picked


def _post(x2, cb, z, o, sgc, sga, conv_w, wc, wa, wo, g2, wr, *, seq):
    t, d_model = x2.shape
    tm = TM_PROJ
    nseq = seq // tm
    hb = tm // BF16_SUBLANES
    nhalo = t // BF16_SUBLANES
    d_conv = cb.shape[1]
    d_q = o.shape[1]

    def row(width):
        return pl.BlockSpec((tm, width), lambda i: (i, 0))

    kern = functools.partial(_post_kernel, tm=tm, nseq=nseq, nsplit=POST_SPLIT)
    return pl.pallas_call(
        kern,
        grid=(t // tm,),
        in_specs=[row(d_model), row(d_conv), row(d_conv),
                  pl.BlockSpec((BF16_SUBLANES, d_conv),
                               lambda i: (jnp.maximum(i * hb - 1, 0), 0)),
                  pl.BlockSpec((BF16_SUBLANES, d_conv),
                               lambda i: (jnp.minimum((i + 1) * hb, nhalo - 1), 0)),
                  row(d_q), row(d_model), row(d_model),
                  _resident(conv_w.shape), _resident(wc.shape), _resident(wa.shape),
                  _resident(wo.shape), _resident(g2.shape), _resident(wr.shape)],
        out_specs=[row(d_model),
                   pl.BlockSpec(_row_tiles((tm, d_model)), lambda i: (i, 0)),
                   row(LANES),
                   pl.BlockSpec((1, LANES), lambda i: (0, 0))],
        out_shape=[jax.ShapeDtypeStruct((t, d_model), F32),
                   jax.ShapeDtypeStruct(_row_tiles((t, d_model)), F32),
                   jax.ShapeDtypeStruct((t, LANES), F32),
                   jax.ShapeDtypeStruct((1, LANES), F32)],
        compiler_params=_cparams(("arbitrary",)),
        name="post",
    )(x2, cb, z, z, z, o, sgc, sga, conv_w, wc, wa, wo, g2, wr)


def _rank_kernel(meta_ref, cnt_ref, l0_ref, l1_ref, runs_ref,
                 carry_ref, pstart_ref, *, tb, blk):
    i = pl.program_id(0)
    lane = lax.broadcasted_iota(jnp.int32, (tb, LANES), 1).astype(F32)
    meta = meta_ref[...]
    oh1 = jnp.where(lane == meta[:, 0:1], 1.0, 0.0)
    oh2 = jnp.where(lane == meta[:, 1:2], 1.0, 0.0)
    c = oh1 + oh2

    @pl.when(i == 0)
    def _():
        cnt = jnp.broadcast_to(cnt_ref[...], (8, LANES))
        nblk = jnp.floor((cnt + (blk - 1)) * (1.0 / blk))
        r = lax.broadcasted_iota(jnp.int32, (LANES, LANES), 0)
        col = lax.broadcasted_iota(jnp.int32, (LANES, LANES), 1)
        upper = jnp.where(r < col, 1.0, 0.0).astype(BF16)
        pblk = jnp.dot(nblk.astype(BF16), upper, preferred_element_type=F32)
        pstart_ref[...] = pblk[0:1, :] * blk
        carry_ref[...] = jnp.zeros(carry_ref.shape, F32)

    r = lax.broadcasted_iota(jnp.int32, (tb, tb), 0)
    col = lax.broadcasted_iota(jnp.int32, (tb, tb), 1)
    lower = jnp.where(col < r, 1.0, 0.0).astype(BF16)
    prefix = jnp.dot(lower, c.astype(BF16), preferred_element_type=F32)
    run_start = carry_ref[...] + pstart_ref[...]
    run_len = jnp.sum(c, axis=0, keepdims=True)

    nchunk = jnp.floor((run_len + (RUN_CHUNK - 1)) * (1.0 / RUN_CHUNK))
    r = lax.broadcasted_iota(jnp.int32, (LANES, LANES), 0)
    col = lax.broadcasted_iota(jnp.int32, (LANES, LANES), 1)
    upper = jnp.where(r < col, 1.0, 0.0).astype(BF16)
    local_start = jnp.dot(jnp.broadcast_to(nchunk, (8, LANES)).astype(BF16), upper,
                          preferred_element_type=F32)[0:1, :] * RUN_CHUNK
    local = prefix + local_start

    digits = jnp.zeros((tb, LANES), F32)
    for j, onehot in enumerate((oh1, oh2)):
        value = jnp.sum(onehot * local, axis=-1, keepdims=True)
        high = jnp.floor(value * (1.0 / 256.0))
        digits = jnp.where(lane == 2 * j, high,
                           jnp.where(lane == 2 * j + 1, value - high * 256.0, digits))
    r = lax.broadcasted_iota(jnp.int32, (tb, tb), 0)
    col = lax.broadcasted_iota(jnp.int32, (tb, tb), 1)
    eye = jnp.where(col == r, 1.0, 0.0).astype(BF16)
    rows = lax.dot_general(digits.astype(BF16), eye, (((0,), (0,)), ((), ())),
                           preferred_element_type=F32)
    for j, out_ref in enumerate((l0_ref, l1_ref)):
        out_ref[...] = (rows[2 * j:2 * j + 1, :] * 256.0
                        + rows[2 * j + 1:2 * j + 2, :]).astype(jnp.int32)
    row = lax.broadcasted_iota(jnp.int32, (8, LANES), 0)
    runs = jnp.where(row == 0, run_start, jnp.where(row == 1, run_len, 0.0))
    runs_ref[...] = runs.astype(jnp.int32)
    carry_ref[...] += run_len


def _rank(meta, cnt, *, blk):
    t = meta.shape[0]
    tb = TB_RANK
    kern = functools.partial(_rank_kernel, tb=tb, blk=blk)
    per_token = pl.BlockSpec((1, tb), lambda i: (0, i))
    return pl.pallas_call(
        kern,
        grid=(t // tb,),
        in_specs=[pl.BlockSpec((tb, LANES), lambda i: (i, 0)),
                  pl.BlockSpec((1, LANES), lambda i: (0, 0))],
        out_specs=[per_token] * 2 + [pl.BlockSpec((8, LANES), lambda i: (i, 0))],
        out_shape=[jax.ShapeDtypeStruct((1, t), jnp.int32)] * 2
                  + [jax.ShapeDtypeStruct((t // tb * 8, LANES), jnp.int32)],
        scratch_shapes=[pltpu.VMEM((1, LANES), F32), pltpu.VMEM((1, LANES), F32)],
        compiler_params=_cparams(("arbitrary",)),
        name="rank",
    )(meta, cnt)


def _dispatch_kernel(local0_sm, local1_sm, full_src_sm, full_dst_sm, nfull_sm,
                     rem_src_sm, rem_dst_sm, rem_len_sm, pad_start_sm, pad_len_sm, nused_sm,
                     u2_ref, xs_ref, buf_ref, zero_ref, sem, zsem,
                     *, tb, blk, nblocks, max_full):
    step = pl.program_id(0)
    nsteps = pl.num_programs(0)
    base = step * tb
    nbits = blk.bit_length() - 1
    slot = step % 2
    tile_buf = buf_ref.at[slot]

    def wait_tile(s):
        pltpu.make_async_copy(_rows(buf_ref.at[s], 0, tb * TOP_K),
                              _rows(xs_ref, 0, tb * TOP_K), sem.at[s]).wait()

    def pad_copies(visit):
        def per_expert(e, carry):
            start, length = pad_start_sm[e], pad_len_sm[e]
            for bit in range(nbits):
                size = 1 << bit
                higher = lax.shift_left(lax.shift_right_logical(length, bit + 1), bit + 1)

                @pl.when((lax.shift_right_logical(length, bit) & 1) == 1)
                def _():
                    visit(pltpu.make_async_copy(
                        zero_ref.at[pl.ds(0, size * ROW_TILE)],
                        _rows(xs_ref, start + higher, size), zsem))
            return carry

        lax.fori_loop(0, N_EXPERTS, per_expert, 0)

        def per_block(b, carry):
            visit(pltpu.make_async_copy(zero_ref, _rows(xs_ref, b * blk, blk), zsem))
            return carry

        lax.fori_loop(nused_sm[0], nblocks, per_block, 0)

    @pl.when(step == 0)
    def _():
        zero_ref[...] = jnp.zeros(zero_ref.shape, F32)
        pad_copies(lambda c: c.start())

    @pl.when(step >= 2)
    def _():
        wait_tile(slot)

    def move(r, carry):
        row = _rows(u2_ref, r)[...]
        _rows(tile_buf, local0_sm[base + r])[...] = row
        _rows(tile_buf, local1_sm[base + r])[...] = row
        return carry

    lax.fori_loop(0, tb, move, 0, unroll=8)

    def full_chunk(c, carry):
        pltpu.make_async_copy(
            _rows(tile_buf, full_src_sm[step * max_full + c], RUN_CHUNK),
            _rows(xs_ref, full_dst_sm[step * max_full + c], RUN_CHUNK),
            sem.at[slot]).start()
        return carry

    lax.fori_loop(0, nfull_sm[step], full_chunk, 0)

    def remainder(e, carry):
        idx = step * N_EXPERTS + e
        src, dst, length = rem_src_sm[idx], rem_dst_sm[idx], rem_len_sm[idx]
        for bit in range(RUN_CHUNK.bit_length() - 1):
            size = 1 << bit
            higher = lax.shift_left(lax.shift_right_logical(length, bit + 1), bit + 1)

            @pl.when((lax.shift_right_logical(length, bit) & 1) == 1)
            def _():
                pltpu.make_async_copy(_rows(tile_buf, src + higher, size),
                                      _rows(xs_ref, dst + higher, size),
                                      sem.at[slot]).start()
        return carry

    lax.fori_loop(0, N_EXPERTS, remainder, 0)

    @pl.when(step == nsteps - 1)
    def _():
        @pl.when(step >= 1)
        def _():
            wait_tile(1 - slot)

        wait_tile(slot)
        pad_copies(lambda c: c.wait())


def _tile_tables(run_start, run_len, *, tb):
    shift = RUN_CHUNK.bit_length() - 1
    nchunk = (run_len + RUN_CHUNK - 1) >> shift
    cum = jnp.cumsum(nchunk, axis=1)
    first = cum - nchunk
    local_start = first * RUN_CHUNK

    def flat_list(count, max_count, value_at):
        ccum = jnp.cumsum(count, axis=1)
        cfirst = (ccum - count)[:, None, :]
        c_idx = jnp.arange(max_count, dtype=jnp.int32)[None, :, None]
        owns = (cfirst <= c_idx) & (c_idx < ccum[:, None, :])
        return [jnp.sum(jnp.where(owns, v[:, None, :] + (c_idx - cfirst) * RUN_CHUNK, 0),
                        axis=2).astype(jnp.int32) for v in value_at], ccum[:, -1]

    max_chunks = tb * TOP_K // RUN_CHUNK + N_EXPERTS
    (chunk_src,), tile_chunks = flat_list(nchunk, max_chunks, [run_start])
    max_full = tb * TOP_K // RUN_CHUNK
    nfull = run_len >> shift
    (full_src, full_dst), tile_full = flat_list(nfull, max_full, [local_start, run_start])
    whole = nfull * RUN_CHUNK
    return dict(
        max_chunks=max_chunks, chunk_src=chunk_src.reshape(-1),
        tile_chunks=tile_chunks.astype(jnp.int32),
        max_full=max_full, full_src=full_src.reshape(-1), full_dst=full_dst.reshape(-1),
        tile_full=tile_full.astype(jnp.int32),
        rem_src=(local_start + whole).reshape(-1).astype(jnp.int32),
        rem_dst=(run_start + whole).reshape(-1).astype(jnp.int32),
        rem_len=(run_len - whole).reshape(-1).astype(jnp.int32))


def _dispatch(local0, local1, tables, pad_start, pad_len, nused, u2, *, p_rows, blk):
    t = u2.shape[0] // ROW_TILE
    tb = TB_RANK
    d_model = ROW_TILE * LANES
    kern = functools.partial(_dispatch_kernel, tb=tb, blk=blk, nblocks=p_rows // blk,
                             max_full=tables["max_full"])
    buf_rows = tables["max_chunks"] * RUN_CHUNK
    return pl.pallas_call(
        kern,
        grid_spec=pltpu.PrefetchScalarGridSpec(
            num_scalar_prefetch=11,
            grid=(t // tb,),
            in_specs=[pl.BlockSpec((tb * ROW_TILE, LANES), lambda i, *_: (i, 0))],
            out_specs=pl.BlockSpec(memory_space=pl.ANY),
            scratch_shapes=[pltpu.VMEM((2,) + _row_tiles((buf_rows, d_model)), F32),
                            pltpu.VMEM((blk * ROW_TILE, LANES), F32),
                            pltpu.SemaphoreType.DMA((2,)), pltpu.SemaphoreType.DMA]),
        out_shape=jax.ShapeDtypeStruct((p_rows * ROW_TILE, LANES), F32),
        compiler_params=_cparams(("arbitrary",)),
        name="dispatch",
    )(local0, local1, tables["full_src"], tables["full_dst"], tables["tile_full"],
      tables["rem_src"], tables["rem_dst"], tables["rem_len"],
      pad_start, pad_len, nused, u2)


def _experts_kernel(be_sm, first_sm, next_sm, slot_sm, nused_sm,
                    xs_ref, w1_hbm, w3_hbm, w2_hbm, ys_ref,
                    w1f_ref, w3f_ref, w2f_ref, w13b_ref, w2b_ref, wsem, *, d_expert):
    b = pl.program_id(0)
    used = b < nused_sm[0]

    def weight_copies(expert, slot):
        return [pltpu.make_async_copy(src.at[expert], dst.at[slot], wsem.at[slot])
                for src, dst in ((w1_hbm, w1f_ref), (w3_hbm, w3f_ref), (w2_hbm, w2f_ref))]

    @pl.when(b == 0)
    def _():
        for c in weight_copies(be_sm[0], slot_sm[0]):
            c.start()

    @pl.when(first_sm[b] == 1)
    def _():
        slot = slot_sm[b]
        for c in weight_copies(be_sm[b], slot):
            c.wait()

        @pl.when(next_sm[b] >= 0)
        def _():
            for c in weight_copies(next_sm[b], 1 - slot):
                c.start()

        w13b_ref[:, :d_expert] = w1f_ref[slot].astype(BF16)
        w13b_ref[:, d_expert:] = w3f_ref[slot].astype(BF16)
        w2b_ref[...] = w2f_ref[slot].astype(BF16)

    @pl.when(used)
    def _():
        xb = _load_row_tiles(xs_ref).astype(BF16)
        h = jnp.dot(xb, w13b_ref[...], preferred_element_type=F32)
        h1 = h[:, :d_expert]
        h3 = h[:, d_expert:]
        a = (h1 * _sigmoid(h1) * h3).astype(BF16)
        _store_row_tiles(ys_ref, jnp.dot(a, w2b_ref[...], preferred_element_type=F32))

    @pl.when(jnp.logical_not(used))
    def _():
        ys_ref[...] = jnp.zeros(ys_ref.shape, F32)


def _experts(blk_expert, blk_first, blk_next, blk_slot, nused, xs, w1, w3, w2, *, blk):
    p_rows = xs.shape[0] // ROW_TILE
    d_model, d_expert = w1.shape[-2:]
    kern = functools.partial(_experts_kernel, d_expert=d_expert)
    rows_blk = (blk * ROW_TILE, LANES)

    def rows_in(b, be, first, nxt, slot, nu):
        return (jnp.maximum(jnp.minimum(b, nu[0] - 1), 0), 0)

    hbm = pl.BlockSpec(memory_space=pl.ANY)
    return pl.pallas_call(
        kern,
        grid_spec=pltpu.PrefetchScalarGridSpec(
            num_scalar_prefetch=5,
            grid=(p_rows // blk,),
            in_specs=[pl.BlockSpec(rows_blk, rows_in), hbm, hbm, hbm],
            out_specs=pl.BlockSpec(rows_blk, lambda b, *_: (b, 0)),
            scratch_shapes=[pltpu.VMEM((2, d_model, d_expert), F32),
                            pltpu.VMEM((2, d_model, d_expert), F32),
                            pltpu.VMEM((2, d_expert, d_model), F32),
                            pltpu.VMEM((d_model, 2 * d_expert), BF16),
                            pltpu.VMEM((d_expert, d_model), BF16),
                            pltpu.SemaphoreType.DMA((2,))]),
        out_shape=jax.ShapeDtypeStruct((p_rows * ROW_TILE, LANES), F32),
        compiler_params=_cparams(("arbitrary",)),
        name="experts",
    )(blk_expert, blk_first, blk_next, blk_slot, nused, xs, w1, w3, w2)


def _combine_kernel(local0_sm, local1_sm, chunk_src_sm, nchunk_sm,
                    x1_ref, meta_ref, ys_ref, out_ref,
                    buf_ref, g0_ref, g1_ref, sem, *, tb, max_chunks):
    step = pl.program_id(0)
    nsteps = pl.num_programs(0)

    def start_chunks(tile):
        slot = tile % 2

        def per_chunk(c, carry):
            pltpu.make_async_copy(
                _rows(ys_ref, chunk_src_sm[tile * max_chunks + c], RUN_CHUNK),
                _rows(buf_ref.at[slot], c * RUN_CHUNK, RUN_CHUNK),
                sem.at[slot]).start()
            return carry

        lax.fori_loop(0, nchunk_sm[tile], per_chunk, 0)

    def wait_chunks(tile):
        slot = tile % 2
        n = nchunk_sm[tile]
        for bit in range(max_chunks.bit_length()):
            @pl.when((lax.shift_right_logical(n, bit) & 1) == 1)
            def _():
                rows = RUN_CHUNK << bit
                pltpu.make_async_copy(_rows(ys_ref, 0, rows),
                                      _rows(buf_ref.at[slot], 0, rows),
                                      sem.at[slot]).wait()

    @pl.when(step == 0)
    def _():
        start_chunks(step)

    @pl.when(step + 1 < nsteps)
    def _():
        start_chunks(step + 1)

    wait_chunks(step)

    base = step * tb
    tile_buf = buf_ref.at[step % 2]

    def move(r, carry):
        for g_ref, local_sm in ((g0_ref, local0_sm), (g1_ref, local1_sm)):
            _rows(g_ref, r)[...] = _rows(tile_buf, local_sm[base + r])[...]
        return carry

    lax.fori_loop(0, tb, move, 0, unroll=8)
    meta = meta_ref[...]
    moe = (_load_row_tiles(g0_ref) * meta[:, 2:3]
           + _load_row_tiles(g1_ref) * meta[:, 3:4])
    out_ref[...] = x1_ref[...] + moe


def _combine(local0, local1, tables, x1, meta, ys):
    t, d_model = x1.shape
    tb = TB_RANK
    max_chunks = tables["max_chunks"]
    kern = functools.partial(_combine_kernel, tb=tb, max_chunks=max_chunks)
    buf_rows = max_chunks * RUN_CHUNK
    return pl.pallas_call(
        kern,
        grid_spec=pltpu.PrefetchScalarGridSpec(
            num_scalar_prefetch=4,
            grid=(t // tb,),
            in_specs=[pl.BlockSpec((tb, d_model), lambda i, *_: (i, 0)),
                      pl.BlockSpec((tb, LANES), lambda i, *_: (i, 0)),
                      pl.BlockSpec(memory_space=pl.ANY)],
            out_specs=pl.BlockSpec((tb, d_model), lambda i, *_: (i, 0)),
            scratch_shapes=[pltpu.VMEM((2,) + _row_tiles((buf_rows, d_model)), F32),
                            pltpu.VMEM(_row_tiles((tb, d_model)), F32),
                            pltpu.VMEM(_row_tiles((tb, d_model)), F32),
                            pltpu.SemaphoreType.DMA((2,))]),
        out_shape=jax.ShapeDtypeStruct((t, d_model), F32),
        compiler_params=_cparams(("arbitrary",)),
        name="combine",
    )(local0, local1, tables["chunk_src"], tables["tile_chunks"], x1, meta, ys)


def _layer(h2, *, batch, seq, norm1_g, w_in, conv_w, q_norm_g, k_norm_g,
           w_conv_out, w_attn_out, w_o, norm2_g, w_group, w_router, w1, w3, w2):
    t, d_model = h2.shape
    c, sa, sb = _rope_tables(seq)
    scale = HEAD_DIM ** -0.5 * LOG2_E
    tables_q = tuple(jnp.asarray(tab * np.float32(scale)) for tab in (c, sa, sb))
    tables_k = tuple(jnp.asarray(tab) for tab in (c, sa, sb))

    cb, z, q, k, v, sgc, sga = _inproj(
        h2, norm1_g[None, :], w_in.astype(BF16), q_norm_g[None, :], k_norm_g[None, :],
        tables_q, tables_k, seq=seq)
    o = _attention(q, k, v, batch=batch, seq=seq)

    n_route = N_GROUPS + N_EXPERTS
    wr = jnp.concatenate(
        [w_group, w_router, jnp.zeros((d_model, LANES - n_route), F32)], axis=1)
    wr_hi = wr.astype(BF16)
    wr = jnp.concatenate([wr_hi, (wr - wr_hi.astype(F32)).astype(BF16)], axis=1)
    x1, u2, meta, cnt = _post(h2, cb, z, o, sgc, sga, conv_w,
                              w_conv_out.astype(BF16), w_attn_out.astype(BF16),
                              w_o.astype(BF16), norm2_g[None, :], wr, seq=seq)

    blk = MOE_ROWS
    local0, local1, runs = _rank(meta, cnt, blk=blk)
    local0, local1 = local0.reshape(-1), local1.reshape(-1)
    runs = runs.reshape(-1, 8, LANES)
    tables = _tile_tables(runs[:, 0, :N_EXPERTS], runs[:, 1, :N_EXPERTS], tb=TB_RANK)
    counts = cnt[0, :N_EXPERTS].astype(jnp.int32)
    p_rows = t * TOP_K + N_EXPERTS * blk
    nb = p_rows // blk
    nblk = (counts + blk - 1) // blk
    pend = jnp.cumsum(nblk)
    nused = pend[-1]
    barange = jnp.arange(nb, dtype=jnp.int32)
    bidx = jnp.minimum(barange, nused - 1)
    blk_expert = jnp.minimum(jnp.sum(pend[None, :] <= bidx[:, None], axis=1),
                             N_EXPERTS - 1).astype(jnp.int32)
    live = barange < nused
    blk_first = (live & (barange == (pend - nblk)[blk_expert])).astype(jnp.int32)
    after = pend[blk_expert]
    blk_next = jnp.where(live & (after < nused),
                         blk_expert[jnp.minimum(after, nb - 1)], -1).astype(jnp.int32)
    blk_slot = ((jnp.cumsum(nblk > 0) - 1)[blk_expert] % 2).astype(jnp.int32)
    pad_start = ((pend - nblk) * blk + counts).astype(jnp.int32)
    pad_len = (nblk * blk - counts).astype(jnp.int32)
    nused1 = nused.reshape(1).astype(jnp.int32)

    xs = _dispatch(local0, local1, tables, pad_start, pad_len, nused1, u2,
                   p_rows=p_rows, blk=blk)
    ys = _experts(blk_expert, blk_first, blk_next, blk_slot, nused1, xs, w1, w3, w2,
                  blk=blk)
    return _combine(local0, local1, tables, x1, meta, ys)


def kernel(x, norm1_g, w_in, conv_w, q_norm_g, k_norm_g, w_conv_out, w_attn_out, w_o,
           norm2_g, w_group, w_router, w1, w3, w2):
    batch, seq, d_model = x.shape
    h2 = x.reshape(batch * seq, d_model)
    for l in range(norm1_g.shape[0]):
        h2 = _layer(h2, batch=batch, seq=seq, norm1_g=norm1_g[l], w_in=w_in[l],
                    conv_w=conv_w[l], q_norm_g=q_norm_g[l], k_norm_g=k_norm_g[l],
                    w_conv_out=w_conv_out[l], w_attn_out=w_attn_out[l], w_o=w_o[l],
                    norm2_g=norm2_g[l], w_group=w_group[l], w_router=w_router[l],
                    w1=w1[l], w3=w3[l], w2=w2[l])
    return h2.reshape(batch, seq, d_model)
```

```python
import functools

import jax
import jax.numpy as jnp
import numpy as np
from jax import lax
from jax.experimental import pallas as pl
from jax.experimental.pallas import tpu as pltpu

F32 = jnp.float32
BF16 = jnp.bfloat16

GRID_W = 64
EPS = 1e-6
N_HEADS = 8
N_KV_HEADS = 2
HEAD_DIM = 128
ROPE_THETA = 10000.0
N_GROUPS = 8
EXPERTS_PER_GROUP = 8
N_EXPERTS = N_GROUPS * EXPERTS_PER_GROUP
TOP_K = 2
LOG2_E = 1.4426950408889634

LANES = 128
MXU_DIM = 256
V7X_VMEM_LIMIT_BYTES = 56000 * 1024

TM_PROJ = 512
POST_SPLIT = 1
TQ_ATTN = 512
TK_ATTN = 512
TB_RANK = 512
TB_ROWS = 256
RUN_CHUNK = 8
WEIGHT_RING = 3
MOE_ROWS = 256
BF16_SUBLANES = 16
F32_SUBLANES = 8


def _cparams(sem):
    return pltpu.CompilerParams(dimension_semantics=sem,
                                vmem_limit_bytes=V7X_VMEM_LIMIT_BYTES)


ROW_TILE = 8


def _row_tiles(shape2d):
    rows, width = shape2d
    assert width == ROW_TILE * LANES
    return (rows * ROW_TILE, LANES)


def _rows(ref, r, n=1):
    return ref.at[pl.ds(pl.multiple_of(r * ROW_TILE, ROW_TILE), n * ROW_TILE)]


def _load_row_tiles(ref):
    rows = ref.shape[0] // ROW_TILE
    return jnp.concatenate(
        [ref[pl.ds(s, rows, stride=ROW_TILE), :] for s in range(ROW_TILE)], axis=1)


def _store_row_tiles(ref, value, row0=0):
    rows = value.shape[0]
    for s in range(ROW_TILE):
        ref[pl.ds(row0 * ROW_TILE + s, rows, stride=ROW_TILE), :] = (
            value[:, s * LANES:(s + 1) * LANES])


def _resident(shape):
    nd = len(shape)
    return pl.BlockSpec(shape, lambda *_: (0,) * nd, pipeline_mode=pl.Buffered(1))


def _head_norm_rope(xh, g, c, sa, sb):
    ms = jnp.mean(xh * xh, axis=-1, keepdims=True)
    y = xh * lax.rsqrt(ms + EPS) * g
    y_next = pltpu.roll(y, HEAD_DIM - 1, axis=1)
    y_prev = pltpu.roll(y, 1, axis=1)
    return y * c + y_next * sa + y_prev * sb


def _sigmoid(x):
    return 0.5 * jnp.tanh(0.5 * x) + 0.5


def _inproj_kernel(x_ref, g1_ref, w_ref, gq_ref, gk_ref,
                   cq_ref, saq_ref, sbq_ref, ck_ref, sak_ref, sbk_ref,
                   cb_ref, z_ref, q_ref, k_ref, v_ref, sgc_ref, sga_ref,
                   *, d_conv, d_q, d_kv, d_model):
    x = x_ref[...]
    ms = jnp.mean(x * x, axis=-1, keepdims=True)
    u = (x * lax.rsqrt(ms + EPS) * g1_ref[...]).astype(BF16)

    def proj(lo, width):
        return jnp.dot(u, w_ref[:, lo:lo + width], preferred_element_type=F32)

    o_cb, o_cc, o_cx = 0, d_conv, 2 * d_conv
    o_q = 3 * d_conv
    o_k = o_q + d_q
    o_v = o_k + d_kv
    o_gc = o_v + d_kv
    o_ga = o_gc + d_model

    sgc_ref[...] = _sigmoid(proj(o_gc, d_model)).astype(BF16)
    sga_ref[...] = _sigmoid(proj(o_ga, d_model)).astype(BF16)

    q = proj(o_q, d_q)
    gq = gq_ref[...]
    cq, saq, sbq = cq_ref[...], saq_ref[...], sbq_ref[...]
    for h in range(d_q // HEAD_DIM):
        sl = slice(h * HEAD_DIM, (h + 1) * HEAD_DIM)
        q_ref[:, sl] = _head_norm_rope(q[:, sl], gq, cq, saq, sbq).astype(BF16)

    k = proj(o_k, d_kv)
    gk = gk_ref[...]
    ck, sak, sbk = ck_ref[...], sak_ref[...], sbk_ref[...]
    for h in range(d_kv // HEAD_DIM):
        sl = slice(h * HEAD_DIM, (h + 1) * HEAD_DIM)
        k_ref[:, sl] = _head_norm_rope(k[:, sl], gk, ck, sak, sbk).astype(BF16)

    z_ref[...] = (proj(o_cc, d_conv) * proj(o_cx, d_conv)).astype(BF16)
    v_ref[...] = proj(o_v, d_kv).astype(BF16)
    cb_ref[...] = proj(o_cb, d_conv).astype(BF16)


def _rope_tables(seq):
    rows = seq // GRID_W
    axis_dim = HEAD_DIM // 2
    row = np.repeat(np.arange(rows, dtype=np.float32), GRID_W)
    col = np.tile(np.arange(GRID_W, dtype=np.float32), rows)
    inv = (np.float32(ROPE_THETA)
           ** (-np.arange(0, axis_dim, 2, dtype=np.float32) / np.float32(axis_dim)))
    ang = np.concatenate([row[:, None] * inv, col[:, None] * inv], axis=-1)
    ang = ang.astype(np.float32)
    cos, sin = np.cos(ang), np.sin(ang)
    zero = np.zeros_like(sin)
    c = np.repeat(cos, 2, axis=-1)
    sa = np.stack([-sin, zero], axis=-1).reshape(seq, HEAD_DIM)
    sb = np.stack([zero, sin], axis=-1).reshape(seq, HEAD_DIM)
    return c, sa, sb


def _inproj(x2, g1, w_in_bf, gq, gk, tables_q, tables_k, *, seq):
    t, d_model = x2.shape
    d_q = N_HEADS * HEAD_DIM
    d_kv = N_KV_HEADS * HEAD_DIM
    d_in = w_in_bf.shape[1]
    d_conv = (d_in - d_q - 2 * d_kv - 2 * d_model) // 3
    tm = TM_PROJ
    nseq = seq // tm

    def row(width):
        return pl.BlockSpec((tm, width), lambda i: (i, 0))

    table = pl.BlockSpec((tm, HEAD_DIM), lambda i: (i % nseq, 0))
    kern = functools.partial(_inproj_kernel, d_conv=d_conv, d_q=d_q, d_kv=d_kv,
                             d_model=d_model)
    out_shape = [jax.ShapeDtypeStruct((t, w), BF16)
                 for w in (d_conv, d_conv, d_q, d_kv, d_kv, d_model, d_model)]
    return pl.pallas_call(
        kern,
        grid=(t // tm,),
        in_specs=[row(d_model), _resident((1, d_model)), _resident((d_model, d_in)),
                  _resident((1, HEAD_DIM)), _resident((1, HEAD_DIM)),
                  table, table, table, table, table, table],
        out_specs=[row(d_conv), row(d_conv), row(d_q), row(d_kv), row(d_kv),
                   row(d_model), row(d_model)],
        out_shape=out_shape,
        compiler_params=_cparams(("arbitrary",)),
        name="inproj",
    )(x2, g1, w_in_bf, gq, gk, *tables_q, *tables_k)


def _attn_kernel(q_ref, k_ref, v_ref, o_ref, qs_ref, vext_ref, m_ref, acc_ref,
                 *, tq, chunks, group):
    @pl.when(pl.program_id(2) == 0)
    def _():
        vext_ref[:, :HEAD_DIM] = v_ref[...]
        vext_ref[:, HEAD_DIM:] = jnp.ones((vext_ref.shape[0], HEAD_DIM), BF16)

    for g in range(group):
        qs_ref[g * tq:(g + 1) * tq, :] = q_ref[:, g * HEAD_DIM:(g + 1) * HEAD_DIM]
    m_ref[...] = jnp.full(m_ref.shape, -jnp.inf, F32)
    acc_ref[...] = jnp.zeros(acc_ref.shape, F32)

    lo = 0
    for tk in chunks:
        keys = slice(lo, lo + tk)
        lo += tk
        s = lax.dot_general(qs_ref[...], k_ref[keys, :], (((1,), (1,)), ((), ())),
                            preferred_element_type=F32)
        m_prev = m_ref[...]
        m_new = jnp.maximum(m_prev, jnp.max(s, axis=-1, keepdims=True))
        alpha = jnp.exp2(m_prev - m_new)
        p = jnp.concatenate(
            [jnp.exp2(s[:, c * LANES:(c + 1) * LANES] - m_new) for c in range(tk // LANES)],
            axis=1).astype(BF16)
        pv = jnp.dot(p, vext_ref[keys, :], preferred_element_type=F32)
        acc_ref[...] = jnp.concatenate([alpha, alpha], axis=1) * acc_ref[...] + pv
        m_ref[...] = m_new

    out = acc_ref[:, :HEAD_DIM] / acc_ref[:, HEAD_DIM:]
    for g in range(group):
        o_ref[:, g * HEAD_DIM:(g + 1) * HEAD_DIM] = out[g * tq:(g + 1) * tq].astype(BF16)


def _attention(q, k, v, *, batch, seq):
    t = q.shape[0]
    group = N_HEADS // N_KV_HEADS
    tq, tk = TQ_ATTN, TK_ATTN
    nq = seq // tq
    gw = group * HEAD_DIM
    chunks = (tk // 2,) + (tk,) * (seq // tk - 1) + (tk // 2,)
    kern = functools.partial(_attn_kernel, tq=tq, chunks=chunks, group=group)
    return pl.pallas_call(
        kern,
        grid=(batch, N_KV_HEADS, nq),
        in_specs=[pl.BlockSpec((tq, gw), lambda b, h, i: (b * nq + i, h)),
                  pl.BlockSpec((seq, HEAD_DIM), lambda b, h, i: (b, h)),
                  pl.BlockSpec((seq, HEAD_DIM), lambda b, h, i: (b, h))],
        out_specs=pl.BlockSpec((tq, gw), lambda b, h, i: (b * nq + i, h)),
        out_shape=jax.ShapeDtypeStruct((t, N_HEADS * HEAD_DIM), BF16),
        scratch_shapes=[pltpu.VMEM((group * tq, HEAD_DIM), BF16),
                        pltpu.VMEM((seq, 2 * HEAD_DIM), BF16),
                        pltpu.VMEM((group * tq, LANES), F32),
                        pltpu.VMEM((group * tq, 2 * HEAD_DIM), F32)],
        compiler_params=_cparams(("arbitrary", "arbitrary", "arbitrary")),
        name="attention",
    )(q, k, v)


def _route(logits):
    rows = logits.shape[0]
    lane = lax.broadcasted_iota(jnp.int32, (rows, LANES), 1).astype(F32)
    neg = -jnp.inf
    big = float(2 * LANES)
    is_group = lane < N_GROUPS
    gl = jnp.where(is_group, logits, neg)
    gmax = jnp.max(gl, axis=-1, keepdims=True)
    gidx = jnp.min(jnp.where(gl == gmax, lane, big), axis=-1, keepdims=True)
    gsum = jnp.sum(jnp.where(is_group, jnp.exp(logits - gmax), 0.0), axis=-1,
                   keepdims=True)
    pg = 1.0 / gsum
    lane_group = jnp.floor(lane * (1.0 / EXPERTS_PER_GROUP)) - 1.0
    mine = (lane_group == gidx) & (lane >= N_GROUPS) & (lane < N_GROUPS + N_EXPERTS)
    sel = jnp.where(mine, logits, neg)
    v1 = jnp.max(sel, axis=-1, keepdims=True)
    i1 = jnp.min(jnp.where(sel == v1, lane, big), axis=-1, keepdims=True)
    sel2 = jnp.where(lane == i1, neg, sel)
    v2 = jnp.max(sel2, axis=-1, keepdims=True)
    i2 = jnp.min(jnp.where(sel2 == v2, lane, big), axis=-1, keepdims=True)
    t2 = jnp.exp(v2 - v1)
    den = 1.0 + t2
    wgt1 = pg * (1.0 / den)
    wgt2 = pg * (t2 / den)
    e1 = i1 - N_GROUPS
    e2 = i2 - N_GROUPS
    meta = jnp.where(lane == 0, e1,
                     jnp.where(lane == 1, e2,
                               jnp.where(lane == 2, wgt1,
                                         jnp.where(lane == 3, wgt2, 0.0))))
    picked = jnp.where((lane == e1) | (lane == e2), 1.0, 0.0)
    return meta, jnp.sum(picked, axis=0, keepdims=True)


def _post_kernel(x_ref, cb_ref, z_ref, zprev_ref, znext_ref, o_ref, sgc_ref, sga_ref,
                 cw_ref, wc_ref, wa_ref, wo_ref, g2_ref, wr_ref,
                 x1_ref, u2_ref, meta_ref, cnt_ref, *, tm, nseq, nsplit):
    i = pl.program_id(0)
    at_start = (i % nseq) == 0
    at_end = (i % nseq) == nseq - 1
    rows = tm // nsplit
    rowid = lax.broadcasted_iota(jnp.int32, (rows, 1), 0)

    @pl.when(i == 0)
    def _():
        cnt_ref[...] = jnp.zeros(cnt_ref.shape, F32)

    for part_id in range(nsplit):
        r0 = part_id * rows
        rs = slice(r0, r0 + rows)
        y_attn = jnp.dot(o_ref[rs, :], wa_ref[...], preferred_element_type=F32)

        y_conv = None
        for c in range(z_ref.shape[1] // MXU_DIM):
            sl = slice(c * MXU_DIM, (c + 1) * MXU_DIM)
            z = z_ref[rs, sl].astype(F32)
            if part_id == 0:
                prev_row = zprev_ref[BF16_SUBLANES - 1:BF16_SUBLANES, sl].astype(F32)
                prev_row = jnp.where(at_start, 0.0, prev_row)
            else:
                prev_row = z_ref[r0 - 1:r0, sl].astype(F32)
            if part_id == nsplit - 1:
                next_row = znext_ref[0:1, sl].astype(F32)
                next_row = jnp.where(at_end, 0.0, next_row)
            else:
                next_row = z_ref[r0 + rows:r0 + rows + 1, sl].astype(F32)
            zp = jnp.where(rowid == 0, prev_row, pltpu.roll(z, 1, axis=0))
            zn = jnp.where(rowid == rows - 1, next_row, pltpu.roll(z, rows - 1, axis=0))
            conv = cw_ref[0:1, sl] * zp + cw_ref[1:2, sl] * z + cw_ref[2:3, sl] * zn
            cbz = (cb_ref[rs, sl].astype(F32) * conv).astype(BF16)
            part = jnp.dot(cbz, wc_ref[sl, :], preferred_element_type=F32)
            y_conv = part if y_conv is None else y_conv + part
        merged = (sgc_ref[rs, :].astype(F32) * y_conv
                  + sga_ref[rs, :].astype(F32) * y_attn).astype(BF16)
        x1 = x_ref[rs, :] + jnp.dot(merged, wo_ref[...], preferred_element_type=F32)
        x1_ref[rs, :] = x1

        ms = jnp.mean(x1 * x1, axis=-1, keepdims=True)
        u2 = x1 * lax.rsqrt(ms + EPS) * g2_ref[...]
        _store_row_tiles(u2_ref, u2, r0)

        u2_hi = u2.astype(BF16)
        u2_lo = (u2 - u2_hi.astype(F32)).astype(BF16)
        hi_part = jnp.dot(u2_hi, wr_ref[...], preferred_element_type=F32)
        lo_part = jnp.dot(u2_lo, wr_ref[:, :LANES], preferred_element_type=F32)
        logits = hi_part[:, :LANES] + (hi_part[:, LANES:] + lo_part)
        meta, picked = _route(logits)
        meta_ref[rs, :] = meta
        cnt_ref[...] += picked


def _post(x2, cb, z, o, sgc, sga, conv_w, wc, wa, wo, g2, wr, *, seq):
    t, d_model = x2.shape
    tm = TM_PROJ
    nseq = seq // tm
    hb = tm // BF16_SUBLANES
    nhalo = t // BF16_SUBLANES
    d_conv = cb.shape[1]
    d_q = o.shape[1]

    def row(width):
        return pl.BlockSpec((tm, width), lambda i: (i, 0))

    kern = functools.partial(_post_kernel, tm=tm, nseq=nseq, nsplit=POST_SPLIT)
    return pl.pallas_call(
        kern,
        grid=(t // tm,),
        in_specs=[row(d_model), row(d_conv), row(d_conv),
                  pl.BlockSpec((BF16_SUBLANES, d_conv),
                               lambda i: (jnp.maximum(i * hb - 1, 0), 0)),
                  pl.BlockSpec((BF16_SUBLANES, d_conv),
                               lambda i: (jnp.minimum((i + 1) * hb, nhalo - 1), 0)),
                  row(d_q), row(d_model), row(d_model),
                  _resident(conv_w.shape), _resident(wc.shape), _resident(wa.shape),
                  _resident(wo.shape), _resident(g2.shape), _resident(wr.shape)],
        out_specs=[row(d_model),
                   pl.BlockSpec(_row_tiles((tm, d_model)), lambda i: (i, 0)),
                   row(LANES),
                   pl.BlockSpec((1, LANES), lambda i: (0, 0))],
        out_shape=[jax.ShapeDtypeStruct((t, d_model), F32),
                   jax.ShapeDtypeStruct(_row_tiles((t, d_model)), F32),
                   jax.ShapeDtypeStruct((t, LANES), F32),
                   jax.ShapeDtypeStruct((1, LANES), F32)],
        compiler_params=_cparams(("arbitrary",)),
        name="post",
    )(x2, cb, z, z, z, o, sgc, sga, conv_w, wc, wa, wo, g2, wr)


def _rank_kernel(meta_ref, cnt_ref, l0_ref, l1_ref, runs_ref,
                 carry_ref, pstart_ref, *, tb, blk):
    i = pl.program_id(0)
    lane = lax.broadcasted_iota(jnp.int32, (tb, LANES), 1).astype(F32)
    meta = meta_ref[...]
    oh1 = jnp.where(lane == meta[:, 0:1], 1.0, 0.0)
    oh2 = jnp.where(lane == meta[:, 1:2], 1.0, 0.0)
    c = oh1 + oh2

    @pl.when(i == 0)
    def _():
        cnt = jnp.broadcast_to(cnt_ref[...], (8, LANES))
        nblk = jnp.floor((cnt + (blk - 1)) * (1.0 / blk))
        r = lax.broadcasted_iota(jnp.int32, (LANES, LANES), 0)
        col = lax.broadcasted_iota(jnp.int32, (LANES, LANES), 1)
        upper = jnp.where(r < col, 1.0, 0.0).astype(BF16)
        pblk = jnp.dot(nblk.astype(BF16), upper, preferred_element_type=F32)
        pstart_ref[...] = pblk[0:1, :] * blk
        carry_ref[...] = jnp.zeros(carry_ref.shape, F32)

    r = lax.broadcasted_iota(jnp.int32, (tb, tb), 0)
    col = lax.broadcasted_iota(jnp.int32, (tb, tb), 1)
    lower = jnp.where(col < r, 1.0, 0.0).astype(BF16)
    prefix = jnp.dot(lower, c.astype(BF16), preferred_element_type=F32)
    run_start = carry_ref[...] + pstart_ref[...]
    run_len = jnp.sum(c, axis=0, keepdims=True)

    nchunk = jnp.floor((run_len + (RUN_CHUNK - 1)) * (1.0 / RUN_CHUNK))
    r = lax.broadcasted_iota(jnp.int32, (LANES, LANES), 0)
    col = lax.broadcasted_iota(jnp.int32, (LANES, LANES), 1)
    upper = jnp.where(r < col, 1.0, 0.0).astype(BF16)
    local_start = jnp.dot(jnp.broadcast_to(nchunk, (8, LANES)).astype(BF16), upper,
                          preferred_element_type=F32)[0:1, :] * RUN_CHUNK
    local = prefix + local_start

    digits = jnp.zeros((tb, LANES), F32)
    for j, onehot in enumerate((oh1, oh2)):
        value = jnp.sum(onehot * local, axis=-1, keepdims=True)
        high = jnp.floor(value * (1.0 / 256.0))
        digits = jnp.where(lane == 2 * j, high,
                           jnp.where(lane == 2 * j + 1, value - high * 256.0, digits))
    r = lax.broadcasted_iota(jnp.int32, (tb, tb), 0)
    col = lax.broadcasted_iota(jnp.int32, (tb, tb), 1)
    eye = jnp.where(col == r, 1.0, 0.0).astype(BF16)
    rows = lax.dot_general(digits.astype(BF16), eye, (((0,), (0,)), ((), ())),
                           preferred_element_type=F32)
    for j, out_ref in enumerate((l0_ref, l1_ref)):
        out_ref[...] = (rows[2 * j:2 * j + 1, :] * 256.0
                        + rows[2 * j + 1:2 * j + 2, :]).astype(jnp.int32)
    row = lax.broadcasted_iota(jnp.int32, (8, LANES), 0)
    runs = jnp.where(row == 0, run_start, jnp.where(row == 1, run_len, 0.0))
    runs_ref[...] = runs.astype(jnp.int32)
    carry_ref[...] += run_len


def _rank(meta, cnt, *, blk):
    t = meta.shape[0]
    tb = TB_RANK
    kern = functools.partial(_rank_kernel, tb=tb, blk=blk)
    per_token = pl.BlockSpec((1, tb), lambda i: (0, i))
    return pl.pallas_call(
        kern,
        grid=(t // tb,),
        in_specs=[pl.BlockSpec((tb, LANES), lambda i: (i, 0)),
                  pl.BlockSpec((1, LANES), lambda i: (0, 0))],
        out_specs=[per_token] * 2 + [pl.BlockSpec((8, LANES), lambda i: (i, 0))],
        out_shape=[jax.ShapeDtypeStruct((1, t), jnp.int32)] * 2
                  + [jax.ShapeDtypeStruct((t // tb * 8, LANES), jnp.int32)],
        scratch_shapes=[pltpu.VMEM((1, LANES), F32), pltpu.VMEM((1, LANES), F32)],
        compiler_params=_cparams(("arbitrary",)),
        name="rank",
    )(meta, cnt)


def _dispatch_kernel(local0_sm, local1_sm, full_src_sm, full_dst_sm, nfull_sm,
                     rem_src_sm, rem_dst_sm, rem_len_sm, pad_start_sm, pad_len_sm, nused_sm,
                     u2_ref, xs_ref, buf_ref, zero_ref, sem, zsem,
                     *, tb, blk, nblocks, max_full):
    step = pl.program_id(0)
    nsteps = pl.num_programs(0)
    base = step * tb
    nbits = blk.bit_length() - 1
    slot = step % 2
    tile_buf = buf_ref.at[slot]

    def wait_tile(s):
        pltpu.make_async_copy(_rows(buf_ref.at[s], 0, tb * TOP_K),
                              _rows(xs_ref, 0, tb * TOP_K), sem.at[s]).wait()

    def pad_copies(visit):
        def per_expert(e, carry):
            start, length = pad_start_sm[e], pad_len_sm[e]
            for bit in range(nbits):
                size = 1 << bit
                higher = lax.shift_left(lax.shift_right_logical(length, bit + 1), bit + 1)

                @pl.when((lax.shift_right_logical(length, bit) & 1) == 1)
                def _():
                    visit(pltpu.make_async_copy(
                        zero_ref.at[pl.ds(0, size * ROW_TILE)],
                        _rows(xs_ref, start + higher, size), zsem))
            return carry

        lax.fori_loop(0, N_EXPERTS, per_expert, 0)

        def per_block(b, carry):
            visit(pltpu.make_async_copy(zero_ref, _rows(xs_ref, b * blk, blk), zsem))
            return carry

        lax.fori_loop(nused_sm[0], nblocks, per_block, 0)

    @pl.when(step == 0)
    def _():
        zero_ref[...] = jnp.zeros(zero_ref.shape, F32)
        pad_copies(lambda c: c.start())

    @pl.when(step >= 2)
    def _():
        wait_tile(slot)

    def move(r, carry):
        row = _rows(u2_ref, r)[...]
        _rows(tile_buf, local0_sm[base + r])[...] = row
        _rows(tile_buf, local1_sm[base + r])[...] = row
        return carry

    lax.fori_loop(0, tb, move, 0, unroll=8)

    def full_chunk(c, carry):
        pltpu.make_async_copy(
            _rows(tile_buf, full_src_sm[step * max_full + c], RUN_CHUNK),
            _rows(xs_ref, full_dst_sm[step * max_full + c], RUN_CHUNK),
            sem.at[slot]).start()
        return carry

    lax.fori_loop(0, nfull_sm[step], full_chunk, 0)

    def remainder(e, carry):
        idx = step * N_EXPERTS + e
        src, dst, length = rem_src_sm[idx], rem_dst_sm[idx], rem_len_sm[idx]
        for bit in range(RUN_CHUNK.bit_length() - 1):
            size = 1 << bit
            higher = lax.shift_left(lax.shift_right_logical(length, bit + 1), bit + 1)

            @pl.when((lax.shift_right_logical(length, bit) & 1) == 1)
            def _():
                pltpu.make_async_copy(_rows(tile_buf, src + higher, size),
                                      _rows(xs_ref, dst + higher, size),
                                      sem.at[slot]).start()
        return carry

    lax.fori_loop(0, N_EXPERTS, remainder, 0)

    @pl.when(step == nsteps - 1)
    def _():
        @pl.when(step >= 1)
        def _():
            wait_tile(1 - slot)

        wait_tile(slot)
        pad_copies(lambda c: c.wait())


def _tile_tables(run_start, run_len, *, tb):
    shift = RUN_CHUNK.bit_length() - 1
    nchunk = (run_len + RUN_CHUNK - 1) >> shift
    cum = jnp.cumsum(nchunk, axis=1)
    first = cum - nchunk
    local_start = first * RUN_CHUNK

    def flat_list(count, max_count, value_at):
        ccum = jnp.cumsum(count, axis=1)
        cfirst = (ccum - count)[:, None, :]
        c_idx = jnp.arange(max_count, dtype=jnp.int32)[None, :, None]
        owns = (cfirst <= c_idx) & (c_idx < ccum[:, None, :])
        return [jnp.sum(jnp.where(owns, v[:, None, :] + (c_idx - cfirst) * RUN_CHUNK, 0),
                        axis=2).astype(jnp.int32) for v in value_at], ccum[:, -1]

    max_chunks = tb * TOP_K // RUN_CHUNK + N_EXPERTS
    (chunk_src,), tile_chunks = flat_list(nchunk, max_chunks, [run_start])
    max_full = tb * TOP_K // RUN_CHUNK
    nfull = run_len >> shift
    (full_src, full_dst), tile_full = flat_list(nfull, max_full, [local_start, run_start])
    whole = nfull * RUN_CHUNK
    return dict(
        max_chunks=max_chunks, chunk_src=chunk_src.reshape(-1),
        tile_chunks=tile_chunks.astype(jnp.int32),
        max_full=max_full, full_src=full_src.reshape(-1), full_dst=full_dst.reshape(-1),
        tile_full=tile_full.astype(jnp.int32),
        rem_src=(local_start + whole).reshape(-1).astype(jnp.int32),
        rem_dst=(run_start + whole).reshape(-1).astype(jnp.int32),
        rem_len=(run_len - whole).reshape(-1).astype(jnp.int32))


def _dispatch(local0, local1, tables, pad_start, pad_len, nused, u2, *, p_rows, blk):
    t = u2.shape[0] // ROW_TILE
    tb = TB_RANK
    d_model = ROW_TILE * LANES
    kern = functools.partial(_dispatch_kernel, tb=tb, blk=blk, nblocks=p_rows // blk,
                             max_full=tables["max_full"])
    buf_rows = tables["max_chunks"] * RUN_CHUNK
    return pl.pallas_call(
        kern,
        grid_spec=pltpu.PrefetchScalarGridSpec(
            num_scalar_prefetch=11,
            grid=(t // tb,),
            in_specs=[pl.BlockSpec((tb * ROW_TILE, LANES), lambda i, *_: (i, 0))],
            out_specs=pl.BlockSpec(memory_space=pl.ANY),
            scratch_shapes=[pltpu.VMEM((2,) + _row_tiles((buf_rows, d_model)), F32),
                            pltpu.VMEM((blk * ROW_TILE, LANES), F32),
                            pltpu.SemaphoreType.DMA((2,)), pltpu.SemaphoreType.DMA]),
        out_shape=jax.ShapeDtypeStruct((p_rows * ROW_TILE, LANES), F32),
        compiler_params=_cparams(("arbitrary",)),
        name="dispatch",
    )(local0, local1, tables["full_src"], tables["full_dst"], tables["tile_full"],
      tables["rem_src"], tables["rem_dst"], tables["rem_len"],
      pad_start, pad_len, nused, u2)


def _experts_kernel(be_sm, first_sm, ahead_sm, slot_sm, head_sm, nused_sm,
                    xs_ref, w1_hbm, w3_hbm, w2_hbm, ys_ref,
                    w1f_ref, w3f_ref, w2f_ref, w13b_ref, w2b_ref, wsem, *, d_expert):
    b = pl.program_id(0)
    used = b < nused_sm[0]

    def weight_copies(expert, slot):
        return [pltpu.make_async_copy(src.at[expert], dst.at[slot], wsem.at[slot])
                for src, dst in ((w1_hbm, w1f_ref), (w3_hbm, w3f_ref), (w2_hbm, w2f_ref))]

    @pl.when(b == 0)
    def _():
        for j in range(WEIGHT_RING - 1):
            @pl.when(head_sm[j] >= 0)
            def _():
                for c in weight_copies(head_sm[j], j):
                    c.start()

    @pl.when(first_sm[b] == 1)
    def _():
        slot = slot_sm[b]
        for c in weight_copies(be_sm[b], slot):
            c.wait()

        @pl.when(ahead_sm[b] >= 0)
        def _():
            ahead_slot = lax.rem(slot + (WEIGHT_RING - 1), WEIGHT_RING)
            for c in weight_copies(ahead_sm[b], ahead_slot):
                c.start()

        w13b_ref[:, :d_expert] = w1f_ref[slot].astype(BF16)
        w13b_ref[:, d_expert:] = w3f_ref[slot].astype(BF16)
        w2b_ref[...] = w2f_ref[slot].astype(BF16)

    @pl.when(used)
    def _():
        xb = _load_row_tiles(xs_ref).astype(BF16)
        h = jnp.dot(xb, w13b_ref[...], preferred_element_type=F32)
        h1 = h[:, :d_expert]
        h3 = h[:, d_expert:]
        a = (h1 * _sigmoid(h1) * h3).astype(BF16)
        _store_row_tiles(ys_ref, jnp.dot(a, w2b_ref[...], preferred_element_type=F32))

    @pl.when(jnp.logical_not(used))
    def _():
        ys_ref[...] = jnp.zeros(ys_ref.shape, F32)


def _experts(sched, nused, xs, w1, w3, w2, *, blk):
    p_rows = xs.shape[0] // ROW_TILE
    d_model, d_expert = w1.shape[-2:]
    kern = functools.partial(_experts_kernel, d_expert=d_expert)
    rows_blk = (blk * ROW_TILE, LANES)

    def rows_in(b, be, first, ahead, slot, head, nu):
        return (jnp.maximum(jnp.minimum(b, nu[0] - 1), 0), 0)

    hbm = pl.BlockSpec(memory_space=pl.ANY)
    return pl.pallas_call(
        kern,
        grid_spec=pltpu.PrefetchScalarGridSpec(
            num_scalar_prefetch=6,
            grid=(p_rows // blk,),
            in_specs=[pl.BlockSpec(rows_blk, rows_in), hbm, hbm, hbm],
            out_specs=pl.BlockSpec(rows_blk, lambda b, *_: (b, 0)),
            scratch_shapes=[pltpu.VMEM((WEIGHT_RING, d_model, d_expert), F32),
                            pltpu.VMEM((WEIGHT_RING, d_model, d_expert), F32),
                            pltpu.VMEM((WEIGHT_RING, d_expert, d_model), F32),
                            pltpu.VMEM((d_model, 2 * d_expert), BF16),
                            pltpu.VMEM((d_expert, d_model), BF16),
                            pltpu.SemaphoreType.DMA((WEIGHT_RING,))]),
        out_shape=jax.ShapeDtypeStruct((p_rows * ROW_TILE, LANES), F32),
        compiler_params=_cparams(("arbitrary",)),
        name="experts",
    )(sched["expert"], sched["first"], sched["ahead"], sched["slot"], sched["head"],
      nused, xs, w1, w3, w2)


def _expert_schedule(counts, *, blk, nb):
    nblk = (counts + blk - 1) // blk
    pend = jnp.cumsum(nblk)
    nused = pend[-1]
    barange = jnp.arange(nb, dtype=jnp.int32)
    bidx = jnp.minimum(barange, nused - 1)
    expert = jnp.minimum(jnp.sum(pend[None, :] <= bidx[:, None], axis=1),
                         N_EXPERTS - 1).astype(jnp.int32)
    live = barange < nused
    first = (live & (barange == (pend - nblk)[expert])).astype(jnp.int32)
    has_rows = nblk > 0
    used_rank = jnp.cumsum(has_rows) - 1
    j_idx = jnp.arange(N_EXPERTS + WEIGHT_RING, dtype=jnp.int32)
    e_idx = jnp.arange(N_EXPERTS, dtype=jnp.int32)
    hit = has_rows[None, :] & (used_rank[None, :] == j_idx[:, None])
    used_list = jnp.sum(jnp.where(hit, e_idx[None, :] + 1, 0), axis=1) - 1
    rank_b = used_rank[expert]
    ahead = jnp.where(live, used_list[rank_b + (WEIGHT_RING - 1)], -1)
    return dict(
        expert=expert, first=first, ahead=ahead.astype(jnp.int32),
        slot=(rank_b % WEIGHT_RING).astype(jnp.int32),
        head=used_list[:WEIGHT_RING - 1].astype(jnp.int32),
        nused=nused.reshape(1).astype(jnp.int32),
        pad_start=((pend - nblk) * blk + counts).astype(jnp.int32),
        pad_len=(nblk * blk - counts).astype(jnp.int32))


def _combine_kernel(local0_sm, local1_sm, chunk_src_sm, nchunk_sm,
                    x1_ref, meta_ref, ys_ref, out_ref,
                    buf_ref, g0_ref, g1_ref, sem, *, tb, max_chunks):
    step = pl.program_id(0)
    nsteps = pl.num_programs(0)

    def start_chunks(tile):
        slot = tile % 2

        def per_chunk(c, carry):
            pltpu.make_async_copy(
                _rows(ys_ref, chunk_src_sm[tile * max_chunks + c], RUN_CHUNK),
                _rows(buf_ref.at[slot], c * RUN_CHUNK, RUN_CHUNK),
                sem.at[slot]).start()
            return carry

        lax.fori_loop(0, nchunk_sm[tile], per_chunk, 0)

    def wait_chunks(tile):
        slot = tile % 2
        n = nchunk_sm[tile]
        for bit in range(max_chunks.bit_length()):
            @pl.when((lax.shift_right_logical(n, bit) & 1) == 1)
            def _():
                rows = RUN_CHUNK << bit
                pltpu.make_async_copy(_rows(ys_ref, 0, rows),
                                      _rows(buf_ref.at[slot], 0, rows),
                                      sem.at[slot]).wait()

    @pl.when(step == 0)
    def _():
        start_chunks(step)

    @pl.when(step + 1 < nsteps)
    def _():
        start_chunks(step + 1)

    wait_chunks(step)

    base = step * tb
    tile_buf = buf_ref.at[step % 2]

    def move(r, carry):
        for g_ref, local_sm in ((g0_ref, local0_sm), (g1_ref, local1_sm)):
            _rows(g_ref, r)[...] = _rows(tile_buf, local_sm[base + r])[...]
        return carry

    lax.fori_loop(0, tb, move, 0, unroll=8)
    meta = meta_ref[...]
    moe = (_load_row_tiles(g0_ref) * meta[:, 2:3]
           + _load_row_tiles(g1_ref) * meta[:, 3:4])
    out_ref[...] = x1_ref[...] + moe


def _combine(local0, local1, tables, x1, meta, ys):
    t, d_model = x1.shape
    tb = TB_RANK
    max_chunks = tables["max_chunks"]
    kern = functools.partial(_combine_kernel, tb=tb, max_chunks=max_chunks)
    buf_rows = max_chunks * RUN_CHUNK
    return pl.pallas_call(
        kern,
        grid_spec=pltpu.PrefetchScalarGridSpec(
            num_scalar_prefetch=4,
            grid=(t // tb,),
            in_specs=[pl.BlockSpec((tb, d_model), lambda i, *_: (i, 0)),
                      pl.BlockSpec((tb, LANES), lambda i, *_: (i, 0)),
                      pl.BlockSpec(memory_space=pl.ANY)],
            out_specs=pl.BlockSpec((tb, d_model), lambda i, *_: (i, 0)),
            scratch_shapes=[pltpu.VMEM((2,) + _row_tiles((buf_rows, d_model)), F32),
                            pltpu.VMEM(_row_tiles((tb, d_model)), F32),
                            pltpu.VMEM(_row_tiles((tb, d_model)), F32),
                            pltpu.SemaphoreType.DMA((2,))]),
        out_shape=jax.ShapeDtypeStruct((t, d_model), F32),
        compiler_params=_cparams(("arbitrary",)),
        name="combine",
    )(local0, local1, tables["chunk_src"], tables["tile_chunks"], x1, meta, ys)


def _layer(h2, *, batch, seq, norm1_g, w_in, conv_w, q_norm_g, k_norm_g,
           w_conv_out, w_attn_out, w_o, norm2_g, w_group, w_router, w1, w3, w2):
    t, d_model = h2.shape
    c, sa, sb = _rope_tables(seq)
    scale = HEAD_DIM ** -0.5 * LOG2_E
    tables_q = tuple(jnp.asarray(tab * np.float32(scale)) for tab in (c, sa, sb))
    tables_k = tuple(jnp.asarray(tab) for tab in (c, sa, sb))

    cb, z, q, k, v, sgc, sga = _inproj(
        h2, norm1_g[None, :], w_in.astype(BF16), q_norm_g[None, :], k_norm_g[None, :],
        tables_q, tables_k, seq=seq)
    o = _attention(q, k, v, batch=batch, seq=seq)

    n_route = N_GROUPS + N_EXPERTS
    wr = jnp.concatenate(
        [w_group, w_router, jnp.zeros((d_model, LANES - n_route), F32)], axis=1)
    wr_hi = wr.astype(BF16)
    wr = jnp.concatenate([wr_hi, (wr - wr_hi.astype(F32)).astype(BF16)], axis=1)
    x1, u2, meta, cnt = _post(h2, cb, z, o, sgc, sga, conv_w,
                              w_conv_out.astype(BF16), w_attn_out.astype(BF16),
                              w_o.astype(BF16), norm2_g[None, :], wr, seq=seq)

    blk = MOE_ROWS
    local0, local1, runs = _rank(meta, cnt, blk=blk)
    local0, local1 = local0.reshape(-1), local1.reshape(-1)
    runs = runs.reshape(-1, 8, LANES)
    tables = _tile_tables(runs[:, 0, :N_EXPERTS], runs[:, 1, :N_EXPERTS], tb=TB_RANK)
    p_rows = t * TOP_K + N_EXPERTS * blk
    sched = _expert_schedule(cnt[0, :N_EXPERTS].astype(jnp.int32), blk=blk,
                             nb=p_rows // blk)

    xs = _dispatch(local0, local1, tables, sched["pad_start"], sched["pad_len"],
                   sched["nused"], u2, p_rows=p_rows, blk=blk)
    ys = _experts(sched, sched["nused"], xs, w1, w3, w2, blk=blk)
    return _combine(local0, local1, tables, x1, meta, ys)


def kernel(x, norm1_g, w_in, conv_w, q_norm_g, k_norm_g, w_conv_out, w_attn_out, w_o,
           norm2_g, w_group, w_router, w1, w3, w2):
    batch, seq, d_model = x.shape
    h2 = x.reshape(batch * seq, d_model)
    for l in range(norm1_g.shape[0]):
        h2 = _layer(h2, batch=batch, seq=seq, norm1_g=norm1_g[l], w_in=w_in[l],
                    conv_w=conv_w[l], q_norm_g=q_norm_g[l], k_norm_g=k_norm_g[l],
                    w_conv_out=w_conv_out[l], w_attn_out=w_attn_out[l], w_o=w_o[l],
                    norm2_g=norm2_g[l], w_group=w_group[l], w_router=w_router[l],
                    w1=w1[l], w3=w3[l], w2=w2[l])
    return h2.reshape(batch, seq, d_model)
```

```python
import functools

import jax
import jax.numpy as jnp
import numpy as np
from jax import lax
from jax.experimental import pallas as pl
from jax.experimental.pallas import tpu as pltpu

F32 = jnp.float32
BF16 = jnp.bfloat16

GRID_W = 64
EPS = 1e-6
N_HEADS = 8
N_KV_HEADS = 2
HEAD_DIM = 128
ROPE_THETA = 10000.0
N_GROUPS = 8
EXPERTS_PER_GROUP = 8
N_EXPERTS = N_GROUPS * EXPERTS_PER_GROUP
TOP_K = 2
LOG2_E = 1.4426950408889634

LANES = 128
MXU_DIM = 256
V7X_VMEM_LIMIT_BYTES = 56000 * 1024

TM_PROJ = 512
POST_SPLIT = 1
TQ_ATTN = 512
TK_ATTN = 512
TB_RANK = 512
TB_ROWS = 256
RUN_CHUNK = 8
WEIGHT_RING = 3
MOE_ROWS = 128
BF16_SUBLANES = 16
F32_SUBLANES = 8


def _cparams(sem):
    return pltpu.CompilerParams(dimension_semantics=sem,
                                vmem_limit_bytes=V7X_VMEM_LIMIT_BYTES)


ROW_TILE = 8


def _row_tiles(shape2d):
    rows, width = shape2d
    assert width == ROW_TILE * LANES
    return (rows * ROW_TILE, LANES)


def _rows(ref, r, n=1):
    return ref.at[pl.ds(pl.multiple_of(r * ROW_TILE, ROW_TILE), n * ROW_TILE)]


def _load_row_tiles(ref):
    rows = ref.shape[0] // ROW_TILE
    return jnp.concatenate(
        [ref[pl.ds(s, rows, stride=ROW_TILE), :] for s in range(ROW_TILE)], axis=1)


def _store_row_tiles(ref, value, row0=0):
    rows = value.shape[0]
    for s in range(ROW_TILE):
        ref[pl.ds(row0 * ROW_TILE + s, rows, stride=ROW_TILE), :] = (
            value[:, s * LANES:(s + 1) * LANES])


def _resident(shape):
    nd = len(shape)
    return pl.BlockSpec(shape, lambda *_: (0,) * nd, pipeline_mode=pl.Buffered(1))


def _head_norm_rope(xh, g, c, sa, sb):
    ms = jnp.mean(xh * xh, axis=-1, keepdims=True)
    y = xh * lax.rsqrt(ms + EPS) * g
    y_next = pltpu.roll(y, HEAD_DIM - 1, axis=1)
    y_prev = pltpu.roll(y, 1, axis=1)
    return y * c + y_next * sa + y_prev * sb


def _sigmoid(x):
    return 0.5 * jnp.tanh(0.5 * x) + 0.5


def _inproj_kernel(x_ref, g1_ref, w_ref, gq_ref, gk_ref,
                   cq_ref, saq_ref, sbq_ref, ck_ref, sak_ref, sbk_ref,
                   cb_ref, z_ref, q_ref, k_ref, v_ref, sgc_ref, sga_ref,
                   *, d_conv, d_q, d_kv, d_model):
    x = x_ref[...]
    ms = jnp.mean(x * x, axis=-1, keepdims=True)
    u = (x * lax.rsqrt(ms + EPS) * g1_ref[...]).astype(BF16)

    def proj(lo, width):
        return jnp.dot(u, w_ref[:, lo:lo + width], preferred_element_type=F32)

    o_cb, o_cc, o_cx = 0, d_conv, 2 * d_conv
    o_q = 3 * d_conv
    o_k = o_q + d_q
    o_v = o_k + d_kv
    o_gc = o_v + d_kv
    o_ga = o_gc + d_model

    sgc_ref[...] = _sigmoid(proj(o_gc, d_model)).astype(BF16)
    sga_ref[...] = _sigmoid(proj(o_ga, d_model)).astype(BF16)

    q = proj(o_q, d_q)
    gq = gq_ref[...]
    cq, saq, sbq = cq_ref[...], saq_ref[...], sbq_ref[...]
    for h in range(d_q // HEAD_DIM):
        sl = slice(h * HEAD_DIM, (h + 1) * HEAD_DIM)
        q_ref[:, sl] = _head_norm_rope(q[:, sl], gq, cq, saq, sbq).astype(BF16)

    k = proj(o_k, d_kv)
    gk = gk_ref[...]
    ck, sak, sbk = ck_ref[...], sak_ref[...], sbk_ref[...]
    for h in range(d_kv // HEAD_DIM):
        sl = slice(h * HEAD_DIM, (h + 1) * HEAD_DIM)
        k_ref[:, sl] = _head_norm_rope(k[:, sl], gk, ck, sak, sbk).astype(BF16)

    z_ref[...] = (proj(o_cc, d_conv) * proj(o_cx, d_conv)).astype(BF16)
    v_ref[...] = proj(o_v, d_kv).astype(BF16)
    cb_ref[...] = proj(o_cb, d_conv).astype(BF16)


def _rope_tables(seq):
    rows = seq // GRID_W
    axis_dim = HEAD_DIM // 2
    row = np.repeat(np.arange(rows, dtype=np.float32), GRID_W)
    col = np.tile(np.arange(GRID_W, dtype=np.float32), rows)
    inv = (np.float32(ROPE_THETA)
           ** (-np.arange(0, axis_dim, 2, dtype=np.float32) / np.float32(axis_dim)))
    ang = np.concatenate([row[:, None] * inv, col[:, None] * inv], axis=-1)
    ang = ang.astype(np.float32)
    cos, sin = np.cos(ang), np.sin(ang)
    zero = np.zeros_like(sin)
    c = np.repeat(cos, 2, axis=-1)
    sa = np.stack([-sin, zero], axis=-1).reshape(seq, HEAD_DIM)
    sb = np.stack([zero, sin], axis=-1).reshape(seq, HEAD_DIM)
    return c, sa, sb


def _inproj(x2, g1, w_in_bf, gq, gk, tables_q, tables_k, *, seq):
    t, d_model = x2.shape
    d_q = N_HEADS * HEAD_DIM
    d_kv = N_KV_HEADS * HEAD_DIM
    d_in = w_in_bf.shape[1]
    d_conv = (d_in - d_q - 2 * d_kv - 2 * d_model) // 3
    tm = TM_PROJ
    nseq = seq // tm

    def row(width):
        return pl.BlockSpec((tm, width), lambda i: (i, 0))

    table = pl.BlockSpec((tm, HEAD_DIM), lambda i: (i % nseq, 0))
    kern = functools.partial(_inproj_kernel, d_conv=d_conv, d_q=d_q, d_kv=d_kv,
                             d_model=d_model)
    out_shape = [jax.ShapeDtypeStruct((t, w), BF16)
                 for w in (d_conv, d_conv, d_q, d_kv, d_kv, d_model, d_model)]
    return pl.pallas_call(
        kern,
        grid=(t // tm,),
        in_specs=[row(d_model), _resident((1, d_model)), _resident((d_model, d_in)),
                  _resident((1, HEAD_DIM)), _resident((1, HEAD_DIM)),
                  table, table, table, table, table, table],
        out_specs=[row(d_conv), row(d_conv), row(d_q), row(d_kv), row(d_kv),
                   row(d_model), row(d_model)],
        out_shape=out_shape,
        compiler_params=_cparams(("arbitrary",)),
        name="inproj",
    )(x2, g1, w_in_bf, gq, gk, *tables_q, *tables_k)


def _attn_kernel(q_ref, k_ref, v_ref, o_ref, qs_ref, vext_ref, m_ref, acc_ref,
                 *, tq, chunks, group):
    @pl.when(pl.program_id(2) == 0)
    def _():
        vext_ref[:, :HEAD_DIM] = v_ref[...]
        vext_ref[:, HEAD_DIM:] = jnp.ones((vext_ref.shape[0], HEAD_DIM), BF16)

    for g in range(group):
        qs_ref[g * tq:(g + 1) * tq, :] = q_ref[:, g * HEAD_DIM:(g + 1) * HEAD_DIM]
    m_ref[...] = jnp.full(m_ref.shape, -jnp.inf, F32)
    acc_ref[...] = jnp.zeros(acc_ref.shape, F32)

    lo = 0
    for tk in chunks:
        keys = slice(lo, lo + tk)
        lo += tk
        s = lax.dot_general(qs_ref[...], k_ref[keys, :], (((1,), (1,)), ((), ())),
                            preferred_element_type=F32)
        m_prev = m_ref[...]
        m_new = jnp.maximum(m_prev, jnp.max(s, axis=-1, keepdims=True))
        alpha = jnp.exp2(m_prev - m_new)
        p = jnp.concatenate(
            [jnp.exp2(s[:, c * LANES:(c + 1) * LANES] - m_new) for c in range(tk // LANES)],
            axis=1).astype(BF16)
        pv = jnp.dot(p, vext_ref[keys, :], preferred_element_type=F32)
        acc_ref[...] = jnp.concatenate([alpha, alpha], axis=1) * acc_ref[...] + pv
        m_ref[...] = m_new

    out = acc_ref[:, :HEAD_DIM] / acc_ref[:, HEAD_DIM:]
    for g in range(group):
        o_ref[:, g * HEAD_DIM:(g + 1) * HEAD_DIM] = out[g * tq:(g + 1) * tq].astype(BF16)


def _attention(q, k, v, *, batch, seq):
    t = q.shape[0]
    group = N_HEADS // N_KV_HEADS
    tq, tk = TQ_ATTN, TK_ATTN
    nq = seq // tq
    gw = group * HEAD_DIM
    chunks = (tk // 2,) + (tk,) * (seq // tk - 1) + (tk // 2,)
    kern = functools.partial(_attn_kernel, tq=tq, chunks=chunks, group=group)
    return pl.pallas_call(
        kern,
        grid=(batch, N_KV_HEADS, nq),
        in_specs=[pl.BlockSpec((tq, gw), lambda b, h, i: (b * nq + i, h)),
                  pl.BlockSpec((seq, HEAD_DIM), lambda b, h, i: (b, h)),
                  pl.BlockSpec((seq, HEAD_DIM), lambda b, h, i: (b, h))],
        out_specs=pl.BlockSpec((tq, gw), lambda b, h, i: (b * nq + i, h)),
        out_shape=jax.ShapeDtypeStruct((t, N_HEADS * HEAD_DIM), BF16),
        scratch_shapes=[pltpu.VMEM((group * tq, HEAD_DIM), BF16),
                        pltpu.VMEM((seq, 2 * HEAD_DIM), BF16),
                        pltpu.VMEM((group * tq, LANES), F32),
                        pltpu.VMEM((group * tq, 2 * HEAD_DIM), F32)],
        compiler_params=_cparams(("arbitrary", "arbitrary", "arbitrary")),
        name="attention",
    )(q, k, v)


def _route(logits):
    rows = logits.shape[0]
    lane = lax.broadcasted_iota(jnp.int32, (rows, LANES), 1).astype(F32)
    neg = -jnp.inf
    big = float(2 * LANES)
    is_group = lane < N_GROUPS
    gl = jnp.where(is_group, logits, neg)
    gmax = jnp.max(gl, axis=-1, keepdims=True)
    gidx = jnp.min(jnp.where(gl == gmax, lane, big), axis=-1, keepdims=True)
    gsum = jnp.sum(jnp.where(is_group, jnp.exp(logits - gmax), 0.0), axis=-1,
                   keepdims=True)
    pg = 1.0 / gsum
    lane_group = jnp.floor(lane * (1.0 / EXPERTS_PER_GROUP)) - 1.0
    mine = (lane_group == gidx) & (lane >= N_GROUPS) & (lane < N_GROUPS + N_EXPERTS)
    sel = jnp.where(mine, logits, neg)
    v1 = jnp.max(sel, axis=-1, keepdims=True)
    i1 = jnp.min(jnp.where(sel == v1, lane, big), axis=-1, keepdims=True)
    sel2 = jnp.where(lane == i1, neg, sel)
    v2 = jnp.max(sel2, axis=-1, keepdims=True)
    i2 = jnp.min(jnp.where(sel2 == v2, lane, big), axis=-1, keepdims=True)
    t2 = jnp.exp(v2 - v1)
    den = 1.0 + t2
    wgt1 = pg * (1.0 / den)
    wgt2 = pg * (t2 / den)
    e1 = i1 - N_GROUPS
    e2 = i2 - N_GROUPS
    meta = jnp.where(lane == 0, e1,
                     jnp.where(lane == 1, e2,
                               jnp.where(lane == 2, wgt1,
                                         jnp.where(lane == 3, wgt2, 0.0))))
    picked = jnp.where((lane == e1) | (lane == e2), 1.0, 0.0)
    return meta, jnp.sum(picked, axis=0, keepdims=True)


def _post_kernel(x_ref, cb_ref, z_ref, zprev_ref, znext_ref, o_ref, sgc_ref, sga_ref,
                 cw_ref, wc_ref, wa_ref, wo_ref, g2_ref, wr_ref,
                 x1_ref, u2_ref, meta_ref, cnt_ref, *, tm, nseq, nsplit):
    i = pl.program_id(0)
    at_start = (i % nseq) == 0
    at_end = (i % nseq) == nseq - 1
    rows = tm // nsplit
    rowid = lax.broadcasted_iota(jnp.int32, (rows, 1), 0)

    @pl.when(i == 0)
    def _():
        cnt_ref[...] = jnp.zeros(cnt_ref.shape, F32)

    for part_id in range(nsplit):
        r0 = part_id * rows
        rs = slice(r0, r0 + rows)
        y_attn = jnp.dot(o_ref[rs, :], wa_ref[...], preferred_element_type=F32)

        y_conv = None
        for c in range(z_ref.shape[1] // MXU_DIM):
            sl = slice(c * MXU_DIM, (c + 1) * MXU_DIM)
            z = z_ref[rs, sl].astype(F32)
            if part_id == 0:
                prev_row = zprev_ref[BF16_SUBLANES - 1:BF16_SUBLANES, sl].astype(F32)
                prev_row = jnp.where(at_start, 0.0, prev_row)
            else:
                prev_row = z_ref[r0 - 1:r0, sl].astype(F32)
            if part_id == nsplit - 1:
                next_row = znext_ref[0:1, sl].astype(F32)
                next_row = jnp.where(at_end, 0.0, next_row)
            else:
                next_row = z_ref[r0 + rows:r0 + rows + 1, sl].astype(F32)
            zp = jnp.where(rowid == 0, prev_row, pltpu.roll(z, 1, axis=0))
            zn = jnp.where(rowid == rows - 1, next_row, pltpu.roll(z, rows - 1, axis=0))
            conv = cw_ref[0:1, sl] * zp + cw_ref[1:2, sl] * z + cw_ref[2:3, sl] * zn
            cbz = (cb_ref[rs, sl].astype(F32) * conv).astype(BF16)
            part = jnp.dot(cbz, wc_ref[sl, :], preferred_element_type=F32)
            y_conv = part if y_conv is None else y_conv + part
        merged = (sgc_ref[rs, :].astype(F32) * y_conv
                  + sga_ref[rs, :].astype(F32) * y_attn).astype(BF16)
        x1 = x_ref[rs, :] + jnp.dot(merged, wo_ref[...], preferred_element_type=F32)
        x1_ref[rs, :] = x1

        ms = jnp.mean(x1 * x1, axis=-1, keepdims=True)
        u2 = x1 * lax.rsqrt(ms + EPS) * g2_ref[...]
        _store_row_tiles(u2_ref, u2, r0)

        u2_hi = u2.astype(BF16)
        u2_lo = (u2 - u2_hi.astype(F32)).astype(BF16)
        hi_part = jnp.dot(u2_hi, wr_ref[...], preferred_element_type=F32)
        lo_part = jnp.dot(u2_lo, wr_ref[:, :LANES], preferred_element_type=F32)
        logits = hi_part[:, :LANES] + (hi_part[:, LANES:] + lo_part)
        meta, picked = _route(logits)
        meta_ref[rs, :] = meta
        cnt_ref[...] += picked


def _post(x2, cb, z, o, sgc, sga, conv_w, wc, wa, wo, g2, wr, *, seq):
    t, d_model = x2.shape
    tm = TM_PROJ
    nseq = seq // tm
    hb = tm // BF16_SUBLANES
    nhalo = t // BF16_SUBLANES
    d_conv = cb.shape[1]
    d_q = o.shape[1]

    def row(width):
        return pl.BlockSpec((tm, width), lambda i: (i, 0))

    kern = functools.partial(_post_kernel, tm=tm, nseq=nseq, nsplit=POST_SPLIT)
    return pl.pallas_call(
        kern,
        grid=(t // tm,),
        in_specs=[row(d_model), row(d_conv), row(d_conv),
                  pl.BlockSpec((BF16_SUBLANES, d_conv),
                               lambda i: (jnp.maximum(i * hb - 1, 0), 0)),
                  pl.BlockSpec((BF16_SUBLANES, d_conv),
                               lambda i: (jnp.minimum((i + 1) * hb, nhalo - 1), 0)),
                  row(d_q), row(d_model), row(d_model),
                  _resident(conv_w.shape), _resident(wc.shape), _resident(wa.shape),
                  _resident(wo.shape), _resident(g2.shape), _resident(wr.shape)],
        out_specs=[row(d_model),
                   pl.BlockSpec(_row_tiles((tm, d_model)), lambda i: (i, 0)),
                   row(LANES),
                   pl.BlockSpec((1, LANES), lambda i: (0, 0))],
        out_shape=[jax.ShapeDtypeStruct((t, d_model), F32),
                   jax.ShapeDtypeStruct(_row_tiles((t, d_model)), F32),
                   jax.ShapeDtypeStruct((t, LANES), F32),
                   jax.ShapeDtypeStruct((1, LANES), F32)],
        compiler_params=_cparams(("arbitrary",)),
        name="post",
    )(x2, cb, z, z, z, o, sgc, sga, conv_w, wc, wa, wo, g2, wr)


def _rank_kernel(meta_ref, cnt_ref, l0_ref, l1_ref, runs_ref,
                 carry_ref, pstart_ref, *, tb, blk):
    i = pl.program_id(0)
    lane = lax.broadcasted_iota(jnp.int32, (tb, LANES), 1).astype(F32)
    meta = meta_ref[...]
    oh1 = jnp.where(lane == meta[:, 0:1], 1.0, 0.0)
    oh2 = jnp.where(lane == meta[:, 1:2], 1.0, 0.0)
    c = oh1 + oh2

    @pl.when(i == 0)
    def _():
        cnt = jnp.broadcast_to(cnt_ref[...], (8, LANES))
        nblk = jnp.floor((cnt + (blk - 1)) * (1.0 / blk))
        r = lax.broadcasted_iota(jnp.int32, (LANES, LANES), 0)
        col = lax.broadcasted_iota(jnp.int32, (LANES, LANES), 1)
        upper = jnp.where(r < col, 1.0, 0.0).astype(BF16)
        pblk = jnp.dot(nblk.astype(BF16), upper, preferred_element_type=F32)
        pstart_ref[...] = pblk[0:1, :] * blk
        carry_ref[...] = jnp.zeros(carry_ref.shape, F32)

    r = lax.broadcasted_iota(jnp.int32, (tb, tb), 0)
    col = lax.broadcasted_iota(jnp.int32, (tb, tb), 1)
    lower = jnp.where(col < r, 1.0, 0.0).astype(BF16)
    prefix = jnp.dot(lower, c.astype(BF16), preferred_element_type=F32)
    run_start = carry_ref[...] + pstart_ref[...]
    run_len = jnp.sum(c, axis=0, keepdims=True)

    nchunk = jnp.floor((run_len + (RUN_CHUNK - 1)) * (1.0 / RUN_CHUNK))
    r = lax.broadcasted_iota(jnp.int32, (LANES, LANES), 0)
    col = lax.broadcasted_iota(jnp.int32, (LANES, LANES), 1)
    upper = jnp.where(r < col, 1.0, 0.0).astype(BF16)
    local_start = jnp.dot(jnp.broadcast_to(nchunk, (8, LANES)).astype(BF16), upper,
                          preferred_element_type=F32)[0:1, :] * RUN_CHUNK
    local = prefix + local_start

    digits = jnp.zeros((tb, LANES), F32)
    for j, onehot in enumerate((oh1, oh2)):
        value = jnp.sum(onehot * local, axis=-1, keepdims=True)
        high = jnp.floor(value * (1.0 / 256.0))
        digits = jnp.where(lane == 2 * j, high,
                           jnp.where(lane == 2 * j + 1, value - high * 256.0, digits))
    r = lax.broadcasted_iota(jnp.int32, (tb, tb), 0)
    col = lax.broadcasted_iota(jnp.int32, (tb, tb), 1)
    eye = jnp.where(col == r, 1.0, 0.0).astype(BF16)
    rows = lax.dot_general(digits.astype(BF16), eye, (((0,), (0,)), ((), ())),
                           preferred_element_type=F32)
    for j, out_ref in enumerate((l0_ref, l1_ref)):
        out_ref[...] = (rows[2 * j:2 * j + 1, :] * 256.0
                        + rows[2 * j + 1:2 * j + 2, :]).astype(jnp.int32)
    row = lax.broadcasted_iota(jnp.int32, (8, LANES), 0)
    runs = jnp.where(row == 0, run_start, jnp.where(row == 1, run_len, 0.0))
    runs_ref[...] = runs.astype(jnp.int32)
    carry_ref[...] += run_len


def _rank(meta, cnt, *, blk):
    t = meta.shape[0]
    tb = TB_RANK
    kern = functools.partial(_rank_kernel, tb=tb, blk=blk)
    per_token = pl.BlockSpec((1, tb), lambda i: (0, i))
    return pl.pallas_call(
        kern,
        grid=(t // tb,),
        in_specs=[pl.BlockSpec((tb, LANES), lambda i: (i, 0)),
                  pl.BlockSpec((1, LANES), lambda i: (0, 0))],
        out_specs=[per_token] * 2 + [pl.BlockSpec((8, LANES), lambda i: (i, 0))],
        out_shape=[jax.ShapeDtypeStruct((1, t), jnp.int32)] * 2
                  + [jax.ShapeDtypeStruct((t // tb * 8, LANES), jnp.int32)],
        scratch_shapes=[pltpu.VMEM((1, LANES), F32), pltpu.VMEM((1, LANES), F32)],
        compiler_params=_cparams(("arbitrary",)),
        name="rank",
    )(meta, cnt)


def _dispatch_kernel(local0_sm, local1_sm, full_src_sm, full_dst_sm, nfull_sm,
                     rem_src_sm, rem_dst_sm, rem_len_sm, pad_start_sm, pad_len_sm, nused_sm,
                     u2_ref, xs_ref, buf_ref, zero_ref, sem, zsem,
                     *, tb, blk, nblocks, max_full):
    step = pl.program_id(0)
    nsteps = pl.num_programs(0)
    base = step * tb
    nbits = blk.bit_length() - 1
    slot = step % 2
    tile_buf = buf_ref.at[slot]

    def wait_tile(s):
        pltpu.make_async_copy(_rows(buf_ref.at[s], 0, tb * TOP_K),
                              _rows(xs_ref, 0, tb * TOP_K), sem.at[s]).wait()

    def pad_copies(visit):
        def per_expert(e, carry):
            start, length = pad_start_sm[e], pad_len_sm[e]
            for bit in range(nbits):
                size = 1 << bit
                higher = lax.shift_left(lax.shift_right_logical(length, bit + 1), bit + 1)

                @pl.when((lax.shift_right_logical(length, bit) & 1) == 1)
                def _():
                    visit(pltpu.make_async_copy(
                        zero_ref.at[pl.ds(0, size * ROW_TILE)],
                        _rows(xs_ref, start + higher, size), zsem))
            return carry

        lax.fori_loop(0, N_EXPERTS, per_expert, 0)

        def per_block(b, carry):
            visit(pltpu.make_async_copy(zero_ref, _rows(xs_ref, b * blk, blk), zsem))
            return carry

        lax.fori_loop(nused_sm[0], nblocks, per_block, 0)

    @pl.when(step == 0)
    def _():
        zero_ref[...] = jnp.zeros(zero_ref.shape, F32)
        pad_copies(lambda c: c.start())

    @pl.when(step >= 2)
    def _():
        wait_tile(slot)

    def move(r, carry):
        row = _rows(u2_ref, r)[...]
        _rows(tile_buf, local0_sm[base + r])[...] = row
        _rows(tile_buf, local1_sm[base + r])[...] = row
        return carry

    lax.fori_loop(0, tb, move, 0, unroll=8)

    def full_chunk(c, carry):
        pltpu.make_async_copy(
            _rows(tile_buf, full_src_sm[step * max_full + c], RUN_CHUNK),
            _rows(xs_ref, full_dst_sm[step * max_full + c], RUN_CHUNK),
            sem.at[slot]).start()
        return carry

    lax.fori_loop(0, nfull_sm[step], full_chunk, 0)

    def remainder(e, carry):
        idx = step * N_EXPERTS + e
        src, dst, length = rem_src_sm[idx], rem_dst_sm[idx], rem_len_sm[idx]
        for bit in range(RUN_CHUNK.bit_length() - 1):
            size = 1 << bit
            higher = lax.shift_left(lax.shift_right_logical(length, bit + 1), bit + 1)

            @pl.when((lax.shift_right_logical(length, bit) & 1) == 1)
            def _():
                pltpu.make_async_copy(_rows(tile_buf, src + higher, size),
                                      _rows(xs_ref, dst + higher, size),
                                      sem.at[slot]).start()
        return carry

    lax.fori_loop(0, N_EXPERTS, remainder, 0)

    @pl.when(step == nsteps - 1)
    def _():
        @pl.when(step >= 1)
        def _():
            wait_tile(1 - slot)

        wait_tile(slot)
        pad_copies(lambda c: c.wait())


def _tile_tables(run_start, run_len, *, tb):
    shift = RUN_CHUNK.bit_length() - 1
    nchunk = (run_len + RUN_CHUNK - 1) >> shift
    cum = jnp.cumsum(nchunk, axis=1)
    first = cum - nchunk
    local_start = first * RUN_CHUNK

    def flat_list(count, max_count, value_at):
        ccum = jnp.cumsum(count, axis=1)
        cfirst = (ccum - count)[:, None, :]
        c_idx = jnp.arange(max_count, dtype=jnp.int32)[None, :, None]
        owns = (cfirst <= c_idx) & (c_idx < ccum[:, None, :])
        return [jnp.sum(jnp.where(owns, v[:, None, :] + (c_idx - cfirst) * RUN_CHUNK, 0),
                        axis=2).astype(jnp.int32) for v in value_at], ccum[:, -1]

    max_chunks = tb * TOP_K // RUN_CHUNK + N_EXPERTS
    (chunk_src,), tile_chunks = flat_list(nchunk, max_chunks, [run_start])
    max_full = tb * TOP_K // RUN_CHUNK
    nfull = run_len >> shift
    (full_src, full_dst), tile_full = flat_list(nfull, max_full, [local_start, run_start])
    whole = nfull * RUN_CHUNK
    return dict(
        max_chunks=max_chunks, chunk_src=chunk_src.reshape(-1),
        tile_chunks=tile_chunks.astype(jnp.int32),
        max_full=max_full, full_src=full_src.reshape(-1), full_dst=full_dst.reshape(-1),
        tile_full=tile_full.astype(jnp.int32),
        rem_src=(local_start + whole).reshape(-1).astype(jnp.int32),
        rem_dst=(run_start + whole).reshape(-1).astype(jnp.int32),
        rem_len=(run_len - whole).reshape(-1).astype(jnp.int32))


def _dispatch(local0, local1, tables, pad_start, pad_len, nused, u2, *, p_rows, blk):
    t = u2.shape[0] // ROW_TILE
    tb = TB_RANK
    d_model = ROW_TILE * LANES
    kern = functools.partial(_dispatch_kernel, tb=tb, blk=blk, nblocks=p_rows // blk,
                             max_full=tables["max_full"])
    buf_rows = tables["max_chunks"] * RUN_CHUNK
    return pl.pallas_call(
        kern,
        grid_spec=pltpu.PrefetchScalarGridSpec(
            num_scalar_prefetch=11,
            grid=(t // tb,),
            in_specs=[pl.BlockSpec((tb * ROW_TILE, LANES), lambda i, *_: (i, 0))],
            out_specs=pl.BlockSpec(memory_space=pl.ANY),
            scratch_shapes=[pltpu.VMEM((2,) + _row_tiles((buf_rows, d_model)), F32),
                            pltpu.VMEM((blk * ROW_TILE, LANES), F32),
                            pltpu.SemaphoreType.DMA((2,)), pltpu.SemaphoreType.DMA]),
        out_shape=jax.ShapeDtypeStruct((p_rows * ROW_TILE, LANES), F32),
        compiler_params=_cparams(("arbitrary",)),
        name="dispatch",
    )(local0, local1, tables["full_src"], tables["full_dst"], tables["tile_full"],
      tables["rem_src"], tables["rem_dst"], tables["rem_len"],
      pad_start, pad_len, nused, u2)


def _experts_kernel(be_sm, first_sm, ahead_sm, slot_sm, head_sm, nused_sm,
                    xs_ref, w1_hbm, w3_hbm, w2_hbm, ys_ref,
                    w1f_ref, w3f_ref, w2f_ref, w13b_ref, w2b_ref, wsem, *, d_expert):
    b = pl.program_id(0)
    used = b < nused_sm[0]

    def weight_copies(expert, slot):
        return [pltpu.make_async_copy(src.at[expert], dst.at[slot], wsem.at[slot])
                for src, dst in ((w1_hbm, w1f_ref), (w3_hbm, w3f_ref), (w2_hbm, w2f_ref))]

    @pl.when(b == 0)
    def _():
        for j in range(WEIGHT_RING - 1):
            @pl.when(head_sm[j] >= 0)
            def _():
                for c in weight_copies(head_sm[j], j):
                    c.start()

    @pl.when(first_sm[b] == 1)
    def _():
        slot = slot_sm[b]
        for c in weight_copies(be_sm[b], slot):
            c.wait()

        @pl.when(ahead_sm[b] >= 0)
        def _():
            ahead_slot = lax.rem(slot + (WEIGHT_RING - 1), WEIGHT_RING)
            for c in weight_copies(ahead_sm[b], ahead_slot):
                c.start()

        w13b_ref[:, :d_expert] = w1f_ref[slot].astype(BF16)
        w13b_ref[:, d_expert:] = w3f_ref[slot].astype(BF16)
        w2b_ref[...] = w2f_ref[slot].astype(BF16)

    @pl.when(used)
    def _():
        xb = _load_row_tiles(xs_ref).astype(BF16)
        h = jnp.dot(xb, w13b_ref[...], preferred_element_type=F32)
        h1 = h[:, :d_expert]
        h3 = h[:, d_expert:]
        a = (h1 * _sigmoid(h1) * h3).astype(BF16)
        _store_row_tiles(ys_ref, jnp.dot(a, w2b_ref[...], preferred_element_type=F32))

    @pl.when(jnp.logical_not(used))
    def _():
        ys_ref[...] = jnp.zeros(ys_ref.shape, F32)


def _experts(sched, nused, xs, w1, w3, w2, *, blk):
    p_rows = xs.shape[0] // ROW_TILE
    d_model, d_expert = w1.shape[-2:]
    kern = functools.partial(_experts_kernel, d_expert=d_expert)
    rows_blk = (blk * ROW_TILE, LANES)

    def rows_in(b, be, first, ahead, slot, head, nu):
        return (jnp.maximum(jnp.minimum(b, nu[0] - 1), 0), 0)

    hbm = pl.BlockSpec(memory_space=pl.ANY)
    return pl.pallas_call(
        kern,
        grid_spec=pltpu.PrefetchScalarGridSpec(
            num_scalar_prefetch=6,
            grid=(p_rows // blk,),
            in_specs=[pl.BlockSpec(rows_blk, rows_in), hbm, hbm, hbm],
            out_specs=pl.BlockSpec(rows_blk, lambda b, *_: (b, 0)),
            scratch_shapes=[pltpu.VMEM((WEIGHT_RING, d_model, d_expert), F32),
                            pltpu.VMEM((WEIGHT_RING, d_model, d_expert), F32),
                            pltpu.VMEM((WEIGHT_RING, d_expert, d_model), F32),
                            pltpu.VMEM((d_model, 2 * d_expert), BF16),
                            pltpu.VMEM((d_expert, d_model), BF16),
                            pltpu.SemaphoreType.DMA((WEIGHT_RING,))]),
        out_shape=jax.ShapeDtypeStruct((p_rows * ROW_TILE, LANES), F32),
        compiler_params=_cparams(("arbitrary",)),
        name="experts",
    )(sched["expert"], sched["first"], sched["ahead"], sched["slot"], sched["head"],
      nused, xs, w1, w3, w2)


def _expert_schedule(counts, *, blk, nb):
    nblk = (counts + blk - 1) // blk
    pend = jnp.cumsum(nblk)
    nused = pend[-1]
    barange = jnp.arange(nb, dtype=jnp.int32)
    bidx = jnp.minimum(barange, nused - 1)
    expert = jnp.minimum(jnp.sum(pend[None, :] <= bidx[:, None], axis=1),
                         N_EXPERTS - 1).astype(jnp.int32)
    live = barange < nused
    first = (live & (barange == (pend - nblk)[expert])).astype(jnp.int32)
    has_rows = nblk > 0
    used_rank = jnp.cumsum(has_rows) - 1
    j_idx = jnp.arange(N_EXPERTS + WEIGHT_RING, dtype=jnp.int32)
    e_idx = jnp.arange(N_EXPERTS, dtype=jnp.int32)
    hit = has_rows[None, :] & (used_rank[None, :] == j_idx[:, None])
    used_list = jnp.sum(jnp.where(hit, e_idx[None, :] + 1, 0), axis=1) - 1
    rank_b = used_rank[expert]
    ahead = jnp.where(live, used_list[rank_b + (WEIGHT_RING - 1)], -1)
    return dict(
        expert=expert, first=first, ahead=ahead.astype(jnp.int32),
        slot=(rank_b % WEIGHT_RING).astype(jnp.int32),
        head=used_list[:WEIGHT_RING - 1].astype(jnp.int32),
        nused=nused.reshape(1).astype(jnp.int32),
        pad_start=((pend - nblk) * blk + counts).astype(jnp.int32),
        pad_len=(nblk * blk - counts).astype(jnp.int32))


def _combine_kernel(local0_sm, local1_sm, chunk_src_sm, nchunk_sm,
                    x1_ref, meta_ref, ys_ref, out_ref,
                    buf_ref, g0_ref, g1_ref, sem, *, tb, max_chunks):
    step = pl.program_id(0)
    nsteps = pl.num_programs(0)

    def start_chunks(tile):
        slot = tile % 2

        def per_chunk(c, carry):
            pltpu.make_async_copy(
                _rows(ys_ref, chunk_src_sm[tile * max_chunks + c], RUN_CHUNK),
                _rows(buf_ref.at[slot], c * RUN_CHUNK, RUN_CHUNK),
                sem.at[slot]).start()
            return carry

        lax.fori_loop(0, nchunk_sm[tile], per_chunk, 0)

    def wait_chunks(tile):
        slot = tile % 2
        n = nchunk_sm[tile]
        for bit in range(max_chunks.bit_length()):
            @pl.when((lax.shift_right_logical(n, bit) & 1) == 1)
            def _():
                rows = RUN_CHUNK << bit
                pltpu.make_async_copy(_rows(ys_ref, 0, rows),
                                      _rows(buf_ref.at[slot], 0, rows),
                                      sem.at[slot]).wait()

    @pl.when(step == 0)
    def _():
        start_chunks(step)

    @pl.when(step + 1 < nsteps)
    def _():
        start_chunks(step + 1)

    wait_chunks(step)

    base = step * tb
    tile_buf = buf_ref.at[step % 2]

    def move(r, carry):
        for g_ref, local_sm in ((g0_ref, local0_sm), (g1_ref, local1_sm)):
            _rows(g_ref, r)[...] = _rows(tile_buf, local_sm[base + r])[...]
        return carry

    lax.fori_loop(0, tb, move, 0, unroll=8)
    meta = meta_ref[...]
    moe = (_load_row_tiles(g0_ref) * meta[:, 2:3]
           + _load_row_tiles(g1_ref) * meta[:, 3:4])
    out_ref[...] = x1_ref[...] + moe


def _combine(local0, local1, tables, x1, meta, ys):
    t, d_model = x1.shape
    tb = TB_RANK
    max_chunks = tables["max_chunks"]
    kern = functools.partial(_combine_kernel, tb=tb, max_chunks=max_chunks)
    buf_rows = max_chunks * RUN_CHUNK
    return pl.pallas_call(
        kern,
        grid_spec=pltpu.PrefetchScalarGridSpec(
            num_scalar_prefetch=4,
            grid=(t // tb,),
            in_specs=[pl.BlockSpec((tb, d_model), lambda i, *_: (i, 0)),
                      pl.BlockSpec((tb, LANES), lambda i, *_: (i, 0)),
                      pl.BlockSpec(memory_space=pl.ANY)],
            out_specs=pl.BlockSpec((tb, d_model), lambda i, *_: (i, 0)),
            scratch_shapes=[pltpu.VMEM((2,) + _row_tiles((buf_rows, d_model)), F32),
                            pltpu.VMEM(_row_tiles((tb, d_model)), F32),
                            pltpu.VMEM(_row_tiles((tb, d_model)), F32),
                            pltpu.SemaphoreType.DMA((2,))]),
        out_shape=jax.ShapeDtypeStruct((t, d_model), F32),
        compiler_params=_cparams(("arbitrary",)),
        name="combine",
    )(local0, local1, tables["chunk_src"], tables["tile_chunks"], x1, meta, ys)


def _layer(h2, *, batch, seq, norm1_g, w_in, conv_w, q_norm_g, k_norm_g,
           w_conv_out, w_attn_out, w_o, norm2_g, w_group, w_router, w1, w3, w2):
    t, d_model = h2.shape
    c, sa, sb = _rope_tables(seq)
    scale = HEAD_DIM ** -0.5 * LOG2_E
    tables_q = tuple(jnp.asarray(tab * np.float32(scale)) for tab in (c, sa, sb))
    tables_k = tuple(jnp.asarray(tab) for tab in (c, sa, sb))

    cb, z, q, k, v, sgc, sga = _inproj(
        h2, norm1_g[None, :], w_in.astype(BF16), q_norm_g[None, :], k_norm_g[None, :],
        tables_q, tables_k, seq=seq)
    o = _attention(q, k, v, batch=batch, seq=seq)

    n_route = N_GROUPS + N_EXPERTS
    wr = jnp.concatenate(
        [w_group, w_router, jnp.zeros((d_model, LANES - n_route), F32)], axis=1)
    wr_hi = wr.astype(BF16)
    wr = jnp.concatenate([wr_hi, (wr - wr_hi.astype(F32)).astype(BF16)], axis=1)
    x1, u2, meta, cnt = _post(h2, cb, z, o, sgc, sga, conv_w,
                              w_conv_out.astype(BF16), w_attn_out.astype(BF16),
                              w_o.astype(BF16), norm2_g[None, :], wr, seq=seq)

    blk = MOE_ROWS
    local0, local1, runs = _rank(meta, cnt, blk=blk)
    local0, local1 = local0.reshape(-1), local1.reshape(-1)
    runs = runs.reshape(-1, 8, LANES)
    tables = _tile_tables(runs[:, 0, :N_EXPERTS], runs[:, 1, :N_EXPERTS], tb=TB_RANK)
    p_rows = t * TOP_K + N_EXPERTS * blk
    sched = _expert_schedule(cnt[0, :N_EXPERTS].astype(jnp.int32), blk=blk,
                             nb=p_rows // blk)

    xs = _dispatch(local0, local1, tables, sched["pad_start"], sched["pad_len"],
                   sched["nused"], u2, p_rows=p_rows, blk=blk)
    ys = _experts(sched, sched["nused"], xs, w1, w3, w2, blk=blk)
    return _combine(local0, local1, tables, x1, meta, ys)


def kernel(x, norm1_g, w_in, conv_w, q_norm_g, k_norm_g, w_conv_out, w_attn_out, w_o,
           norm2_g, w_group, w_router, w1, w3, w2):
    batch, seq, d_model = x.shape
    h2 = x.reshape(batch * seq, d_model)
    for l in range(norm1_g.shape[0]):
        h2 = _layer(h2, batch=batch, seq=seq, norm1_g=norm1_g[l], w_in=w_in[l],
                    conv_w=conv_w[l], q_norm_g=q_norm_g[l], k_norm_g=k_norm_g[l],
                    w_conv_out=w_conv_out[l], w_attn_out=w_attn_out[l], w_o=w_o[l],
                    norm2_g=norm2_g[l], w_group=w_group[l], w_router=w_router[l],
                    w1=w1[l], w3=w3[l], w2=w2[l])
    return h2.reshape(batch, seq, d_model)
```

```python
import functools

import jax
import jax.numpy as jnp
import numpy as np
from jax import lax
from jax.experimental import pallas as pl
from jax.experimental.pallas import tpu as pltpu

F32 = jnp.float32
BF16 = jnp.bfloat16

GRID_W = 64
EPS = 1e-6
N_HEADS = 8
N_KV_HEADS = 2
HEAD_DIM = 128
ROPE_THETA = 10000.0
N_GROUPS = 8
EXPERTS_PER_GROUP = 8
N_EXPERTS = N_GROUPS * EXPERTS_PER_GROUP
TOP_K = 2
LOG2_E = 1.4426950408889634

LANES = 128
MXU_DIM = 256
V7X_VMEM_LIMIT_BYTES = 56000 * 1024

TM_PROJ = 512
POST_SPLIT = 1
TQ_ATTN = 512
TK_ATTN = 512
TB_RANK = 512
TB_ROWS = 256
RUN_CHUNK = 8
WEIGHT_RING = 3
WEIGHT_DMA_PRIORITY = 1
MOE_ROWS = 256
BF16_SUBLANES = 16
F32_SUBLANES = 8


def _cparams(sem):
    return pltpu.CompilerParams(dimension_semantics=sem,
                                vmem_limit_bytes=V7X_VMEM_LIMIT_BYTES)


ROW_TILE = 8


def _row_tiles(shape2d):
    rows, width = shape2d
    assert width == ROW_TILE * LANES
    return (rows * ROW_TILE, LANES)


def _rows(ref, r, n=1):
    return ref.at[pl.ds(pl.multiple_of(r * ROW_TILE, ROW_TILE), n * ROW_TILE)]


def _load_row_tiles(ref):
    rows = ref.shape[0] // ROW_TILE
    return jnp.concatenate(
        [ref[pl.ds(s, rows, stride=ROW_TILE), :] for s in range(ROW_TILE)], axis=1)


def _store_row_tiles(ref, value, row0=0):
    rows = value.shape[0]
    for s in range(ROW_TILE):
        ref[pl.ds(row0 * ROW_TILE + s, rows, stride=ROW_TILE), :] = (
            value[:, s * LANES:(s + 1) * LANES])


def _resident(shape):
    nd = len(shape)
    return pl.BlockSpec(shape, lambda *_: (0,) * nd, pipeline_mode=pl.Buffered(1))


def _head_norm_rope(xh, g, c, sa, sb):
    ms = jnp.mean(xh * xh, axis=-1, keepdims=True)
    y = xh * lax.rsqrt(ms + EPS) * g
    y_next = pltpu.roll(y, HEAD_DIM - 1, axis=1)
    y_prev = pltpu.roll(y, 1, axis=1)
    return y * c + y_next * sa + y_prev * sb


def _sigmoid(x):
    return 0.5 * jnp.tanh(0.5 * x) + 0.5


def _inproj_kernel(x_ref, g1_ref, w_ref, gq_ref, gk_ref,
                   cq_ref, saq_ref, sbq_ref, ck_ref, sak_ref, sbk_ref,
                   cb_ref, z_ref, q_ref, k_ref, v_ref, sgc_ref, sga_ref,
                   *, d_conv, d_q, d_kv, d_model):
    x = x_ref[...]
    ms = jnp.mean(x * x, axis=-1, keepdims=True)
    u = (x * lax.rsqrt(ms + EPS) * g1_ref[...]).astype(BF16)

    def proj(lo, width):
        return jnp.dot(u, w_ref[:, lo:lo + width], preferred_element_type=F32)

    o_cb, o_cc, o_cx = 0, d_conv, 2 * d_conv
    o_q = 3 * d_conv
    o_k = o_q + d_q
    o_v = o_k + d_kv
    o_gc = o_v + d_kv
    o_ga = o_gc + d_model

    sgc_ref[...] = _sigmoid(proj(o_gc, d_model)).astype(BF16)
    sga_ref[...] = _sigmoid(proj(o_ga, d_model)).astype(BF16)

    q = proj(o_q, d_q)
    gq = gq_ref[...]
    cq, saq, sbq = cq_ref[...], saq_ref[...], sbq_ref[...]
    for h in range(d_q // HEAD_DIM):
        sl = slice(h * HEAD_DIM, (h + 1) * HEAD_DIM)
        q_ref[:, sl] = _head_norm_rope(q[:, sl], gq, cq, saq, sbq).astype(BF16)

    k = proj(o_k, d_kv)
    gk = gk_ref[...]
    ck, sak, sbk = ck_ref[...], sak_ref[...], sbk_ref[...]
    for h in range(d_kv // HEAD_DIM):
        sl = slice(h * HEAD_DIM, (h + 1) * HEAD_DIM)
        k_ref[:, sl] = _head_norm_rope(k[:, sl], gk, ck, sak, sbk).astype(BF16)

    z_ref[...] = (proj(o_cc, d_conv) * proj(o_cx, d_conv)).astype(BF16)
    v_ref[...] = proj(o_v, d_kv).astype(BF16)
    cb_ref[...] = proj(o_cb, d_conv).astype(BF16)


def _rope_tables(seq):
    rows = seq // GRID_W
    axis_dim = HEAD_DIM // 2
    row = np.repeat(np.arange(rows, dtype=np.float32), GRID_W)
    col = np.tile(np.arange(GRID_W, dtype=np.float32), rows)
    inv = (np.float32(ROPE_THETA)
           ** (-np.arange(0, axis_dim, 2, dtype=np.float32) / np.float32(axis_dim)))
    ang = np.concatenate([row[:, None] * inv, col[:, None] * inv], axis=-1)
    ang = ang.astype(np.float32)
    cos, sin = np.cos(ang), np.sin(ang)
    zero = np.zeros_like(sin)
    c = np.repeat(cos, 2, axis=-1)
    sa = np.stack([-sin, zero], axis=-1).reshape(seq, HEAD_DIM)
    sb = np.stack([zero, sin], axis=-1).reshape(seq, HEAD_DIM)
    return c, sa, sb


def _inproj(x2, g1, w_in_bf, gq, gk, tables_q, tables_k, *, seq):
    t, d_model = x2.shape
    d_q = N_HEADS * HEAD_DIM
    d_kv = N_KV_HEADS * HEAD_DIM
    d_in = w_in_bf.shape[1]
    d_conv = (d_in - d_q - 2 * d_kv - 2 * d_model) // 3
    tm = TM_PROJ
    nseq = seq // tm

    def row(width):
        return pl.BlockSpec((tm, width), lambda i: (i, 0))

    table = pl.BlockSpec((tm, HEAD_DIM), lambda i: (i % nseq, 0))
    kern = functools.partial(_inproj_kernel, d_conv=d_conv, d_q=d_q, d_kv=d_kv,
                             d_model=d_model)
    out_shape = [jax.ShapeDtypeStruct((t, w), BF16)
                 for w in (d_conv, d_conv, d_q, d_kv, d_kv, d_model, d_model)]
    return pl.pallas_call(
        kern,
        grid=(t // tm,),
        in_specs=[row(d_model), _resident((1, d_model)), _resident((d_model, d_in)),
                  _resident((1, HEAD_DIM)), _resident((1, HEAD_DIM)),
                  table, table, table, table, table, table],
        out_specs=[row(d_conv), row(d_conv), row(d_q), row(d_kv), row(d_kv),
                   row(d_model), row(d_model)],
        out_shape=out_shape,
        compiler_params=_cparams(("arbitrary",)),
        name="inproj",
    )(x2, g1, w_in_bf, gq, gk, *tables_q, *tables_k)


def _attn_kernel(q_ref, k_ref, v_ref, o_ref, qs_ref, vext_ref, m_ref, acc_ref,
                 *, tq, chunks, group):
    @pl.when(pl.program_id(2) == 0)
    def _():
        vext_ref[:, :HEAD_DIM] = v_ref[...]
        vext_ref[:, HEAD_DIM:] = jnp.ones((vext_ref.shape[0], HEAD_DIM), BF16)

    for g in range(group):
        qs_ref[g * tq:(g + 1) * tq, :] = q_ref[:, g * HEAD_DIM:(g + 1) * HEAD_DIM]
    m_ref[...] = jnp.full(m_ref.shape, -jnp.inf, F32)
    acc_ref[...] = jnp.zeros(acc_ref.shape, F32)

    lo = 0
    for tk in chunks:
        keys = slice(lo, lo + tk)
        lo += tk
        s = lax.dot_general(qs_ref[...], k_ref[keys, :], (((1,), (1,)), ((), ())),
                            preferred_element_type=F32)
        m_prev = m_ref[...]
        m_new = jnp.maximum(m_prev, jnp.max(s, axis=-1, keepdims=True))
        alpha = jnp.exp2(m_prev - m_new)
        p = jnp.concatenate(
            [jnp.exp2(s[:, c * LANES:(c + 1) * LANES] - m_new) for c in range(tk // LANES)],
            axis=1).astype(BF16)
        pv = jnp.dot(p, vext_ref[keys, :], preferred_element_type=F32)
        acc_ref[...] = jnp.concatenate([alpha, alpha], axis=1) * acc_ref[...] + pv
        m_ref[...] = m_new

    out = acc_ref[:, :HEAD_DIM] / acc_ref[:, HEAD_DIM:]
    for g in range(group):
        o_ref[:, g * HEAD_DIM:(g + 1) * HEAD_DIM] = out[g * tq:(g + 1) * tq].astype(BF16)


def _attention(q, k, v, *, batch, seq):
    t = q.shape[0]
    group = N_HEADS // N_KV_HEADS
    tq, tk = TQ_ATTN, TK_ATTN
    nq = seq // tq
    gw = group * HEAD_DIM
    chunks = (tk // 2,) + (tk,) * (seq // tk - 1) + (tk // 2,)
    kern = functools.partial(_attn_kernel, tq=tq, chunks=chunks, group=group)
    return pl.pallas_call(
        kern,
        grid=(batch, N_KV_HEADS, nq),
        in_specs=[pl.BlockSpec((tq, gw), lambda b, h, i: (b * nq + i, h)),
                  pl.BlockSpec((seq, HEAD_DIM), lambda b, h, i: (b, h)),
                  pl.BlockSpec((seq, HEAD_DIM), lambda b, h, i: (b, h))],
        out_specs=pl.BlockSpec((tq, gw), lambda b, h, i: (b * nq + i, h)),
        out_shape=jax.ShapeDtypeStruct((t, N_HEADS * HEAD_DIM), BF16),
        scratch_shapes=[pltpu.VMEM((group * tq, HEAD_DIM), BF16),
                        pltpu.VMEM((seq, 2 * HEAD_DIM), BF16),
                        pltpu.VMEM((group * tq, LANES), F32),
                        pltpu.VMEM((group * tq, 2 * HEAD_DIM), F32)],
        compiler_params=_cparams(("arbitrary", "arbitrary", "arbitrary")),
        name="attention",
    )(q, k, v)


def _route(logits):
    rows = logits.shape[0]
    lane = lax.broadcasted_iota(jnp.int32, (rows, LANES), 1).astype(F32)
    neg = -jnp.inf
    big = float(2 * LANES)
    is_group = lane < N_GROUPS
    gl = jnp.where(is_group, logits, neg)
    gmax = jnp.max(gl, axis=-1, keepdims=True)
    gidx = jnp.min(jnp.where(gl == gmax, lane, big), axis=-1, keepdims=True)
    gsum = jnp.sum(jnp.where(is_group, jnp.exp(logits - gmax), 0.0), axis=-1,
                   keepdims=True)
    pg = 1.0 / gsum
    lane_group = jnp.floor(lane * (1.0 / EXPERTS_PER_GROUP)) - 1.0
    mine = (lane_group == gidx) & (lane >= N_GROUPS) & (lane < N_GROUPS + N_EXPERTS)
    sel = jnp.where(mine, logits, neg)
    v1 = jnp.max(sel, axis=-1, keepdims=True)
    i1 = jnp.min(jnp.where(sel == v1, lane, big), axis=-1, keepdims=True)
    sel2 = jnp.where(lane == i1, neg, sel)
    v2 = jnp.max(sel2, axis=-1, keepdims=True)
    i2 = jnp.min(jnp.where(sel2 == v2, lane, big), axis=-1, keepdims=True)
    t2 = jnp.exp(v2 - v1)
    den = 1.0 + t2
    wgt1 = pg * (1.0 / den)
    wgt2 = pg * (t2 / den)
    e1 = i1 - N_GROUPS
    e2 = i2 - N_GROUPS
    meta = jnp.where(lane == 0, e1,
                     jnp.where(lane == 1, e2,
                               jnp.where(lane == 2, wgt1,
                                         jnp.where(lane == 3, wgt2, 0.0))))
    picked = jnp.where((lane == e1) | (lane == e2), 1.0, 0.0)
    return meta, jnp.sum(picked, axis=0, keepdims=True)


def _post_kernel(x_ref, cb_ref, z_ref, zprev_ref, znext_ref, o_ref, sgc_ref, sga_ref,
                 cw_ref, wc_ref, wa_ref, wo_ref, g2_ref, wr_ref,
                 x1_ref, u2_ref, meta_ref, cnt_ref, *, tm, nseq, nsplit):
    i = pl.program_id(0)
    at_start = (i % nseq) == 0
    at_end = (i % nseq) == nseq - 1
    rows = tm // nsplit
    rowid = lax.broadcasted_iota(jnp.int32, (rows, 1), 0)

    @pl.when(i == 0)
    def _():
        cnt_ref[...] = jnp.zeros(cnt_ref.shape, F32)

    for part_id in range(nsplit):
        r0 = part_id * rows
        rs = slice(r0, r0 + rows)
        y_attn = jnp.dot(o_ref[rs, :], wa_ref[...], preferred_element_type=F32)

        y_conv = None
        for c in range(z_ref.shape[1] // MXU_DIM):
            sl = slice(c * MXU_DIM, (c + 1) * MXU_DIM)
            z = z_ref[rs, sl].astype(F32)
            if part_id == 0:
                prev_row = zprev_ref[BF16_SUBLANES - 1:BF16_SUBLANES, sl].astype(F32)
                prev_row = jnp.where(at_start, 0.0, prev_row)
            else:
                prev_row = z_ref[r0 - 1:r0, sl].astype(F32)
            if part_id == nsplit - 1:
                next_row = znext_ref[0:1, sl].astype(F32)
                next_row = jnp.where(at_end, 0.0, next_row)
            else:
                next_row = z_ref[r0 + rows:r0 + rows + 1, sl].astype(F32)
            zp = jnp.where(rowid == 0, prev_row, pltpu.roll(z, 1, axis=0))
            zn = jnp.where(rowid == rows - 1, next_row, pltpu.roll(z, rows - 1, axis=0))
            conv = cw_ref[0:1, sl] * zp + cw_ref[1:2, sl] * z + cw_ref[2:3, sl] * zn
            cbz = (cb_ref[rs, sl].astype(F32) * conv).astype(BF16)
            part = jnp.dot(cbz, wc_ref[sl, :], preferred_element_type=F32)
            y_conv = part if y_conv is None else y_conv + part
        merged = (sgc_ref[rs, :].astype(F32) * y_conv
                  + sga_ref[rs, :].astype(F32) * y_attn).astype(BF16)
        x1 = x_ref[rs, :] + jnp.dot(merged, wo_ref[...], preferred_element_type=F32)
        x1_ref[rs, :] = x1

        ms = jnp.mean(x1 * x1, axis=-1, keepdims=True)
        u2 = x1 * lax.rsqrt(ms + EPS) * g2_ref[...]
        _store_row_tiles(u2_ref, u2, r0)

        u2_hi = u2.astype(BF16)
        u2_lo = (u2 - u2_hi.astype(F32)).astype(BF16)
        hi_part = jnp.dot(u2_hi, wr_ref[...], preferred_element_type=F32)
        lo_part = jnp.dot(u2_lo, wr_ref[:, :LANES], preferred_element_type=F32)
        logits = hi_part[:, :LANES] + (hi_part[:, LANES:] + lo_part)
        meta, picked = _route(logits)
        meta_ref[rs, :] = meta
        cnt_ref[...] += picked


def _post(x2, cb, z, o, sgc, sga, conv_w, wc, wa, wo, g2, wr, *, seq):
    t, d_model = x2.shape
    tm = TM_PROJ
    nseq = seq // tm
    hb = tm // BF16_SUBLANES
    nhalo = t // BF16_SUBLANES
    d_conv = cb.shape[1]
    d_q = o.shape[1]

    def row(width):
        return pl.BlockSpec((tm, width), lambda i: (i, 0))

    kern = functools.partial(_post_kernel, tm=tm, nseq=nseq, nsplit=POST_SPLIT)
    return pl.pallas_call(
        kern,
        grid=(t // tm,),
        in_specs=[row(d_model), row(d_conv), row(d_conv),
                  pl.BlockSpec((BF16_SUBLANES, d_conv),
                               lambda i: (jnp.maximum(i * hb - 1, 0), 0)),
                  pl.BlockSpec((BF16_SUBLANES, d_conv),
                               lambda i: (jnp.minimum((i + 1) * hb, nhalo - 1), 0)),
                  row(d_q), row(d_model), row(d_model),
                  _resident(conv_w.shape), _resident(wc.shape), _resident(wa.shape),
                  _resident(wo.shape), _resident(g2.shape), _resident(wr.shape)],
        out_specs=[row(d_model),
                   pl.BlockSpec(_row_tiles((tm, d_model)), lambda i: (i, 0)),
                   row(LANES),
                   pl.BlockSpec((1, LANES), lambda i: (0, 0))],
        out_shape=[jax.ShapeDtypeStruct((t, d_model), F32),
                   jax.ShapeDtypeStruct(_row_tiles((t, d_model)), F32),
                   jax.ShapeDtypeStruct((t, LANES), F32),
                   jax.ShapeDtypeStruct((1, LANES), F32)],
        compiler_params=_cparams(("arbitrary",)),
        name="post",
    )(x2, cb, z, z, z, o, sgc, sga, conv_w, wc, wa, wo, g2, wr)


def _rank_kernel(meta_ref, cnt_ref, l0_ref, l1_ref, runs_ref,
                 carry_ref, pstart_ref, *, tb, blk):
    i = pl.program_id(0)
    lane = lax.broadcasted_iota(jnp.int32, (tb, LANES), 1).astype(F32)
    meta = meta_ref[...]
    oh1 = jnp.where(lane == meta[:, 0:1], 1.0, 0.0)
    oh2 = jnp.where(lane == meta[:, 1:2], 1.0, 0.0)
    c = oh1 + oh2

    @pl.when(i == 0)
    def _():
        cnt = jnp.broadcast_to(cnt_ref[...], (8, LANES))
        nblk = jnp.floor((cnt + (blk - 1)) * (1.0 / blk))
        r = lax.broadcasted_iota(jnp.int32, (LANES, LANES), 0)
        col = lax.broadcasted_iota(jnp.int32, (LANES, LANES), 1)
        upper = jnp.where(r < col, 1.0, 0.0).astype(BF16)
        pblk = jnp.dot(nblk.astype(BF16), upper, preferred_element_type=F32)
        pstart_ref[...] = pblk[0:1, :] * blk
        carry_ref[...] = jnp.zeros(carry_ref.shape, F32)

    r = lax.broadcasted_iota(jnp.int32, (tb, tb), 0)
    col = lax.broadcasted_iota(jnp.int32, (tb, tb), 1)
    lower = jnp.where(col < r, 1.0, 0.0).astype(BF16)
    prefix = jnp.dot(lower, c.astype(BF16), preferred_element_type=F32)
    run_start = carry_ref[...] + pstart_ref[...]
    run_len = jnp.sum(c, axis=0, keepdims=True)

    nchunk = jnp.floor((run_len + (RUN_CHUNK - 1)) * (1.0 / RUN_CHUNK))
    r = lax.broadcasted_iota(jnp.int32, (LANES, LANES), 0)
    col = lax.broadcasted_iota(jnp.int32, (LANES, LANES), 1)
    upper = jnp.where(r < col, 1.0, 0.0).astype(BF16)
    local_start = jnp.dot(jnp.broadcast_to(nchunk, (8, LANES)).astype(BF16), upper,
                          preferred_element_type=F32)[0:1, :] * RUN_CHUNK
    local = prefix + local_start

    digits = jnp.zeros((tb, LANES), F32)
    for j, onehot in enumerate((oh1, oh2)):
        value = jnp.sum(onehot * local, axis=-1, keepdims=True)
        high = jnp.floor(value * (1.0 / 256.0))
        digits = jnp.where(lane == 2 * j, high,
                           jnp.where(lane == 2 * j + 1, value - high * 256.0, digits))
    r = lax.broadcasted_iota(jnp.int32, (tb, tb), 0)
    col = lax.broadcasted_iota(jnp.int32, (tb, tb), 1)
    eye = jnp.where(col == r, 1.0, 0.0).astype(BF16)
    rows = lax.dot_general(digits.astype(BF16), eye, (((0,), (0,)), ((), ())),
                           preferred_element_type=F32)
    for j, out_ref in enumerate((l0_ref, l1_ref)):
        out_ref[...] = (rows[2 * j:2 * j + 1, :] * 256.0
                        + rows[2 * j + 1:2 * j + 2, :]).astype(jnp.int32)
    row = lax.broadcasted_iota(jnp.int32, (8, LANES), 0)
    runs = jnp.where(row == 0, run_start, jnp.where(row == 1, run_len, 0.0))
    runs_ref[...] = runs.astype(jnp.int32)
    carry_ref[...] += run_len


def _rank(meta, cnt, *, blk):
    t = meta.shape[0]
    tb = TB_RANK
    kern = functools.partial(_rank_kernel, tb=tb, blk=blk)
    per_token = pl.BlockSpec((1, tb), lambda i: (0, i))
    return pl.pallas_call(
        kern,
        grid=(t // tb,),
        in_specs=[pl.BlockSpec((tb, LANES), lambda i: (i, 0)),
                  pl.BlockSpec((1, LANES), lambda i: (0, 0))],
        out_specs=[per_token] * 2 + [pl.BlockSpec((8, LANES), lambda i: (i, 0))],
        out_shape=[jax.ShapeDtypeStruct((1, t), jnp.int32)] * 2
                  + [jax.ShapeDtypeStruct((t // tb * 8, LANES), jnp.int32)],
        scratch_shapes=[pltpu.VMEM((1, LANES), F32), pltpu.VMEM((1, LANES), F32)],
        compiler_params=_cparams(("arbitrary",)),
        name="rank",
    )(meta, cnt)


def _dispatch_kernel(local0_sm, local1_sm, full_src_sm, full_dst_sm, nfull_sm,
                     rem_src_sm, rem_dst_sm, rem_len_sm, pad_start_sm, pad_len_sm, nused_sm,
                     u2_ref, xs_ref, buf_ref, zero_ref, sem, zsem,
                     *, tb, blk, nblocks, max_full):
    step = pl.program_id(0)
    nsteps = pl.num_programs(0)
    base = step * tb
    nbits = blk.bit_length() - 1
    slot = step % 2
    tile_buf = buf_ref.at[slot]

    def wait_tile(s):
        pltpu.make_async_copy(_rows(buf_ref.at[s], 0, tb * TOP_K),
                              _rows(xs_ref, 0, tb * TOP_K), sem.at[s]).wait()

    def pad_copies(visit):
        def per_expert(e, carry):
            start, length = pad_start_sm[e], pad_len_sm[e]
            for bit in range(nbits):
                size = 1 << bit
                higher = lax.shift_left(lax.shift_right_logical(length, bit + 1), bit + 1)

                @pl.when((lax.shift_right_logical(length, bit) & 1) == 1)
                def _():
                    visit(pltpu.make_async_copy(
                        zero_ref.at[pl.ds(0, size * ROW_TILE)],
                        _rows(xs_ref, start + higher, size), zsem))
            return carry

        lax.fori_loop(0, N_EXPERTS, per_expert, 0)

        def per_block(b, carry):
            visit(pltpu.make_async_copy(zero_ref, _rows(xs_ref, b * blk, blk), zsem))
            return carry

        lax.fori_loop(nused_sm[0], nblocks, per_block, 0)

    @pl.when(step == 0)
    def _():
        zero_ref[...] = jnp.zeros(zero_ref.shape, F32)
        pad_copies(lambda c: c.start())

    @pl.when(step >= 2)
    def _():
        wait_tile(slot)

    def move(r, carry):
        row = _rows(u2_ref, r)[...]
        _rows(tile_buf, local0_sm[base + r])[...] = row
        _rows(tile_buf, local1_sm[base + r])[...] = row
        return carry

    lax.fori_loop(0, tb, move, 0, unroll=8)

    def full_chunk(c, carry):
        pltpu.make_async_copy(
            _rows(tile_buf, full_src_sm[step * max_full + c], RUN_CHUNK),
            _rows(xs_ref, full_dst_sm[step * max_full + c], RUN_CHUNK),
            sem.at[slot]).start()
        return carry

    lax.fori_loop(0, nfull_sm[step], full_chunk, 0)

    def remainder(e, carry):
        idx = step * N_EXPERTS + e
        src, dst, length = rem_src_sm[idx], rem_dst_sm[idx], rem_len_sm[idx]
        for bit in range(RUN_CHUNK.bit_length() - 1):
            size = 1 << bit
            higher = lax.shift_left(lax.shift_right_logical(length, bit + 1), bit + 1)

            @pl.when((lax.shift_right_logical(length, bit) & 1) == 1)
            def _():
                pltpu.make_async_copy(_rows(tile_buf, src + higher, size),
                                      _rows(xs_ref, dst + higher, size),
                                      sem.at[slot]).start()
        return carry

    lax.fori_loop(0, N_EXPERTS, remainder, 0)

    @pl.when(step == nsteps - 1)
    def _():
        @pl.when(step >= 1)
        def _():
            wait_tile(1 - slot)

        wait_tile(slot)
        pad_copies(lambda c: c.wait())


def _tile_tables(run_start, run_len, *, tb):
    shift = RUN_CHUNK.bit_length() - 1
    nchunk = (run_len + RUN_CHUNK - 1) >> shift
    cum = jnp.cumsum(nchunk, axis=1)
    first = cum - nchunk
    local_start = first * RUN_CHUNK

    def flat_list(count, max_count, value_at):
        ccum = jnp.cumsum(count, axis=1)
        cfirst = (ccum - count)[:, None, :]
        c_idx = jnp.arange(max_count, dtype=jnp.int32)[None, :, None]
        owns = (cfirst <= c_idx) & (c_idx < ccum[:, None, :])
        return [jnp.sum(jnp.where(owns, v[:, None, :] + (c_idx - cfirst) * RUN_CHUNK, 0),
                        axis=2).astype(jnp.int32) for v in value_at], ccum[:, -1]

    max_chunks = tb * TOP_K // RUN_CHUNK + N_EXPERTS
    (chunk_src,), tile_chunks = flat_list(nchunk, max_chunks, [run_start])
    max_full = tb * TOP_K // RUN_CHUNK
    nfull = run_len >> shift
    (full_src, full_dst), tile_full = flat_list(nfull, max_full, [local_start, run_start])
    whole = nfull * RUN_CHUNK
    return dict(
        max_chunks=max_chunks, chunk_src=chunk_src.reshape(-1),
        tile_chunks=tile_chunks.astype(jnp.int32),
        max_full=max_full, full_src=full_src.reshape(-1), full_dst=full_dst.reshape(-1),
        tile_full=tile_full.astype(jnp.int32),
        rem_src=(local_start + whole).reshape(-1).astype(jnp.int32),
        rem_dst=(run_start + whole).reshape(-1).astype(jnp.int32),
        rem_len=(run_len - whole).reshape(-1).astype(jnp.int32))


def _dispatch(local0, local1, tables, pad_start, pad_len, nused, u2, *, p_rows, blk):
    t = u2.shape[0] // ROW_TILE
    tb = TB_RANK
    d_model = ROW_TILE * LANES
    kern = functools.partial(_dispatch_kernel, tb=tb, blk=blk, nblocks=p_rows // blk,
                             max_full=tables["max_full"])
    buf_rows = tables["max_chunks"] * RUN_CHUNK
    return pl.pallas_call(
        kern,
        grid_spec=pltpu.PrefetchScalarGridSpec(
            num_scalar_prefetch=11,
            grid=(t // tb,),
            in_specs=[pl.BlockSpec((tb * ROW_TILE, LANES), lambda i, *_: (i, 0))],
            out_specs=pl.BlockSpec(memory_space=pl.ANY),
            scratch_shapes=[pltpu.VMEM((2,) + _row_tiles((buf_rows, d_model)), F32),
                            pltpu.VMEM((blk * ROW_TILE, LANES), F32),
                            pltpu.SemaphoreType.DMA((2,)), pltpu.SemaphoreType.DMA]),
        out_shape=jax.ShapeDtypeStruct((p_rows * ROW_TILE, LANES), F32),
        compiler_params=_cparams(("arbitrary",)),
        name="dispatch",
    )(local0, local1, tables["full_src"], tables["full_dst"], tables["tile_full"],
      tables["rem_src"], tables["rem_dst"], tables["rem_len"],
      pad_start, pad_len, nused, u2)


def _experts_kernel(be_sm, first_sm, ahead_sm, slot_sm, head_sm, nused_sm,
                    xs_ref, w1_hbm, w3_hbm, w2_hbm, ys_ref,
                    w1f_ref, w3f_ref, w2f_ref, w13b_ref, w2b_ref, wsem, *, d_expert):
    b = pl.program_id(0)
    used = b < nused_sm[0]

    def weight_copies(expert, slot):
        return [pltpu.make_async_copy(src.at[expert], dst.at[slot], wsem.at[slot])
                for src, dst in ((w1_hbm, w1f_ref), (w3_hbm, w3f_ref), (w2_hbm, w2f_ref))]

    @pl.when(b == 0)
    def _():
        for j in range(WEIGHT_RING - 1):
            @pl.when(head_sm[j] >= 0)
            def _():
                for c in weight_copies(head_sm[j], j):
                    c.start(priority=WEIGHT_DMA_PRIORITY)

    @pl.when(first_sm[b] == 1)
    def _():
        slot = slot_sm[b]
        for c in weight_copies(be_sm[b], slot):
            c.wait()

        @pl.when(ahead_sm[b] >= 0)
        def _():
            ahead_slot = lax.rem(slot + (WEIGHT_RING - 1), WEIGHT_RING)
            for c in weight_copies(ahead_sm[b], ahead_slot):
                c.start(priority=WEIGHT_DMA_PRIORITY)

        w13b_ref[:, :d_expert] = w1f_ref[slot].astype(BF16)
        w13b_ref[:, d_expert:] = w3f_ref[slot].astype(BF16)
        w2b_ref[...] = w2f_ref[slot].astype(BF16)

    @pl.when(used)
    def _():
        xb = _load_row_tiles(xs_ref).astype(BF16)
        h = jnp.dot(xb, w13b_ref[...], preferred_element_type=F32)
        h1 = h[:, :d_expert]
        h3 = h[:, d_expert:]
        a = (h1 * _sigmoid(h1) * h3).astype(BF16)
        _store_row_tiles(ys_ref, jnp.dot(a, w2b_ref[...], preferred_element_type=F32))

    @pl.when(jnp.logical_not(used))
    def _():
        ys_ref[...] = jnp.zeros(ys_ref.shape, F32)


def _experts(sched, nused, xs, w1, w3, w2, *, blk):
    p_rows = xs.shape[0] // ROW_TILE
    d_model, d_expert = w1.shape[-2:]
    kern = functools.partial(_experts_kernel, d_expert=d_expert)
    rows_blk = (blk * ROW_TILE, LANES)

    def rows_in(b, be, first, ahead, slot, head, nu):
        return (jnp.maximum(jnp.minimum(b, nu[0] - 1), 0), 0)

    hbm = pl.BlockSpec(memory_space=pl.ANY)
    return pl.pallas_call(
        kern,
        grid_spec=pltpu.PrefetchScalarGridSpec(
            num_scalar_prefetch=6,
            grid=(p_rows // blk,),
            in_specs=[pl.BlockSpec(rows_blk, rows_in), hbm, hbm, hbm],
            out_specs=pl.BlockSpec(rows_blk, lambda b, *_: (b, 0)),
            scratch_shapes=[pltpu.VMEM((WEIGHT_RING, d_model, d_expert), F32),
                            pltpu.VMEM((WEIGHT_RING, d_model, d_expert), F32),
                            pltpu.VMEM((WEIGHT_RING, d_expert, d_model), F32),
                            pltpu.VMEM((d_model, 2 * d_expert), BF16),
                            pltpu.VMEM((d_expert, d_model), BF16),
                            pltpu.SemaphoreType.DMA((WEIGHT_RING,))]),
        out_shape=jax.ShapeDtypeStruct((p_rows * ROW_TILE, LANES), F32),
        compiler_params=_cparams(("arbitrary",)),
        name="experts",
    )(sched["expert"], sched["first"], sched["ahead"], sched["slot"], sched["head"],
      nused, xs, w1, w3, w2)


def _expert_schedule(counts, *, blk, nb):
    nblk = (counts + blk - 1) // blk
    pend = jnp.cumsum(nblk)
    nused = pend[-1]
    barange = jnp.arange(nb, dtype=jnp.int32)
    bidx = jnp.minimum(barange, nused - 1)
    expert = jnp.minimum(jnp.sum(pend[None, :] <= bidx[:, None], axis=1),
                         N_EXPERTS - 1).astype(jnp.int32)
    live = barange < nused
    first = (live & (barange == (pend - nblk)[expert])).astype(jnp.int32)
    has_rows = nblk > 0
    used_rank = jnp.cumsum(has_rows) - 1
    j_idx = jnp.arange(N_EXPERTS + WEIGHT_RING, dtype=jnp.int32)
    e_idx = jnp.arange(N_EXPERTS, dtype=jnp.int32)
    hit = has_rows[None, :] & (used_rank[None, :] == j_idx[:, None])
    used_list = jnp.sum(jnp.where(hit, e_idx[None, :] + 1, 0), axis=1) - 1
    rank_b = used_rank[expert]
    ahead = jnp.where(live, used_list[rank_b + (WEIGHT_RING - 1)], -1)
    return dict(
        expert=expert, first=first, ahead=ahead.astype(jnp.int32),
        slot=(rank_b % WEIGHT_RING).astype(jnp.int32),
        head=used_list[:WEIGHT_RING - 1].astype(jnp.int32),
        nused=nused.reshape(1).astype(jnp.int32),
        pad_start=((pend - nblk) * blk + counts).astype(jnp.int32),
        pad_len=(nblk * blk - counts).astype(jnp.int32))


def _combine_kernel(local0_sm, local1_sm, chunk_src_sm, nchunk_sm,
                    x1_ref, meta_ref, ys_ref, out_ref,
                    buf_ref, g0_ref, g1_ref, sem, *, tb, max_chunks):
    step = pl.program_id(0)
    nsteps = pl.num_programs(0)

    def start_chunks(tile):
        slot = tile % 2

        def per_chunk(c, carry):
            pltpu.make_async_copy(
                _rows(ys_ref, chunk_src_sm[tile * max_chunks + c], RUN_CHUNK),
                _rows(buf_ref.at[slot], c * RUN_CHUNK, RUN_CHUNK),
                sem.at[slot]).start()
            return carry

        lax.fori_loop(0, nchunk_sm[tile], per_chunk, 0)

    def wait_chunks(tile):
        slot = tile % 2
        n = nchunk_sm[tile]
        for bit in range(max_chunks.bit_length()):
            @pl.when((lax.shift_right_logical(n, bit) & 1) == 1)
            def _():
                rows = RUN_CHUNK << bit
                pltpu.make_async_copy(_rows(ys_ref, 0, rows),
                                      _rows(buf_ref.at[slot], 0, rows),
                                      sem.at[slot]).wait()

    @pl.when(step == 0)
    def _():
        start_chunks(step)

    @pl.when(step + 1 < nsteps)
    def _():
        start_chunks(step + 1)

    wait_chunks(step)

    base = step * tb
    tile_buf = buf_ref.at[step % 2]

    def move(r, carry):
        for g_ref, local_sm in ((g0_ref, local0_sm), (g1_ref, local1_sm)):
            _rows(g_ref, r)[...] = _rows(tile_buf, local_sm[base + r])[...]
        return carry

    lax.fori_loop(0, tb, move, 0, unroll=8)
    meta = meta_ref[...]
    moe = (_load_row_tiles(g0_ref) * meta[:, 2:3]
           + _load_row_tiles(g1_ref) * meta[:, 3:4])
    out_ref[...] = x1_ref[...] + moe


def _combine(local0, local1, tables, x1, meta, ys):
    t, d_model = x1.shape
    tb = TB_RANK
    max_chunks = tables["max_chunks"]
    kern = functools.partial(_combine_kernel, tb=tb, max_chunks=max_chunks)
    buf_rows = max_chunks * RUN_CHUNK
    return pl.pallas_call(
        kern,
        grid_spec=pltpu.PrefetchScalarGridSpec(
            num_scalar_prefetch=4,
            grid=(t // tb,),
            in_specs=[pl.BlockSpec((tb, d_model), lambda i, *_: (i, 0)),
                      pl.BlockSpec((tb, LANES), lambda i, *_: (i, 0)),
                      pl.BlockSpec(memory_space=pl.ANY)],
            out_specs=pl.BlockSpec((tb, d_model), lambda i, *_: (i, 0)),
            scratch_shapes=[pltpu.VMEM((2,) + _row_tiles((buf_rows, d_model)), F32),
                            pltpu.VMEM(_row_tiles((tb, d_model)), F32),
                            pltpu.VMEM(_row_tiles((tb, d_model)), F32),
                            pltpu.SemaphoreType.DMA((2,))]),
        out_shape=jax.ShapeDtypeStruct((t, d_model), F32),
        compiler_params=_cparams(("arbitrary",)),
        name="combine",
    )(local0, local1, tables["chunk_src"], tables["tile_chunks"], x1, meta, ys)


def _layer(h2, *, batch, seq, norm1_g, w_in, conv_w, q_norm_g, k_norm_g,
           w_conv_out, w_attn_out, w_o, norm2_g, w_group, w_router, w1, w3, w2):
    t, d_model = h2.shape
    c, sa, sb = _rope_tables(seq)
    scale = HEAD_DIM ** -0.5 * LOG2_E
    tables_q = tuple(jnp.asarray(tab * np.float32(scale)) for tab in (c, sa, sb))
    tables_k = tuple(jnp.asarray(tab) for tab in (c, sa, sb))

    cb, z, q, k, v, sgc, sga = _inproj(
        h2, norm1_g[None, :], w_in.astype(BF16), q_norm_g[None, :], k_norm_g[None, :],
        tables_q, tables_k, seq=seq)
    o = _attention(q, k, v, batch=batch, seq=seq)

    n_route = N_GROUPS + N_EXPERTS
    wr = jnp.concatenate(
        [w_group, w_router, jnp.zeros((d_model, LANES - n_route), F32)], axis=1)
    wr_hi = wr.astype(BF16)
    wr = jnp.concatenate([wr_hi, (wr - wr_hi.astype(F32)).astype(BF16)], axis=1)
    x1, u2, meta, cnt = _post(h2, cb, z, o, sgc, sga, conv_w,
                              w_conv_out.astype(BF16), w_attn_out.astype(BF16),
                              w_o.astype(BF16), norm2_g[None, :], wr, seq=seq)

    blk = MOE_ROWS
    local0, local1, runs = _rank(meta, cnt, blk=blk)
    local0, local1 = local0.reshape(-1), local1.reshape(-1)
    runs = runs.reshape(-1, 8, LANES)
    tables = _tile_tables(runs[:, 0, :N_EXPERTS], runs[:, 1, :N_EXPERTS], tb=TB_RANK)
    p_rows = t * TOP_K + N_EXPERTS * blk
    sched = _expert_schedule(cnt[0, :N_EXPERTS].astype(jnp.int32), blk=blk,
                             nb=p_rows // blk)

    xs = _dispatch(local0, local1, tables, sched["pad_start"], sched["pad_len"],
                   sched["nused"], u2, p_rows=p_rows, blk=blk)
    ys = _experts(sched, sched["nused"], xs, w1, w3, w2, blk=blk)
    return _combine(local0, local1, tables, x1, meta, ys)


def kernel(x, norm1_g, w_in, conv_w, q_norm_g, k_norm_g, w_conv_out, w_attn_out, w_o,
           norm2_g, w_group, w_router, w1, w3, w2):
    batch, seq, d_model = x.shape
    h2 = x.reshape(batch * seq, d_model)
    for l in range(norm1_g.shape[0]):
        h2 = _layer(h2, batch=batch, seq=seq, norm1_g=norm1_g[l], w_in=w_in[l],
                    conv_w=conv_w[l], q_norm_g=q_norm_g[l], k_norm_g=k_norm_g[l],
                    w_conv_out=w_conv_out[l], w_attn_out=w_attn_out[l], w_o=w_o[l],
                    norm2_g=norm2_g[l], w_group=w_group[l], w_router=w_router[l],
                    w1=w1[l], w3=w3[l], w2=w2[l])
    return h2.reshape(batch, seq, d_model)
```

```python
import functools

import jax
import jax.numpy as jnp
import numpy as np
from jax import lax
from jax.experimental import pallas as pl
from jax.experimental.pallas import tpu as pltpu

F32 = jnp.float32
BF16 = jnp.bfloat16

GRID_W = 64
EPS = 1e-6
N_HEADS = 8
N_KV_HEADS = 2
HEAD_DIM = 128
ROPE_THETA = 10000.0
N_GROUPS = 8
EXPERTS_PER_GROUP = 8
N_EXPERTS = N_GROUPS * EXPERTS_PER_GROUP
TOP_K = 2
LOG2_E = 1.4426950408889634

LANES = 128
MXU_DIM = 256
BF16_SUBLANES = 16
V7X_VMEM_LIMIT_BYTES = 56000 * 1024

TM_PROJ = 512
TQ_ATTN = 512
TK_ATTN = 512
TB_RANK = 512
RUN_CHUNK = 8
MOE_ROWS = 256
WEIGHT_RING = 3
WEIGHT_DMA_PRIORITY = 1
DIGIT = 256.0


def _cparams(sem):
    return pltpu.CompilerParams(dimension_semantics=sem,
                                vmem_limit_bytes=V7X_VMEM_LIMIT_BYTES)


ROW_TILE = 8


def _row_tiles(shape2d):
    rows, width = shape2d
    assert width == ROW_TILE * LANES
    return (rows * ROW_TILE, LANES)


def _rows(ref, r, n=1):
    return ref.at[pl.ds(pl.multiple_of(r * ROW_TILE, ROW_TILE), n * ROW_TILE)]


def _load_row_tiles(ref):
    rows = ref.shape[0] // ROW_TILE
    return jnp.concatenate(
        [ref[pl.ds(s, rows, stride=ROW_TILE), :] for s in range(ROW_TILE)], axis=1)


def _store_row_tiles(ref, value):
    rows = value.shape[0]
    for s in range(ROW_TILE):
        ref[pl.ds(s, rows, stride=ROW_TILE), :] = value[:, s * LANES:(s + 1) * LANES]


def _resident(shape):
    nd = len(shape)
    return pl.BlockSpec(shape, lambda *_: (0,) * nd, pipeline_mode=pl.Buffered(1))


def _lane_prefix(row):
    r = lax.broadcasted_iota(jnp.int32, (LANES, LANES), 0)
    col = lax.broadcasted_iota(jnp.int32, (LANES, LANES), 1)
    upper = jnp.where(r < col, 1.0, 0.0).astype(BF16)
    return jnp.dot(jnp.broadcast_to(row, (8, LANES)).astype(BF16), upper,
                   preferred_element_type=F32)[0:1, :]


def _head_norm_rope(xh, g, c, sa, sb):
    ms = jnp.mean(xh * xh, axis=-1, keepdims=True)
    y = xh * lax.rsqrt(ms + EPS) * g
    y_next = pltpu.roll(y, HEAD_DIM - 1, axis=1)
    y_prev = pltpu.roll(y, 1, axis=1)
    return y * c + y_next * sa + y_prev * sb


def _sigmoid(x):
    return 0.5 * jnp.tanh(0.5 * x) + 0.5


def _inproj_kernel(x_ref, g1_ref, w_ref, gq_ref, gk_ref,
                   cq_ref, saq_ref, sbq_ref, ck_ref, sak_ref, sbk_ref,
                   cb_ref, z_ref, q_ref, k_ref, v_ref, sgc_ref, sga_ref,
                   *, d_conv, d_q, d_kv, d_model):
    x = x_ref[...]
    ms = jnp.mean(x * x, axis=-1, keepdims=True)
    u = (x * lax.rsqrt(ms + EPS) * g1_ref[...]).astype(BF16)

    def proj(lo, width):
        return jnp.dot(u, w_ref[:, lo:lo + width], preferred_element_type=F32)

    o_cb, o_cc, o_cx = 0, d_conv, 2 * d_conv
    o_q = 3 * d_conv
    o_k = o_q + d_q
    o_v = o_k + d_kv
    o_gc = o_v + d_kv
    o_ga = o_gc + d_model

    sgc_ref[...] = _sigmoid(proj(o_gc, d_model)).astype(BF16)
    sga_ref[...] = _sigmoid(proj(o_ga, d_model)).astype(BF16)

    q = proj(o_q, d_q)
    gq = gq_ref[...]
    cq, saq, sbq = cq_ref[...], saq_ref[...], sbq_ref[...]
    for h in range(d_q // HEAD_DIM):
        sl = slice(h * HEAD_DIM, (h + 1) * HEAD_DIM)
        q_ref[:, sl] = _head_norm_rope(q[:, sl], gq, cq, saq, sbq).astype(BF16)

    k = proj(o_k, d_kv)
    gk = gk_ref[...]
    ck, sak, sbk = ck_ref[...], sak_ref[...], sbk_ref[...]
    for h in range(d_kv // HEAD_DIM):
        sl = slice(h * HEAD_DIM, (h + 1) * HEAD_DIM)
        k_ref[:, sl] = _head_norm_rope(k[:, sl], gk, ck, sak, sbk).astype(BF16)

    z_ref[...] = (proj(o_cc, d_conv) * proj(o_cx, d_conv)).astype(BF16)
    v_ref[...] = proj(o_v, d_kv).astype(BF16)
    cb_ref[...] = proj(o_cb, d_conv).astype(BF16)


def _rope_tables(seq):
    rows = seq // GRID_W
    axis_dim = HEAD_DIM // 2
    row = np.repeat(np.arange(rows, dtype=np.float32), GRID_W)
    col = np.tile(np.arange(GRID_W, dtype=np.float32), rows)
    inv = (np.float32(ROPE_THETA)
           ** (-np.arange(0, axis_dim, 2, dtype=np.float32) / np.float32(axis_dim)))
    ang = np.concatenate([row[:, None] * inv, col[:, None] * inv], axis=-1)
    ang = ang.astype(np.float32)
    cos, sin = np.cos(ang), np.sin(ang)
    zero = np.zeros_like(sin)
    c = np.repeat(cos, 2, axis=-1)
    sa = np.stack([-sin, zero], axis=-1).reshape(seq, HEAD_DIM)
    sb = np.stack([zero, sin], axis=-1).reshape(seq, HEAD_DIM)
    return c, sa, sb


def _inproj(x2, g1, w_in_bf, gq, gk, tables_q, tables_k, *, seq):
    t, d_model = x2.shape
    d_q = N_HEADS * HEAD_DIM
    d_kv = N_KV_HEADS * HEAD_DIM
    d_in = w_in_bf.shape[1]
    d_conv = (d_in - d_q - 2 * d_kv - 2 * d_model) // 3
    tm = TM_PROJ
    nseq = seq // tm

    def row(width):
        return pl.BlockSpec((tm, width), lambda i: (i, 0))

    table = pl.BlockSpec((tm, HEAD_DIM), lambda i: (i % nseq, 0))
    kern = functools.partial(_inproj_kernel, d_conv=d_conv, d_q=d_q, d_kv=d_kv,
                             d_model=d_model)
    out_shape = [jax.ShapeDtypeStruct((t, w), BF16)
                 for w in (d_conv, d_conv, d_q, d_kv, d_kv, d_model, d_model)]
    return pl.pallas_call(
        kern,
        grid=(t // tm,),
        in_specs=[row(d_model), _resident((1, d_model)), _resident((d_model, d_in)),
                  _resident((1, HEAD_DIM)), _resident((1, HEAD_DIM)),
                  table, table, table, table, table, table],
        out_specs=[row(d_conv), row(d_conv), row(d_q), row(d_kv), row(d_kv),
                   row(d_model), row(d_model)],
        out_shape=out_shape,
        compiler_params=_cparams(("arbitrary",)),
        name="inproj",
    )(x2, g1, w_in_bf, gq, gk, *tables_q, *tables_k)


def _attn_kernel(q_ref, k_ref, v_ref, o_ref, qs_ref, vext_ref, m_ref, acc_ref,
                 *, tq, chunks, group):
    @pl.when(pl.program_id(2) == 0)
    def _():
        vext_ref[:, :HEAD_DIM] = v_ref[...]
        vext_ref[:, HEAD_DIM:] = jnp.ones((vext_ref.shape[0], HEAD_DIM), BF16)

    for g in range(group):
        qs_ref[g * tq:(g + 1) * tq, :] = q_ref[:, g * HEAD_DIM:(g + 1) * HEAD_DIM]
    m_ref[...] = jnp.full(m_ref.shape, -jnp.inf, F32)
    acc_ref[...] = jnp.zeros(acc_ref.shape, F32)

    lo = 0
    for tk in chunks:
        keys = slice(lo, lo + tk)
        lo += tk
        s = lax.dot_general(qs_ref[...], k_ref[keys, :], (((1,), (1,)), ((), ())),
                            preferred_element_type=F32)
        m_prev = m_ref[...]
        m_new = jnp.maximum(m_prev, jnp.max(s, axis=-1, keepdims=True))
        alpha = jnp.exp2(m_prev - m_new)
        p = jnp.concatenate(
            [jnp.exp2(s[:, c * LANES:(c + 1) * LANES] - m_new) for c in range(tk // LANES)],
            axis=1).astype(BF16)
        pv = jnp.dot(p, vext_ref[keys, :], preferred_element_type=F32)
        acc_ref[...] = jnp.concatenate([alpha, alpha], axis=1) * acc_ref[...] + pv
        m_ref[...] = m_new

    out = acc_ref[:, :HEAD_DIM] / acc_ref[:, HEAD_DIM:]
    for g in range(group):
        o_ref[:, g * HEAD_DIM:(g + 1) * HEAD_DIM] = out[g * tq:(g + 1) * tq].astype(BF16)


def _attention(q, k, v, *, batch, seq):
    t = q.shape[0]
    group = N_HEADS // N_KV_HEADS
    tq, tk = TQ_ATTN, TK_ATTN
    nq = seq // tq
    gw = group * HEAD_DIM
    chunks = (tk // 2,) + (tk,) * (seq // tk - 1) + (tk // 2,)
    kern = functools.partial(_attn_kernel, tq=tq, chunks=chunks, group=group)
    return pl.pallas_call(
        kern,
        grid=(batch, N_KV_HEADS, nq),
        in_specs=[pl.BlockSpec((tq, gw), lambda b, h, i: (b * nq + i, h)),
                  pl.BlockSpec((seq, HEAD_DIM), lambda b, h, i: (b, h)),
                  pl.BlockSpec((seq, HEAD_DIM), lambda b, h, i: (b, h))],
        out_specs=pl.BlockSpec((tq, gw), lambda b, h, i: (b * nq + i, h)),
        out_shape=jax.ShapeDtypeStruct((t, N_HEADS * HEAD_DIM), BF16),
        scratch_shapes=[pltpu.VMEM((group * tq, HEAD_DIM), BF16),
                        pltpu.VMEM((seq, 2 * HEAD_DIM), BF16),
                        pltpu.VMEM((group * tq, LANES), F32),
                        pltpu.VMEM((group * tq, 2 * HEAD_DIM), F32)],
        compiler_params=_cparams(("arbitrary", "arbitrary", "arbitrary")),
        name="attention",
    )(q, k, v)


def _route(logits):
    rows = logits.shape[0]
    lane = lax.broadcasted_iota(jnp.int32, (rows, LANES), 1).astype(F32)
    neg = -jnp.inf
    big = float(2 * LANES)
    is_group = lane < N_GROUPS
    gl = jnp.where(is_group, logits, neg)
    gmax = jnp.max(gl, axis=-1, keepdims=True)
    gidx = jnp.min(jnp.where(gl == gmax, lane, big), axis=-1, keepdims=True)
    gsum = jnp.sum(jnp.where(is_group, jnp.exp(logits - gmax), 0.0), axis=-1,
                   keepdims=True)
    pg = 1.0 / gsum
    lane_group = jnp.floor(lane * (1.0 / EXPERTS_PER_GROUP)) - 1.0
    mine = (lane_group == gidx) & (lane >= N_GROUPS) & (lane < N_GROUPS + N_EXPERTS)
    sel = jnp.where(mine, logits, neg)
    v1 = jnp.max(sel, axis=-1, keepdims=True)
    i1 = jnp.min(jnp.where(sel == v1, lane, big), axis=-1, keepdims=True)
    sel2 = jnp.where(lane == i1, neg, sel)
    v2 = jnp.max(sel2, axis=-1, keepdims=True)
    i2 = jnp.min(jnp.where(sel2 == v2, lane, big), axis=-1, keepdims=True)
    t2 = jnp.exp(v2 - v1)
    den = 1.0 + t2
    wgt1 = pg * (1.0 / den)
    wgt2 = pg * (t2 / den)
    e1 = i1 - N_GROUPS
    e2 = i2 - N_GROUPS
    meta = jnp.where(lane == 0, e1,
                     jnp.where(lane == 1, e2,
                               jnp.where(lane == 2, wgt1,
                                         jnp.where(lane == 3, wgt2, 0.0))))
    picked = jnp.where((lane == e1) | (lane == e2), 1.0, 0.0)
    return meta, jnp.sum(picked, axis=0, keepdims=True)


def _post_kernel(x_ref, cb_ref, z_ref, zprev_ref, znext_ref, o_ref, sgc_ref, sga_ref,
                 cw_ref, wc_ref, wa_ref, wo_ref, g2_ref, wr_ref,
                 x1_ref, u2_ref, meta_ref, cnt_ref, *, tm, nseq):
    i = pl.program_id(0)
    at_start = (i % nseq) == 0
    at_end = (i % nseq) == nseq - 1
    rowid = lax.broadcasted_iota(jnp.int32, (tm, 1), 0)

    @pl.when(i == 0)
    def _():
        cnt_ref[...] = jnp.zeros(cnt_ref.shape, F32)

    y_attn = jnp.dot(o_ref[...], wa_ref[...], preferred_element_type=F32)

    y_conv = None
    for c in range(z_ref.shape[1] // MXU_DIM):
        sl = slice(c * MXU_DIM, (c + 1) * MXU_DIM)
        z = z_ref[:, sl].astype(F32)
        prev_row = zprev_ref[BF16_SUBLANES - 1:BF16_SUBLANES, sl].astype(F32)
        next_row = znext_ref[0:1, sl].astype(F32)
        prev_row = jnp.where(at_start, 0.0, prev_row)
        next_row = jnp.where(at_end, 0.0, next_row)
        zp = jnp.where(rowid == 0, prev_row, pltpu.roll(z, 1, axis=0))
        zn = jnp.where(rowid == tm - 1, next_row, pltpu.roll(z, tm - 1, axis=0))
        conv = cw_ref[0:1, sl] * zp + cw_ref[1:2, sl] * z + cw_ref[2:3, sl] * zn
        cbz = (cb_ref[:, sl].astype(F32) * conv).astype(BF16)
        part = jnp.dot(cbz, wc_ref[sl, :], preferred_element_type=F32)
        y_conv = part if y_conv is None else y_conv + part
    merged = (sgc_ref[...].astype(F32) * y_conv
              + sga_ref[...].astype(F32) * y_attn).astype(BF16)
    x1 = x_ref[...] + jnp.dot(merged, wo_ref[...], preferred_element_type=F32)
    x1_ref[...] = x1

    ms = jnp.mean(x1 * x1, axis=-1, keepdims=True)
    u2 = x1 * lax.rsqrt(ms + EPS) * g2_ref[...]
    _store_row_tiles(u2_ref, u2)

    u2_hi = u2.astype(BF16)
    u2_lo = (u2 - u2_hi.astype(F32)).astype(BF16)
    hi_part = jnp.dot(u2_hi, wr_ref[...], preferred_element_type=F32)
    lo_part = jnp.dot(u2_lo, wr_ref[:, :LANES], preferred_element_type=F32)
    logits = hi_part[:, :LANES] + (hi_part[:, LANES:] + lo_part)
    meta, picked = _route(logits)
    meta_ref[...] = meta
    cnt_ref[...] += picked


def _post(x2, cb, z, o, sgc, sga, conv_w, wc, wa, wo, g2, wr, *, seq):
    t, d_model = x2.shape
    tm = TM_PROJ
    nseq = seq // tm
    hb = tm // BF16_SUBLANES
    nhalo = t // BF16_SUBLANES
    d_conv = cb.shape[1]
    d_q = o.shape[1]

    def row(width):
        return pl.BlockSpec((tm, width), lambda i: (i, 0))

    kern = functools.partial(_post_kernel, tm=tm, nseq=nseq)
    return pl.pallas_call(
        kern,
        grid=(t // tm,),
        in_specs=[row(d_model), row(d_conv), row(d_conv),
                  pl.BlockSpec((BF16_SUBLANES, d_conv),
                               lambda i: (jnp.maximum(i * hb - 1, 0), 0)),
                  pl.BlockSpec((BF16_SUBLANES, d_conv),
                               lambda i: (jnp.minimum((i + 1) * hb, nhalo - 1), 0)),
                  row(d_q), row(d_model), row(d_model),
                  _resident(conv_w.shape), _resident(wc.shape), _resident(wa.shape),
                  _resident(wo.shape), _resident(g2.shape), _resident(wr.shape)],
        out_specs=[row(d_model),
                   pl.BlockSpec(_row_tiles((tm, d_model)), lambda i: (i, 0)),
                   row(LANES),
                   pl.BlockSpec((1, LANES), lambda i: (0, 0))],
        out_shape=[jax.ShapeDtypeStruct((t, d_model), F32),
                   jax.ShapeDtypeStruct(_row_tiles((t, d_model)), F32),
                   jax.ShapeDtypeStruct((t, LANES), F32),
                   jax.ShapeDtypeStruct((1, LANES), F32)],
        compiler_params=_cparams(("arbitrary",)),
        name="post",
    )(x2, cb, z, z, z, o, sgc, sga, conv_w, wc, wa, wo, g2, wr)


def _rank_kernel(meta_ref, cnt_ref, l0_ref, l1_ref, runs_ref,
                 carry_ref, pstart_ref, *, tb):
    i = pl.program_id(0)
    lane = lax.broadcasted_iota(jnp.int32, (tb, LANES), 1).astype(F32)
    meta = meta_ref[...]
    oh1 = jnp.where(lane == meta[:, 0:1], 1.0, 0.0)
    oh2 = jnp.where(lane == meta[:, 1:2], 1.0, 0.0)
    c = oh1 + oh2

    @pl.when(i == 0)
    def _():
        cnt = cnt_ref[...]
        high = jnp.floor(cnt * (1.0 / DIGIT))
        pstart_ref[...] = _lane_prefix(high) * DIGIT + _lane_prefix(cnt - high * DIGIT)
        carry_ref[...] = jnp.zeros(carry_ref.shape, F32)

    r = lax.broadcasted_iota(jnp.int32, (tb, tb), 0)
    col = lax.broadcasted_iota(jnp.int32, (tb, tb), 1)
    lower = jnp.where(col < r, 1.0, 0.0).astype(BF16)
    prefix = jnp.dot(lower, c.astype(BF16), preferred_element_type=F32)
    run_start = carry_ref[...] + pstart_ref[...]
    run_len = jnp.sum(c, axis=0, keepdims=True)

    nchunk = jnp.floor((run_len + (RUN_CHUNK - 1)) * (1.0 / RUN_CHUNK))
    local = prefix + _lane_prefix(nchunk) * RUN_CHUNK

    digits = jnp.zeros((tb, LANES), F32)
    for j, onehot in enumerate((oh1, oh2)):
        value = jnp.sum(onehot * local, axis=-1, keepdims=True)
        high = jnp.floor(value * (1.0 / DIGIT))
        digits = jnp.where(lane == 2 * j, high,
                           jnp.where(lane == 2 * j + 1, value - high * DIGIT, digits))
    eye = jnp.where(col == r, 1.0, 0.0).astype(BF16)
    rows = lax.dot_general(digits.astype(BF16), eye, (((0,), (0,)), ((), ())),
                           preferred_element_type=F32)
    for j, out_ref in enumerate((l0_ref, l1_ref)):
        out_ref[...] = (rows[2 * j:2 * j + 1, :] * DIGIT
                        + rows[2 * j + 1:2 * j + 2, :]).astype(jnp.int32)
    row = lax.broadcasted_iota(jnp.int32, (8, LANES), 0)
    runs = jnp.where(row == 0, run_start, jnp.where(row == 1, run_len, 0.0))
    runs_ref[...] = runs.astype(jnp.int32)
    carry_ref[...] += run_len


def _rank(meta, cnt):
    t = meta.shape[0]
    tb = TB_RANK
    kern = functools.partial(_rank_kernel, tb=tb)
    per_token = pl.BlockSpec((1, tb), lambda i: (0, i))
    return pl.pallas_call(
        kern,
        grid=(t // tb,),
        in_specs=[pl.BlockSpec((tb, LANES), lambda i: (i, 0)),
                  pl.BlockSpec((1, LANES), lambda i: (0, 0))],
        out_specs=[per_token] * 2 + [pl.BlockSpec((8, LANES), lambda i: (i, 0))],
        out_shape=[jax.ShapeDtypeStruct((1, t), jnp.int32)] * 2
                  + [jax.ShapeDtypeStruct((t // tb * 8, LANES), jnp.int32)],
        scratch_shapes=[pltpu.VMEM((1, LANES), F32), pltpu.VMEM((1, LANES), F32)],
        compiler_params=_cparams(("arbitrary",)),
        name="rank",
    )(meta, cnt)


def _dispatch_kernel(local0_sm, local1_sm, full_src_sm, full_dst_sm, nfull_sm,
                     rem_src_sm, rem_dst_sm, rem_len_sm,
                     u2_ref, xs_ref, buf_ref, zero_ref, sem, zsem,
                     *, tb, blk, data_rows, max_full):
    step = pl.program_id(0)
    nsteps = pl.num_programs(0)
    base = step * tb
    slot = step % 2
    tile_buf = buf_ref.at[slot]

    def wait_tile(s):
        pltpu.make_async_copy(_rows(buf_ref.at[s], 0, tb * TOP_K),
                              _rows(xs_ref, 0, tb * TOP_K), sem.at[s]).wait()

    spare_fill = pltpu.make_async_copy(zero_ref, _rows(xs_ref, data_rows, blk), zsem)

    @pl.when(step == 0)
    def _():
        zero_ref[...] = jnp.zeros(zero_ref.shape, F32)
        spare_fill.start()

    @pl.when(step >= 2)
    def _():
        wait_tile(slot)

    def move(r, carry):
        row = _rows(u2_ref, r)[...]
        _rows(tile_buf, local0_sm[base + r])[...] = row
        _rows(tile_buf, local1_sm[base + r])[...] = row
        return carry

    lax.fori_loop(0, tb, move, 0, unroll=8)

    def full_chunk(c, carry):
        pltpu.make_async_copy(
            _rows(tile_buf, full_src_sm[step * max_full + c], RUN_CHUNK),
            _rows(xs_ref, full_dst_sm[step * max_full + c], RUN_CHUNK),
            sem.at[slot]).start()
        return carry

    lax.fori_loop(0, nfull_sm[step], full_chunk, 0)

    def remainder(e, carry):
        idx = step * N_EXPERTS + e
        src, dst, length = rem_src_sm[idx], rem_dst_sm[idx], rem_len_sm[idx]
        for bit in range(RUN_CHUNK.bit_length() - 1):
            size = 1 << bit
            higher = lax.shift_left(lax.shift_right_logical(length, bit + 1), bit + 1)

            @pl.when((lax.shift_right_logical(length, bit) & 1) == 1)
            def _():
                pltpu.make_async_copy(_rows(tile_buf, src + higher, size),
                                      _rows(xs_ref, dst + higher, size),
                                      sem.at[slot]).start()
        return carry

    lax.fori_loop(0, N_EXPERTS, remainder, 0)

    @pl.when(step == nsteps - 1)
    def _():
        @pl.when(step >= 1)
        def _():
            wait_tile(1 - slot)

        wait_tile(slot)
        spare_fill.wait()


def _tile_tables(run_start, run_len, *, tb):
    shift = RUN_CHUNK.bit_length() - 1
    nchunk = (run_len + RUN_CHUNK - 1) >> shift
    cum = jnp.cumsum(nchunk, axis=1)
    first = cum - nchunk
    local_start = first * RUN_CHUNK

    def flat_list(count, max_count, value_at):
        ccum = jnp.cumsum(count, axis=1)
        cfirst = (ccum - count)[:, None, :]
        c_idx = jnp.arange(max_count, dtype=jnp.int32)[None, :, None]
        owns = (cfirst <= c_idx) & (c_idx < ccum[:, None, :])
        return [jnp.sum(jnp.where(owns, v[:, None, :] + (c_idx - cfirst) * RUN_CHUNK, 0),
                        axis=2).astype(jnp.int32) for v in value_at], ccum[:, -1]

    max_chunks = tb * TOP_K // RUN_CHUNK + N_EXPERTS
    (chunk_src,), tile_chunks = flat_list(nchunk, max_chunks, [run_start])
    max_full = tb * TOP_K // RUN_CHUNK
    nfull = run_len >> shift
    (full_src, full_dst), tile_full = flat_list(nfull, max_full, [local_start, run_start])
    whole = nfull * RUN_CHUNK
    return dict(
        max_chunks=max_chunks, chunk_src=chunk_src.reshape(-1),
        tile_chunks=tile_chunks.astype(jnp.int32),
        max_full=max_full, full_src=full_src.reshape(-1), full_dst=full_dst.reshape(-1),
        tile_full=tile_full.astype(jnp.int32),
        rem_src=(local_start + whole).reshape(-1).astype(jnp.int32),
        rem_dst=(run_start + whole).reshape(-1).astype(jnp.int32),
        rem_len=(run_len - whole).reshape(-1).astype(jnp.int32))


def _dispatch(local0, local1, tables, u2, *, data_rows, blk):
    t = u2.shape[0] // ROW_TILE
    tb = TB_RANK
    d_model = ROW_TILE * LANES
    kern = functools.partial(_dispatch_kernel, tb=tb, blk=blk, data_rows=data_rows,
                             max_full=tables["max_full"])
    buf_rows = tables["max_chunks"] * RUN_CHUNK
    return pl.pallas_call(
        kern,
        grid_spec=pltpu.PrefetchScalarGridSpec(
            num_scalar_prefetch=8,
            grid=(t // tb,),
            in_specs=[pl.BlockSpec((tb * ROW_TILE, LANES), lambda i, *_: (i, 0))],
            out_specs=pl.BlockSpec(memory_space=pl.ANY),
            scratch_shapes=[pltpu.VMEM((2,) + _row_tiles((buf_rows, d_model)), F32),
                            pltpu.VMEM((blk * ROW_TILE, LANES), F32),
                            pltpu.SemaphoreType.DMA((2,)), pltpu.SemaphoreType.DMA]),
        out_shape=jax.ShapeDtypeStruct(((data_rows + blk) * ROW_TILE, LANES), F32),
        compiler_params=_cparams(("arbitrary",)),
        name="dispatch",
    )(local0, local1, tables["full_src"], tables["full_dst"], tables["tile_full"],
      tables["rem_src"], tables["rem_dst"], tables["rem_len"], u2)


VISIT_DEAD, VISIT_FIRST, VISIT_MERGE = 0, 1, 2


def _experts_kernel(block_sm, mode_sm, lo_sm, hi_sm, expert_sm, first_sm, ahead_sm, slot_sm,
                    head_sm, xs_ref, w1_hbm, w3_hbm, w2_hbm, ys_ref,
                    w1f_ref, w3f_ref, w2f_ref, w13b_ref, w2b_ref, wsem, *, d_expert):
    del block_sm
    v = pl.program_id(0)
    mode = mode_sm[v]

    def weight_copies(expert, slot):
        return [pltpu.make_async_copy(src.at[expert], dst.at[slot], wsem.at[slot])
                for src, dst in ((w1_hbm, w1f_ref), (w3_hbm, w3f_ref), (w2_hbm, w2f_ref))]

    @pl.when(v == 0)
    def _():
        for j in range(WEIGHT_RING - 1):
            @pl.when(head_sm[j] >= 0)
            def _():
                for c in weight_copies(head_sm[j], j):
                    c.start(priority=WEIGHT_DMA_PRIORITY)

    @pl.when(first_sm[v] == 1)
    def _():
        slot = slot_sm[v]
        for c in weight_copies(expert_sm[v], slot):
            c.wait()

        @pl.when(ahead_sm[v] >= 0)
        def _():
            ahead_slot = lax.rem(slot + (WEIGHT_RING - 1), WEIGHT_RING)
            for c in weight_copies(ahead_sm[v], ahead_slot):
                c.start(priority=WEIGHT_DMA_PRIORITY)

        w13b_ref[:, :d_expert] = w1f_ref[slot].astype(BF16)
        w13b_ref[:, d_expert:] = w3f_ref[slot].astype(BF16)
        w2b_ref[...] = w2f_ref[slot].astype(BF16)

    @pl.when(mode != VISIT_DEAD)
    def _():
        xb = _load_row_tiles(xs_ref).astype(BF16)
        h = jnp.dot(xb, w13b_ref[...], preferred_element_type=F32)
        h1 = h[:, :d_expert]
        h3 = h[:, d_expert:]
        a = (h1 * _sigmoid(h1) * h3).astype(BF16)
        y = jnp.dot(a, w2b_ref[...], preferred_element_type=F32)

        @pl.when(mode == VISIT_FIRST)
        def _():
            _store_row_tiles(ys_ref, y)

        @pl.when(mode == VISIT_MERGE)
        def _():
            rowid = lax.broadcasted_iota(jnp.int32, (y.shape[0], 1), 0)
            mine = (rowid >= lo_sm[v]) & (rowid < hi_sm[v])
            _store_row_tiles(ys_ref, jnp.where(mine, y, _load_row_tiles(ys_ref)))

    @pl.when(mode == VISIT_DEAD)
    def _():
        ys_ref[...] = jnp.zeros(ys_ref.shape, F32)


def _experts(sched, xs, w1, w3, w2, *, blk):
    d_model, d_expert = w1.shape[-2:]
    kern = functools.partial(_experts_kernel, d_expert=d_expert)
    rows_blk = pl.BlockSpec((blk * ROW_TILE, LANES), lambda v, block, *_: (block[v], 0))
    hbm = pl.BlockSpec(memory_space=pl.ANY)
    return pl.pallas_call(
        kern,
        grid_spec=pltpu.PrefetchScalarGridSpec(
            num_scalar_prefetch=9,
            grid=(sched["block"].shape[0],),
            in_specs=[rows_blk, hbm, hbm, hbm],
            out_specs=rows_blk,
            scratch_shapes=[pltpu.VMEM((WEIGHT_RING, d_model, d_expert), F32),
                            pltpu.VMEM((WEIGHT_RING, d_model, d_expert), F32),
                            pltpu.VMEM((WEIGHT_RING, d_expert, d_model), F32),
                            pltpu.VMEM((d_model, 2 * d_expert), BF16),
                            pltpu.VMEM((d_expert, d_model), BF16),
                            pltpu.SemaphoreType.DMA((WEIGHT_RING,))]),
        out_shape=jax.ShapeDtypeStruct(xs.shape, F32),
        compiler_params=_cparams(("arbitrary",)),
        name="experts",
    )(sched["block"], sched["mode"], sched["lo"], sched["hi"], sched["expert"],
      sched["first"], sched["ahead"], sched["slot"], sched["head"], xs, w1, w3, w2)


def _expert_schedule(counts, *, blk, data_blocks):
    seg_end = jnp.cumsum(counts)
    seg_start = seg_end - counts
    has_rows = counts > 0
    first_blk = seg_start // blk
    nvis = jnp.where(has_rows, (seg_end - 1) // blk - first_blk + 1, 0)
    vend = jnp.cumsum(nvis)
    vfirst = vend - nvis
    nvisits = vend[-1]
    total = data_blocks + N_EXPERTS
    v_idx = jnp.arange(total, dtype=jnp.int32)
    vi = jnp.minimum(v_idx, nvisits - 1)
    expert = jnp.minimum(jnp.sum(vend[None, :] <= vi[:, None], axis=1),
                         N_EXPERTS - 1).astype(jnp.int32)
    live = v_idx < nvisits
    block = jnp.where(live, first_blk[expert] + (vi - vfirst[expert]), data_blocks)
    lo = jnp.clip(seg_start[expert] - block * blk, 0, blk)
    hi = jnp.clip(seg_end[expert] - block * blk, 0, blk)
    prev_block = jnp.concatenate([jnp.full((1,), -1, block.dtype), block[:-1]])
    mode = jnp.where(live, jnp.where(block != prev_block, VISIT_FIRST, VISIT_MERGE),
                     VISIT_DEAD)
    first = live & (vi == vfirst[expert])
    used_rank = jnp.cumsum(has_rows) - 1
    j_idx = jnp.arange(N_EXPERTS + WEIGHT_RING, dtype=jnp.int32)
    e_idx = jnp.arange(N_EXPERTS, dtype=jnp.int32)
    hit = has_rows[None, :] & (used_rank[None, :] == j_idx[:, None])
    used_list = jnp.sum(jnp.where(hit, e_idx[None, :] + 1, 0), axis=1) - 1
    rank_v = used_rank[expert]
    ahead = jnp.where(live, used_list[rank_v + (WEIGHT_RING - 1)], -1)
    as_i32 = lambda a: a.astype(jnp.int32)
    return dict(block=as_i32(block), mode=as_i32(mode), lo=as_i32(lo), hi=as_i32(hi),
                expert=expert, first=as_i32(first), ahead=as_i32(ahead),
                slot=as_i32(rank_v % WEIGHT_RING), head=as_i32(used_list[:WEIGHT_RING - 1]))


def _combine_kernel(local0_sm, local1_sm, chunk_src_sm, nchunk_sm,
                    x1_ref, meta_ref, ys_ref, out_ref,
                    buf_ref, g0_ref, g1_ref, sem, *, tb, max_chunks):
    step = pl.program_id(0)
    nsteps = pl.num_programs(0)

    def start_chunks(tile):
        slot = tile % 2

        def per_chunk(c, carry):
            pltpu.make_async_copy(
                _rows(ys_ref, chunk_src_sm[tile * max_chunks + c], RUN_CHUNK),
                _rows(buf_ref.at[slot], c * RUN_CHUNK, RUN_CHUNK),
                sem.at[slot]).start()
            return carry

        lax.fori_loop(0, nchunk_sm[tile], per_chunk, 0)

    def wait_chunks(tile):
        slot = tile % 2
        n = nchunk_sm[tile]
        for bit in range(max_chunks.bit_length()):
            @pl.when((lax.shift_right_logical(n, bit) & 1) == 1)
            def _():
                rows = RUN_CHUNK << bit
                pltpu.make_async_copy(_rows(ys_ref, 0, rows),
                                      _rows(buf_ref.at[slot], 0, rows),
                                      sem.at[slot]).wait()

    @pl.when(step == 0)
    def _():
        start_chunks(step)

    @pl.when(step + 1 < nsteps)
    def _():
        start_chunks(step + 1)

    wait_chunks(step)

    base = step * tb
    tile_buf = buf_ref.at[step % 2]

    def move(r, carry):
        for g_ref, local_sm in ((g0_ref, local0_sm), (g1_ref, local1_sm)):
            _rows(g_ref, r)[...] = _rows(tile_buf, local_sm[base + r])[...]
        return carry

    lax.fori_loop(0, tb, move, 0, unroll=8)
    meta = meta_ref[...]
    moe = (_load_row_tiles(g0_ref) * meta[:, 2:3]
           + _load_row_tiles(g1_ref) * meta[:, 3:4])
    out_ref[...] = x1_ref[...] + moe


def _combine(local0, local1, tables, x1, meta, ys):
    t, d_model = x1.shape
    tb = TB_RANK
    max_chunks = tables["max_chunks"]
    kern = functools.partial(_combine_kernel, tb=tb, max_chunks=max_chunks)
    buf_rows = max_chunks * RUN_CHUNK
    return pl.pallas_call(
        kern,
        grid_spec=pltpu.PrefetchScalarGridSpec(
            num_scalar_prefetch=4,
            grid=(t // tb,),
            in_specs=[pl.BlockSpec((tb, d_model), lambda i, *_: (i, 0)),
                      pl.BlockSpec((tb, LANES), lambda i, *_: (i, 0)),
                      pl.BlockSpec(memory_space=pl.ANY)],
            out_specs=pl.BlockSpec((tb, d_model), lambda i, *_: (i, 0)),
            scratch_shapes=[pltpu.VMEM((2,) + _row_tiles((buf_rows, d_model)), F32),
                            pltpu.VMEM(_row_tiles((tb, d_model)), F32),
                            pltpu.VMEM(_row_tiles((tb, d_model)), F32),
                            pltpu.SemaphoreType.DMA((2,))]),
        out_shape=jax.ShapeDtypeStruct((t, d_model), F32),
        compiler_params=_cparams(("arbitrary",)),
        name="combine",
    )(local0, local1, tables["chunk_src"], tables["tile_chunks"], x1, meta, ys)


def _layer(h2, *, batch, seq, norm1_g, w_in, conv_w, q_norm_g, k_norm_g,
           w_conv_out, w_attn_out, w_o, norm2_g, w_group, w_router, w1, w3, w2):
    t, d_model = h2.shape
    c, sa, sb = _rope_tables(seq)
    scale = HEAD_DIM ** -0.5 * LOG2_E
    tables_q = tuple(jnp.asarray(tab * np.float32(scale)) for tab in (c, sa, sb))
    tables_k = tuple(jnp.asarray(tab) for tab in (c, sa, sb))

    cb, z, q, k, v, sgc, sga = _inproj(
        h2, norm1_g[None, :], w_in.astype(BF16), q_norm_g[None, :], k_norm_g[None, :],
        tables_q, tables_k, seq=seq)
    o = _attention(q, k, v, batch=batch, seq=seq)

    n_route = N_GROUPS + N_EXPERTS
    wr = jnp.concatenate(
        [w_group, w_router, jnp.zeros((d_model, LANES - n_route), F32)], axis=1)
    wr_hi = wr.astype(BF16)
    wr = jnp.concatenate([wr_hi, (wr - wr_hi.astype(F32)).astype(BF16)], axis=1)
    x1, u2, meta, cnt = _post(h2, cb, z, o, sgc, sga, conv_w,
                              w_conv_out.astype(BF16), w_attn_out.astype(BF16),
                              w_o.astype(BF16), norm2_g[None, :], wr, seq=seq)

    blk = MOE_ROWS
    data_rows = t * TOP_K
    local0, local1, runs = _rank(meta, cnt)
    local0, local1 = local0.reshape(-1), local1.reshape(-1)
    runs = runs.reshape(-1, 8, LANES)
    tables = _tile_tables(runs[:, 0, :N_EXPERTS], runs[:, 1, :N_EXPERTS], tb=TB_RANK)
    sched = _expert_schedule(cnt[0, :N_EXPERTS].astype(jnp.int32), blk=blk,
                             data_blocks=data_rows // blk)

    xs = _dispatch(local0, local1, tables, u2, data_rows=data_rows, blk=blk)
    ys = _experts(sched, xs, w1, w3, w2, blk=blk)
    return _combine(local0, local1, tables, x1, meta, ys)


def kernel(x, norm1_g, w_in, conv_w, q_norm_g, k_norm_g, w_conv_out, w_attn_out, w_o,
           norm2_g, w_group, w_router, w1, w3, w2):
    batch, seq, d_model = x.shape
    h2 = x.reshape(batch * seq, d_model)
    for l in range(norm1_g.shape[0]):
        h2 = _layer(h2, batch=batch, seq=seq, norm1_g=norm1_g[l], w_in=w_in[l],
                    conv_w=conv_w[l], q_norm_g=q_norm_g[l], k_norm_g=k_norm_g[l],
                    w_conv_out=w_conv_out[l], w_attn_out=w_attn_out[l], w_o=w_o[l],
                    norm2_g=norm2_g[l], w_group=w_group[l], w_router=w_router[l],
                    w1=w1[l], w3=w3[l], w2=w2[l])
    return h2.reshape(batch, seq, d_model)
```

```python
import functools

import jax
import jax.numpy as jnp
import numpy as np
from jax import lax
from jax.experimental import pallas as pl
from jax.experimental.pallas import tpu as pltpu

F32 = jnp.float32
BF16 = jnp.bfloat16

GRID_W = 64
EPS = 1e-6
N_HEADS = 8
N_KV_HEADS = 2
HEAD_DIM = 128
ROPE_THETA = 10000.0
N_GROUPS = 8
EXPERTS_PER_GROUP = 8
N_EXPERTS = N_GROUPS * EXPERTS_PER_GROUP
TOP_K = 2
LOG2_E = 1.4426950408889634

LANES = 128
MXU_DIM = 256
BF16_SUBLANES = 16
V7X_VMEM_LIMIT_BYTES = 56000 * 1024

TM_PROJ = 512
TQ_ATTN = 512
TK_ATTN = 512
TB_RANK = 1024
RUN_CHUNK = 8
MOE_ROWS = 256
WEIGHT_RING = 3
WEIGHT_DMA_PRIORITY = 1
DIGIT = 256.0


def _cparams(sem):
    return pltpu.CompilerParams(dimension_semantics=sem,
                                vmem_limit_bytes=V7X_VMEM_LIMIT_BYTES)


ROW_TILE = 8


def _row_tiles(shape2d):
    rows, width = shape2d
    assert width == ROW_TILE * LANES
    return (rows * ROW_TILE, LANES)


def _rows(ref, r, n=1):
    return ref.at[pl.ds(pl.multiple_of(r * ROW_TILE, ROW_TILE), n * ROW_TILE)]


def _load_row_tiles(ref):
    rows = ref.shape[0] // ROW_TILE
    return jnp.concatenate(
        [ref[pl.ds(s, rows, stride=ROW_TILE), :] for s in range(ROW_TILE)], axis=1)


def _store_row_tiles(ref, value):
    rows = value.shape[0]
    for s in range(ROW_TILE):
        ref[pl.ds(s, rows, stride=ROW_TILE), :] = value[:, s * LANES:(s + 1) * LANES]


def _resident(shape):
    nd = len(shape)
    return pl.BlockSpec(shape, lambda *_: (0,) * nd, pipeline_mode=pl.Buffered(1))


def _lane_prefix(row):
    r = lax.broadcasted_iota(jnp.int32, (LANES, LANES), 0)
    col = lax.broadcasted_iota(jnp.int32, (LANES, LANES), 1)
    upper = jnp.where(r < col, 1.0, 0.0).astype(BF16)
    return jnp.dot(jnp.broadcast_to(row, (8, LANES)).astype(BF16), upper,
                   preferred_element_type=F32)[0:1, :]


def _head_norm_rope(xh, g, c, sa, sb):
    ms = jnp.mean(xh * xh, axis=-1, keepdims=True)
    y = xh * lax.rsqrt(ms + EPS) * g
    y_next = pltpu.roll(y, HEAD_DIM - 1, axis=1)
    y_prev = pltpu.roll(y, 1, axis=1)
    return y * c + y_next * sa + y_prev * sb


def _sigmoid(x):
    return 0.5 * jnp.tanh(0.5 * x) + 0.5


def _inproj_kernel(x_ref, g1_ref, w_ref, gq_ref, gk_ref,
                   cq_ref, saq_ref, sbq_ref, ck_ref, sak_ref, sbk_ref,
                   cb_ref, z_ref, q_ref, k_ref, v_ref, sgc_ref, sga_ref,
                   *, d_conv, d_q, d_kv, d_model):
    x = x_ref[...]
    ms = jnp.mean(x * x, axis=-1, keepdims=True)
    u = (x * lax.rsqrt(ms + EPS) * g1_ref[...]).astype(BF16)

    def proj(lo, width):
        return jnp.dot(u, w_ref[:, lo:lo + width], preferred_element_type=F32)

    o_cb, o_cc, o_cx = 0, d_conv, 2 * d_conv
    o_q = 3 * d_conv
    o_k = o_q + d_q
    o_v = o_k + d_kv
    o_gc = o_v + d_kv
    o_ga = o_gc + d_model

    sgc_ref[...] = _sigmoid(proj(o_gc, d_model)).astype(BF16)
    sga_ref[...] = _sigmoid(proj(o_ga, d_model)).astype(BF16)

    q = proj(o_q, d_q)
    gq = gq_ref[...]
    cq, saq, sbq = cq_ref[...], saq_ref[...], sbq_ref[...]
    for h in range(d_q // HEAD_DIM):
        sl = slice(h * HEAD_DIM, (h + 1) * HEAD_DIM)
        q_ref[:, sl] = _head_norm_rope(q[:, sl], gq, cq, saq, sbq).astype(BF16)

    k = proj(o_k, d_kv)
    gk = gk_ref[...]
    ck, sak, sbk = ck_ref[...], sak_ref[...], sbk_ref[...]
    for h in range(d_kv // HEAD_DIM):
        sl = slice(h * HEAD_DIM, (h + 1) * HEAD_DIM)
        k_ref[:, sl] = _head_norm_rope(k[:, sl], gk, ck, sak, sbk).astype(BF16)

    z_ref[...] = (proj(o_cc, d_conv) * proj(o_cx, d_conv)).astype(BF16)
    v_ref[...] = proj(o_v, d_kv).astype(BF16)
    cb_ref[...] = proj(o_cb, d_conv).astype(BF16)


def _rope_tables(seq):
    rows = seq // GRID_W
    axis_dim = HEAD_DIM // 2
    row = np.repeat(np.arange(rows, dtype=np.float32), GRID_W)
    col = np.tile(np.arange(GRID_W, dtype=np.float32), rows)
    inv = (np.float32(ROPE_THETA)
           ** (-np.arange(0, axis_dim, 2, dtype=np.float32) / np.float32(axis_dim)))
    ang = np.concatenate([row[:, None] * inv, col[:, None] * inv], axis=-1)
    ang = ang.astype(np.float32)
    cos, sin = np.cos(ang), np.sin(ang)
    zero = np.zeros_like(sin)
    c = np.repeat(cos, 2, axis=-1)
    sa = np.stack([-sin, zero], axis=-1).reshape(seq, HEAD_DIM)
    sb = np.stack([zero, sin], axis=-1).reshape(seq, HEAD_DIM)
    return c, sa, sb


def _inproj(x2, g1, w_in_bf, gq, gk, tables_q, tables_k, *, seq):
    t, d_model = x2.shape
    d_q = N_HEADS * HEAD_DIM
    d_kv = N_KV_HEADS * HEAD_DIM
    d_in = w_in_bf.shape[1]
    d_conv = (d_in - d_q - 2 * d_kv - 2 * d_model) // 3
    tm = TM_PROJ
    nseq = seq // tm

    def row(width):
        return pl.BlockSpec((tm, width), lambda i: (i, 0))

    table = pl.BlockSpec((tm, HEAD_DIM), lambda i: (i % nseq, 0))
    kern = functools.partial(_inproj_kernel, d_conv=d_conv, d_q=d_q, d_kv=d_kv,
                             d_model=d_model)
    out_shape = [jax.ShapeDtypeStruct((t, w), BF16)
                 for w in (d_conv, d_conv, d_q, d_kv, d_kv, d_model, d_model)]
    return pl.pallas_call(
        kern,
        grid=(t // tm,),
        in_specs=[row(d_model), _resident((1, d_model)), _resident((d_model, d_in)),
                  _resident((1, HEAD_DIM)), _resident((1, HEAD_DIM)),
                  table, table, table, table, table, table],
        out_specs=[row(d_conv), row(d_conv), row(d_q), row(d_kv), row(d_kv),
                   row(d_model), row(d_model)],
        out_shape=out_shape,
        compiler_params=_cparams(("arbitrary",)),
        name="inproj",
    )(x2, g1, w_in_bf, gq, gk, *tables_q, *tables_k)


def _attn_kernel(q_ref, k_ref, v_ref, o_ref, qs_ref, vext_ref, m_ref, acc_ref,
                 *, tq, chunks, group):
    @pl.when(pl.program_id(2) == 0)
    def _():
        vext_ref[:, :HEAD_DIM] = v_ref[...]
        vext_ref[:, HEAD_DIM:] = jnp.ones((vext_ref.shape[0], HEAD_DIM), BF16)

    for g in range(group):
        qs_ref[g * tq:(g + 1) * tq, :] = q_ref[:, g * HEAD_DIM:(g + 1) * HEAD_DIM]
    m_ref[...] = jnp.full(m_ref.shape, -jnp.inf, F32)
    acc_ref[...] = jnp.zeros(acc_ref.shape, F32)

    lo = 0
    for tk in chunks:
        keys = slice(lo, lo + tk)
        lo += tk
        s = lax.dot_general(qs_ref[...], k_ref[keys, :], (((1,), (1,)), ((), ())),
                            preferred_element_type=F32)
        m_prev = m_ref[...]
        m_new = jnp.maximum(m_prev, jnp.max(s, axis=-1, keepdims=True))
        alpha = jnp.exp2(m_prev - m_new)
        p = jnp.concatenate(
            [jnp.exp2(s[:, c * LANES:(c + 1) * LANES] - m_new) for c in range(tk // LANES)],
            axis=1).astype(BF16)
        pv = jnp.dot(p, vext_ref[keys, :], preferred_element_type=F32)
        acc_ref[...] = jnp.concatenate([alpha, alpha], axis=1) * acc_ref[...] + pv
        m_ref[...] = m_new

    out = acc_ref[:, :HEAD_DIM] / acc_ref[:, HEAD_DIM:]
    for g in range(group):
        o_ref[:, g * HEAD_DIM:(g + 1) * HEAD_DIM] = out[g * tq:(g + 1) * tq].astype(BF16)


def _attention(q, k, v, *, batch, seq):
    t = q.shape[0]
    group = N_HEADS // N_KV_HEADS
    tq, tk = TQ_ATTN, TK_ATTN
    nq = seq // tq
    gw = group * HEAD_DIM
    chunks = (tk // 2,) + (tk,) * (seq // tk - 1) + (tk // 2,)
    kern = functools.partial(_attn_kernel, tq=tq, chunks=chunks, group=group)
    return pl.pallas_call(
        kern,
        grid=(batch, N_KV_HEADS, nq),
        in_specs=[pl.BlockSpec((tq, gw), lambda b, h, i: (b * nq + i, h)),
                  pl.BlockSpec((seq, HEAD_DIM), lambda b, h, i: (b, h)),
                  pl.BlockSpec((seq, HEAD_DIM), lambda b, h, i: (b, h))],
        out_specs=pl.BlockSpec((tq, gw), lambda b, h, i: (b * nq + i, h)),
        out_shape=jax.ShapeDtypeStruct((t, N_HEADS * HEAD_DIM), BF16),
        scratch_shapes=[pltpu.VMEM((group * tq, HEAD_DIM), BF16),
                        pltpu.VMEM((seq, 2 * HEAD_DIM), BF16),
                        pltpu.VMEM((group * tq, LANES), F32),
                        pltpu.VMEM((group * tq, 2 * HEAD_DIM), F32)],
        compiler_params=_cparams(("arbitrary", "arbitrary", "arbitrary")),
        name="attention",
    )(q, k, v)


def _route(logits):
    rows = logits.shape[0]
    lane = lax.broadcasted_iota(jnp.int32, (rows, LANES), 1).astype(F32)
    neg = -jnp.inf
    big = float(2 * LANES)
    is_group = lane < N_GROUPS
    gl = jnp.where(is_group, logits, neg)
    gmax = jnp.max(gl, axis=-1, keepdims=True)
    gidx = jnp.min(jnp.where(gl == gmax, lane, big), axis=-1, keepdims=True)
    gsum = jnp.sum(jnp.where(is_group, jnp.exp(logits - gmax), 0.0), axis=-1,
                   keepdims=True)
    pg = 1.0 / gsum
    lane_group = jnp.floor(lane * (1.0 / EXPERTS_PER_GROUP)) - 1.0
    mine = (lane_group == gidx) & (lane >= N_GROUPS) & (lane < N_GROUPS + N_EXPERTS)
    sel = jnp.where(mine, logits, neg)
    v1 = jnp.max(sel, axis=-1, keepdims=True)
    i1 = jnp.min(jnp.where(sel == v1, lane, big), axis=-1, keepdims=True)
    sel2 = jnp.where(lane == i1, neg, sel)
    v2 = jnp.max(sel2, axis=-1, keepdims=True)
    i2 = jnp.min(jnp.where(sel2 == v2, lane, big), axis=-1, keepdims=True)
    t2 = jnp.exp(v2 - v1)
    den = 1.0 + t2
    wgt1 = pg * (1.0 / den)
    wgt2 = pg * (t2 / den)
    e1 = i1 - N_GROUPS
    e2 = i2 - N_GROUPS
    meta = jnp.where(lane == 0, e1,
                     jnp.where(lane == 1, e2,
                               jnp.where(lane == 2, wgt1,
                                         jnp.where(lane == 3, wgt2, 0.0))))
    picked = jnp.where((lane == e1) | (lane == e2), 1.0, 0.0)
    return meta, jnp.sum(picked, axis=0, keepdims=True)


def _post_kernel(x_ref, cb_ref, z_ref, zprev_ref, znext_ref, o_ref, sgc_ref, sga_ref,
                 cw_ref, wc_ref, wa_ref, wo_ref, g2_ref, wr_ref,
                 x1_ref, u2_ref, meta_ref, cnt_ref, *, tm, nseq):
    i = pl.program_id(0)
    at_start = (i % nseq) == 0
    at_end = (i % nseq) == nseq - 1
    rowid = lax.broadcasted_iota(jnp.int32, (tm, 1), 0)

    @pl.when(i == 0)
    def _():
        cnt_ref[...] = jnp.zeros(cnt_ref.shape, F32)

    y_attn = jnp.dot(o_ref[...], wa_ref[...], preferred_element_type=F32)

    y_conv = None
    for c in range(z_ref.shape[1] // MXU_DIM):
        sl = slice(c * MXU_DIM, (c + 1) * MXU_DIM)
        z = z_ref[:, sl].astype(F32)
        prev_row = zprev_ref[BF16_SUBLANES - 1:BF16_SUBLANES, sl].astype(F32)
        next_row = znext_ref[0:1, sl].astype(F32)
        prev_row = jnp.where(at_start, 0.0, prev_row)
        next_row = jnp.where(at_end, 0.0, next_row)
        zp = jnp.where(rowid == 0, prev_row, pltpu.roll(z, 1, axis=0))
        zn = jnp.where(rowid == tm - 1, next_row, pltpu.roll(z, tm - 1, axis=0))
        conv = cw_ref[0:1, sl] * zp + cw_ref[1:2, sl] * z + cw_ref[2:3, sl] * zn
        cbz = (cb_ref[:, sl].astype(F32) * conv).astype(BF16)
        part = jnp.dot(cbz, wc_ref[sl, :], preferred_element_type=F32)
        y_conv = part if y_conv is None else y_conv + part
    merged = (sgc_ref[...].astype(F32) * y_conv
              + sga_ref[...].astype(F32) * y_attn).astype(BF16)
    x1 = x_ref[...] + jnp.dot(merged, wo_ref[...], preferred_element_type=F32)
    x1_ref[...] = x1

    ms = jnp.mean(x1 * x1, axis=-1, keepdims=True)
    u2 = x1 * lax.rsqrt(ms + EPS) * g2_ref[...]
    _store_row_tiles(u2_ref, u2)

    u2_hi = u2.astype(BF16)
    u2_lo = (u2 - u2_hi.astype(F32)).astype(BF16)
    hi_part = jnp.dot(u2_hi, wr_ref[...], preferred_element_type=F32)
    lo_part = jnp.dot(u2_lo, wr_ref[:, :LANES], preferred_element_type=F32)
    logits = hi_part[:, :LANES] + (hi_part[:, LANES:] + lo_part)
    meta, picked = _route(logits)
    meta_ref[...] = meta
    cnt_ref[...] += picked


def _post(x2, cb, z, o, sgc, sga, conv_w, wc, wa, wo, g2, wr, *, seq):
    t, d_model = x2.shape
    tm = TM_PROJ
    nseq = seq // tm
    hb = tm // BF16_SUBLANES
    nhalo = t // BF16_SUBLANES
    d_conv = cb.shape[1]
    d_q = o.shape[1]

    def row(width):
        return pl.BlockSpec((tm, width), lambda i: (i, 0))

    kern = functools.partial(_post_kernel, tm=tm, nseq=nseq)
    return pl.pallas_call(
        kern,
        grid=(t // tm,),
        in_specs=[row(d_model), row(d_conv), row(d_conv),
                  pl.BlockSpec((BF16_SUBLANES, d_conv),
                               lambda i: (jnp.maximum(i * hb - 1, 0), 0)),
                  pl.BlockSpec((BF16_SUBLANES, d_conv),
                               lambda i: (jnp.minimum((i + 1) * hb, nhalo - 1), 0)),
                  row(d_q), row(d_model), row(d_model),
                  _resident(conv_w.shape), _resident(wc.shape), _resident(wa.shape),
                  _resident(wo.shape), _resident(g2.shape), _resident(wr.shape)],
        out_specs=[row(d_model),
                   pl.BlockSpec(_row_tiles((tm, d_model)), lambda i: (i, 0)),
                   row(LANES),
                   pl.BlockSpec((1, LANES), lambda i: (0, 0))],
        out_shape=[jax.ShapeDtypeStruct((t, d_model), F32),
                   jax.ShapeDtypeStruct(_row_tiles((t, d_model)), F32),
                   jax.ShapeDtypeStruct((t, LANES), F32),
                   jax.ShapeDtypeStruct((1, LANES), F32)],
        compiler_params=_cparams(("arbitrary",)),
        name="post",
    )(x2, cb, z, z, z, o, sgc, sga, conv_w, wc, wa, wo, g2, wr)


def _rank_kernel(meta_ref, cnt_ref, l0_ref, l1_ref, runs_ref,
                 carry_ref, pstart_ref, *, tb, blk):
    i = pl.program_id(0)
    lane = lax.broadcasted_iota(jnp.int32, (tb, LANES), 1).astype(F32)
    meta = meta_ref[...]
    oh1 = jnp.where(lane == meta[:, 0:1], 1.0, 0.0)
    oh2 = jnp.where(lane == meta[:, 1:2], 1.0, 0.0)
    c = oh1 + oh2

    @pl.when(i == 0)
    def _():
        nblk = jnp.floor((cnt_ref[...] + (blk - 1)) * (1.0 / blk))
        pstart_ref[...] = _lane_prefix(nblk) * blk
        carry_ref[...] = jnp.zeros(carry_ref.shape, F32)

    r = lax.broadcasted_iota(jnp.int32, (tb, tb), 0)
    col = lax.broadcasted_iota(jnp.int32, (tb, tb), 1)
    lower = jnp.where(col < r, 1.0, 0.0).astype(BF16)
    prefix = jnp.dot(lower, c.astype(BF16), preferred_element_type=F32)
    run_start = carry_ref[...] + pstart_ref[...]
    run_len = jnp.sum(c, axis=0, keepdims=True)

    nchunk = jnp.floor((run_len + (RUN_CHUNK - 1)) * (1.0 / RUN_CHUNK))
    local = prefix + _lane_prefix(nchunk) * RUN_CHUNK

    digits = jnp.zeros((tb, LANES), F32)
    for j, onehot in enumerate((oh1, oh2)):
        value = jnp.sum(onehot * local, axis=-1, keepdims=True)
        high = jnp.floor(value * (1.0 / DIGIT))
        digits = jnp.where(lane == 2 * j, high,
                           jnp.where(lane == 2 * j + 1, value - high * DIGIT, digits))
    eye = jnp.where(col == r, 1.0, 0.0).astype(BF16)
    rows = lax.dot_general(digits.astype(BF16), eye, (((0,), (0,)), ((), ())),
                           preferred_element_type=F32)
    for j, out_ref in enumerate((l0_ref, l1_ref)):
        out_ref[...] = (rows[2 * j:2 * j + 1, :] * DIGIT
                        + rows[2 * j + 1:2 * j + 2, :]).astype(jnp.int32)
    row = lax.broadcasted_iota(jnp.int32, (8, LANES), 0)
    runs = jnp.where(row == 0, run_start, jnp.where(row == 1, run_len, 0.0))
    runs_ref[...] = runs.astype(jnp.int32)
    carry_ref[...] += run_len


def _rank(meta, cnt, *, blk):
    t = meta.shape[0]
    tb = TB_RANK
    kern = functools.partial(_rank_kernel, tb=tb, blk=blk)
    per_token = pl.BlockSpec((1, tb), lambda i: (0, i))
    return pl.pallas_call(
        kern,
        grid=(t // tb,),
        in_specs=[pl.BlockSpec((tb, LANES), lambda i: (i, 0)),
                  pl.BlockSpec((1, LANES), lambda i: (0, 0))],
        out_specs=[per_token] * 2 + [pl.BlockSpec((8, LANES), lambda i: (i, 0))],
        out_shape=[jax.ShapeDtypeStruct((1, t), jnp.int32)] * 2
                  + [jax.ShapeDtypeStruct((t // tb * 8, LANES), jnp.int32)],
        scratch_shapes=[pltpu.VMEM((1, LANES), F32), pltpu.VMEM((1, LANES), F32)],
        compiler_params=_cparams(("arbitrary",)),
        name="rank",
    )(meta, cnt)


def _dispatch_kernel(local0_sm, local1_sm, full_src_sm, full_dst_sm, nfull_sm,
                     rem_src_sm, rem_dst_sm, rem_len_sm, pad_start_sm, pad_len_sm, nused_sm,
                     u2_ref, xs_ref, buf_ref, zero_ref, sem, zsem,
                     *, tb, blk, nblocks, max_full):
    step = pl.program_id(0)
    nsteps = pl.num_programs(0)
    base = step * tb
    nbits = blk.bit_length() - 1
    slot = step % 2
    tile_buf = buf_ref.at[slot]

    def wait_tile(s):
        pltpu.make_async_copy(_rows(buf_ref.at[s], 0, tb * TOP_K),
                              _rows(xs_ref, 0, tb * TOP_K), sem.at[s]).wait()

    def pad_copies(visit):
        def per_expert(e, carry):
            start, length = pad_start_sm[e], pad_len_sm[e]
            for bit in range(nbits):
                size = 1 << bit
                higher = lax.shift_left(lax.shift_right_logical(length, bit + 1), bit + 1)

                @pl.when((lax.shift_right_logical(length, bit) & 1) == 1)
                def _():
                    visit(pltpu.make_async_copy(
                        zero_ref.at[pl.ds(0, size * ROW_TILE)],
                        _rows(xs_ref, start + higher, size), zsem))
            return carry

        lax.fori_loop(0, N_EXPERTS, per_expert, 0)

        def per_block(b, carry):
            visit(pltpu.make_async_copy(zero_ref, _rows(xs_ref, b * blk, blk), zsem))
            return carry

        lax.fori_loop(nused_sm[0], nblocks, per_block, 0)

    @pl.when(step == 0)
    def _():
        zero_ref[...] = jnp.zeros(zero_ref.shape, F32)
        pad_copies(lambda c: c.start())

    @pl.when(step >= 2)
    def _():
        wait_tile(slot)

    def move(r, carry):
        row = _rows(u2_ref, r)[...]
        _rows(tile_buf, local0_sm[base + r])[...] = row
        _rows(tile_buf, local1_sm[base + r])[...] = row
        return carry

    lax.fori_loop(0, tb, move, 0, unroll=8)

    def full_chunk(c, carry):
        pltpu.make_async_copy(
            _rows(tile_buf, full_src_sm[step * max_full + c], RUN_CHUNK),
            _rows(xs_ref, full_dst_sm[step * max_full + c], RUN_CHUNK),
            sem.at[slot]).start()
        return carry

    lax.fori_loop(0, nfull_sm[step], full_chunk, 0)

    def remainder(e, carry):
        idx = step * N_EXPERTS + e
        src, dst, length = rem_src_sm[idx], rem_dst_sm[idx], rem_len_sm[idx]
        for bit in range(RUN_CHUNK.bit_length() - 1):
            size = 1 << bit
            higher = lax.shift_left(lax.shift_right_logical(length, bit + 1), bit + 1)

            @pl.when((lax.shift_right_logical(length, bit) & 1) == 1)
            def _():
                pltpu.make_async_copy(_rows(tile_buf, src + higher, size),
                                      _rows(xs_ref, dst + higher, size),
                                      sem.at[slot]).start()
        return carry

    lax.fori_loop(0, N_EXPERTS, remainder, 0)

    @pl.when(step == nsteps - 1)
    def _():
        @pl.when(step >= 1)
        def _():
            wait_tile(1 - slot)

        wait_tile(slot)
        pad_copies(lambda c: c.wait())


def _tile_tables(run_start, run_len, *, tb):
    shift = RUN_CHUNK.bit_length() - 1
    nchunk = (run_len + RUN_CHUNK - 1) >> shift
    cum = jnp.cumsum(nchunk, axis=1)
    first = cum - nchunk
    local_start = first * RUN_CHUNK

    def flat_list(count, max_count, value_at):
        ccum = jnp.cumsum(count, axis=1)
        cfirst = (ccum - count)[:, None, :]
        c_idx = jnp.arange(max_count, dtype=jnp.int32)[None, :, None]
        owns = (cfirst <= c_idx) & (c_idx < ccum[:, None, :])
        return [jnp.sum(jnp.where(owns, v[:, None, :] + (c_idx - cfirst) * RUN_CHUNK, 0),
                        axis=2).astype(jnp.int32) for v in value_at], ccum[:, -1]

    max_chunks = tb * TOP_K // RUN_CHUNK + N_EXPERTS
    (chunk_src,), tile_chunks = flat_list(nchunk, max_chunks, [run_start])
    max_full = tb * TOP_K // RUN_CHUNK
    nfull = run_len >> shift
    (full_src, full_dst), tile_full = flat_list(nfull, max_full, [local_start, run_start])
    whole = nfull * RUN_CHUNK
    return dict(
        max_chunks=max_chunks, chunk_src=chunk_src.reshape(-1),
        tile_chunks=tile_chunks.astype(jnp.int32),
        max_full=max_full, full_src=full_src.reshape(-1), full_dst=full_dst.reshape(-1),
        tile_full=tile_full.astype(jnp.int32),
        rem_src=(local_start + whole).reshape(-1).astype(jnp.int32),
        rem_dst=(run_start + whole).reshape(-1).astype(jnp.int32),
        rem_len=(run_len - whole).reshape(-1).astype(jnp.int32))


def _dispatch(local0, local1, tables, pad_start, pad_len, nused, u2, *, p_rows, blk):
    t = u2.shape[0] // ROW_TILE
    tb = TB_RANK
    d_model = ROW_TILE * LANES
    kern = functools.partial(_dispatch_kernel, tb=tb, blk=blk, nblocks=p_rows // blk,
                             max_full=tables["max_full"])
    buf_rows = tables["max_chunks"] * RUN_CHUNK
    return pl.pallas_call(
        kern,
        grid_spec=pltpu.PrefetchScalarGridSpec(
            num_scalar_prefetch=11,
            grid=(t // tb,),
            in_specs=[pl.BlockSpec((tb * ROW_TILE, LANES), lambda i, *_: (i, 0))],
            out_specs=pl.BlockSpec(memory_space=pl.ANY),
            scratch_shapes=[pltpu.VMEM((2,) + _row_tiles((buf_rows, d_model)), F32),
                            pltpu.VMEM((blk * ROW_TILE, LANES), F32),
                            pltpu.SemaphoreType.DMA((2,)), pltpu.SemaphoreType.DMA]),
        out_shape=jax.ShapeDtypeStruct((p_rows * ROW_TILE, LANES), F32),
        compiler_params=_cparams(("arbitrary",)),
        name="dispatch",
    )(local0, local1, tables["full_src"], tables["full_dst"], tables["tile_full"],
      tables["rem_src"], tables["rem_dst"], tables["rem_len"],
      pad_start, pad_len, nused, u2)


def _experts_kernel(be_sm, first_sm, ahead_sm, slot_sm, head_sm, nused_sm,
                    xs_ref, w1_hbm, w3_hbm, w2_hbm, ys_ref,
                    w1f_ref, w3f_ref, w2f_ref, w13b_ref, w2b_ref, wsem, *, d_expert):
    b = pl.program_id(0)
    used = b < nused_sm[0]

    def weight_copies(expert, slot):
        return [pltpu.make_async_copy(src.at[expert], dst.at[slot], wsem.at[slot])
                for src, dst in ((w1_hbm, w1f_ref), (w3_hbm, w3f_ref), (w2_hbm, w2f_ref))]

    @pl.when(b == 0)
    def _():
        for j in range(WEIGHT_RING - 1):
            @pl.when(head_sm[j] >= 0)
            def _():
                for c in weight_copies(head_sm[j], j):
                    c.start(priority=WEIGHT_DMA_PRIORITY)

    @pl.when(first_sm[b] == 1)
    def _():
        slot = slot_sm[b]
        for c in weight_copies(be_sm[b], slot):
            c.wait()

        @pl.when(ahead_sm[b] >= 0)
        def _():
            ahead_slot = lax.rem(slot + (WEIGHT_RING - 1), WEIGHT_RING)
            for c in weight_copies(ahead_sm[b], ahead_slot):
                c.start(priority=WEIGHT_DMA_PRIORITY)

        w13b_ref[:, :d_expert] = w1f_ref[slot].astype(BF16)
        w13b_ref[:, d_expert:] = w3f_ref[slot].astype(BF16)
        w2b_ref[...] = w2f_ref[slot].astype(BF16)

    @pl.when(used)
    def _():
        xb = _load_row_tiles(xs_ref).astype(BF16)
        h = jnp.dot(xb, w13b_ref[...], preferred_element_type=F32)
        h1 = h[:, :d_expert]
        h3 = h[:, d_expert:]
        a = (h1 * _sigmoid(h1) * h3).astype(BF16)
        _store_row_tiles(ys_ref, jnp.dot(a, w2b_ref[...], preferred_element_type=F32))

    @pl.when(jnp.logical_not(used))
    def _():
        ys_ref[...] = jnp.zeros(ys_ref.shape, F32)


def _experts(sched, nused, xs, w1, w3, w2, *, blk):
    p_rows = xs.shape[0] // ROW_TILE
    d_model, d_expert = w1.shape[-2:]
    kern = functools.partial(_experts_kernel, d_expert=d_expert)
    rows_blk = (blk * ROW_TILE, LANES)

    def rows_in(b, be, first, ahead, slot, head, nu):
        return (jnp.maximum(jnp.minimum(b, nu[0] - 1), 0), 0)

    hbm = pl.BlockSpec(memory_space=pl.ANY)
    return pl.pallas_call(
        kern,
        grid_spec=pltpu.PrefetchScalarGridSpec(
            num_scalar_prefetch=6,
            grid=(p_rows // blk,),
            in_specs=[pl.BlockSpec(rows_blk, rows_in), hbm, hbm, hbm],
            out_specs=pl.BlockSpec(rows_blk, lambda b, *_: (b, 0)),
            scratch_shapes=[pltpu.VMEM((WEIGHT_RING, d_model, d_expert), F32),
                            pltpu.VMEM((WEIGHT_RING, d_model, d_expert), F32),
                            pltpu.VMEM((WEIGHT_RING, d_expert, d_model), F32),
                            pltpu.VMEM((d_model, 2 * d_expert), BF16),
                            pltpu.VMEM((d_expert, d_model), BF16),
                            pltpu.SemaphoreType.DMA((WEIGHT_RING,))]),
        out_shape=jax.ShapeDtypeStruct((p_rows * ROW_TILE, LANES), F32),
        compiler_params=_cparams(("arbitrary",)),
        name="experts",
    )(sched["expert"], sched["first"], sched["ahead"], sched["slot"], sched["head"],
      nused, xs, w1, w3, w2)


def _expert_schedule(counts, *, blk, nb):
    nblk = (counts + blk - 1) // blk
    pend = jnp.cumsum(nblk)
    nused = pend[-1]
    barange = jnp.arange(nb, dtype=jnp.int32)
    bidx = jnp.minimum(barange, nused - 1)
    expert = jnp.minimum(jnp.sum(pend[None, :] <= bidx[:, None], axis=1),
                         N_EXPERTS - 1).astype(jnp.int32)
    live = barange < nused
    first = (live & (barange == (pend - nblk)[expert])).astype(jnp.int32)
    has_rows = nblk > 0
    used_rank = jnp.cumsum(has_rows) - 1
    j_idx = jnp.arange(N_EXPERTS + WEIGHT_RING, dtype=jnp.int32)
    e_idx = jnp.arange(N_EXPERTS, dtype=jnp.int32)
    hit = has_rows[None, :] & (used_rank[None, :] == j_idx[:, None])
    used_list = jnp.sum(jnp.where(hit, e_idx[None, :] + 1, 0), axis=1) - 1
    rank_b = used_rank[expert]
    ahead = jnp.where(live, used_list[rank_b + (WEIGHT_RING - 1)], -1)
    return dict(
        expert=expert, first=first, ahead=ahead.astype(jnp.int32),
        slot=(rank_b % WEIGHT_RING).astype(jnp.int32),
        head=used_list[:WEIGHT_RING - 1].astype(jnp.int32),
        nused=nused.reshape(1).astype(jnp.int32),
        pad_start=((pend - nblk) * blk + counts).astype(jnp.int32),
        pad_len=(nblk * blk - counts).astype(jnp.int32))


def _combine_kernel(local0_sm, local1_sm, chunk_src_sm, nchunk_sm,
                    x1_ref, meta_ref, ys_ref, out_ref,
                    buf_ref, g0_ref, g1_ref, sem, *, tb, max_chunks):
    step = pl.program_id(0)
    nsteps = pl.num_programs(0)

    def start_chunks(tile):
        slot = tile % 2

        def per_chunk(c, carry):
            pltpu.make_async_copy(
                _rows(ys_ref, chunk_src_sm[tile * max_chunks + c], RUN_CHUNK),
                _rows(buf_ref.at[slot], c * RUN_CHUNK, RUN_CHUNK),
                sem.at[slot]).start()
            return carry

        lax.fori_loop(0, nchunk_sm[tile], per_chunk, 0)

    def wait_chunks(tile):
        slot = tile % 2
        n = nchunk_sm[tile]
        for bit in range(max_chunks.bit_length()):
            @pl.when((lax.shift_right_logical(n, bit) & 1) == 1)
            def _():
                rows = RUN_CHUNK << bit
                pltpu.make_async_copy(_rows(ys_ref, 0, rows),
                                      _rows(buf_ref.at[slot], 0, rows),
                                      sem.at[slot]).wait()

    @pl.when(step == 0)
    def _():
        start_chunks(step)

    @pl.when(step + 1 < nsteps)
    def _():
        start_chunks(step + 1)

    wait_chunks(step)

    base = step * tb
    tile_buf = buf_ref.at[step % 2]

    def move(r, carry):
        for g_ref, local_sm in ((g0_ref, local0_sm), (g1_ref, local1_sm)):
            _rows(g_ref, r)[...] = _rows(tile_buf, local_sm[base + r])[...]
        return carry

    lax.fori_loop(0, tb, move, 0, unroll=8)
    meta = meta_ref[...]
    moe = (_load_row_tiles(g0_ref) * meta[:, 2:3]
           + _load_row_tiles(g1_ref) * meta[:, 3:4])
    out_ref[...] = x1_ref[...] + moe


def _combine(local0, local1, tables, x1, meta, ys):
    t, d_model = x1.shape
    tb = TB_RANK
    max_chunks = tables["max_chunks"]
    kern = functools.partial(_combine_kernel, tb=tb, max_chunks=max_chunks)
    buf_rows = max_chunks * RUN_CHUNK
    return pl.pallas_call(
        kern,
        grid_spec=pltpu.PrefetchScalarGridSpec(
            num_scalar_prefetch=4,
            grid=(t // tb,),
            in_specs=[pl.BlockSpec((tb, d_model), lambda i, *_: (i, 0)),
                      pl.BlockSpec((tb, LANES), lambda i, *_: (i, 0)),
                      pl.BlockSpec(memory_space=pl.ANY)],
            out_specs=pl.BlockSpec((tb, d_model), lambda i, *_: (i, 0)),
            scratch_shapes=[pltpu.VMEM((2,) + _row_tiles((buf_rows, d_model)), F32),
                            pltpu.VMEM(_row_tiles((tb, d_model)), F32),
                            pltpu.VMEM(_row_tiles((tb, d_model)), F32),
                            pltpu.SemaphoreType.DMA((2,))]),
        out_shape=jax.ShapeDtypeStruct((t, d_model), F32),
        compiler_params=_cparams(("arbitrary",)),
        name="combine",
    )(local0, local1, tables["chunk_src"], tables["tile_chunks"], x1, meta, ys)


def _layer(h2, *, batch, seq, norm1_g, w_in, conv_w, q_norm_g, k_norm_g,
           w_conv_out, w_attn_out, w_o, norm2_g, w_group, w_router, w1, w3, w2):
    t, d_model = h2.shape
    c, sa, sb = _rope_tables(seq)
    scale = HEAD_DIM ** -0.5 * LOG2_E
    tables_q = tuple(jnp.asarray(tab * np.float32(scale)) for tab in (c, sa, sb))
    tables_k = tuple(jnp.asarray(tab) for tab in (c, sa, sb))

    cb, z, q, k, v, sgc, sga = _inproj(
        h2, norm1_g[None, :], w_in.astype(BF16), q_norm_g[None, :], k_norm_g[None, :],
        tables_q, tables_k, seq=seq)
    o = _attention(q, k, v, batch=batch, seq=seq)

    n_route = N_GROUPS + N_EXPERTS
    wr = jnp.concatenate(
        [w_group, w_router, jnp.zeros((d_model, LANES - n_route), F32)], axis=1)
    wr_hi = wr.astype(BF16)
    wr = jnp.concatenate([wr_hi, (wr - wr_hi.astype(F32)).astype(BF16)], axis=1)
    x1, u2, meta, cnt = _post(h2, cb, z, o, sgc, sga, conv_w,
                              w_conv_out.astype(BF16), w_attn_out.astype(BF16),
                              w_o.astype(BF16), norm2_g[None, :], wr, seq=seq)

    blk = MOE_ROWS
    local0, local1, runs = _rank(meta, cnt, blk=blk)
    local0, local1 = local0.reshape(-1), local1.reshape(-1)
    runs = runs.reshape(-1, 8, LANES)
    tables = _tile_tables(runs[:, 0, :N_EXPERTS], runs[:, 1, :N_EXPERTS], tb=TB_RANK)
    p_rows = t * TOP_K + N_EXPERTS * blk
    sched = _expert_schedule(cnt[0, :N_EXPERTS].astype(jnp.int32), blk=blk,
                             nb=p_rows // blk)

    xs = _dispatch(local0, local1, tables, sched["pad_start"], sched["pad_len"],
                   sched["nused"], u2, p_rows=p_rows, blk=blk)
    ys = _experts(sched, sched["nused"], xs, w1, w3, w2, blk=blk)
    return _combine(local0, local1, tables, x1, meta, ys)


def kernel(x, norm1_g, w_in, conv_w, q_norm_g, k_norm_g, w_conv_out, w_attn_out, w_o,
           norm2_g, w_group, w_router, w1, w3, w2):
    batch, seq, d_model = x.shape
    h2 = x.reshape(batch * seq, d_model)
    for l in range(norm1_g.shape[0]):
        h2 = _layer(h2, batch=batch, seq=seq, norm1_g=norm1_g[l], w_in=w_in[l],
                    conv_w=conv_w[l], q_norm_g=q_norm_g[l], k_norm_g=k_norm_g[l],
                    w_conv_out=w_conv_out[l], w_attn_out=w_attn_out[l], w_o=w_o[l],
                    norm2_g=norm2_g[l], w_group=w_group[l], w_router=w_router[l],
                    w1=w1[l], w3=w3[l], w2=w2[l])
    return h2.reshape(batch, seq, d_model)
```

```python
import functools

import jax
import jax.numpy as jnp
import numpy as np
from jax import lax
from jax.experimental import pallas as pl
from jax.experimental.pallas import tpu as pltpu

F32 = jnp.float32
BF16 = jnp.bfloat16

GRID_W = 64
EPS = 1e-6
N_HEADS = 8
N_KV_HEADS = 2
HEAD_DIM = 128
ROPE_THETA = 10000.0
N_GROUPS = 8
EXPERTS_PER_GROUP = 8
N_EXPERTS = N_GROUPS * EXPERTS_PER_GROUP
TOP_K = 2
LOG2_E = 1.4426950408889634

LANES = 128
MXU_DIM = 256
BF16_SUBLANES = 16
V7X_VMEM_LIMIT_BYTES = 56000 * 1024

TM_PROJ = 512
TQ_ATTN = 512
TK_ATTN = 512
TB_RANK = 1024
RUN_CHUNK = 8
MOE_ROWS = 256
WEIGHT_RING = 3
WEIGHT_DMA_PRIORITY = 1
DIGIT = 256.0


def _cparams(sem):
    return pltpu.CompilerParams(dimension_semantics=sem,
                                vmem_limit_bytes=V7X_VMEM_LIMIT_BYTES)


ROW_TILE = 8


def _row_tiles(shape2d):
    rows, width = shape2d
    assert width == ROW_TILE * LANES
    return (rows * ROW_TILE, LANES)


def _rows(ref, r, n=1):
    return ref.at[pl.ds(pl.multiple_of(r * ROW_TILE, ROW_TILE), n * ROW_TILE)]


def _load_row_tiles(ref):
    rows = ref.shape[0] // ROW_TILE
    return jnp.concatenate(
        [ref[pl.ds(s, rows, stride=ROW_TILE), :] for s in range(ROW_TILE)], axis=1)


def _store_row_tiles(ref, value):
    rows = value.shape[0]
    for s in range(ROW_TILE):
        ref[pl.ds(s, rows, stride=ROW_TILE), :] = value[:, s * LANES:(s + 1) * LANES]


def _resident(shape):
    nd = len(shape)
    return pl.BlockSpec(shape, lambda *_: (0,) * nd, pipeline_mode=pl.Buffered(1))


def _lane_prefix(row):
    r = lax.broadcasted_iota(jnp.int32, (LANES, LANES), 0)
    col = lax.broadcasted_iota(jnp.int32, (LANES, LANES), 1)
    upper = jnp.where(r < col, 1.0, 0.0).astype(BF16)
    return jnp.dot(jnp.broadcast_to(row, (8, LANES)).astype(BF16), upper,
                   preferred_element_type=F32)[0:1, :]


def _head_norm_rope(xh, g, c, sa, sb):
    ms = jnp.mean(xh * xh, axis=-1, keepdims=True)
    y = xh * lax.rsqrt(ms + EPS) * g
    y_next = pltpu.roll(y, HEAD_DIM - 1, axis=1)
    y_prev = pltpu.roll(y, 1, axis=1)
    return y * c + y_next * sa + y_prev * sb


def _sigmoid(x):
    return 0.5 * jnp.tanh(0.5 * x) + 0.5


def _inproj_kernel(x_ref, g1_ref, w_ref, gq_ref, gk_ref,
                   cq_ref, saq_ref, sbq_ref, ck_ref, sak_ref, sbk_ref,
                   cb_ref, z_ref, q_ref, k_ref, v_ref, sgc_ref, sga_ref,
                   *, d_conv, d_q, d_kv, d_model):
    x = x_ref[...]
    ms = jnp.mean(x * x, axis=-1, keepdims=True)
    u = (x * lax.rsqrt(ms + EPS) * g1_ref[...]).astype(BF16)

    def proj(lo, width):
        return jnp.dot(u, w_ref[:, lo:lo + width], preferred_element_type=F32)

    o_cb, o_cc, o_cx = 0, d_conv, 2 * d_conv
    o_q = 3 * d_conv
    o_k = o_q + d_q
    o_v = o_k + d_kv
    o_gc = o_v + d_kv
    o_ga = o_gc + d_model

    sgc_ref[...] = _sigmoid(proj(o_gc, d_model)).astype(BF16)
    sga_ref[...] = _sigmoid(proj(o_ga, d_model)).astype(BF16)

    q = proj(o_q, d_q)
    gq = gq_ref[...]
    cq, saq, sbq = cq_ref[...], saq_ref[...], sbq_ref[...]
    for h in range(d_q // HEAD_DIM):
        sl = slice(h * HEAD_DIM, (h + 1) * HEAD_DIM)
        q_ref[:, sl] = _head_norm_rope(q[:, sl], gq, cq, saq, sbq).astype(BF16)

    k = proj(o_k, d_kv)
    gk = gk_ref[...]
    ck, sak, sbk = ck_ref[...], sak_ref[...], sbk_ref[...]
    for h in range(d_kv // HEAD_DIM):
        sl = slice(h * HEAD_DIM, (h + 1) * HEAD_DIM)
        k_ref[:, sl] = _head_norm_rope(k[:, sl], gk, ck, sak, sbk).astype(BF16)

    z_ref[...] = (proj(o_cc, d_conv) * proj(o_cx, d_conv)).astype(BF16)
    v_ref[...] = proj(o_v, d_kv).astype(BF16)
    cb_ref[...] = proj(o_cb, d_conv).astype(BF16)


def _rope_tables(seq):
    rows = seq // GRID_W
    axis_dim = HEAD_DIM // 2
    row = np.repeat(np.arange(rows, dtype=np.float32), GRID_W)
    col = np.tile(np.arange(GRID_W, dtype=np.float32), rows)
    inv = (np.float32(ROPE_THETA)
           ** (-np.arange(0, axis_dim, 2, dtype=np.float32) / np.float32(axis_dim)))
    ang = np.concatenate([row[:, None] * inv, col[:, None] * inv], axis=-1)
    ang = ang.astype(np.float32)
    cos, sin = np.cos(ang), np.sin(ang)
    zero = np.zeros_like(sin)
    c = np.repeat(cos, 2, axis=-1)
    sa = np.stack([-sin, zero], axis=-1).reshape(seq, HEAD_DIM)
    sb = np.stack([zero, sin], axis=-1).reshape(seq, HEAD_DIM)
    return c, sa, sb


def _inproj(x2, g1, w_in_bf, gq, gk, tables_q, tables_k, *, seq):
    t, d_model = x2.shape
    d_q = N_HEADS * HEAD_DIM
    d_kv = N_KV_HEADS * HEAD_DIM
    d_in = w_in_bf.shape[1]
    d_conv = (d_in - d_q - 2 * d_kv - 2 * d_model) // 3
    tm = TM_PROJ
    nseq = seq // tm

    def row(width):
        return pl.BlockSpec((tm, width), lambda i: (i, 0))

    table = pl.BlockSpec((tm, HEAD_DIM), lambda i: (i % nseq, 0))
    kern = functools.partial(_inproj_kernel, d_conv=d_conv, d_q=d_q, d_kv=d_kv,
                             d_model=d_model)
    out_shape = [jax.ShapeDtypeStruct((t, w), BF16)
                 for w in (d_conv, d_conv, d_q, d_kv, d_kv, d_model, d_model)]
    return pl.pallas_call(
        kern,
        grid=(t // tm,),
        in_specs=[row(d_model), _resident((1, d_model)), _resident((d_model, d_in)),
                  _resident((1, HEAD_DIM)), _resident((1, HEAD_DIM)),
                  table, table, table, table, table, table],
        out_specs=[row(d_conv), row(d_conv), row(d_q), row(d_kv), row(d_kv),
                   row(d_model), row(d_model)],
        out_shape=out_shape,
        compiler_params=_cparams(("arbitrary",)),
        name="inproj",
    )(x2, g1, w_in_bf, gq, gk, *tables_q, *tables_k)


def _attn_kernel(q_ref, k_ref, v_ref, o_ref, qs_ref, vext_ref, m_ref, acc_ref,
                 *, tq, chunks, group):
    @pl.when(pl.program_id(2) == 0)
    def _():
        vext_ref[:, :HEAD_DIM] = v_ref[...]
        vext_ref[:, HEAD_DIM:] = jnp.ones((vext_ref.shape[0], HEAD_DIM), BF16)

    for g in range(group):
        qs_ref[g * tq:(g + 1) * tq, :] = q_ref[:, g * HEAD_DIM:(g + 1) * HEAD_DIM]
    m_ref[...] = jnp.full(m_ref.shape, -jnp.inf, F32)
    acc_ref[...] = jnp.zeros(acc_ref.shape, F32)

    lo = 0
    for tk in chunks:
        keys = slice(lo, lo + tk)
        lo += tk
        s = lax.dot_general(qs_ref[...], k_ref[keys, :], (((1,), (1,)), ((), ())),
                            preferred_element_type=F32)
        m_prev = m_ref[...]
        m_new = jnp.maximum(m_prev, jnp.max(s, axis=-1, keepdims=True))
        alpha = jnp.exp2(m_prev - m_new)
        p = jnp.concatenate(
            [jnp.exp2(s[:, c * LANES:(c + 1) * LANES] - m_new) for c in range(tk // LANES)],
            axis=1).astype(BF16)
        pv = jnp.dot(p, vext_ref[keys, :], preferred_element_type=F32)
        acc_ref[...] = jnp.concatenate([alpha, alpha], axis=1) * acc_ref[...] + pv
        m_ref[...] = m_new

    out = acc_ref[:, :HEAD_DIM] / acc_ref[:, HEAD_DIM:]
    for g in range(group):
        o_ref[:, g * HEAD_DIM:(g + 1) * HEAD_DIM] = out[g * tq:(g + 1) * tq].astype(BF16)


def _attention(q, k, v, *, batch, seq):
    t = q.shape[0]
    group = N_HEADS // N_KV_HEADS
    tq, tk = TQ_ATTN, TK_ATTN
    nq = seq // tq
    gw = group * HEAD_DIM
    chunks = (tk // 2,) + (tk,) * (seq // tk - 1) + (tk // 2,)
    kern = functools.partial(_attn_kernel, tq=tq, chunks=chunks, group=group)
    return pl.pallas_call(
        kern,
        grid=(batch, N_KV_HEADS, nq),
        in_specs=[pl.BlockSpec((tq, gw), lambda b, h, i: (b * nq + i, h)),
                  pl.BlockSpec((seq, HEAD_DIM), lambda b, h, i: (b, h)),
                  pl.BlockSpec((seq, HEAD_DIM), lambda b, h, i: (b, h))],
        out_specs=pl.BlockSpec((tq, gw), lambda b, h, i: (b * nq + i, h)),
        out_shape=jax.ShapeDtypeStruct((t, N_HEADS * HEAD_DIM), BF16),
        scratch_shapes=[pltpu.VMEM((group * tq, HEAD_DIM), BF16),
                        pltpu.VMEM((seq, 2 * HEAD_DIM), BF16),
                        pltpu.VMEM((group * tq, LANES), F32),
                        pltpu.VMEM((group * tq, 2 * HEAD_DIM), F32)],
        compiler_params=_cparams(("arbitrary", "arbitrary", "arbitrary")),
        name="attention",
    )(q, k, v)


def _route(logits):
    rows = logits.shape[0]
    lane = lax.broadcasted_iota(jnp.int32, (rows, LANES), 1).astype(F32)
    neg = -jnp.inf
    big = float(2 * LANES)
    is_group = lane < N_GROUPS
    gl = jnp.where(is_group, logits, neg)
    gmax = jnp.max(gl, axis=-1, keepdims=True)
    gidx = jnp.min(jnp.where(gl == gmax, lane, big), axis=-1, keepdims=True)
    gsum = jnp.sum(jnp.where(is_group, jnp.exp(logits - gmax), 0.0), axis=-1,
                   keepdims=True)
    pg = 1.0 / gsum
    lane_group = jnp.floor(lane * (1.0 / EXPERTS_PER_GROUP)) - 1.0
    mine = (lane_group == gidx) & (lane >= N_GROUPS) & (lane < N_GROUPS + N_EXPERTS)
    sel = jnp.where(mine, logits, neg)
    v1 = jnp.max(sel, axis=-1, keepdims=True)
    i1 = jnp.min(jnp.where(sel == v1, lane, big), axis=-1, keepdims=True)
    sel2 = jnp.where(lane == i1, neg, sel)
    v2 = jnp.max(sel2, axis=-1, keepdims=True)
    i2 = jnp.min(jnp.where(sel2 == v2, lane, big), axis=-1, keepdims=True)
    t2 = jnp.exp(v2 - v1)
    den = 1.0 + t2
    wgt1 = pg * (1.0 / den)
    wgt2 = pg * (t2 / den)
    e1 = i1 - N_GROUPS
    e2 = i2 - N_GROUPS
    meta = jnp.where(lane == 0, e1,
                     jnp.where(lane == 1, e2,
                               jnp.where(lane == 2, wgt1,
                                         jnp.where(lane == 3, wgt2, 0.0))))
    picked = jnp.where((lane == e1) | (lane == e2), 1.0, 0.0)
    return meta, jnp.sum(picked, axis=0, keepdims=True)


def _post_kernel(x_ref, cb_ref, z_ref, zprev_ref, znext_ref, o_ref, sgc_ref, sga_ref,
                 cw_ref, wc_ref, wa_ref, wo_ref, g2_ref, wr_ref,
                 x1_ref, u2_ref, meta_ref, cnt_ref, *, tm, nseq):
    i = pl.program_id(0)
    at_start = (i % nseq) == 0
    at_end = (i % nseq) == nseq - 1
    rowid = lax.broadcasted_iota(jnp.int32, (tm, 1), 0)

    @pl.when(i == 0)
    def _():
        cnt_ref[...] = jnp.zeros(cnt_ref.shape, F32)

    y_attn = jnp.dot(o_ref[...], wa_ref[...], preferred_element_type=F32)

    y_conv = None
    for c in range(z_ref.shape[1] // MXU_DIM):
        sl = slice(c * MXU_DIM, (c + 1) * MXU_DIM)
        z = z_ref[:, sl].astype(F32)
        prev_row = zprev_ref[BF16_SUBLANES - 1:BF16_SUBLANES, sl].astype(F32)
        next_row = znext_ref[0:1, sl].astype(F32)
        prev_row = jnp.where(at_start, 0.0, prev_row)
        next_row = jnp.where(at_end, 0.0, next_row)
        zp = jnp.where(rowid == 0, prev_row, pltpu.roll(z, 1, axis=0))
        zn = jnp.where(rowid == tm - 1, next_row, pltpu.roll(z, tm - 1, axis=0))
        conv = cw_ref[0:1, sl] * zp + cw_ref[1:2, sl] * z + cw_ref[2:3, sl] * zn
        cbz = (cb_ref[:, sl].astype(F32) * conv).astype(BF16)
        part = jnp.dot(cbz, wc_ref[sl, :], preferred_element_type=F32)
        y_conv = part if y_conv is None else y_conv + part
    merged = (sgc_ref[...].astype(F32) * y_conv
              + sga_ref[...].astype(F32) * y_attn).astype(BF16)
    x1 = x_ref[...] + jnp.dot(merged, wo_ref[...], preferred_element_type=F32)
    x1_ref[...] = x1

    ms = jnp.mean(x1 * x1, axis=-1, keepdims=True)
    u2 = x1 * lax.rsqrt(ms + EPS) * g2_ref[...]
    _store_row_tiles(u2_ref, u2)

    u2_hi = u2.astype(BF16)
    u2_lo = (u2 - u2_hi.astype(F32)).astype(BF16)
    hi_part = jnp.dot(u2_hi, wr_ref[...], preferred_element_type=F32)
    lo_part = jnp.dot(u2_lo, wr_ref[:, :LANES], preferred_element_type=F32)
    logits = hi_part[:, :LANES] + (hi_part[:, LANES:] + lo_part)
    meta, picked = _route(logits)
    meta_ref[...] = meta
    cnt_ref[...] += picked


def _post(x2, cb, z, o, sgc, sga, conv_w, wc, wa, wo, g2, wr, *, seq):
    t, d_model = x2.shape
    tm = TM_PROJ
    nseq = seq // tm
    hb = tm // BF16_SUBLANES
    nhalo = t // BF16_SUBLANES
    d_conv = cb.shape[1]
    d_q = o.shape[1]

    def row(width):
        return pl.BlockSpec((tm, width), lambda i: (i, 0))

    kern = functools.partial(_post_kernel, tm=tm, nseq=nseq)
    return pl.pallas_call(
        kern,
        grid=(t // tm,),
        in_specs=[row(d_model), row(d_conv), row(d_conv),
                  pl.BlockSpec((BF16_SUBLANES, d_conv),
                               lambda i: (jnp.maximum(i * hb - 1, 0), 0)),
                  pl.BlockSpec((BF16_SUBLANES, d_conv),
                               lambda i: (jnp.minimum((i + 1) * hb, nhalo - 1), 0)),
                  row(d_q), row(d_model), row(d_model),
                  _resident(conv_w.shape), _resident(wc.shape), _resident(wa.shape),
                  _resident(wo.shape), _resident(g2.shape), _resident(wr.shape)],
        out_specs=[row(d_model),
                   pl.BlockSpec(_row_tiles((tm, d_model)), lambda i: (i, 0)),
                   row(LANES),
                   pl.BlockSpec((1, LANES), lambda i: (0, 0))],
        out_shape=[jax.ShapeDtypeStruct((t, d_model), F32),
                   jax.ShapeDtypeStruct(_row_tiles((t, d_model)), F32),
                   jax.ShapeDtypeStruct((t, LANES), F32),
                   jax.ShapeDtypeStruct((1, LANES), F32)],
        compiler_params=_cparams(("arbitrary",)),
        name="post",
    )(x2, cb, z, z, z, o, sgc, sga, conv_w, wc, wa, wo, g2, wr)


def _rank_kernel(meta_ref, cnt_ref, l0_ref, l1_ref, runs_ref,
                 carry_ref, pstart_ref, *, tb, blk):
    i = pl.program_id(0)
    lane = lax.broadcasted_iota(jnp.int32, (tb, LANES), 1).astype(F32)
    meta = meta_ref[...]
    oh1 = jnp.where(lane == meta[:, 0:1], 1.0, 0.0)
    oh2 = jnp.where(lane == meta[:, 1:2], 1.0, 0.0)
    c = oh1 + oh2

    @pl.when(i == 0)
    def _():
        nblk = jnp.floor((cnt_ref[...] + (blk - 1)) * (1.0 / blk))
        pstart_ref[...] = _lane_prefix(nblk) * blk
        carry_ref[...] = jnp.zeros(carry_ref.shape, F32)

    r = lax.broadcasted_iota(jnp.int32, (tb, tb), 0)
    col = lax.broadcasted_iota(jnp.int32, (tb, tb), 1)
    lower = jnp.where(col < r, 1.0, 0.0).astype(BF16)
    prefix = jnp.dot(lower, c.astype(BF16), preferred_element_type=F32)
    run_start = carry_ref[...] + pstart_ref[...]
    run_len = jnp.sum(c, axis=0, keepdims=True)

    nchunk = jnp.floor((run_len + (RUN_CHUNK - 1)) * (1.0 / RUN_CHUNK))
    local = prefix + _lane_prefix(nchunk) * RUN_CHUNK

    digits = jnp.zeros((tb, LANES), F32)
    for j, onehot in enumerate((oh1, oh2)):
        value = jnp.sum(onehot * local, axis=-1, keepdims=True)
        high = jnp.floor(value * (1.0 / DIGIT))
        digits = jnp.where(lane == 2 * j, high,
                           jnp.where(lane == 2 * j + 1, value - high * DIGIT, digits))
    eye = jnp.where(col == r, 1.0, 0.0).astype(BF16)
    rows = lax.dot_general(digits.astype(BF16), eye, (((0,), (0,)), ((), ())),
                           preferred_element_type=F32)
    for j, out_ref in enumerate((l0_ref, l1_ref)):
        out_ref[...] = (rows[2 * j:2 * j + 1, :] * DIGIT
                        + rows[2 * j + 1:2 * j + 2, :]).astype(jnp.int32)
    row = lax.broadcasted_iota(jnp.int32, (8, LANES), 0)
    runs = jnp.where(row == 0, run_start, jnp.where(row == 1, run_len, 0.0))
    runs_ref[...] = runs.astype(jnp.int32)
    carry_ref[...] += run_len


def _rank(meta, cnt, *, blk):
    t = meta.shape[0]
    tb = TB_RANK
    kern = functools.partial(_rank_kernel, tb=tb, blk=blk)
    per_token = pl.BlockSpec((1, tb), lambda i: (0, i))
    return pl.pallas_call(
        kern,
        grid=(t // tb,),
        in_specs=[pl.BlockSpec((tb, LANES), lambda i: (i, 0)),
                  pl.BlockSpec((1, LANES), lambda i: (0, 0))],
        out_specs=[per_token] * 2 + [pl.BlockSpec((8, LANES), lambda i: (i, 0))],
        out_shape=[jax.ShapeDtypeStruct((1, t), jnp.int32)] * 2
                  + [jax.ShapeDtypeStruct((t // tb * 8, LANES), jnp.int32)],
        scratch_shapes=[pltpu.VMEM((1, LANES), F32), pltpu.VMEM((1, LANES), F32)],
        compiler_params=_cparams(("arbitrary",)),
        name="rank",
    )(meta, cnt)


def _dispatch_kernel(local0_sm, local1_sm, full_src_sm, full_dst_sm, nfull_sm,
                     rem_src_sm, rem_dst_sm, rem_len_sm, pad_start_sm, pad_len_sm, nused_sm,
                     u2_ref, xs_ref, buf_ref, zero_ref, sem, zsem,
                     *, tb, blk, nblocks, max_full):
    step = pl.program_id(0)
    nsteps = pl.num_programs(0)
    base = step * tb
    nbits = blk.bit_length() - 1
    slot = step % 2
    tile_buf = buf_ref.at[slot]

    def wait_tile(s):
        pltpu.make_async_copy(_rows(buf_ref.at[s], 0, tb * TOP_K),
                              _rows(xs_ref, 0, tb * TOP_K), sem.at[s]).wait()

    def pad_copies(visit):
        def per_expert(e, carry):
            start, length = pad_start_sm[e], pad_len_sm[e]
            for bit in range(nbits):
                size = 1 << bit
                higher = lax.shift_left(lax.shift_right_logical(length, bit + 1), bit + 1)

                @pl.when((lax.shift_right_logical(length, bit) & 1) == 1)
                def _():
                    visit(pltpu.make_async_copy(
                        zero_ref.at[pl.ds(0, size * ROW_TILE)],
                        _rows(xs_ref, start + higher, size), zsem))
            return carry

        lax.fori_loop(0, N_EXPERTS, per_expert, 0)

        def per_block(b, carry):
            visit(pltpu.make_async_copy(zero_ref, _rows(xs_ref, b * blk, blk), zsem))
            return carry

        lax.fori_loop(nused_sm[0], nblocks, per_block, 0)

    @pl.when(step == 0)
    def _():
        zero_ref[...] = jnp.zeros(zero_ref.shape, F32)
        pad_copies(lambda c: c.start())

    @pl.when(step >= 2)
    def _():
        wait_tile(slot)

    def move(r, carry):
        row = _rows(u2_ref, r)[...]
        _rows(tile_buf, local0_sm[base + r])[...] = row
        _rows(tile_buf, local1_sm[base + r])[...] = row
        return carry

    lax.fori_loop(0, tb, move, 0, unroll=8)

    def full_chunk(c, carry):
        pltpu.make_async_copy(
            _rows(tile_buf, full_src_sm[step * max_full + c], RUN_CHUNK),
            _rows(xs_ref, full_dst_sm[step * max_full + c], RUN_CHUNK),
            sem.at[slot]).start()
        return carry

    lax.fori_loop(0, nfull_sm[step], full_chunk, 0)

    def remainder(e, carry):
        idx = step * N_EXPERTS + e
        src, dst, length = rem_src_sm[idx], rem_dst_sm[idx], rem_len_sm[idx]
        for bit in range(RUN_CHUNK.bit_length() - 1):
            size = 1 << bit
            higher = lax.shift_left(lax.shift_right_logical(length, bit + 1), bit + 1)

            @pl.when((lax.shift_right_logical(length, bit) & 1) == 1)
            def _():
                pltpu.make_async_copy(_rows(tile_buf, src + higher, size),
                                      _rows(xs_ref, dst + higher, size),
                                      sem.at[slot]).start()
        return carry

    lax.fori_loop(0, N_EXPERTS, remainder, 0)

    @pl.when(step == nsteps - 1)
    def _():
        @pl.when(step >= 1)
        def _():
            wait_tile(1 - slot)

        wait_tile(slot)
        pad_copies(lambda c: c.wait())


def _tile_tables(run_start, run_len, *, tb):
    shift = RUN_CHUNK.bit_length() - 1
    nchunk = (run_len + RUN_CHUNK - 1) >> shift
    cum = jnp.cumsum(nchunk, axis=1)
    first = cum - nchunk
    local_start = first * RUN_CHUNK

    def flat_list(count, max_count, value_at):
        ccum = jnp.cumsum(count, axis=1)
        cfirst = (ccum - count)[:, None, :]
        c_idx = jnp.arange(max_count, dtype=jnp.int32)[None, :, None]
        owns = (cfirst <= c_idx) & (c_idx < ccum[:, None, :])
        return [jnp.sum(jnp.where(owns, v[:, None, :] + (c_idx - cfirst) * RUN_CHUNK, 0),
                        axis=2).astype(jnp.int32) for v in value_at], ccum[:, -1]

    max_chunks = tb * TOP_K // RUN_CHUNK + N_EXPERTS
    (chunk_src,), tile_chunks = flat_list(nchunk, max_chunks, [run_start])
    max_full = tb * TOP_K // RUN_CHUNK
    nfull = run_len >> shift
    (full_src, full_dst), tile_full = flat_list(nfull, max_full, [local_start, run_start])
    whole = nfull * RUN_CHUNK
    return dict(
        max_chunks=max_chunks, chunk_src=chunk_src.reshape(-1),
        tile_chunks=tile_chunks.astype(jnp.int32),
        max_full=max_full, full_src=full_src.reshape(-1), full_dst=full_dst.reshape(-1),
        tile_full=tile_full.astype(jnp.int32),
        rem_src=(local_start + whole).reshape(-1).astype(jnp.int32),
        rem_dst=(run_start + whole).reshape(-1).astype(jnp.int32),
        rem_len=(run_len - whole).reshape(-1).astype(jnp.int32))


def _dispatch(local0, local1, tables, pad_start, pad_len, nused, u2, *, p_rows, blk):
    t = u2.shape[0] // ROW_TILE
    tb = TB_RANK
    d_model = ROW_TILE * LANES
    kern = functools.partial(_dispatch_kernel, tb=tb, blk=blk, nblocks=p_rows // blk,
                             max_full=tables["max_full"])
    buf_rows = tables["max_chunks"] * RUN_CHUNK
    return pl.pallas_call(
        kern,
        grid_spec=pltpu.PrefetchScalarGridSpec(
            num_scalar_prefetch=11,
            grid=(t // tb,),
            in_specs=[pl.BlockSpec((tb * ROW_TILE, LANES), lambda i, *_: (i, 0))],
            out_specs=pl.BlockSpec(memory_space=pl.ANY),
            scratch_shapes=[pltpu.VMEM((2,) + _row_tiles((buf_rows, d_model)), F32),
                            pltpu.VMEM((blk * ROW_TILE, LANES), F32),
                            pltpu.SemaphoreType.DMA((2,)), pltpu.SemaphoreType.DMA]),
        out_shape=jax.ShapeDtypeStruct((p_rows * ROW_TILE, LANES), F32),
        compiler_params=_cparams(("arbitrary",)),
        name="dispatch",
    )(local0, local1, tables["full_src"], tables["full_dst"], tables["tile_full"],
      tables["rem_src"], tables["rem_dst"], tables["rem_len"],
      pad_start, pad_len, nused, u2)


def _experts_kernel(be_sm, first_sm, ahead_sm, slot_sm, head_sm, nused_sm,
                    xs_ref, w1_hbm, w3_hbm, w2_hbm, ys_ref,
                    w1f_ref, w3f_ref, w2f_ref, w13b_ref, w2b_ref, wsem, *, d_expert):
    b = pl.program_id(0)
    used = b < nused_sm[0]

    def weight_copies(expert, slot):
        return [pltpu.make_async_copy(src.at[expert], dst.at[slot], wsem.at[slot])
                for src, dst in ((w1_hbm, w1f_ref), (w3_hbm, w3f_ref), (w2_hbm, w2f_ref))]

    @pl.when(b == 0)
    def _():
        for j in range(WEIGHT_RING - 1):
            @pl.when(head_sm[j] >= 0)
            def _():
                for c in weight_copies(head_sm[j], j):
                    c.start(priority=WEIGHT_DMA_PRIORITY)

    @pl.when(first_sm[b] == 1)
    def _():
        slot = slot_sm[b]
        for c in weight_copies(be_sm[b], slot):
            c.wait()

        @pl.when(ahead_sm[b] >= 0)
        def _():
            ahead_slot = lax.rem(slot + (WEIGHT_RING - 1), WEIGHT_RING)
            for c in weight_copies(ahead_sm[b], ahead_slot):
                c.start(priority=WEIGHT_DMA_PRIORITY)

        w13b_ref[:, :d_expert] = w1f_ref[slot].astype(BF16)
        w13b_ref[:, d_expert:] = w3f_ref[slot].astype(BF16)
        w2b_ref[...] = w2f_ref[slot].astype(BF16)

    @pl.when(used)
    def _():
        xb = _load_row_tiles(xs_ref).astype(BF16)
        h = jnp.dot(xb, w13b_ref[...], preferred_element_type=F32)
        h1 = h[:, :d_expert]
        h3 = h[:, d_expert:]
        a = (h1 * _sigmoid(h1) * h3).astype(BF16)
        _store_row_tiles(ys_ref, jnp.dot(a, w2b_ref[...], preferred_element_type=F32))


def _experts(sched, nused, xs, w1, w3, w2, *, blk):
    p_rows = xs.shape[0] // ROW_TILE
    d_model, d_expert = w1.shape[-2:]
    kern = functools.partial(_experts_kernel, d_expert=d_expert)

    def used_block(b, be, first, ahead, slot, head, nu):
        return (jnp.maximum(jnp.minimum(b, nu[0] - 1), 0), 0)

    rows_blk = pl.BlockSpec((blk * ROW_TILE, LANES), used_block)
    hbm = pl.BlockSpec(memory_space=pl.ANY)
    n_prefetch = 6
    return pl.pallas_call(
        kern,
        grid_spec=pltpu.PrefetchScalarGridSpec(
            num_scalar_prefetch=n_prefetch,
            grid=(p_rows // blk,),
            in_specs=[rows_blk, hbm, hbm, hbm],
            out_specs=rows_blk,
            scratch_shapes=[pltpu.VMEM((WEIGHT_RING, d_model, d_expert), F32),
                            pltpu.VMEM((WEIGHT_RING, d_model, d_expert), F32),
                            pltpu.VMEM((WEIGHT_RING, d_expert, d_model), F32),
                            pltpu.VMEM((d_model, 2 * d_expert), BF16),
                            pltpu.VMEM((d_expert, d_model), BF16),
                            pltpu.SemaphoreType.DMA((WEIGHT_RING,))]),
        out_shape=jax.ShapeDtypeStruct((p_rows * ROW_TILE, LANES), F32),
        input_output_aliases={n_prefetch: 0},
        compiler_params=_cparams(("arbitrary",)),
        name="experts",
    )(sched["expert"], sched["first"], sched["ahead"], sched["slot"], sched["head"],
      nused, xs, w1, w3, w2)


def _expert_schedule(counts, *, blk, nb):
    nblk = (counts + blk - 1) // blk
    pend = jnp.cumsum(nblk)
    nused = pend[-1]
    barange = jnp.arange(nb, dtype=jnp.int32)
    bidx = jnp.minimum(barange, nused - 1)
    expert = jnp.minimum(jnp.sum(pend[None, :] <= bidx[:, None], axis=1),
                         N_EXPERTS - 1).astype(jnp.int32)
    live = barange < nused
    first = (live & (barange == (pend - nblk)[expert])).astype(jnp.int32)
    has_rows = nblk > 0
    used_rank = jnp.cumsum(has_rows) - 1
    j_idx = jnp.arange(N_EXPERTS + WEIGHT_RING, dtype=jnp.int32)
    e_idx = jnp.arange(N_EXPERTS, dtype=jnp.int32)
    hit = has_rows[None, :] & (used_rank[None, :] == j_idx[:, None])
    used_list = jnp.sum(jnp.where(hit, e_idx[None, :] + 1, 0), axis=1) - 1
    rank_b = used_rank[expert]
    ahead = jnp.where(live, used_list[rank_b + (WEIGHT_RING - 1)], -1)
    return dict(
        expert=expert, first=first, ahead=ahead.astype(jnp.int32),
        slot=(rank_b % WEIGHT_RING).astype(jnp.int32),
        head=used_list[:WEIGHT_RING - 1].astype(jnp.int32),
        nused=nused.reshape(1).astype(jnp.int32),
        pad_start=((pend - nblk) * blk + counts).astype(jnp.int32),
        pad_len=(nblk * blk - counts).astype(jnp.int32))


def _combine_kernel(local0_sm, local1_sm, chunk_src_sm, nchunk_sm,
                    x1_ref, meta_ref, ys_ref, out_ref,
                    buf_ref, g0_ref, g1_ref, sem, *, tb, max_chunks):
    step = pl.program_id(0)
    nsteps = pl.num_programs(0)

    def start_chunks(tile):
        slot = tile % 2

        def per_chunk(c, carry):
            pltpu.make_async_copy(
                _rows(ys_ref, chunk_src_sm[tile * max_chunks + c], RUN_CHUNK),
                _rows(buf_ref.at[slot], c * RUN_CHUNK, RUN_CHUNK),
                sem.at[slot]).start()
            return carry

        lax.fori_loop(0, nchunk_sm[tile], per_chunk, 0)

    def wait_chunks(tile):
        slot = tile % 2
        n = nchunk_sm[tile]
        for bit in range(max_chunks.bit_length()):
            @pl.when((lax.shift_right_logical(n, bit) & 1) == 1)
            def _():
                rows = RUN_CHUNK << bit
                pltpu.make_async_copy(_rows(ys_ref, 0, rows),
                                      _rows(buf_ref.at[slot], 0, rows),
                                      sem.at[slot]).wait()

    @pl.when(step == 0)
    def _():
        start_chunks(step)

    @pl.when(step + 1 < nsteps)
    def _():
        start_chunks(step + 1)

    wait_chunks(step)

    base = step * tb
    tile_buf = buf_ref.at[step % 2]

    def move(r, carry):
        for g_ref, local_sm in ((g0_ref, local0_sm), (g1_ref, local1_sm)):
            _rows(g_ref, r)[...] = _rows(tile_buf, local_sm[base + r])[...]
        return carry

    lax.fori_loop(0, tb, move, 0, unroll=8)
    meta = meta_ref[...]
    moe = (_load_row_tiles(g0_ref) * meta[:, 2:3]
           + _load_row_tiles(g1_ref) * meta[:, 3:4])
    out_ref[...] = x1_ref[...] + moe


def _combine(local0, local1, tables, x1, meta, ys):
    t, d_model = x1.shape
    tb = TB_RANK
    max_chunks = tables["max_chunks"]
    kern = functools.partial(_combine_kernel, tb=tb, max_chunks=max_chunks)
    buf_rows = max_chunks * RUN_CHUNK
    return pl.pallas_call(
        kern,
        grid_spec=pltpu.PrefetchScalarGridSpec(
            num_scalar_prefetch=4,
            grid=(t // tb,),
            in_specs=[pl.BlockSpec((tb, d_model), lambda i, *_: (i, 0)),
                      pl.BlockSpec((tb, LANES), lambda i, *_: (i, 0)),
                      pl.BlockSpec(memory_space=pl.ANY)],
            out_specs=pl.BlockSpec((tb, d_model), lambda i, *_: (i, 0)),
            scratch_shapes=[pltpu.VMEM((2,) + _row_tiles((buf_rows, d_model)), F32),
                            pltpu.VMEM(_row_tiles((tb, d_model)), F32),
                            pltpu.VMEM(_row_tiles((tb, d_model)), F32),
                            pltpu.SemaphoreType.DMA((2,))]),
        out_shape=jax.ShapeDtypeStruct((t, d_model), F32),
        compiler_params=_cparams(("arbitrary",)),
        name="combine",
    )(local0, local1, tables["chunk_src"], tables["tile_chunks"], x1, meta, ys)


def _layer(h2, *, batch, seq, norm1_g, w_in, conv_w, q_norm_g, k_norm_g,
           w_conv_out, w_attn_out, w_o, norm2_g, w_group, w_router, w1, w3, w2):
    t, d_model = h2.shape
    c, sa, sb = _rope_tables(seq)
    scale = HEAD_DIM ** -0.5 * LOG2_E
    tables_q = tuple(jnp.asarray(tab * np.float32(scale)) for tab in (c, sa, sb))
    tables_k = tuple(jnp.asarray(tab) for tab in (c, sa, sb))

    cb, z, q, k, v, sgc, sga = _inproj(
        h2, norm1_g[None, :], w_in.astype(BF16), q_norm_g[None, :], k_norm_g[None, :],
        tables_q, tables_k, seq=seq)
    o = _attention(q, k, v, batch=batch, seq=seq)

    n_route = N_GROUPS + N_EXPERTS
    wr = jnp.concatenate(
        [w_group, w_router, jnp.zeros((d_model, LANES - n_route), F32)], axis=1)
    wr_hi = wr.astype(BF16)
    wr = jnp.concatenate([wr_hi, (wr - wr_hi.astype(F32)).astype(BF16)], axis=1)
    x1, u2, meta, cnt = _post(h2, cb, z, o, sgc, sga, conv_w,
                              w_conv_out.astype(BF16), w_attn_out.astype(BF16),
                              w_o.astype(BF16), norm2_g[None, :], wr, seq=seq)

    blk = MOE_ROWS
    local0, local1, runs = _rank(meta, cnt, blk=blk)
    local0, local1 = local0.reshape(-1), local1.reshape(-1)
    runs = runs.reshape(-1, 8, LANES)
    tables = _tile_tables(runs[:, 0, :N_EXPERTS], runs[:, 1, :N_EXPERTS], tb=TB_RANK)
    p_rows = t * TOP_K + N_EXPERTS * blk
    sched = _expert_schedule(cnt[0, :N_EXPERTS].astype(jnp.int32), blk=blk,
                             nb=p_rows // blk)

    xs = _dispatch(local0, local1, tables, sched["pad_start"], sched["pad_len"],
                   sched["nused"], u2, p_rows=p_rows, blk=blk)
    ys = _experts(sched, sched["nused"], xs, w1, w3, w2, blk=blk)
    return _combine(local0, local1, tables, x1, meta, ys)


def kernel(x, norm1_g, w_in, conv_w, q_norm_g, k_norm_g, w_conv_out, w_attn_out, w_o,
           norm2_g, w_group, w_router, w1, w3, w2):
    batch, seq, d_model = x.shape
    h2 = x.reshape(batch * seq, d_model)
    for l in range(norm1_g.shape[0]):
        h2 = _layer(h2, batch=batch, seq=seq, norm1_g=norm1_g[l], w_in=w_in[l],
                    conv_w=conv_w[l], q_norm_g=q_norm_g[l], k_norm_g=k_norm_g[l],
                    w_conv_out=w_conv_out[l], w_attn_out=w_attn_out[l], w_o=w_o[l],
                    norm2_g=norm2_g[l], w_group=w_group[l], w_router=w_router[l],
                    w1=w1[l], w3=w3[l], w2=w2[l])
    return h2.reshape(batch, seq, d_model)
```

```python
import functools

import jax
import jax.numpy as jnp
import numpy as np
from jax import lax
from jax.experimental import pallas as pl
from jax.experimental.pallas import tpu as pltpu

F32 = jnp.float32
BF16 = jnp.bfloat16

GRID_W = 64
EPS = 1e-6
N_HEADS = 8
N_KV_HEADS = 2
HEAD_DIM = 128
ROPE_THETA = 10000.0
N_GROUPS = 8
EXPERTS_PER_GROUP = 8
N_EXPERTS = N_GROUPS * EXPERTS_PER_GROUP
TOP_K = 2
LOG2_E = 1.4426950408889634

LANES = 128
MXU_DIM = 256
BF16_SUBLANES = 16
V7X_VMEM_LIMIT_BYTES = 56000 * 1024

TM_PROJ = 512
TQ_ATTN = 512
TK_ATTN = 512
TB_RANK = 1024
RUN_CHUNK = 8
MOE_ROWS = 256
WEIGHT_RING = 3
WEIGHT_DMA_PRIORITY = 1
DIGIT = 256.0


def _cparams(sem):
    return pltpu.CompilerParams(dimension_semantics=sem,
                                vmem_limit_bytes=V7X_VMEM_LIMIT_BYTES)


ROW_TILE = 8


def _row_tiles(shape2d):
    rows, width = shape2d
    assert width == ROW_TILE * LANES
    return (rows * ROW_TILE, LANES)


def _rows(ref, r, n=1):
    return ref.at[pl.ds(pl.multiple_of(r * ROW_TILE, ROW_TILE), n * ROW_TILE)]


def _load_row_tiles(ref):
    rows = ref.shape[0] // ROW_TILE
    return jnp.concatenate(
        [ref[pl.ds(s, rows, stride=ROW_TILE), :] for s in range(ROW_TILE)], axis=1)


def _store_row_tiles(ref, value):
    rows = value.shape[0]
    for s in range(ROW_TILE):
        ref[pl.ds(s, rows, stride=ROW_TILE), :] = value[:, s * LANES:(s + 1) * LANES]


def _resident(shape):
    nd = len(shape)
    return pl.BlockSpec(shape, lambda *_: (0,) * nd, pipeline_mode=pl.Buffered(1))


def _lane_prefix(row):
    r = lax.broadcasted_iota(jnp.int32, (LANES, LANES), 0)
    col = lax.broadcasted_iota(jnp.int32, (LANES, LANES), 1)
    upper = jnp.where(r < col, 1.0, 0.0).astype(BF16)
    return jnp.dot(jnp.broadcast_to(row, (8, LANES)).astype(BF16), upper,
                   preferred_element_type=F32)[0:1, :]


def _head_norm_rope(xh, g, c, sa, sb):
    ms = jnp.mean(xh * xh, axis=-1, keepdims=True)
    y = xh * lax.rsqrt(ms + EPS) * g
    y_next = pltpu.roll(y, HEAD_DIM - 1, axis=1)
    y_prev = pltpu.roll(y, 1, axis=1)
    return y * c + y_next * sa + y_prev * sb


def _sigmoid(x):
    return 0.5 * jnp.tanh(0.5 * x) + 0.5


def _inproj_kernel(x_ref, g1_ref, w_ref, gq_ref, gk_ref,
                   cq_ref, saq_ref, sbq_ref, ck_ref, sak_ref, sbk_ref,
                   cb_ref, z_ref, q_ref, k_ref, v_ref, sgc_ref, sga_ref,
                   *, d_conv, d_q, d_kv, d_model):
    x = x_ref[...]
    ms = jnp.mean(x * x, axis=-1, keepdims=True)
    u = (x * lax.rsqrt(ms + EPS) * g1_ref[...]).astype(BF16)

    def proj(lo, width):
        return jnp.dot(u, w_ref[:, lo:lo + width], preferred_element_type=F32)

    o_cb, o_cc, o_cx = 0, d_conv, 2 * d_conv
    o_q = 3 * d_conv
    o_k = o_q + d_q
    o_v = o_k + d_kv
    o_gc = o_v + d_kv
    o_ga = o_gc + d_model

    sgc_ref[...] = _sigmoid(proj(o_gc, d_model)).astype(BF16)
    sga_ref[...] = _sigmoid(proj(o_ga, d_model)).astype(BF16)

    q = proj(o_q, d_q)
    gq = gq_ref[...]
    cq, saq, sbq = cq_ref[...], saq_ref[...], sbq_ref[...]
    for h in range(d_q // HEAD_DIM):
        sl = slice(h * HEAD_DIM, (h + 1) * HEAD_DIM)
        q_ref[:, sl] = _head_norm_rope(q[:, sl], gq, cq, saq, sbq).astype(BF16)

    k = proj(o_k, d_kv)
    gk = gk_ref[...]
    ck, sak, sbk = ck_ref[...], sak_ref[...], sbk_ref[...]
    for h in range(d_kv // HEAD_DIM):
        sl = slice(h * HEAD_DIM, (h + 1) * HEAD_DIM)
        k_ref[:, sl] = _head_norm_rope(k[:, sl], gk, ck, sak, sbk).astype(BF16)

    z_ref[...] = (proj(o_cc, d_conv) * proj(o_cx, d_conv)).astype(BF16)
    v_ref[...] = proj(o_v, d_kv).astype(BF16)
    cb_ref[...] = proj(o_cb, d_conv).astype(BF16)


def _rope_tables(seq):
    rows = seq // GRID_W
    axis_dim = HEAD_DIM // 2
    row = np.repeat(np.arange(rows, dtype=np.float32), GRID_W)
    col = np.tile(np.arange(GRID_W, dtype=np.float32), rows)
    inv = (np.float32(ROPE_THETA)
           ** (-np.arange(0, axis_dim, 2, dtype=np.float32) / np.float32(axis_dim)))
    ang = np.concatenate([row[:, None] * inv, col[:, None] * inv], axis=-1)
    ang = ang.astype(np.float32)
    cos, sin = np.cos(ang), np.sin(ang)
    zero = np.zeros_like(sin)
    c = np.repeat(cos, 2, axis=-1)
    sa = np.stack([-sin, zero], axis=-1).reshape(seq, HEAD_DIM)
    sb = np.stack([zero, sin], axis=-1).reshape(seq, HEAD_DIM)
    return c, sa, sb


def _inproj(x2, g1, w_in_bf, gq, gk, tables_q, tables_k, *, seq):
    t, d_model = x2.shape
    d_q = N_HEADS * HEAD_DIM
    d_kv = N_KV_HEADS * HEAD_DIM
    d_in = w_in_bf.shape[1]
    d_conv = (d_in - d_q - 2 * d_kv - 2 * d_model) // 3
    tm = TM_PROJ
    nseq = seq // tm

    def row(width):
        return pl.BlockSpec((tm, width), lambda i: (i, 0))

    table = pl.BlockSpec((tm, HEAD_DIM), lambda i: (i % nseq, 0))
    kern = functools.partial(_inproj_kernel, d_conv=d_conv, d_q=d_q, d_kv=d_kv,
                             d_model=d_model)
    out_shape = [jax.ShapeDtypeStruct((t, w), BF16)
                 for w in (d_conv, d_conv, d_q, d_kv, d_kv, d_model, d_model)]
    return pl.pallas_call(
        kern,
        grid=(t // tm,),
        in_specs=[row(d_model), _resident((1, d_model)), _resident((d_model, d_in)),
                  _resident((1, HEAD_DIM)), _resident((1, HEAD_DIM)),
                  table, table, table, table, table, table],
        out_specs=[row(d_conv), row(d_conv), row(d_q), row(d_kv), row(d_kv),
                   row(d_model), row(d_model)],
        out_shape=out_shape,
        compiler_params=_cparams(("arbitrary",)),
        name="inproj",
    )(x2, g1, w_in_bf, gq, gk, *tables_q, *tables_k)


def _attn_kernel(q_ref, k_ref, v_ref, o_ref, qs_ref, vext_ref, m_ref, acc_ref,
                 *, tq, chunks, group):
    @pl.when(pl.program_id(2) == 0)
    def _():
        vext_ref[:, :HEAD_DIM] = v_ref[...]
        vext_ref[:, HEAD_DIM:] = jnp.ones((vext_ref.shape[0], HEAD_DIM), BF16)

    for g in range(group):
        qs_ref[g * tq:(g + 1) * tq, :] = q_ref[:, g * HEAD_DIM:(g + 1) * HEAD_DIM]
    m_ref[...] = jnp.full(m_ref.shape, -jnp.inf, F32)
    acc_ref[...] = jnp.zeros(acc_ref.shape, F32)

    lo = 0
    for tk in chunks:
        keys = slice(lo, lo + tk)
        lo += tk
        s = lax.dot_general(qs_ref[...], k_ref[keys, :], (((1,), (1,)), ((), ())),
                            preferred_element_type=F32)
        m_prev = m_ref[...]
        m_new = jnp.maximum(m_prev, jnp.max(s, axis=-1, keepdims=True))
        alpha = jnp.exp2(m_prev - m_new)
        p = jnp.concatenate(
            [jnp.exp2(s[:, c * LANES:(c + 1) * LANES] - m_new) for c in range(tk // LANES)],
            axis=1).astype(BF16)
        pv = jnp.dot(p, vext_ref[keys, :], preferred_element_type=F32)
        acc_ref[...] = jnp.concatenate([alpha, alpha], axis=1) * acc_ref[...] + pv
        m_ref[...] = m_new

    out = acc_ref[:, :HEAD_DIM] / acc_ref[:, HEAD_DIM:]
    for g in range(group):
        o_ref[:, g * HEAD_DIM:(g + 1) * HEAD_DIM] = out[g * tq:(g + 1) * tq].astype(BF16)


def _attention(q, k, v, *, batch, seq):
    t = q.shape[0]
    group = N_HEADS // N_KV_HEADS
    tq, tk = TQ_ATTN, TK_ATTN
    nq = seq // tq
    gw = group * HEAD_DIM
    chunks = (tk // 2,) + (tk,) * (seq // tk - 1) + (tk // 2,)
    kern = functools.partial(_attn_kernel, tq=tq, chunks=chunks, group=group)
    return pl.pallas_call(
        kern,
        grid=(batch, N_KV_HEADS, nq),
        in_specs=[pl.BlockSpec((tq, gw), lambda b, h, i: (b * nq + i, h)),
                  pl.BlockSpec((seq, HEAD_DIM), lambda b, h, i: (b, h)),
                  pl.BlockSpec((seq, HEAD_DIM), lambda b, h, i: (b, h))],
        out_specs=pl.BlockSpec((tq, gw), lambda b, h, i: (b * nq + i, h)),
        out_shape=jax.ShapeDtypeStruct((t, N_HEADS * HEAD_DIM), BF16),
        scratch_shapes=[pltpu.VMEM((group * tq, HEAD_DIM), BF16),
                        pltpu.VMEM((seq, 2 * HEAD_DIM), BF16),
                        pltpu.VMEM((group * tq, LANES), F32),
                        pltpu.VMEM((group * tq, 2 * HEAD_DIM), F32)],
        compiler_params=_cparams(("arbitrary", "arbitrary", "arbitrary")),
        name="attention",
    )(q, k, v)


def _route(logits):
    rows = logits.shape[0]
    lane = lax.broadcasted_iota(jnp.int32, (rows, LANES), 1).astype(F32)
    neg = -jnp.inf
    big = float(2 * LANES)
    is_group = lane < N_GROUPS
    gl = jnp.where(is_group, logits, neg)
    gmax = jnp.max(gl, axis=-1, keepdims=True)
    gidx = jnp.min(jnp.where(gl == gmax, lane, big), axis=-1, keepdims=True)
    gsum = jnp.sum(jnp.where(is_group, jnp.exp(logits - gmax), 0.0), axis=-1,
                   keepdims=True)
    pg = 1.0 / gsum
    lane_group = jnp.floor(lane * (1.0 / EXPERTS_PER_GROUP)) - 1.0
    mine = (lane_group == gidx) & (lane >= N_GROUPS) & (lane < N_GROUPS + N_EXPERTS)
    sel = jnp.where(mine, logits, neg)
    v1 = jnp.max(sel, axis=-1, keepdims=True)
    i1 = jnp.min(jnp.where(sel == v1, lane, big), axis=-1, keepdims=True)
    sel2 = jnp.where(lane == i1, neg, sel)
    v2 = jnp.max(sel2, axis=-1, keepdims=True)
    i2 = jnp.min(jnp.where(sel2 == v2, lane, big), axis=-1, keepdims=True)
    t2 = jnp.exp(v2 - v1)
    den = 1.0 + t2
    wgt1 = pg * (1.0 / den)
    wgt2 = pg * (t2 / den)
    e1 = i1 - N_GROUPS
    e2 = i2 - N_GROUPS
    meta = jnp.where(lane == 0, e1,
                     jnp.where(lane == 1, e2,
                               jnp.where(lane == 2, wgt1,
                                         jnp.where(lane == 3, wgt2, 0.0))))
    picked = jnp.where((lane == e1) | (lane == e2), 1.0, 0.0)
    return meta, jnp.sum(picked, axis=0, keepdims=True)


def _post_kernel(x_ref, cb_ref, z_ref, zprev_ref, znext_ref, o_ref, sgc_ref, sga_ref,
                 cw_ref, wc_ref, wa_ref, wo_ref, g2_ref, wr_ref,
                 x1_ref, u2_ref, meta_ref, cnt_ref, *, tm, nseq):
    i = pl.program_id(0)
    at_start = (i % nseq) == 0
    at_end = (i % nseq) == nseq - 1
    rowid = lax.broadcasted_iota(jnp.int32, (tm, 1), 0)

    @pl.when(i == 0)
    def _():
        cnt_ref[...] = jnp.zeros(cnt_ref.shape, F32)

    y_attn = jnp.dot(o_ref[...], wa_ref[...], preferred_element_type=F32)

    y_conv = None
    for c in range(z_ref.shape[1] // MXU_DIM):
        sl = slice(c * MXU_DIM, (c + 1) * MXU_DIM)
        z = z_ref[:, sl].astype(F32)
        prev_row = zprev_ref[BF16_SUBLANES - 1:BF16_SUBLANES, sl].astype(F32)
        next_row = znext_ref[0:1, sl].astype(F32)
        prev_row = jnp.where(at_start, 0.0, prev_row)
        next_row = jnp.where(at_end, 0.0, next_row)
        zp = jnp.where(rowid == 0, prev_row, pltpu.roll(z, 1, axis=0))
        zn = jnp.where(rowid == tm - 1, next_row, pltpu.roll(z, tm - 1, axis=0))
        conv = cw_ref[0:1, sl] * zp + cw_ref[1:2, sl] * z + cw_ref[2:3, sl] * zn
        cbz = (cb_ref[:, sl].astype(F32) * conv).astype(BF16)
        part = jnp.dot(cbz, wc_ref[sl, :], preferred_element_type=F32)
        y_conv = part if y_conv is None else y_conv + part
    merged = (sgc_ref[...].astype(F32) * y_conv
              + sga_ref[...].astype(F32) * y_attn).astype(BF16)
    x1 = x_ref[...] + jnp.dot(merged, wo_ref[...], preferred_element_type=F32)
    x1_ref[...] = x1

    ms = jnp.mean(x1 * x1, axis=-1, keepdims=True)
    u2 = x1 * lax.rsqrt(ms + EPS) * g2_ref[...]
    _store_row_tiles(u2_ref, u2)

    u2_hi = u2.astype(BF16)
    u2_lo = (u2 - u2_hi.astype(F32)).astype(BF16)
    hi_part = jnp.dot(u2_hi, wr_ref[...], preferred_element_type=F32)
    lo_part = jnp.dot(u2_lo, wr_ref[:, :LANES], preferred_element_type=F32)
    logits = hi_part[:, :LANES] + (hi_part[:, LANES:] + lo_part)
    meta, picked = _route(logits)
    meta_ref[...] = meta
    cnt_ref[...] += picked


def _post(x2, cb, z, o, sgc, sga, conv_w, wc, wa, wo, g2, wr, *, seq):
    t, d_model = x2.shape
    tm = TM_PROJ
    nseq = seq // tm
    hb = tm // BF16_SUBLANES
    nhalo = t // BF16_SUBLANES
    d_conv = cb.shape[1]
    d_q = o.shape[1]

    def row(width):
        return pl.BlockSpec((tm, width), lambda i: (i, 0))

    kern = functools.partial(_post_kernel, tm=tm, nseq=nseq)
    return pl.pallas_call(
        kern,
        grid=(t // tm,),
        in_specs=[row(d_model), row(d_conv), row(d_conv),
                  pl.BlockSpec((BF16_SUBLANES, d_conv),
                               lambda i: (jnp.maximum(i * hb - 1, 0), 0)),
                  pl.BlockSpec((BF16_SUBLANES, d_conv),
                               lambda i: (jnp.minimum((i + 1) * hb, nhalo - 1), 0)),
                  row(d_q), row(d_model), row(d_model),
                  _resident(conv_w.shape), _resident(wc.shape), _resident(wa.shape),
                  _resident(wo.shape), _resident(g2.shape), _resident(wr.shape)],
        out_specs=[row(d_model),
                   pl.BlockSpec(_row_tiles((tm, d_model)), lambda i: (i, 0)),
                   row(LANES),
                   pl.BlockSpec((1, LANES), lambda i: (0, 0))],
        out_shape=[jax.ShapeDtypeStruct((t, d_model), F32),
                   jax.ShapeDtypeStruct(_row_tiles((t, d_model)), F32),
                   jax.ShapeDtypeStruct((t, LANES), F32),
                   jax.ShapeDtypeStruct((1, LANES), F32)],
        compiler_params=_cparams(("arbitrary",)),
        name="post",
    )(x2, cb, z, z, z, o, sgc, sga, conv_w, wc, wa, wo, g2, wr)


def _rank_kernel(meta_ref, cnt_ref, l0_ref, l1_ref, runs_ref,
                 carry_ref, pstart_ref, lower_ref, eye_ref, *, tb, blk):
    i = pl.program_id(0)
    lane = lax.broadcasted_iota(jnp.int32, (tb, LANES), 1).astype(F32)
    meta = meta_ref[...]
    oh1 = jnp.where(lane == meta[:, 0:1], 1.0, 0.0)
    oh2 = jnp.where(lane == meta[:, 1:2], 1.0, 0.0)
    c = oh1 + oh2

    @pl.when(i == 0)
    def _():
        nblk = jnp.floor((cnt_ref[...] + (blk - 1)) * (1.0 / blk))
        pstart_ref[...] = _lane_prefix(nblk) * blk
        carry_ref[...] = jnp.zeros(carry_ref.shape, F32)
        r = lax.broadcasted_iota(jnp.int32, (tb, tb), 0)
        col = lax.broadcasted_iota(jnp.int32, (tb, tb), 1)
        lower_ref[...] = jnp.where(col < r, 1.0, 0.0).astype(BF16)
        eye_ref[...] = jnp.where(col == r, 1.0, 0.0).astype(BF16)

    prefix = jnp.dot(lower_ref[...], c.astype(BF16), preferred_element_type=F32)
    run_start = carry_ref[...] + pstart_ref[...]
    run_len = jnp.sum(c, axis=0, keepdims=True)

    nchunk = jnp.floor((run_len + (RUN_CHUNK - 1)) * (1.0 / RUN_CHUNK))
    local = prefix + _lane_prefix(nchunk) * RUN_CHUNK

    digits = jnp.zeros((tb, LANES), F32)
    for j, onehot in enumerate((oh1, oh2)):
        value = jnp.sum(onehot * local, axis=-1, keepdims=True)
        high = jnp.floor(value * (1.0 / DIGIT))
        digits = jnp.where(lane == 2 * j, high,
                           jnp.where(lane == 2 * j + 1, value - high * DIGIT, digits))
    rows = lax.dot_general(digits.astype(BF16), eye_ref[...], (((0,), (0,)), ((), ())),
                           preferred_element_type=F32)
    for j, out_ref in enumerate((l0_ref, l1_ref)):
        out_ref[...] = (rows[2 * j:2 * j + 1, :] * DIGIT
                        + rows[2 * j + 1:2 * j + 2, :]).astype(jnp.int32)
    row = lax.broadcasted_iota(jnp.int32, (8, LANES), 0)
    runs = jnp.where(row == 0, run_start, jnp.where(row == 1, run_len, 0.0))
    runs_ref[...] = runs.astype(jnp.int32)
    carry_ref[...] += run_len


def _rank(meta, cnt, *, blk):
    t = meta.shape[0]
    tb = TB_RANK
    kern = functools.partial(_rank_kernel, tb=tb, blk=blk)
    per_token = pl.BlockSpec((1, tb), lambda i: (0, i))
    return pl.pallas_call(
        kern,
        grid=(t // tb,),
        in_specs=[pl.BlockSpec((tb, LANES), lambda i: (i, 0)),
                  pl.BlockSpec((1, LANES), lambda i: (0, 0))],
        out_specs=[per_token] * 2 + [pl.BlockSpec((8, LANES), lambda i: (i, 0))],
        out_shape=[jax.ShapeDtypeStruct((1, t), jnp.int32)] * 2
                  + [jax.ShapeDtypeStruct((t // tb * 8, LANES), jnp.int32)],
        scratch_shapes=[pltpu.VMEM((1, LANES), F32), pltpu.VMEM((1, LANES), F32),
                        pltpu.VMEM((tb, tb), BF16), pltpu.VMEM((tb, tb), BF16)],
        compiler_params=_cparams(("arbitrary",)),
        name="rank",
    )(meta, cnt)


def _dispatch_kernel(local0_sm, local1_sm, full_src_sm, full_dst_sm, nfull_sm,
                     rem_src_sm, rem_dst_sm, rem_len_sm, pad_start_sm, pad_len_sm, nused_sm,
                     u2_ref, xs_ref, buf_ref, zero_ref, sem, zsem,
                     *, tb, blk, nblocks, max_full):
    step = pl.program_id(0)
    nsteps = pl.num_programs(0)
    base = step * tb
    nbits = blk.bit_length() - 1
    slot = step % 2
    tile_buf = buf_ref.at[slot]

    def wait_tile(s):
        pltpu.make_async_copy(_rows(buf_ref.at[s], 0, tb * TOP_K),
                              _rows(xs_ref, 0, tb * TOP_K), sem.at[s]).wait()

    def pad_copies(visit):
        def per_expert(e, carry):
            start, length = pad_start_sm[e], pad_len_sm[e]
            for bit in range(nbits):
                size = 1 << bit
                higher = lax.shift_left(lax.shift_right_logical(length, bit + 1), bit + 1)

                @pl.when((lax.shift_right_logical(length, bit) & 1) == 1)
                def _():
                    visit(pltpu.make_async_copy(
                        zero_ref.at[pl.ds(0, size * ROW_TILE)],
                        _rows(xs_ref, start + higher, size), zsem))
            return carry

        lax.fori_loop(0, N_EXPERTS, per_expert, 0)

        def per_block(b, carry):
            visit(pltpu.make_async_copy(zero_ref, _rows(xs_ref, b * blk, blk), zsem))
            return carry

        lax.fori_loop(nused_sm[0], nblocks, per_block, 0)

    @pl.when(step == 0)
    def _():
        zero_ref[...] = jnp.zeros(zero_ref.shape, F32)
        pad_copies(lambda c: c.start())

    @pl.when(step >= 2)
    def _():
        wait_tile(slot)

    def move(r, carry):
        row = _rows(u2_ref, r)[...]
        _rows(tile_buf, local0_sm[base + r])[...] = row
        _rows(tile_buf, local1_sm[base + r])[...] = row
        return carry

    lax.fori_loop(0, tb, move, 0, unroll=8)

    def full_chunk(c, carry):
        pltpu.make_async_copy(
            _rows(tile_buf, full_src_sm[step * max_full + c], RUN_CHUNK),
            _rows(xs_ref, full_dst_sm[step * max_full + c], RUN_CHUNK),
            sem.at[slot]).start()
        return carry

    lax.fori_loop(0, nfull_sm[step], full_chunk, 0)

    def remainder(e, carry):
        idx = step * N_EXPERTS + e
        src, dst, length = rem_src_sm[idx], rem_dst_sm[idx], rem_len_sm[idx]
        for bit in range(RUN_CHUNK.bit_length() - 1):
            size = 1 << bit
            higher = lax.shift_left(lax.shift_right_logical(length, bit + 1), bit + 1)

            @pl.when((lax.shift_right_logical(length, bit) & 1) == 1)
            def _():
                pltpu.make_async_copy(_rows(tile_buf, src + higher, size),
                                      _rows(xs_ref, dst + higher, size),
                                      sem.at[slot]).start()
        return carry

    lax.fori_loop(0, N_EXPERTS, remainder, 0)

    @pl.when(step == nsteps - 1)
    def _():
        @pl.when(step >= 1)
        def _():
            wait_tile(1 - slot)

        wait_tile(slot)
        pad_copies(lambda c: c.wait())


def _tile_tables(run_start, run_len, *, tb):
    shift = RUN_CHUNK.bit_length() - 1
    nchunk = (run_len + RUN_CHUNK - 1) >> shift
    cum = jnp.cumsum(nchunk, axis=1)
    first = cum - nchunk
    local_start = first * RUN_CHUNK

    def flat_list(count, max_count, value_at):
        ccum = jnp.cumsum(count, axis=1)
        cfirst = (ccum - count)[:, None, :]
        c_idx = jnp.arange(max_count, dtype=jnp.int32)[None, :, None]
        owns = (cfirst <= c_idx) & (c_idx < ccum[:, None, :])
        return [jnp.sum(jnp.where(owns, v[:, None, :] + (c_idx - cfirst) * RUN_CHUNK, 0),
                        axis=2).astype(jnp.int32) for v in value_at], ccum[:, -1]

    max_chunks = tb * TOP_K // RUN_CHUNK + N_EXPERTS
    (chunk_src,), tile_chunks = flat_list(nchunk, max_chunks, [run_start])
    max_full = tb * TOP_K // RUN_CHUNK
    nfull = run_len >> shift
    (full_src, full_dst), tile_full = flat_list(nfull, max_full, [local_start, run_start])
    whole = nfull * RUN_CHUNK
    return dict(
        max_chunks=max_chunks, chunk_src=chunk_src.reshape(-1),
        tile_chunks=tile_chunks.astype(jnp.int32),
        max_full=max_full, full_src=full_src.reshape(-1), full_dst=full_dst.reshape(-1),
        tile_full=tile_full.astype(jnp.int32),
        rem_src=(local_start + whole).reshape(-1).astype(jnp.int32),
        rem_dst=(run_start + whole).reshape(-1).astype(jnp.int32),
        rem_len=(run_len - whole).reshape(-1).astype(jnp.int32))


def _dispatch(local0, local1, tables, pad_start, pad_len, nused, u2, *, p_rows, blk):
    t = u2.shape[0] // ROW_TILE
    tb = TB_RANK
    d_model = ROW_TILE * LANES
    kern = functools.partial(_dispatch_kernel, tb=tb, blk=blk, nblocks=p_rows // blk,
                             max_full=tables["max_full"])
    buf_rows = tables["max_chunks"] * RUN_CHUNK
    return pl.pallas_call(
        kern,
        grid_spec=pltpu.PrefetchScalarGridSpec(
            num_scalar_prefetch=11,
            grid=(t // tb,),
            in_specs=[pl.BlockSpec((tb * ROW_TILE, LANES), lambda i, *_: (i, 0))],
            out_specs=pl.BlockSpec(memory_space=pl.ANY),
            scratch_shapes=[pltpu.VMEM((2,) + _row_tiles((buf_rows, d_model)), F32),
                            pltpu.VMEM((blk * ROW_TILE, LANES), F32),
                            pltpu.SemaphoreType.DMA((2,)), pltpu.SemaphoreType.DMA]),
        out_shape=jax.ShapeDtypeStruct((p_rows * ROW_TILE, LANES), F32),
        compiler_params=_cparams(("arbitrary",)),
        name="dispatch",
    )(local0, local1, tables["full_src"], tables["full_dst"], tables["tile_full"],
      tables["rem_src"], tables["rem_dst"], tables["rem_len"],
      pad_start, pad_len, nused, u2)


def _experts_kernel(be_sm, first_sm, ahead_sm, slot_sm, head_sm, nused_sm,
                    xs_ref, w1_hbm, w3_hbm, w2_hbm, ys_ref,
                    w1f_ref, w3f_ref, w2f_ref, w13b_ref, w2b_ref, wsem, *, d_expert):
    b = pl.program_id(0)
    used = b < nused_sm[0]

    def weight_copies(expert, slot):
        return [pltpu.make_async_copy(src.at[expert], dst.at[slot], wsem.at[slot])
                for src, dst in ((w1_hbm, w1f_ref), (w3_hbm, w3f_ref), (w2_hbm, w2f_ref))]

    @pl.when(b == 0)
    def _():
        for j in range(WEIGHT_RING - 1):
            @pl.when(head_sm[j] >= 0)
            def _():
                for c in weight_copies(head_sm[j], j):
                    c.start(priority=WEIGHT_DMA_PRIORITY)

    @pl.when(first_sm[b] == 1)
    def _():
        slot = slot_sm[b]
        for c in weight_copies(be_sm[b], slot):
            c.wait()

        @pl.when(ahead_sm[b] >= 0)
        def _():
            ahead_slot = lax.rem(slot + (WEIGHT_RING - 1), WEIGHT_RING)
            for c in weight_copies(ahead_sm[b], ahead_slot):
                c.start(priority=WEIGHT_DMA_PRIORITY)

        w13b_ref[:, :d_expert] = w1f_ref[slot].astype(BF16)
        w13b_ref[:, d_expert:] = w3f_ref[slot].astype(BF16)
        w2b_ref[...] = w2f_ref[slot].astype(BF16)

    @pl.when(used)
    def _():
        xb = _load_row_tiles(xs_ref).astype(BF16)
        h = jnp.dot(xb, w13b_ref[...], preferred_element_type=F32)
        h1 = h[:, :d_expert]
        h3 = h[:, d_expert:]
        a = (h1 * _sigmoid(h1) * h3).astype(BF16)
        _store_row_tiles(ys_ref, jnp.dot(a, w2b_ref[...], preferred_element_type=F32))


def _experts(sched, nused, xs, w1, w3, w2, *, blk):
    p_rows = xs.shape[0] // ROW_TILE
    d_model, d_expert = w1.shape[-2:]
    kern = functools.partial(_experts_kernel, d_expert=d_expert)

    def used_block(b, be, first, ahead, slot, head, nu):
        return (jnp.maximum(jnp.minimum(b, nu[0] - 1), 0), 0)

    rows_blk = pl.BlockSpec((blk * ROW_TILE, LANES), used_block)
    hbm = pl.BlockSpec(memory_space=pl.ANY)
    n_prefetch = 6
    return pl.pallas_call(
        kern,
        grid_spec=pltpu.PrefetchScalarGridSpec(
            num_scalar_prefetch=n_prefetch,
            grid=(p_rows // blk,),
            in_specs=[rows_blk, hbm, hbm, hbm],
            out_specs=rows_blk,
            scratch_shapes=[pltpu.VMEM((WEIGHT_RING, d_model, d_expert), F32),
                            pltpu.VMEM((WEIGHT_RING, d_model, d_expert), F32),
                            pltpu.VMEM((WEIGHT_RING, d_expert, d_model), F32),
                            pltpu.VMEM((d_model, 2 * d_expert), BF16),
                            pltpu.VMEM((d_expert, d_model), BF16),
                            pltpu.SemaphoreType.DMA((WEIGHT_RING,))]),
        out_shape=jax.ShapeDtypeStruct((p_rows * ROW_TILE, LANES), F32),
        input_output_aliases={n_prefetch: 0},
        compiler_params=_cparams(("arbitrary",)),
        name="experts",
    )(sched["expert"], sched["first"], sched["ahead"], sched["slot"], sched["head"],
      nused, xs, w1, w3, w2)


def _expert_schedule(counts, *, blk, nb):
    nblk = (counts + blk - 1) // blk
    pend = jnp.cumsum(nblk)
    nused = pend[-1]
    barange = jnp.arange(nb, dtype=jnp.int32)
    bidx = jnp.minimum(barange, nused - 1)
    expert = jnp.minimum(jnp.sum(pend[None, :] <= bidx[:, None], axis=1),
                         N_EXPERTS - 1).astype(jnp.int32)
    live = barange < nused
    first = (live & (barange == (pend - nblk)[expert])).astype(jnp.int32)
    has_rows = nblk > 0
    used_rank = jnp.cumsum(has_rows) - 1
    j_idx = jnp.arange(N_EXPERTS + WEIGHT_RING, dtype=jnp.int32)
    e_idx = jnp.arange(N_EXPERTS, dtype=jnp.int32)
    hit = has_rows[None, :] & (used_rank[None, :] == j_idx[:, None])
    used_list = jnp.sum(jnp.where(hit, e_idx[None, :] + 1, 0), axis=1) - 1
    rank_b = used_rank[expert]
    ahead = jnp.where(live, used_list[rank_b + (WEIGHT_RING - 1)], -1)
    return dict(
        expert=expert, first=first, ahead=ahead.astype(jnp.int32),
        slot=(rank_b % WEIGHT_RING).astype(jnp.int32),
        head=used_list[:WEIGHT_RING - 1].astype(jnp.int32),
        nused=nused.reshape(1).astype(jnp.int32),
        pad_start=((pend - nblk) * blk + counts).astype(jnp.int32),
        pad_len=(nblk * blk - counts).astype(jnp.int32))


def _combine_kernel(local0_sm, local1_sm, chunk_src_sm, nchunk_sm,
                    x1_ref, meta_ref, ys_ref, out_ref,
                    buf_ref, g0_ref, g1_ref, sem, *, tb, max_chunks):
    step = pl.program_id(0)
    nsteps = pl.num_programs(0)

    def start_chunks(tile):
        slot = tile % 2

        def per_chunk(c, carry):
            pltpu.make_async_copy(
                _rows(ys_ref, chunk_src_sm[tile * max_chunks + c], RUN_CHUNK),
                _rows(buf_ref.at[slot], c * RUN_CHUNK, RUN_CHUNK),
                sem.at[slot]).start()
            return carry

        lax.fori_loop(0, nchunk_sm[tile], per_chunk, 0)

    def wait_chunks(tile):
        slot = tile % 2
        n = nchunk_sm[tile]
        for bit in range(max_chunks.bit_length()):
            @pl.when((lax.shift_right_logical(n, bit) & 1) == 1)
            def _():
                rows = RUN_CHUNK << bit
                pltpu.make_async_copy(_rows(ys_ref, 0, rows),
                                      _rows(buf_ref.at[slot], 0, rows),
                                      sem.at[slot]).wait()

    @pl.when(step == 0)
    def _():
        start_chunks(step)

    @pl.when(step + 1 < nsteps)
    def _():
        start_chunks(step + 1)

    wait_chunks(step)

    base = step * tb
    tile_buf = buf_ref.at[step % 2]

    def move(r, carry):
        for g_ref, local_sm in ((g0_ref, local0_sm), (g1_ref, local1_sm)):
            _rows(g_ref, r)[...] = _rows(tile_buf, local_sm[base + r])[...]
        return carry

    lax.fori_loop(0, tb, move, 0, unroll=8)
    meta = meta_ref[...]
    moe = (_load_row_tiles(g0_ref) * meta[:, 2:3]
           + _load_row_tiles(g1_ref) * meta[:, 3:4])
    out_ref[...] = x1_ref[...] + moe


def _combine(local0, local1, tables, x1, meta, ys):
    t, d_model = x1.shape
    tb = TB_RANK
    max_chunks = tables["max_chunks"]
    kern = functools.partial(_combine_kernel, tb=tb, max_chunks=max_chunks)
    buf_rows = max_chunks * RUN_CHUNK
    return pl.pallas_call(
        kern,
        grid_spec=pltpu.PrefetchScalarGridSpec(
            num_scalar_prefetch=4,
            grid=(t // tb,),
            in_specs=[pl.BlockSpec((tb, d_model), lambda i, *_: (i, 0)),
                      pl.BlockSpec((tb, LANES), lambda i, *_: (i, 0)),
                      pl.BlockSpec(memory_space=pl.ANY)],
            out_specs=pl.BlockSpec((tb, d_model), lambda i, *_: (i, 0)),
            scratch_shapes=[pltpu.VMEM((2,) + _row_tiles((buf_rows, d_model)), F32),
                            pltpu.VMEM(_row_tiles((tb, d_model)), F32),
                            pltpu.VMEM(_row_tiles((tb, d_model)), F32),
                            pltpu.SemaphoreType.DMA((2,))]),
        out_shape=jax.ShapeDtypeStruct((t, d_model), F32),
        compiler_params=_cparams(("arbitrary",)),
        name="combine",
    )(local0, local1, tables["chunk_src"], tables["tile_chunks"], x1, meta, ys)


def _layer(h2, *, batch, seq, norm1_g, w_in, conv_w, q_norm_g, k_norm_g,
           w_conv_out, w_attn_out, w_o, norm2_g, w_group, w_router, w1, w3, w2):
    t, d_model = h2.shape
    c, sa, sb = _rope_tables(seq)
    scale = HEAD_DIM ** -0.5 * LOG2_E
    tables_q = tuple(jnp.asarray(tab * np.float32(scale)) for tab in (c, sa, sb))
    tables_k = tuple(jnp.asarray(tab) for tab in (c, sa, sb))

    cb, z, q, k, v, sgc, sga = _inproj(
        h2, norm1_g[None, :], w_in.astype(BF16), q_norm_g[None, :], k_norm_g[None, :],
        tables_q, tables_k, seq=seq)
    o = _attention(q, k, v, batch=batch, seq=seq)

    n_route = N_GROUPS + N_EXPERTS
    wr = jnp.concatenate(
        [w_group, w_router, jnp.zeros((d_model, LANES - n_route), F32)], axis=1)
    wr_hi = wr.astype(BF16)
    wr = jnp.concatenate([wr_hi, (wr - wr_hi.astype(F32)).astype(BF16)], axis=1)
    x1, u2, meta, cnt = _post(h2, cb, z, o, sgc, sga, conv_w,
                              w_conv_out.astype(BF16), w_attn_out.astype(BF16),
                              w_o.astype(BF16), norm2_g[None, :], wr, seq=seq)

    blk = MOE_ROWS
    local0, local1, runs = _rank(meta, cnt, blk=blk)
    local0, local1 = local0.reshape(-1), local1.reshape(-1)
    runs = runs.reshape(-1, 8, LANES)
    tables = _tile_tables(runs[:, 0, :N_EXPERTS], runs[:, 1, :N_EXPERTS], tb=TB_RANK)
    p_rows = t * TOP_K + N_EXPERTS * blk
    sched = _expert_schedule(cnt[0, :N_EXPERTS].astype(jnp.int32), blk=blk,
                             nb=p_rows // blk)

    xs = _dispatch(local0, local1, tables, sched["pad_start"], sched["pad_len"],
                   sched["nused"], u2, p_rows=p_rows, blk=blk)
    ys = _experts(sched, sched["nused"], xs, w1, w3, w2, blk=blk)
    return _combine(local0, local1, tables, x1, meta, ys)


def kernel(x, norm1_g, w_in, conv_w, q_norm_g, k_norm_g, w_conv_out, w_attn_out, w_o,
           norm2_g, w_group, w_router, w1, w3, w2):
    batch, seq, d_model = x.shape
    h2 = x.reshape(batch * seq, d_model)
    for l in range(norm1_g.shape[0]):
        h2 = _layer(h2, batch=batch, seq=seq, norm1_g=norm1_g[l], w_in=w_in[l],
                    conv_w=conv_w[l], q_norm_g=q_norm_g[l], k_norm_g=k_norm_g[l],
                    w_conv_out=w_conv_out[l], w_attn_out=w_attn_out[l], w_o=w_o[l],
                    norm2_g=norm2_g[l], w_group=w_group[l], w_router=w_router[l],
                    w1=w1[l], w3=w3[l], w2=w2[l])
    return h2.reshape(batch, seq, d_model)
```

```python
import functools

import jax
import jax.numpy as jnp
import numpy as np
from jax import lax
from jax.experimental import pallas as pl
from jax.experimental.pallas import tpu as pltpu

F32 = jnp.float32
BF16 = jnp.bfloat16

GRID_W = 64
EPS = 1e-6
N_HEADS = 8
N_KV_HEADS = 2
HEAD_DIM = 128
ROPE_THETA = 10000.0
N_GROUPS = 8
EXPERTS_PER_GROUP = 8
N_EXPERTS = N_GROUPS * EXPERTS_PER_GROUP
TOP_K = 2
LOG2_E = 1.4426950408889634

LANES = 128
MXU_DIM = 256
BF16_SUBLANES = 16
V7X_VMEM_LIMIT_BYTES = 56000 * 1024

TM_PROJ = 512
TQ_ATTN = 512
TK_ATTN = 512
TB_RANK = 1024
RUN_CHUNK = 8
MOE_ROWS = 256
WEIGHT_RING = 3
WEIGHT_DMA_PRIORITY = 1
DIGIT = 256.0


def _cparams(sem):
    return pltpu.CompilerParams(dimension_semantics=sem,
                                vmem_limit_bytes=V7X_VMEM_LIMIT_BYTES)


ROW_TILE = 8


def _row_tiles(shape2d):
    rows, width = shape2d
    assert width == ROW_TILE * LANES
    return (rows * ROW_TILE, LANES)


def _rows(ref, r, n=1):
    return ref.at[pl.ds(pl.multiple_of(r * ROW_TILE, ROW_TILE), n * ROW_TILE)]


def _load_row_tiles(ref):
    rows = ref.shape[0] // ROW_TILE
    return jnp.concatenate(
        [ref[pl.ds(s, rows, stride=ROW_TILE), :] for s in range(ROW_TILE)], axis=1)


def _store_row_tiles(ref, value):
    rows = value.shape[0]
    for s in range(ROW_TILE):
        ref[pl.ds(s, rows, stride=ROW_TILE), :] = value[:, s * LANES:(s + 1) * LANES]


def _resident(shape):
    nd = len(shape)
    return pl.BlockSpec(shape, lambda *_: (0,) * nd, pipeline_mode=pl.Buffered(1))


def _lane_prefix(row):
    r = lax.broadcasted_iota(jnp.int32, (LANES, LANES), 0)
    col = lax.broadcasted_iota(jnp.int32, (LANES, LANES), 1)
    upper = jnp.where(r < col, 1.0, 0.0).astype(BF16)
    return jnp.dot(jnp.broadcast_to(row, (8, LANES)).astype(BF16), upper,
                   preferred_element_type=F32)[0:1, :]


def _head_norm_rope(xh, g, c, sa, sb):
    ms = jnp.mean(xh * xh, axis=-1, keepdims=True)
    y = xh * lax.rsqrt(ms + EPS) * g
    y_next = pltpu.roll(y, HEAD_DIM - 1, axis=1)
    y_prev = pltpu.roll(y, 1, axis=1)
    return y * c + y_next * sa + y_prev * sb


def _rmsnorm(x, g):
    ms = jnp.mean(x * x, axis=-1, keepdims=True)
    return x * lax.rsqrt(ms + EPS) * g


def _sigmoid(x):
    return 0.5 * jnp.tanh(0.5 * x) + 0.5


def _inproj_kernel(x_ref, g1_ref, w_ref, gq_ref, gk_ref,
                   cq_ref, saq_ref, sbq_ref, ck_ref, sak_ref, sbk_ref,
                   cb_ref, z_ref, q_ref, k_ref, v_ref, sgc_ref, sga_ref,
                   *, d_conv, d_q, d_kv, d_model):
    x = x_ref[...]
    ms = jnp.mean(x * x, axis=-1, keepdims=True)
    u = (x * lax.rsqrt(ms + EPS) * g1_ref[...]).astype(BF16)

    def proj(lo, width):
        return jnp.dot(u, w_ref[:, lo:lo + width], preferred_element_type=F32)

    o_cb, o_cc, o_cx = 0, d_conv, 2 * d_conv
    o_q = 3 * d_conv
    o_k = o_q + d_q
    o_v = o_k + d_kv
    o_gc = o_v + d_kv
    o_ga = o_gc + d_model

    sgc_ref[...] = _sigmoid(proj(o_gc, d_model)).astype(BF16)
    sga_ref[...] = _sigmoid(proj(o_ga, d_model)).astype(BF16)

    q = proj(o_q, d_q)
    gq = gq_ref[...]
    cq, saq, sbq = cq_ref[...], saq_ref[...], sbq_ref[...]
    for h in range(d_q // HEAD_DIM):
        sl = slice(h * HEAD_DIM, (h + 1) * HEAD_DIM)
        q_ref[:, sl] = _head_norm_rope(q[:, sl], gq, cq, saq, sbq).astype(BF16)

    k = proj(o_k, d_kv)
    gk = gk_ref[...]
    ck, sak, sbk = ck_ref[...], sak_ref[...], sbk_ref[...]
    for h in range(d_kv // HEAD_DIM):
        sl = slice(h * HEAD_DIM, (h + 1) * HEAD_DIM)
        k_ref[:, sl] = _head_norm_rope(k[:, sl], gk, ck, sak, sbk).astype(BF16)

    z_ref[...] = (proj(o_cc, d_conv) * proj(o_cx, d_conv)).astype(BF16)
    v_ref[...] = proj(o_v, d_kv).astype(BF16)
    cb_ref[...] = proj(o_cb, d_conv).astype(BF16)


def _rope_tables(seq):
    rows = seq // GRID_W
    axis_dim = HEAD_DIM // 2
    row = np.repeat(np.arange(rows, dtype=np.float32), GRID_W)
    col = np.tile(np.arange(GRID_W, dtype=np.float32), rows)
    inv = (np.float32(ROPE_THETA)
           ** (-np.arange(0, axis_dim, 2, dtype=np.float32) / np.float32(axis_dim)))
    ang = np.concatenate([row[:, None] * inv, col[:, None] * inv], axis=-1)
    ang = ang.astype(np.float32)
    cos, sin = np.cos(ang), np.sin(ang)
    zero = np.zeros_like(sin)
    c = np.repeat(cos, 2, axis=-1)
    sa = np.stack([-sin, zero], axis=-1).reshape(seq, HEAD_DIM)
    sb = np.stack([zero, sin], axis=-1).reshape(seq, HEAD_DIM)
    return c, sa, sb


def _inproj(x2, g1, w_in_bf, gq, gk, tables_q, tables_k, *, seq):
    t, d_model = x2.shape
    d_q = N_HEADS * HEAD_DIM
    d_kv = N_KV_HEADS * HEAD_DIM
    d_in = w_in_bf.shape[1]
    d_conv = (d_in - d_q - 2 * d_kv - 2 * d_model) // 3
    tm = TM_PROJ
    nseq = seq // tm

    def row(width):
        return pl.BlockSpec((tm, width), lambda i: (i, 0))

    table = pl.BlockSpec((tm, HEAD_DIM), lambda i: (i % nseq, 0))
    kern = functools.partial(_inproj_kernel, d_conv=d_conv, d_q=d_q, d_kv=d_kv,
                             d_model=d_model)
    out_shape = [jax.ShapeDtypeStruct((t, w), BF16)
                 for w in (d_conv, d_conv, d_q, d_kv, d_kv, d_model, d_model)]
    return pl.pallas_call(
        kern,
        grid=(t // tm,),
        in_specs=[row(d_model), _resident((1, d_model)), _resident((d_model, d_in)),
                  _resident((1, HEAD_DIM)), _resident((1, HEAD_DIM)),
                  table, table, table, table, table, table],
        out_specs=[row(d_conv), row(d_conv), row(d_q), row(d_kv), row(d_kv),
                   row(d_model), row(d_model)],
        out_shape=out_shape,
        compiler_params=_cparams(("arbitrary",)),
        name="inproj",
    )(x2, g1, w_in_bf, gq, gk, *tables_q, *tables_k)


def _attn_kernel(q_ref, k_ref, v_ref, o_ref, qs_ref, vext_ref, m_ref, acc_ref,
                 *, tq, chunks, group):
    @pl.when(pl.program_id(2) == 0)
    def _():
        vext_ref[:, :HEAD_DIM] = v_ref[...]
        vext_ref[:, HEAD_DIM:] = jnp.ones((vext_ref.shape[0], HEAD_DIM), BF16)

    for g in range(group):
        qs_ref[g * tq:(g + 1) * tq, :] = q_ref[:, g * HEAD_DIM:(g + 1) * HEAD_DIM]
    m_ref[...] = jnp.full(m_ref.shape, -jnp.inf, F32)
    acc_ref[...] = jnp.zeros(acc_ref.shape, F32)

    lo = 0
    for tk in chunks:
        keys = slice(lo, lo + tk)
        lo += tk
        s = lax.dot_general(qs_ref[...], k_ref[keys, :], (((1,), (1,)), ((), ())),
                            preferred_element_type=F32)
        m_prev = m_ref[...]
        m_new = jnp.maximum(m_prev, jnp.max(s, axis=-1, keepdims=True))
        alpha = jnp.exp2(m_prev - m_new)
        p = jnp.concatenate(
            [jnp.exp2(s[:, c * LANES:(c + 1) * LANES] - m_new) for c in range(tk // LANES)],
            axis=1).astype(BF16)
        pv = jnp.dot(p, vext_ref[keys, :], preferred_element_type=F32)
        acc_ref[...] = jnp.concatenate([alpha, alpha], axis=1) * acc_ref[...] + pv
        m_ref[...] = m_new

    out = acc_ref[:, :HEAD_DIM] / acc_ref[:, HEAD_DIM:]
    for g in range(group):
        o_ref[:, g * HEAD_DIM:(g + 1) * HEAD_DIM] = out[g * tq:(g + 1) * tq].astype(BF16)


def _attention(q, k, v, *, batch, seq):
    t = q.shape[0]
    group = N_HEADS // N_KV_HEADS
    tq, tk = TQ_ATTN, TK_ATTN
    nq = seq // tq
    gw = group * HEAD_DIM
    chunks = (tk // 2,) + (tk,) * (seq // tk - 1) + (tk // 2,)
    kern = functools.partial(_attn_kernel, tq=tq, chunks=chunks, group=group)
    return pl.pallas_call(
        kern,
        grid=(batch, N_KV_HEADS, nq),
        in_specs=[pl.BlockSpec((tq, gw), lambda b, h, i: (b * nq + i, h)),
                  pl.BlockSpec((seq, HEAD_DIM), lambda b, h, i: (b, h)),
                  pl.BlockSpec((seq, HEAD_DIM), lambda b, h, i: (b, h))],
        out_specs=pl.BlockSpec((tq, gw), lambda b, h, i: (b * nq + i, h)),
        out_shape=jax.ShapeDtypeStruct((t, N_HEADS * HEAD_DIM), BF16),
        scratch_shapes=[pltpu.VMEM((group * tq, HEAD_DIM), BF16),
                        pltpu.VMEM((seq, 2 * HEAD_DIM), BF16),
                        pltpu.VMEM((group * tq, LANES), F32),
                        pltpu.VMEM((group * tq, 2 * HEAD_DIM), F32)],
        compiler_params=_cparams(("arbitrary", "arbitrary", "arbitrary")),
        name="attention",
    )(q, k, v)


def _route(logits):
    rows = logits.shape[0]
    lane = lax.broadcasted_iota(jnp.int32, (rows, LANES), 1).astype(F32)
    neg = -jnp.inf
    big = float(2 * LANES)
    is_group = lane < N_GROUPS
    gl = jnp.where(is_group, logits, neg)
    gmax = jnp.max(gl, axis=-1, keepdims=True)
    gidx = jnp.min(jnp.where(gl == gmax, lane, big), axis=-1, keepdims=True)
    gsum = jnp.sum(jnp.where(is_group, jnp.exp(logits - gmax), 0.0), axis=-1,
                   keepdims=True)
    pg = 1.0 / gsum
    lane_group = jnp.floor(lane * (1.0 / EXPERTS_PER_GROUP)) - 1.0
    mine = (lane_group == gidx) & (lane >= N_GROUPS) & (lane < N_GROUPS + N_EXPERTS)
    sel = jnp.where(mine, logits, neg)
    v1 = jnp.max(sel, axis=-1, keepdims=True)
    i1 = jnp.min(jnp.where(sel == v1, lane, big), axis=-1, keepdims=True)
    sel2 = jnp.where(lane == i1, neg, sel)
    v2 = jnp.max(sel2, axis=-1, keepdims=True)
    i2 = jnp.min(jnp.where(sel2 == v2, lane, big), axis=-1, keepdims=True)
    t2 = jnp.exp(v2 - v1)
    den = 1.0 + t2
    wgt1 = pg * (1.0 / den)
    wgt2 = pg * (t2 / den)
    e1 = i1 - N_GROUPS
    e2 = i2 - N_GROUPS
    meta = jnp.where(lane == 0, e1,
                     jnp.where(lane == 1, e2,
                               jnp.where(lane == 2, wgt1,
                                         jnp.where(lane == 3, wgt2, 0.0))))
    picked = jnp.where((lane == e1) | (lane == e2), 1.0, 0.0)
    return meta, jnp.sum(picked, axis=0, keepdims=True)


def _post_kernel(x_ref, cb_ref, z_ref, zprev_ref, znext_ref, o_ref, sgc_ref, sga_ref,
                 cw_ref, wc_ref, wa_ref, wo_ref, g2_ref, wr_ref,
                 x1_ref, meta_ref, cnt_ref, *, tm, nseq):
    i = pl.program_id(0)
    at_start = (i % nseq) == 0
    at_end = (i % nseq) == nseq - 1
    rowid = lax.broadcasted_iota(jnp.int32, (tm, 1), 0)

    @pl.when(i == 0)
    def _():
        cnt_ref[...] = jnp.zeros(cnt_ref.shape, F32)

    y_attn = jnp.dot(o_ref[...], wa_ref[...], preferred_element_type=F32)

    y_conv = None
    for c in range(z_ref.shape[1] // MXU_DIM):
        sl = slice(c * MXU_DIM, (c + 1) * MXU_DIM)
        z = z_ref[:, sl].astype(F32)
        prev_row = zprev_ref[BF16_SUBLANES - 1:BF16_SUBLANES, sl].astype(F32)
        next_row = znext_ref[0:1, sl].astype(F32)
        prev_row = jnp.where(at_start, 0.0, prev_row)
        next_row = jnp.where(at_end, 0.0, next_row)
        zp = jnp.where(rowid == 0, prev_row, pltpu.roll(z, 1, axis=0))
        zn = jnp.where(rowid == tm - 1, next_row, pltpu.roll(z, tm - 1, axis=0))
        conv = cw_ref[0:1, sl] * zp + cw_ref[1:2, sl] * z + cw_ref[2:3, sl] * zn
        cbz = (cb_ref[:, sl].astype(F32) * conv).astype(BF16)
        part = jnp.dot(cbz, wc_ref[sl, :], preferred_element_type=F32)
        y_conv = part if y_conv is None else y_conv + part
    merged = (sgc_ref[...].astype(F32) * y_conv
              + sga_ref[...].astype(F32) * y_attn).astype(BF16)
    x1 = x_ref[...] + jnp.dot(merged, wo_ref[...], preferred_element_type=F32)
    x1_ref[...] = x1

    u2 = _rmsnorm(x1, g2_ref[...])

    u2_hi = u2.astype(BF16)
    u2_lo = (u2 - u2_hi.astype(F32)).astype(BF16)
    hi_part = jnp.dot(u2_hi, wr_ref[...], preferred_element_type=F32)
    lo_part = jnp.dot(u2_lo, wr_ref[:, :LANES], preferred_element_type=F32)
    logits = hi_part[:, :LANES] + (hi_part[:, LANES:] + lo_part)
    meta, picked = _route(logits)
    meta_ref[...] = meta
    cnt_ref[...] += picked


def _post(x2, cb, z, o, sgc, sga, conv_w, wc, wa, wo, g2, wr, *, seq):
    t, d_model = x2.shape
    tm = TM_PROJ
    nseq = seq // tm
    hb = tm // BF16_SUBLANES
    nhalo = t // BF16_SUBLANES
    d_conv = cb.shape[1]
    d_q = o.shape[1]

    def row(width):
        return pl.BlockSpec((tm, width), lambda i: (i, 0))

    kern = functools.partial(_post_kernel, tm=tm, nseq=nseq)
    return pl.pallas_call(
        kern,
        grid=(t // tm,),
        in_specs=[row(d_model), row(d_conv), row(d_conv),
                  pl.BlockSpec((BF16_SUBLANES, d_conv),
                               lambda i: (jnp.maximum(i * hb - 1, 0), 0)),
                  pl.BlockSpec((BF16_SUBLANES, d_conv),
                               lambda i: (jnp.minimum((i + 1) * hb, nhalo - 1), 0)),
                  row(d_q), row(d_model), row(d_model),
                  _resident(conv_w.shape), _resident(wc.shape), _resident(wa.shape),
                  _resident(wo.shape), _resident(g2.shape), _resident(wr.shape)],
        out_specs=[row(d_model), row(LANES),
                   pl.BlockSpec((1, LANES), lambda i: (0, 0))],
        out_shape=[jax.ShapeDtypeStruct((t, d_model), F32),
                   jax.ShapeDtypeStruct((t, LANES), F32),
                   jax.ShapeDtypeStruct((1, LANES), F32)],
        compiler_params=_cparams(("arbitrary",)),
        name="post",
    )(x2, cb, z, z, z, o, sgc, sga, conv_w, wc, wa, wo, g2, wr)


def _rank_kernel(meta_ref, cnt_ref, l0_ref, l1_ref, runs_ref,
                 carry_ref, pstart_ref, lower_ref, eye_ref, *, tb, blk):
    i = pl.program_id(0)
    lane = lax.broadcasted_iota(jnp.int32, (tb, LANES), 1).astype(F32)
    meta = meta_ref[...]
    oh1 = jnp.where(lane == meta[:, 0:1], 1.0, 0.0)
    oh2 = jnp.where(lane == meta[:, 1:2], 1.0, 0.0)
    c = oh1 + oh2

    @pl.when(i == 0)
    def _():
        nblk = jnp.floor((cnt_ref[...] + (blk - 1)) * (1.0 / blk))
        pstart_ref[...] = _lane_prefix(nblk) * blk
        carry_ref[...] = jnp.zeros(carry_ref.shape, F32)
        r = lax.broadcasted_iota(jnp.int32, (tb, tb), 0)
        col = lax.broadcasted_iota(jnp.int32, (tb, tb), 1)
        lower_ref[...] = jnp.where(col < r, 1.0, 0.0).astype(BF16)
        eye_ref[...] = jnp.where(col == r, 1.0, 0.0).astype(BF16)

    prefix = jnp.dot(lower_ref[...], c.astype(BF16), preferred_element_type=F32)
    run_start = carry_ref[...] + pstart_ref[...]
    run_len = jnp.sum(c, axis=0, keepdims=True)

    nchunk = jnp.floor((run_len + (RUN_CHUNK - 1)) * (1.0 / RUN_CHUNK))
    local = prefix + _lane_prefix(nchunk) * RUN_CHUNK

    digits = jnp.zeros((tb, LANES), F32)
    for j, onehot in enumerate((oh1, oh2)):
        value = jnp.sum(onehot * local, axis=-1, keepdims=True)
        high = jnp.floor(value * (1.0 / DIGIT))
        digits = jnp.where(lane == 2 * j, high,
                           jnp.where(lane == 2 * j + 1, value - high * DIGIT, digits))
    rows = lax.dot_general(digits.astype(BF16), eye_ref[...], (((0,), (0,)), ((), ())),
                           preferred_element_type=F32)
    for j, out_ref in enumerate((l0_ref, l1_ref)):
        out_ref[...] = (rows[2 * j:2 * j + 1, :] * DIGIT
                        + rows[2 * j + 1:2 * j + 2, :]).astype(jnp.int32)
    row = lax.broadcasted_iota(jnp.int32, (8, LANES), 0)
    runs = jnp.where(row == 0, run_start, jnp.where(row == 1, run_len, 0.0))
    runs_ref[...] = runs.astype(jnp.int32)
    carry_ref[...] += run_len


def _rank(meta, cnt, *, blk):
    t = meta.shape[0]
    tb = TB_RANK
    kern = functools.partial(_rank_kernel, tb=tb, blk=blk)
    per_token = pl.BlockSpec((1, tb), lambda i: (0, i))
    return pl.pallas_call(
        kern,
        grid=(t // tb,),
        in_specs=[pl.BlockSpec((tb, LANES), lambda i: (i, 0)),
                  pl.BlockSpec((1, LANES), lambda i: (0, 0))],
        out_specs=[per_token] * 2 + [pl.BlockSpec((8, LANES), lambda i: (i, 0))],
        out_shape=[jax.ShapeDtypeStruct((1, t), jnp.int32)] * 2
                  + [jax.ShapeDtypeStruct((t // tb * 8, LANES), jnp.int32)],
        scratch_shapes=[pltpu.VMEM((1, LANES), F32), pltpu.VMEM((1, LANES), F32),
                        pltpu.VMEM((tb, tb), BF16), pltpu.VMEM((tb, tb), BF16)],
        compiler_params=_cparams(("arbitrary",)),
        name="rank",
    )(meta, cnt)


def _dispatch_kernel(local0_sm, local1_sm, full_src_sm, full_dst_sm, nfull_sm,
                     rem_src_sm, rem_dst_sm, rem_len_sm, pad_start_sm, pad_len_sm, nused_sm,
                     x1_ref, g2_ref, xs_ref, u2_ref, buf_ref, zero_ref, sem, zsem,
                     *, tb, blk, nblocks, max_full):
    step = pl.program_id(0)
    nsteps = pl.num_programs(0)
    base = step * tb
    nbits = blk.bit_length() - 1
    slot = step % 2
    tile_buf = buf_ref.at[slot]

    def wait_tile(s):
        pltpu.make_async_copy(_rows(buf_ref.at[s], 0, tb * TOP_K),
                              _rows(xs_ref, 0, tb * TOP_K), sem.at[s]).wait()

    def pad_copies(visit):
        def per_expert(e, carry):
            start, length = pad_start_sm[e], pad_len_sm[e]
            for bit in range(nbits):
                size = 1 << bit
                higher = lax.shift_left(lax.shift_right_logical(length, bit + 1), bit + 1)

                @pl.when((lax.shift_right_logical(length, bit) & 1) == 1)
                def _():
                    visit(pltpu.make_async_copy(
                        zero_ref.at[pl.ds(0, size * ROW_TILE)],
                        _rows(xs_ref, start + higher, size), zsem))
            return carry

        lax.fori_loop(0, N_EXPERTS, per_expert, 0)

        def per_block(b, carry):
            visit(pltpu.make_async_copy(zero_ref, _rows(xs_ref, b * blk, blk), zsem))
            return carry

        lax.fori_loop(nused_sm[0], nblocks, per_block, 0)

    @pl.when(step == 0)
    def _():
        zero_ref[...] = jnp.zeros(zero_ref.shape, F32)
        pad_copies(lambda c: c.start())

    @pl.when(step >= 2)
    def _():
        wait_tile(slot)

    _store_row_tiles(u2_ref, _rmsnorm(x1_ref[...], g2_ref[...]))

    def move(r, carry):
        row = _rows(u2_ref, r)[...]
        _rows(tile_buf, local0_sm[base + r])[...] = row
        _rows(tile_buf, local1_sm[base + r])[...] = row
        return carry

    lax.fori_loop(0, tb, move, 0, unroll=8)

    def full_chunk(c, carry):
        pltpu.make_async_copy(
            _rows(tile_buf, full_src_sm[step * max_full + c], RUN_CHUNK),
            _rows(xs_ref, full_dst_sm[step * max_full + c], RUN_CHUNK),
            sem.at[slot]).start()
        return carry

    lax.fori_loop(0, nfull_sm[step], full_chunk, 0)

    def remainder(e, carry):
        idx = step * N_EXPERTS + e
        src, dst, length = rem_src_sm[idx], rem_dst_sm[idx], rem_len_sm[idx]
        for bit in range(RUN_CHUNK.bit_length() - 1):
            size = 1 << bit
            higher = lax.shift_left(lax.shift_right_logical(length, bit + 1), bit + 1)

            @pl.when((lax.shift_right_logical(length, bit) & 1) == 1)
            def _():
                pltpu.make_async_copy(_rows(tile_buf, src + higher, size),
                                      _rows(xs_ref, dst + higher, size),
                                      sem.at[slot]).start()
        return carry

    lax.fori_loop(0, N_EXPERTS, remainder, 0)

    @pl.when(step == nsteps - 1)
    def _():
        @pl.when(step >= 1)
        def _():
            wait_tile(1 - slot)

        wait_tile(slot)
        pad_copies(lambda c: c.wait())


def _tile_tables(run_start, run_len, *, tb):
    shift = RUN_CHUNK.bit_length() - 1
    nchunk = (run_len + RUN_CHUNK - 1) >> shift
    cum = jnp.cumsum(nchunk, axis=1)
    first = cum - nchunk
    local_start = first * RUN_CHUNK

    def flat_list(count, max_count, value_at):
        ccum = jnp.cumsum(count, axis=1)
        cfirst = (ccum - count)[:, None, :]
        c_idx = jnp.arange(max_count, dtype=jnp.int32)[None, :, None]
        owns = (cfirst <= c_idx) & (c_idx < ccum[:, None, :])
        return [jnp.sum(jnp.where(owns, v[:, None, :] + (c_idx - cfirst) * RUN_CHUNK, 0),
                        axis=2).astype(jnp.int32) for v in value_at], ccum[:, -1]

    max_chunks = tb * TOP_K // RUN_CHUNK + N_EXPERTS
    (chunk_src,), tile_chunks = flat_list(nchunk, max_chunks, [run_start])
    max_full = tb * TOP_K // RUN_CHUNK
    nfull = run_len >> shift
    (full_src, full_dst), tile_full = flat_list(nfull, max_full, [local_start, run_start])
    whole = nfull * RUN_CHUNK
    return dict(
        max_chunks=max_chunks, chunk_src=chunk_src.reshape(-1),
        tile_chunks=tile_chunks.astype(jnp.int32),
        max_full=max_full, full_src=full_src.reshape(-1), full_dst=full_dst.reshape(-1),
        tile_full=tile_full.astype(jnp.int32),
        rem_src=(local_start + whole).reshape(-1).astype(jnp.int32),
        rem_dst=(run_start + whole).reshape(-1).astype(jnp.int32),
        rem_len=(run_len - whole).reshape(-1).astype(jnp.int32))


def _dispatch(local0, local1, tables, pad_start, pad_len, nused, x1, g2, *, p_rows, blk):
    t, d_model = x1.shape
    tb = TB_RANK
    kern = functools.partial(_dispatch_kernel, tb=tb, blk=blk, nblocks=p_rows // blk,
                             max_full=tables["max_full"])
    buf_rows = tables["max_chunks"] * RUN_CHUNK
    return pl.pallas_call(
        kern,
        grid_spec=pltpu.PrefetchScalarGridSpec(
            num_scalar_prefetch=11,
            grid=(t // tb,),
            in_specs=[pl.BlockSpec((tb, d_model), lambda i, *_: (i, 0)),
                      pl.BlockSpec(g2.shape, lambda i, *_: (0, 0))],
            out_specs=pl.BlockSpec(memory_space=pl.ANY),
            scratch_shapes=[pltpu.VMEM(_row_tiles((tb, d_model)), F32),
                            pltpu.VMEM((2,) + _row_tiles((buf_rows, d_model)), F32),
                            pltpu.VMEM((blk * ROW_TILE, LANES), F32),
                            pltpu.SemaphoreType.DMA((2,)), pltpu.SemaphoreType.DMA]),
        out_shape=jax.ShapeDtypeStruct((p_rows * ROW_TILE, LANES), F32),
        compiler_params=_cparams(("arbitrary",)),
        name="dispatch",
    )(local0, local1, tables["full_src"], tables["full_dst"], tables["tile_full"],
      tables["rem_src"], tables["rem_dst"], tables["rem_len"],
      pad_start, pad_len, nused, x1, g2)


def _experts_kernel(be_sm, first_sm, ahead_sm, slot_sm, head_sm, nused_sm,
                    xs_ref, w1_hbm, w3_hbm, w2_hbm, ys_ref,
                    w1f_ref, w3f_ref, w2f_ref, w13b_ref, w2b_ref, wsem, *, d_expert):
    b = pl.program_id(0)
    used = b < nused_sm[0]

    def weight_copies(expert, slot):
        return [pltpu.make_async_copy(src.at[expert], dst.at[slot], wsem.at[slot])
                for src, dst in ((w1_hbm, w1f_ref), (w3_hbm, w3f_ref), (w2_hbm, w2f_ref))]

    @pl.when(b == 0)
    def _():
        for j in range(WEIGHT_RING - 1):
            @pl.when(head_sm[j] >= 0)
            def _():
                for c in weight_copies(head_sm[j], j):
                    c.start(priority=WEIGHT_DMA_PRIORITY)

    @pl.when(first_sm[b] == 1)
    def _():
        slot = slot_sm[b]
        for c in weight_copies(be_sm[b], slot):
            c.wait()

        @pl.when(ahead_sm[b] >= 0)
        def _():
            ahead_slot = lax.rem(slot + (WEIGHT_RING - 1), WEIGHT_RING)
            for c in weight_copies(ahead_sm[b], ahead_slot):
                c.start(priority=WEIGHT_DMA_PRIORITY)

        w13b_ref[:, :d_expert] = w1f_ref[slot].astype(BF16)
        w13b_ref[:, d_expert:] = w3f_ref[slot].astype(BF16)
        w2b_ref[...] = w2f_ref[slot].astype(BF16)

    @pl.when(used)
    def _():
        xb = _load_row_tiles(xs_ref).astype(BF16)
        h = jnp.dot(xb, w13b_ref[...], preferred_element_type=F32)
        h1 = h[:, :d_expert]
        h3 = h[:, d_expert:]
        a = (h1 * _sigmoid(h1) * h3).astype(BF16)
        _store_row_tiles(ys_ref, jnp.dot(a, w2b_ref[...], preferred_element_type=F32))


def _experts(sched, nused, xs, w1, w3, w2, *, blk):
    p_rows = xs.shape[0] // ROW_TILE
    d_model, d_expert = w1.shape[-2:]
    kern = functools.partial(_experts_kernel, d_expert=d_expert)

    def used_block(b, be, first, ahead, slot, head, nu):
        return (jnp.maximum(jnp.minimum(b, nu[0] - 1), 0), 0)

    rows_blk = pl.BlockSpec((blk * ROW_TILE, LANES), used_block)
    hbm = pl.BlockSpec(memory_space=pl.ANY)
    n_prefetch = 6
    return pl.pallas_call(
        kern,
        grid_spec=pltpu.PrefetchScalarGridSpec(
            num_scalar_prefetch=n_prefetch,
            grid=(p_rows // blk,),
            in_specs=[rows_blk, hbm, hbm, hbm],
            out_specs=rows_blk,
            scratch_shapes=[pltpu.VMEM((WEIGHT_RING, d_model, d_expert), F32),
                            pltpu.VMEM((WEIGHT_RING, d_model, d_expert), F32),
                            pltpu.VMEM((WEIGHT_RING, d_expert, d_model), F32),
                            pltpu.VMEM((d_model, 2 * d_expert), BF16),
                            pltpu.VMEM((d_expert, d_model), BF16),
                            pltpu.SemaphoreType.DMA((WEIGHT_RING,))]),
        out_shape=jax.ShapeDtypeStruct((p_rows * ROW_TILE, LANES), F32),
        input_output_aliases={n_prefetch: 0},
        compiler_params=_cparams(("arbitrary",)),
        name="experts",
    )(sched["expert"], sched["first"], sched["ahead"], sched["slot"], sched["head"],
      nused, xs, w1, w3, w2)


def _expert_schedule(counts, *, blk, nb):
    nblk = (counts + blk - 1) // blk
    pend = jnp.cumsum(nblk)
    nused = pend[-1]
    barange = jnp.arange(nb, dtype=jnp.int32)
    bidx = jnp.minimum(barange, nused - 1)
    expert = jnp.minimum(jnp.sum(pend[None, :] <= bidx[:, None], axis=1),
                         N_EXPERTS - 1).astype(jnp.int32)
    live = barange < nused
    first = (live & (barange == (pend - nblk)[expert])).astype(jnp.int32)
    has_rows = nblk > 0
    used_rank = jnp.cumsum(has_rows) - 1
    j_idx = jnp.arange(N_EXPERTS + WEIGHT_RING, dtype=jnp.int32)
    e_idx = jnp.arange(N_EXPERTS, dtype=jnp.int32)
    hit = has_rows[None, :] & (used_rank[None, :] == j_idx[:, None])
    used_list = jnp.sum(jnp.where(hit, e_idx[None, :] + 1, 0), axis=1) - 1
    rank_b = used_rank[expert]
    ahead = jnp.where(live, used_list[rank_b + (WEIGHT_RING - 1)], -1)
    return dict(
        expert=expert, first=first, ahead=ahead.astype(jnp.int32),
        slot=(rank_b % WEIGHT_RING).astype(jnp.int32),
        head=used_list[:WEIGHT_RING - 1].astype(jnp.int32),
        nused=nused.reshape(1).astype(jnp.int32),
        pad_start=((pend - nblk) * blk + counts).astype(jnp.int32),
        pad_len=(nblk * blk - counts).astype(jnp.int32))


def _combine_kernel(local0_sm, local1_sm, chunk_src_sm, nchunk_sm,
                    x1_ref, meta_ref, ys_ref, out_ref,
                    buf_ref, g0_ref, g1_ref, sem, *, tb, max_chunks):
    step = pl.program_id(0)
    nsteps = pl.num_programs(0)

    def start_chunks(tile):
        slot = tile % 2

        def per_chunk(c, carry):
            pltpu.make_async_copy(
                _rows(ys_ref, chunk_src_sm[tile * max_chunks + c], RUN_CHUNK),
                _rows(buf_ref.at[slot], c * RUN_CHUNK, RUN_CHUNK),
                sem.at[slot]).start()
            return carry

        lax.fori_loop(0, nchunk_sm[tile], per_chunk, 0)

    def wait_chunks(tile):
        slot = tile % 2
        n = nchunk_sm[tile]
        for bit in range(max_chunks.bit_length()):
            @pl.when((lax.shift_right_logical(n, bit) & 1) == 1)
            def _():
                rows = RUN_CHUNK << bit
                pltpu.make_async_copy(_rows(ys_ref, 0, rows),
                                      _rows(buf_ref.at[slot], 0, rows),
                                      sem.at[slot]).wait()

    @pl.when(step == 0)
    def _():
        start_chunks(step)

    @pl.when(step + 1 < nsteps)
    def _():
        start_chunks(step + 1)

    wait_chunks(step)

    base = step * tb
    tile_buf = buf_ref.at[step % 2]

    def move(r, carry):
        for g_ref, local_sm in ((g0_ref, local0_sm), (g1_ref, local1_sm)):
            _rows(g_ref, r)[...] = _rows(tile_buf, local_sm[base + r])[...]
        return carry

    lax.fori_loop(0, tb, move, 0, unroll=8)
    meta = meta_ref[...]
    moe = (_load_row_tiles(g0_ref) * meta[:, 2:3]
           + _load_row_tiles(g1_ref) * meta[:, 3:4])
    out_ref[...] = x1_ref[...] + moe


def _combine(local0, local1, tables, x1, meta, ys):
    t, d_model = x1.shape
    tb = TB_RANK
    max_chunks = tables["max_chunks"]
    kern = functools.partial(_combine_kernel, tb=tb, max_chunks=max_chunks)
    buf_rows = max_chunks * RUN_CHUNK
    return pl.pallas_call(
        kern,
        grid_spec=pltpu.PrefetchScalarGridSpec(
            num_scalar_prefetch=4,
            grid=(t // tb,),
            in_specs=[pl.BlockSpec((tb, d_model), lambda i, *_: (i, 0)),
                      pl.BlockSpec((tb, LANES), lambda i, *_: (i, 0)),
                      pl.BlockSpec(memory_space=pl.ANY)],
            out_specs=pl.BlockSpec((tb, d_model), lambda i, *_: (i, 0)),
            scratch_shapes=[pltpu.VMEM((2,) + _row_tiles((buf_rows, d_model)), F32),
                            pltpu.VMEM(_row_tiles((tb, d_model)), F32),
                            pltpu.VMEM(_row_tiles((tb, d_model)), F32),
                            pltpu.SemaphoreType.DMA((2,))]),
        out_shape=jax.ShapeDtypeStruct((t, d_model), F32),
        compiler_params=_cparams(("arbitrary",)),
        name="combine",
    )(local0, local1, tables["chunk_src"], tables["tile_chunks"], x1, meta, ys)


def _layer(h2, *, batch, seq, norm1_g, w_in, conv_w, q_norm_g, k_norm_g,
           w_conv_out, w_attn_out, w_o, norm2_g, w_group, w_router, w1, w3, w2):
    t, d_model = h2.shape
    c, sa, sb = _rope_tables(seq)
    scale = HEAD_DIM ** -0.5 * LOG2_E
    tables_q = tuple(jnp.asarray(tab * np.float32(scale)) for tab in (c, sa, sb))
    tables_k = tuple(jnp.asarray(tab) for tab in (c, sa, sb))

    cb, z, q, k, v, sgc, sga = _inproj(
        h2, norm1_g[None, :], w_in.astype(BF16), q_norm_g[None, :], k_norm_g[None, :],
        tables_q, tables_k, seq=seq)
    o = _attention(q, k, v, batch=batch, seq=seq)

    n_route = N_GROUPS + N_EXPERTS
    wr = jnp.concatenate(
        [w_group, w_router, jnp.zeros((d_model, LANES - n_route), F32)], axis=1)
    wr_hi = wr.astype(BF16)
    wr = jnp.concatenate([wr_hi, (wr - wr_hi.astype(F32)).astype(BF16)], axis=1)
    g2 = norm2_g[None, :]
    x1, meta, cnt = _post(h2, cb, z, o, sgc, sga, conv_w,
                          w_conv_out.astype(BF16), w_attn_out.astype(BF16),
                          w_o.astype(BF16), g2, wr, seq=seq)

    blk = MOE_ROWS
    local0, local1, runs = _rank(meta, cnt, blk=blk)
    local0, local1 = local0.reshape(-1), local1.reshape(-1)
    runs = runs.reshape(-1, 8, LANES)
    tables = _tile_tables(runs[:, 0, :N_EXPERTS], runs[:, 1, :N_EXPERTS], tb=TB_RANK)
    p_rows = t * TOP_K + N_EXPERTS * blk
    sched = _expert_schedule(cnt[0, :N_EXPERTS].astype(jnp.int32), blk=blk,
                             nb=p_rows // blk)

    xs = _dispatch(local0, local1, tables, sched["pad_start"], sched["pad_len"],
                   sched["nused"], x1, g2, p_rows=p_rows, blk=blk)
    ys = _experts(sched, sched["nused"], xs, w1, w3, w2, blk=blk)
    return _combine(local0, local1, tables, x1, meta, ys)


def kernel(x, norm1_g, w_in, conv_w, q_norm_g, k_norm_g, w_conv_out, w_attn_out, w_o,
           norm2_g, w_group, w_router, w1, w3, w2):
    batch, seq, d_model = x.shape
    h2 = x.reshape(batch * seq, d_model)
    for l in range(norm1_g.shape[0]):
        h2 = _layer(h2, batch=batch, seq=seq, norm1_g=norm1_g[l], w_in=w_in[l],
                    conv_w=conv_w[l], q_norm_g=q_norm_g[l], k_norm_g=k_norm_g[l],
                    w_conv_out=w_conv_out[l], w_attn_out=w_attn_out[l], w_o=w_o[l],
                    norm2_g=norm2_g[l], w_group=w_group[l], w_router=w_router[l],
                    w1=w1[l], w3=w3[l], w2=w2[l])
    return h2.reshape(batch, seq, d_model)
```

```python
import functools

import jax
import jax.numpy as jnp
import numpy as np
from jax import lax
from jax.experimental import pallas as pl
from jax.experimental.pallas import tpu as pltpu

F32 = jnp.float32
BF16 = jnp.bfloat16

GRID_W = 64
EPS = 1e-6
N_HEADS = 8
N_KV_HEADS = 2
HEAD_DIM = 128
ROPE_THETA = 10000.0
N_GROUPS = 8
EXPERTS_PER_GROUP = 8
N_EXPERTS = N_GROUPS * EXPERTS_PER_GROUP
TOP_K = 2
LOG2_E = 1.4426950408889634

LANES = 128
MXU_DIM = 256
BF16_SUBLANES = 16
V7X_VMEM_LIMIT_BYTES = 56000 * 1024

TM_PROJ = 512
TQ_ATTN = 512
TK_ATTN = 512
TB_RANK = 1024
RUN_CHUNK = 8
SPARE_ROWS = RUN_CHUNK
MOE_ROWS = 256
WEIGHT_RING = 3
WEIGHT_DMA_PRIORITY = 1
DIGIT = 256.0


def _cparams(sem):
    return pltpu.CompilerParams(dimension_semantics=sem,
                                vmem_limit_bytes=V7X_VMEM_LIMIT_BYTES)


ROW_TILE = 8


def _row_tiles(shape2d):
    rows, width = shape2d
    assert width == ROW_TILE * LANES
    return (rows * ROW_TILE, LANES)


def _rows(ref, r, n=1):
    return ref.at[pl.ds(pl.multiple_of(r * ROW_TILE, ROW_TILE), n * ROW_TILE)]


def _load_row_tiles(ref):
    rows = ref.shape[0] // ROW_TILE
    return jnp.concatenate(
        [ref[pl.ds(s, rows, stride=ROW_TILE), :] for s in range(ROW_TILE)], axis=1)


def _store_row_tiles(ref, value):
    rows = value.shape[0]
    for s in range(ROW_TILE):
        ref[pl.ds(s, rows, stride=ROW_TILE), :] = value[:, s * LANES:(s + 1) * LANES]


def _resident(shape):
    nd = len(shape)
    return pl.BlockSpec(shape, lambda *_: (0,) * nd, pipeline_mode=pl.Buffered(1))


def _lane_prefix(row):
    r = lax.broadcasted_iota(jnp.int32, (LANES, LANES), 0)
    col = lax.broadcasted_iota(jnp.int32, (LANES, LANES), 1)
    upper = jnp.where(r < col, 1.0, 0.0).astype(BF16)
    return jnp.dot(jnp.broadcast_to(row, (8, LANES)).astype(BF16), upper,
                   preferred_element_type=F32)[0:1, :]


def _head_norm_rope(xh, g, c, sa, sb):
    ms = jnp.mean(xh * xh, axis=-1, keepdims=True)
    y = xh * lax.rsqrt(ms + EPS) * g
    y_next = pltpu.roll(y, HEAD_DIM - 1, axis=1)
    y_prev = pltpu.roll(y, 1, axis=1)
    return y * c + y_next * sa + y_prev * sb


def _sigmoid(x):
    return 0.5 * jnp.tanh(0.5 * x) + 0.5


def _inproj_kernel(x_ref, g1_ref, w_ref, gq_ref, gk_ref,
                   cq_ref, saq_ref, sbq_ref, ck_ref, sak_ref, sbk_ref,
                   cb_ref, z_ref, q_ref, k_ref, v_ref, sgc_ref, sga_ref,
                   *, d_conv, d_q, d_kv, d_model):
    x = x_ref[...]
    ms = jnp.mean(x * x, axis=-1, keepdims=True)
    u = (x * lax.rsqrt(ms + EPS) * g1_ref[...]).astype(BF16)

    def proj(lo, width):
        return jnp.dot(u, w_ref[:, lo:lo + width], preferred_element_type=F32)

    o_cb, o_cc, o_cx = 0, d_conv, 2 * d_conv
    o_q = 3 * d_conv
    o_k = o_q + d_q
    o_v = o_k + d_kv
    o_gc = o_v + d_kv
    o_ga = o_gc + d_model

    sgc_ref[...] = _sigmoid(proj(o_gc, d_model)).astype(BF16)
    sga_ref[...] = _sigmoid(proj(o_ga, d_model)).astype(BF16)

    q = proj(o_q, d_q)
    gq = gq_ref[...]
    cq, saq, sbq = cq_ref[...], saq_ref[...], sbq_ref[...]
    for h in range(d_q // HEAD_DIM):
        sl = slice(h * HEAD_DIM, (h + 1) * HEAD_DIM)
        q_ref[:, sl] = _head_norm_rope(q[:, sl], gq, cq, saq, sbq).astype(BF16)

    k = proj(o_k, d_kv)
    gk = gk_ref[...]
    ck, sak, sbk = ck_ref[...], sak_ref[...], sbk_ref[...]
    for h in range(d_kv // HEAD_DIM):
        sl = slice(h * HEAD_DIM, (h + 1) * HEAD_DIM)
        k_ref[:, sl] = _head_norm_rope(k[:, sl], gk, ck, sak, sbk).astype(BF16)

    z_ref[...] = (proj(o_cc, d_conv) * proj(o_cx, d_conv)).astype(BF16)
    v_ref[...] = proj(o_v, d_kv).astype(BF16)
    cb_ref[...] = proj(o_cb, d_conv).astype(BF16)


def _rope_tables(seq):
    rows = seq // GRID_W
    axis_dim = HEAD_DIM // 2
    row = np.repeat(np.arange(rows, dtype=np.float32), GRID_W)
    col = np.tile(np.arange(GRID_W, dtype=np.float32), rows)
    inv = (np.float32(ROPE_THETA)
           ** (-np.arange(0, axis_dim, 2, dtype=np.float32) / np.float32(axis_dim)))
    ang = np.concatenate([row[:, None] * inv, col[:, None] * inv], axis=-1)
    ang = ang.astype(np.float32)
    cos, sin = np.cos(ang), np.sin(ang)
    zero = np.zeros_like(sin)
    c = np.repeat(cos, 2, axis=-1)
    sa = np.stack([-sin, zero], axis=-1).reshape(seq, HEAD_DIM)
    sb = np.stack([zero, sin], axis=-1).reshape(seq, HEAD_DIM)
    return c, sa, sb


def _inproj(x2, g1, w_in_bf, gq, gk, tables_q, tables_k, *, seq):
    t, d_model = x2.shape
    d_q = N_HEADS * HEAD_DIM
    d_kv = N_KV_HEADS * HEAD_DIM
    d_in = w_in_bf.shape[1]
    d_conv = (d_in - d_q - 2 * d_kv - 2 * d_model) // 3
    tm = TM_PROJ
    nseq = seq // tm

    def row(width):
        return pl.BlockSpec((tm, width), lambda i: (i, 0))

    table = pl.BlockSpec((tm, HEAD_DIM), lambda i: (i % nseq, 0))
    kern = functools.partial(_inproj_kernel, d_conv=d_conv, d_q=d_q, d_kv=d_kv,
                             d_model=d_model)
    out_shape = [jax.ShapeDtypeStruct((t, w), BF16)
                 for w in (d_conv, d_conv, d_q, d_kv, d_kv, d_model, d_model)]
    return pl.pallas_call(
        kern,
        grid=(t // tm,),
        in_specs=[row(d_model), _resident((1, d_model)), _resident((d_model, d_in)),
                  _resident((1, HEAD_DIM)), _resident((1, HEAD_DIM)),
                  table, table, table, table, table, table],
        out_specs=[row(d_conv), row(d_conv), row(d_q), row(d_kv), row(d_kv),
                   row(d_model), row(d_model)],
        out_shape=out_shape,
        compiler_params=_cparams(("arbitrary",)),
        name="inproj",
    )(x2, g1, w_in_bf, gq, gk, *tables_q, *tables_k)


def _attn_kernel(q_ref, k_ref, v_ref, o_ref, qs_ref, vext_ref, m_ref, acc_ref,
                 *, tq, chunks, group):
    @pl.when(pl.program_id(2) == 0)
    def _():
        vext_ref[:, :HEAD_DIM] = v_ref[...]
        vext_ref[:, HEAD_DIM:] = jnp.ones((vext_ref.shape[0], HEAD_DIM), BF16)

    for g in range(group):
        qs_ref[g * tq:(g + 1) * tq, :] = q_ref[:, g * HEAD_DIM:(g + 1) * HEAD_DIM]
    m_ref[...] = jnp.full(m_ref.shape, -jnp.inf, F32)
    acc_ref[...] = jnp.zeros(acc_ref.shape, F32)

    lo = 0
    for tk in chunks:
        keys = slice(lo, lo + tk)
        lo += tk
        s = lax.dot_general(qs_ref[...], k_ref[keys, :], (((1,), (1,)), ((), ())),
                            preferred_element_type=F32)
        m_prev = m_ref[...]
        m_new = jnp.maximum(m_prev, jnp.max(s, axis=-1, keepdims=True))
        alpha = jnp.exp2(m_prev - m_new)
        p = jnp.concatenate(
            [jnp.exp2(s[:, c * LANES:(c + 1) * LANES] - m_new) for c in range(tk // LANES)],
            axis=1).astype(BF16)
        pv = jnp.dot(p, vext_ref[keys, :], preferred_element_type=F32)
        acc_ref[...] = jnp.concatenate([alpha, alpha], axis=1) * acc_ref[...] + pv
        m_ref[...] = m_new

    out = acc_ref[:, :HEAD_DIM] / acc_ref[:, HEAD_DIM:]
    for g in range(group):
        o_ref[:, g * HEAD_DIM:(g + 1) * HEAD_DIM] = out[g * tq:(g + 1) * tq].astype(BF16)


def _attention(q, k, v, *, batch, seq):
    t = q.shape[0]
    group = N_HEADS // N_KV_HEADS
    tq, tk = TQ_ATTN, TK_ATTN
    nq = seq // tq
    gw = group * HEAD_DIM
    chunks = (tk // 2,) + (tk,) * (seq // tk - 1) + (tk // 2,)
    kern = functools.partial(_attn_kernel, tq=tq, chunks=chunks, group=group)
    return pl.pallas_call(
        kern,
        grid=(batch, N_KV_HEADS, nq),
        in_specs=[pl.BlockSpec((tq, gw), lambda b, h, i: (b * nq + i, h)),
                  pl.BlockSpec((seq, HEAD_DIM), lambda b, h, i: (b, h)),
                  pl.BlockSpec((seq, HEAD_DIM), lambda b, h, i: (b, h))],
        out_specs=pl.BlockSpec((tq, gw), lambda b, h, i: (b * nq + i, h)),
        out_shape=jax.ShapeDtypeStruct((t, N_HEADS * HEAD_DIM), BF16),
        scratch_shapes=[pltpu.VMEM((group * tq, HEAD_DIM), BF16),
                        pltpu.VMEM((seq, 2 * HEAD_DIM), BF16),
                        pltpu.VMEM((group * tq, LANES), F32),
                        pltpu.VMEM((group * tq, 2 * HEAD_DIM), F32)],
        compiler_params=_cparams(("arbitrary", "arbitrary", "arbitrary")),
        name="attention",
    )(q, k, v)


def _route(logits):
    rows = logits.shape[0]
    lane = lax.broadcasted_iota(jnp.int32, (rows, LANES), 1).astype(F32)
    neg = -jnp.inf
    big = float(2 * LANES)
    is_group = lane < N_GROUPS
    gl = jnp.where(is_group, logits, neg)
    gmax = jnp.max(gl, axis=-1, keepdims=True)
    gidx = jnp.min(jnp.where(gl == gmax, lane, big), axis=-1, keepdims=True)
    gsum = jnp.sum(jnp.where(is_group, jnp.exp(logits - gmax), 0.0), axis=-1,
                   keepdims=True)
    pg = 1.0 / gsum
    lane_group = jnp.floor(lane * (1.0 / EXPERTS_PER_GROUP)) - 1.0
    mine = (lane_group == gidx) & (lane >= N_GROUPS) & (lane < N_GROUPS + N_EXPERTS)
    sel = jnp.where(mine, logits, neg)
    v1 = jnp.max(sel, axis=-1, keepdims=True)
    i1 = jnp.min(jnp.where(sel == v1, lane, big), axis=-1, keepdims=True)
    sel2 = jnp.where(lane == i1, neg, sel)
    v2 = jnp.max(sel2, axis=-1, keepdims=True)
    i2 = jnp.min(jnp.where(sel2 == v2, lane, big), axis=-1, keepdims=True)
    t2 = jnp.exp(v2 - v1)
    den = 1.0 + t2
    wgt1 = pg * (1.0 / den)
    wgt2 = pg * (t2 / den)
    e1 = i1 - N_GROUPS
    e2 = i2 - N_GROUPS
    meta = jnp.where(lane == 0, e1,
                     jnp.where(lane == 1, e2,
                               jnp.where(lane == 2, wgt1,
                                         jnp.where(lane == 3, wgt2, 0.0))))
    picked = jnp.where((lane == e1) | (lane == e2), 1.0, 0.0)
    return meta, jnp.sum(picked, axis=0, keepdims=True)


def _post_kernel(x_ref, cb_ref, z_ref, zprev_ref, znext_ref, o_ref, sgc_ref, sga_ref,
                 cw_ref, wc_ref, wa_ref, wo_ref, g2_ref, wr_ref,
                 x1_ref, u2_ref, meta_ref, cnt_ref, *, tm, nseq):
    i = pl.program_id(0)
    at_start = (i % nseq) == 0
    at_end = (i % nseq) == nseq - 1
    rowid = lax.broadcasted_iota(jnp.int32, (tm, 1), 0)

    @pl.when(i == 0)
    def _():
        cnt_ref[...] = jnp.zeros(cnt_ref.shape, F32)

    y_attn = jnp.dot(o_ref[...], wa_ref[...], preferred_element_type=F32)

    y_conv = None
    for c in range(z_ref.shape[1] // MXU_DIM):
        sl = slice(c * MXU_DIM, (c + 1) * MXU_DIM)
        z = z_ref[:, sl].astype(F32)
        prev_row = zprev_ref[BF16_SUBLANES - 1:BF16_SUBLANES, sl].astype(F32)
        next_row = znext_ref[0:1, sl].astype(F32)
        prev_row = jnp.where(at_start, 0.0, prev_row)
        next_row = jnp.where(at_end, 0.0, next_row)
        zp = jnp.where(rowid == 0, prev_row, pltpu.roll(z, 1, axis=0))
        zn = jnp.where(rowid == tm - 1, next_row, pltpu.roll(z, tm - 1, axis=0))
        conv = cw_ref[0:1, sl] * zp + cw_ref[1:2, sl] * z + cw_ref[2:3, sl] * zn
        cbz = (cb_ref[:, sl].astype(F32) * conv).astype(BF16)
        part = jnp.dot(cbz, wc_ref[sl, :], preferred_element_type=F32)
        y_conv = part if y_conv is None else y_conv + part
    merged = (sgc_ref[...].astype(F32) * y_conv
              + sga_ref[...].astype(F32) * y_attn).astype(BF16)
    x1 = x_ref[...] + jnp.dot(merged, wo_ref[...], preferred_element_type=F32)
    x1_ref[...] = x1

    ms = jnp.mean(x1 * x1, axis=-1, keepdims=True)
    u2 = x1 * lax.rsqrt(ms + EPS) * g2_ref[...]
    _store_row_tiles(u2_ref, u2)

    u2_hi = u2.astype(BF16)
    u2_lo = (u2 - u2_hi.astype(F32)).astype(BF16)
    hi_part = jnp.dot(u2_hi, wr_ref[...], preferred_element_type=F32)
    lo_part = jnp.dot(u2_lo, wr_ref[:, :LANES], preferred_element_type=F32)
    logits = hi_part[:, :LANES] + (hi_part[:, LANES:] + lo_part)
    meta, picked = _route(logits)
    meta_ref[...] = meta
    cnt_ref[...] += picked


def _post(x2, cb, z, o, sgc, sga, conv_w, wc, wa, wo, g2, wr, *, seq):
    t, d_model = x2.shape
    tm = TM_PROJ
    nseq = seq // tm
    hb = tm // BF16_SUBLANES
    nhalo = t // BF16_SUBLANES
    d_conv = cb.shape[1]
    d_q = o.shape[1]

    def row(width):
        return pl.BlockSpec((tm, width), lambda i: (i, 0))

    kern = functools.partial(_post_kernel, tm=tm, nseq=nseq)
    return pl.pallas_call(
        kern,
        grid=(t // tm,),
        in_specs=[row(d_model), row(d_conv), row(d_conv),
                  pl.BlockSpec((BF16_SUBLANES, d_conv),
                               lambda i: (jnp.maximum(i * hb - 1, 0), 0)),
                  pl.BlockSpec((BF16_SUBLANES, d_conv),
                               lambda i: (jnp.minimum((i + 1) * hb, nhalo - 1), 0)),
                  row(d_q), row(d_model), row(d_model),
                  _resident(conv_w.shape), _resident(wc.shape), _resident(wa.shape),
                  _resident(wo.shape), _resident(g2.shape), _resident(wr.shape)],
        out_specs=[row(d_model),
                   pl.BlockSpec(_row_tiles((tm, d_model)), lambda i: (i, 0)),
                   row(LANES),
                   pl.BlockSpec((1, LANES), lambda i: (0, 0))],
        out_shape=[jax.ShapeDtypeStruct((t, d_model), F32),
                   jax.ShapeDtypeStruct(_row_tiles((t, d_model)), F32),
                   jax.ShapeDtypeStruct((t, LANES), F32),
                   jax.ShapeDtypeStruct((1, LANES), F32)],
        compiler_params=_cparams(("arbitrary",)),
        name="post",
    )(x2, cb, z, z, z, o, sgc, sga, conv_w, wc, wa, wo, g2, wr)


def _rank_kernel(meta_ref, cnt_ref, l0_ref, l1_ref, runs_ref,
                 carry_ref, pstart_ref, lower_ref, eye_ref, *, tb):
    i = pl.program_id(0)
    lane = lax.broadcasted_iota(jnp.int32, (tb, LANES), 1).astype(F32)
    meta = meta_ref[...]
    oh1 = jnp.where(lane == meta[:, 0:1], 1.0, 0.0)
    oh2 = jnp.where(lane == meta[:, 1:2], 1.0, 0.0)
    c = oh1 + oh2

    @pl.when(i == 0)
    def _():
        cnt = cnt_ref[...]
        high = jnp.floor(cnt * (1.0 / DIGIT))
        pstart_ref[...] = _lane_prefix(high) * DIGIT + _lane_prefix(cnt - high * DIGIT)
        carry_ref[...] = jnp.zeros(carry_ref.shape, F32)
        r = lax.broadcasted_iota(jnp.int32, (tb, tb), 0)
        col = lax.broadcasted_iota(jnp.int32, (tb, tb), 1)
        lower_ref[...] = jnp.where(col < r, 1.0, 0.0).astype(BF16)
        eye_ref[...] = jnp.where(col == r, 1.0, 0.0).astype(BF16)

    prefix = jnp.dot(lower_ref[...], c.astype(BF16), preferred_element_type=F32)
    run_start = carry_ref[...] + pstart_ref[...]
    run_len = jnp.sum(c, axis=0, keepdims=True)

    nchunk = jnp.floor((run_len + (RUN_CHUNK - 1)) * (1.0 / RUN_CHUNK))
    local = prefix + _lane_prefix(nchunk) * RUN_CHUNK

    digits = jnp.zeros((tb, LANES), F32)
    for j, onehot in enumerate((oh1, oh2)):
        value = jnp.sum(onehot * local, axis=-1, keepdims=True)
        high = jnp.floor(value * (1.0 / DIGIT))
        digits = jnp.where(lane == 2 * j, high,
                           jnp.where(lane == 2 * j + 1, value - high * DIGIT, digits))
    rows = lax.dot_general(digits.astype(BF16), eye_ref[...], (((0,), (0,)), ((), ())),
                           preferred_element_type=F32)
    for j, out_ref in enumerate((l0_ref, l1_ref)):
        out_ref[...] = (rows[2 * j:2 * j + 1, :] * DIGIT
                        + rows[2 * j + 1:2 * j + 2, :]).astype(jnp.int32)
    row = lax.broadcasted_iota(jnp.int32, (8, LANES), 0)
    runs = jnp.where(row == 0, run_start, jnp.where(row == 1, run_len, 0.0))
    runs_ref[...] = runs.astype(jnp.int32)
    carry_ref[...] += run_len


def _rank(meta, cnt):
    t = meta.shape[0]
    tb = TB_RANK
    kern = functools.partial(_rank_kernel, tb=tb)
    per_token = pl.BlockSpec((1, tb), lambda i: (0, i))
    return pl.pallas_call(
        kern,
        grid=(t // tb,),
        in_specs=[pl.BlockSpec((tb, LANES), lambda i: (i, 0)),
                  pl.BlockSpec((1, LANES), lambda i: (0, 0))],
        out_specs=[per_token] * 2 + [pl.BlockSpec((8, LANES), lambda i: (i, 0))],
        out_shape=[jax.ShapeDtypeStruct((1, t), jnp.int32)] * 2
                  + [jax.ShapeDtypeStruct((t // tb * 8, LANES), jnp.int32)],
        scratch_shapes=[pltpu.VMEM((1, LANES), F32), pltpu.VMEM((1, LANES), F32),
                        pltpu.VMEM((tb, tb), BF16), pltpu.VMEM((tb, tb), BF16)],
        compiler_params=_cparams(("arbitrary",)),
        name="rank",
    )(meta, cnt)


def _dispatch_kernel(local0_sm, local1_sm, full_src_sm, full_dst_sm, nfull_sm,
                     rem_src_sm, rem_dst_sm, rem_len_sm,
                     u2_ref, xs_ref, buf_ref, sem, *, tb, max_full):
    step = pl.program_id(0)
    nsteps = pl.num_programs(0)
    base = step * tb
    slot = step % 2
    tile_buf = buf_ref.at[slot]

    def wait_tile(s):
        pltpu.make_async_copy(_rows(buf_ref.at[s], 0, tb * TOP_K),
                              _rows(xs_ref, 0, tb * TOP_K), sem.at[s]).wait()

    @pl.when(step >= 2)
    def _():
        wait_tile(slot)

    def move(r, carry):
        row = _rows(u2_ref, r)[...]
        _rows(tile_buf, local0_sm[base + r])[...] = row
        _rows(tile_buf, local1_sm[base + r])[...] = row
        return carry

    lax.fori_loop(0, tb, move, 0, unroll=8)

    def full_chunk(c, carry):
        pltpu.make_async_copy(
            _rows(tile_buf, full_src_sm[step * max_full + c], RUN_CHUNK),
            _rows(xs_ref, full_dst_sm[step * max_full + c], RUN_CHUNK),
            sem.at[slot]).start()
        return carry

    lax.fori_loop(0, nfull_sm[step], full_chunk, 0)

    def remainder(e, carry):
        idx = step * N_EXPERTS + e
        src, dst, length = rem_src_sm[idx], rem_dst_sm[idx], rem_len_sm[idx]
        for bit in range(RUN_CHUNK.bit_length() - 1):
            size = 1 << bit
            higher = lax.shift_left(lax.shift_right_logical(length, bit + 1), bit + 1)

            @pl.when((lax.shift_right_logical(length, bit) & 1) == 1)
            def _():
                pltpu.make_async_copy(_rows(tile_buf, src + higher, size),
                                      _rows(xs_ref, dst + higher, size),
                                      sem.at[slot]).start()
        return carry

    lax.fori_loop(0, N_EXPERTS, remainder, 0)

    @pl.when(step == nsteps - 1)
    def _():
        @pl.when(step >= 1)
        def _():
            wait_tile(1 - slot)

        wait_tile(slot)


def _tile_tables(run_start, run_len, *, tb):
    shift = RUN_CHUNK.bit_length() - 1
    nchunk = (run_len + RUN_CHUNK - 1) >> shift
    cum = jnp.cumsum(nchunk, axis=1)
    first = cum - nchunk
    local_start = first * RUN_CHUNK

    def flat_list(count, max_count, value_at):
        ccum = jnp.cumsum(count, axis=1)
        cfirst = (ccum - count)[:, None, :]
        c_idx = jnp.arange(max_count, dtype=jnp.int32)[None, :, None]
        owns = (cfirst <= c_idx) & (c_idx < ccum[:, None, :])
        return [jnp.sum(jnp.where(owns, v[:, None, :] + (c_idx - cfirst) * RUN_CHUNK, 0),
                        axis=2).astype(jnp.int32) for v in value_at], ccum[:, -1]

    max_chunks = tb * TOP_K // RUN_CHUNK + N_EXPERTS
    (chunk_src,), tile_chunks = flat_list(nchunk, max_chunks, [run_start])
    max_full = tb * TOP_K // RUN_CHUNK
    nfull = run_len >> shift
    (full_src, full_dst), tile_full = flat_list(nfull, max_full, [local_start, run_start])
    whole = nfull * RUN_CHUNK
    return dict(
        max_chunks=max_chunks, chunk_src=chunk_src.reshape(-1),
        tile_chunks=tile_chunks.astype(jnp.int32),
        max_full=max_full, full_src=full_src.reshape(-1), full_dst=full_dst.reshape(-1),
        tile_full=tile_full.astype(jnp.int32),
        rem_src=(local_start + whole).reshape(-1).astype(jnp.int32),
        rem_dst=(run_start + whole).reshape(-1).astype(jnp.int32),
        rem_len=(run_len - whole).reshape(-1).astype(jnp.int32))


def _dispatch(local0, local1, tables, u2, *, data_rows):
    t = u2.shape[0] // ROW_TILE
    tb = TB_RANK
    d_model = ROW_TILE * LANES
    kern = functools.partial(_dispatch_kernel, tb=tb, max_full=tables["max_full"])
    buf_rows = tables["max_chunks"] * RUN_CHUNK
    return pl.pallas_call(
        kern,
        grid_spec=pltpu.PrefetchScalarGridSpec(
            num_scalar_prefetch=8,
            grid=(t // tb,),
            in_specs=[pl.BlockSpec((tb * ROW_TILE, LANES), lambda i, *_: (i, 0))],
            out_specs=pl.BlockSpec(memory_space=pl.ANY),
            scratch_shapes=[pltpu.VMEM((2,) + _row_tiles((buf_rows, d_model)), F32),
                            pltpu.SemaphoreType.DMA((2,))]),
        out_shape=jax.ShapeDtypeStruct((data_rows * ROW_TILE, LANES), F32),
        compiler_params=_cparams(("arbitrary",)),
        name="dispatch",
    )(local0, local1, tables["full_src"], tables["full_dst"], tables["tile_full"],
      tables["rem_src"], tables["rem_dst"], tables["rem_len"], u2)


def _experts_kernel(start_sm, nvalid_sm, expert_sm, first_sm, ahead_sm, slot_sm, head_sm,
                    xs_hbm, w1_hbm, w3_hbm, w2_hbm, ys_hbm,
                    xbuf_ref, ybuf_ref, zrow_ref, w1f_ref, w3f_ref, w2f_ref,
                    w13b_ref, w2b_ref, wsem, isem, osem, zsem,
                    *, d_expert, blk, data_rows):
    v = pl.program_id(0)
    nsteps = pl.num_programs(0)
    nbits = blk.bit_length()

    def block_copies(step, inbound, visit):
        n, start, slot = nvalid_sm[step], start_sm[step], lax.rem(step, 2)
        for bit in range(nbits):
            size = 1 << bit
            higher = lax.shift_left(lax.shift_right_logical(n, bit + 1), bit + 1)

            @pl.when((lax.shift_right_logical(n, bit) & 1) == 1)
            def _():
                if inbound:
                    visit(pltpu.make_async_copy(_rows(xs_hbm, start + higher, size),
                                                _rows(xbuf_ref.at[slot], higher, size),
                                                isem.at[slot]))
                else:
                    visit(pltpu.make_async_copy(_rows(ybuf_ref.at[slot], higher, size),
                                                _rows(ys_hbm, start + higher, size),
                                                osem.at[slot]))

    def weight_copies(expert, slot):
        return [pltpu.make_async_copy(src.at[expert], dst.at[slot], wsem.at[slot])
                for src, dst in ((w1_hbm, w1f_ref), (w3_hbm, w3f_ref), (w2_hbm, w2f_ref))]

    spare_fill = pltpu.make_async_copy(zrow_ref, _rows(ys_hbm, data_rows, SPARE_ROWS), zsem)

    @pl.when(v == 0)
    def _():
        xbuf_ref[...] = jnp.zeros(xbuf_ref.shape, F32)
        zrow_ref[...] = jnp.zeros(zrow_ref.shape, F32)
        spare_fill.start()
        for j in range(WEIGHT_RING - 1):
            @pl.when(head_sm[j] >= 0)
            def _():
                for c in weight_copies(head_sm[j], j):
                    c.start(priority=WEIGHT_DMA_PRIORITY)
        block_copies(v, True, lambda c: c.start())

    @pl.when(v + 1 < nsteps)
    def _():
        block_copies(v + 1, True, lambda c: c.start())

    @pl.when(first_sm[v] == 1)
    def _():
        slot = slot_sm[v]
        for c in weight_copies(expert_sm[v], slot):
            c.wait()

        @pl.when(ahead_sm[v] >= 0)
        def _():
            ahead_slot = lax.rem(slot + (WEIGHT_RING - 1), WEIGHT_RING)
            for c in weight_copies(ahead_sm[v], ahead_slot):
                c.start(priority=WEIGHT_DMA_PRIORITY)

        w13b_ref[:, :d_expert] = w1f_ref[slot].astype(BF16)
        w13b_ref[:, d_expert:] = w3f_ref[slot].astype(BF16)
        w2b_ref[...] = w2f_ref[slot].astype(BF16)

    block_copies(v, True, lambda c: c.wait())

    @pl.when(v >= 2)
    def _():
        block_copies(v - 2, False, lambda c: c.wait())

    @pl.when(nvalid_sm[v] > 0)
    def _():
        slot = lax.rem(v, 2)
        xb = _load_row_tiles(xbuf_ref.at[slot]).astype(BF16)
        h = jnp.dot(xb, w13b_ref[...], preferred_element_type=F32)
        h1 = h[:, :d_expert]
        h3 = h[:, d_expert:]
        a = (h1 * _sigmoid(h1) * h3).astype(BF16)
        _store_row_tiles(ybuf_ref.at[slot],
                         jnp.dot(a, w2b_ref[...], preferred_element_type=F32))

    block_copies(v, False, lambda c: c.start())

    @pl.when(v == nsteps - 1)
    def _():
        @pl.when(v >= 1)
        def _():
            block_copies(v - 1, False, lambda c: c.wait())

        block_copies(v, False, lambda c: c.wait())
        spare_fill.wait()


def _experts(sched, xs, w1, w3, w2, *, blk):
    data_rows = xs.shape[0] // ROW_TILE
    d_model, d_expert = w1.shape[-2:]
    kern = functools.partial(_experts_kernel, d_expert=d_expert, blk=blk,
                             data_rows=data_rows)
    hbm = pl.BlockSpec(memory_space=pl.ANY)
    block_buf = pltpu.VMEM((2,) + _row_tiles((blk, d_model)), F32)
    return pl.pallas_call(
        kern,
        grid_spec=pltpu.PrefetchScalarGridSpec(
            num_scalar_prefetch=7,
            grid=(sched["start"].shape[0],),
            in_specs=[hbm, hbm, hbm, hbm],
            out_specs=hbm,
            scratch_shapes=[block_buf, block_buf,
                            pltpu.VMEM(_row_tiles((SPARE_ROWS, d_model)), F32),
                            pltpu.VMEM((WEIGHT_RING, d_model, d_expert), F32),
                            pltpu.VMEM((WEIGHT_RING, d_model, d_expert), F32),
                            pltpu.VMEM((WEIGHT_RING, d_expert, d_model), F32),
                            pltpu.VMEM((d_model, 2 * d_expert), BF16),
                            pltpu.VMEM((d_expert, d_model), BF16),
                            pltpu.SemaphoreType.DMA((WEIGHT_RING,)),
                            pltpu.SemaphoreType.DMA((2,)), pltpu.SemaphoreType.DMA((2,)),
                            pltpu.SemaphoreType.DMA]),
        out_shape=jax.ShapeDtypeStruct(((data_rows + SPARE_ROWS) * ROW_TILE, LANES), F32),
        compiler_params=_cparams(("arbitrary",)),
        name="experts",
    )(sched["start"], sched["nvalid"], sched["expert"], sched["first"], sched["ahead"],
      sched["slot"], sched["head"], xs, w1, w3, w2)


def _expert_schedule(counts, *, blk, data_blocks):
    seg_end = jnp.cumsum(counts)
    seg_start = seg_end - counts
    nblk = (counts + blk - 1) // blk
    vend = jnp.cumsum(nblk)
    vfirst = vend - nblk
    nsteps_live = vend[-1]
    total = data_blocks + N_EXPERTS
    v_idx = jnp.arange(total, dtype=jnp.int32)
    vi = jnp.maximum(jnp.minimum(v_idx, nsteps_live - 1), 0)
    expert = jnp.minimum(jnp.sum(vend[None, :] <= vi[:, None], axis=1),
                         N_EXPERTS - 1).astype(jnp.int32)
    live = v_idx < nsteps_live
    within = vi - vfirst[expert]
    start = seg_start[expert] + within * blk
    nvalid = jnp.where(live, jnp.clip(seg_end[expert] - start, 0, blk), 0)
    first = live & (within == 0)
    has_rows = nblk > 0
    used_rank = jnp.cumsum(has_rows) - 1
    j_idx = jnp.arange(N_EXPERTS + WEIGHT_RING, dtype=jnp.int32)
    e_idx = jnp.arange(N_EXPERTS, dtype=jnp.int32)
    hit = has_rows[None, :] & (used_rank[None, :] == j_idx[:, None])
    used_list = jnp.sum(jnp.where(hit, e_idx[None, :] + 1, 0), axis=1) - 1
    rank_v = used_rank[expert]
    ahead = jnp.where(live, used_list[rank_v + (WEIGHT_RING - 1)], -1)
    as_i32 = lambda a: a.astype(jnp.int32)
    return dict(start=as_i32(jnp.where(live, start, 0)), nvalid=as_i32(nvalid),
                expert=expert, first=as_i32(first), ahead=as_i32(ahead),
                slot=as_i32(rank_v % WEIGHT_RING), head=as_i32(used_list[:WEIGHT_RING - 1]))


def _combine_kernel(local0_sm, local1_sm, chunk_src_sm, nchunk_sm,
                    x1_ref, meta_ref, ys_ref, out_ref,
                    buf_ref, g0_ref, g1_ref, sem, *, tb, max_chunks):
    step = pl.program_id(0)
    nsteps = pl.num_programs(0)

    def start_chunks(tile):
        slot = tile % 2

        def per_chunk(c, carry):
            pltpu.make_async_copy(
                _rows(ys_ref, chunk_src_sm[tile * max_chunks + c], RUN_CHUNK),
                _rows(buf_ref.at[slot], c * RUN_CHUNK, RUN_CHUNK),
                sem.at[slot]).start()
            return carry

        lax.fori_loop(0, nchunk_sm[tile], per_chunk, 0)

    def wait_chunks(tile):
        slot = tile % 2
        n = nchunk_sm[tile]
        for bit in range(max_chunks.bit_length()):
            @pl.when((lax.shift_right_logical(n, bit) & 1) == 1)
            def _():
                rows = RUN_CHUNK << bit
                pltpu.make_async_copy(_rows(ys_ref, 0, rows),
                                      _rows(buf_ref.at[slot], 0, rows),
                                      sem.at[slot]).wait()

    @pl.when(step == 0)
    def _():
        start_chunks(step)

    @pl.when(step + 1 < nsteps)
    def _():
        start_chunks(step + 1)

    wait_chunks(step)

    base = step * tb
    tile_buf = buf_ref.at[step % 2]

    def move(r, carry):
        for g_ref, local_sm in ((g0_ref, local0_sm), (g1_ref, local1_sm)):
            _rows(g_ref, r)[...] = _rows(tile_buf, local_sm[base + r])[...]
        return carry

    lax.fori_loop(0, tb, move, 0, unroll=8)
    meta = meta_ref[...]
    moe = (_load_row_tiles(g0_ref) * meta[:, 2:3]
           + _load_row_tiles(g1_ref) * meta[:, 3:4])
    out_ref[...] = x1_ref[...] + moe


def _combine(local0, local1, tables, x1, meta, ys):
    t, d_model = x1.shape
    tb = TB_RANK
    max_chunks = tables["max_chunks"]
    kern = functools.partial(_combine_kernel, tb=tb, max_chunks=max_chunks)
    buf_rows = max_chunks * RUN_CHUNK
    return pl.pallas_call(
        kern,
        grid_spec=pltpu.PrefetchScalarGridSpec(
            num_scalar_prefetch=4,
            grid=(t // tb,),
            in_specs=[pl.BlockSpec((tb, d_model), lambda i, *_: (i, 0)),
                      pl.BlockSpec((tb, LANES), lambda i, *_: (i, 0)),
                      pl.BlockSpec(memory_space=pl.ANY)],
            out_specs=pl.BlockSpec((tb, d_model), lambda i, *_: (i, 0)),
            scratch_shapes=[pltpu.VMEM((2,) + _row_tiles((buf_rows, d_model)), F32),
                            pltpu.VMEM(_row_tiles((tb, d_model)), F32),
                            pltpu.VMEM(_row_tiles((tb, d_model)), F32),
                            pltpu.SemaphoreType.DMA((2,))]),
        out_shape=jax.ShapeDtypeStruct((t, d_model), F32),
        compiler_params=_cparams(("arbitrary",)),
        name="combine",
    )(local0, local1, tables["chunk_src"], tables["tile_chunks"], x1, meta, ys)


def _layer(h2, *, batch, seq, norm1_g, w_in, conv_w, q_norm_g, k_norm_g,
           w_conv_out, w_attn_out, w_o, norm2_g, w_group, w_router, w1, w3, w2):
    t, d_model = h2.shape
    c, sa, sb = _rope_tables(seq)
    scale = HEAD_DIM ** -0.5 * LOG2_E
    tables_q = tuple(jnp.asarray(tab * np.float32(scale)) for tab in (c, sa, sb))
    tables_k = tuple(jnp.asarray(tab) for tab in (c, sa, sb))

    cb, z, q, k, v, sgc, sga = _inproj(
        h2, norm1_g[None, :], w_in.astype(BF16), q_norm_g[None, :], k_norm_g[None, :],
        tables_q, tables_k, seq=seq)
    o = _attention(q, k, v, batch=batch, seq=seq)

    n_route = N_GROUPS + N_EXPERTS
    wr = jnp.concatenate(
        [w_group, w_router, jnp.zeros((d_model, LANES - n_route), F32)], axis=1)
    wr_hi = wr.astype(BF16)
    wr = jnp.concatenate([wr_hi, (wr - wr_hi.astype(F32)).astype(BF16)], axis=1)
    x1, u2, meta, cnt = _post(h2, cb, z, o, sgc, sga, conv_w,
                              w_conv_out.astype(BF16), w_attn_out.astype(BF16),
                              w_o.astype(BF16), norm2_g[None, :], wr, seq=seq)

    blk = MOE_ROWS
    data_rows = t * TOP_K
    local0, local1, runs = _rank(meta, cnt)
    local0, local1 = local0.reshape(-1), local1.reshape(-1)
    runs = runs.reshape(-1, 8, LANES)
    tables = _tile_tables(runs[:, 0, :N_EXPERTS], runs[:, 1, :N_EXPERTS], tb=TB_RANK)
    sched = _expert_schedule(cnt[0, :N_EXPERTS].astype(jnp.int32), blk=blk,
                             data_blocks=data_rows // blk)

    xs = _dispatch(local0, local1, tables, u2, data_rows=data_rows)
    ys = _experts(sched, xs, w1, w3, w2, blk=blk)
    return _combine(local0, local1, tables, x1, meta, ys)


def kernel(x, norm1_g, w_in, conv_w, q_norm_g, k_norm_g, w_conv_out, w_attn_out, w_o,
           norm2_g, w_group, w_router, w1, w3, w2):
    batch, seq, d_model = x.shape
    h2 = x.reshape(batch * seq, d_model)
    for l in range(norm1_g.shape[0]):
        h2 = _layer(h2, batch=batch, seq=seq, norm1_g=norm1_g[l], w_in=w_in[l],
                    conv_w=conv_w[l], q_norm_g=q_norm_g[l], k_norm_g=k_norm_g[l],
                    w_conv_out=w_conv_out[l], w_attn_out=w_attn_out[l], w_o=w_o[l],
                    norm2_g=norm2_g[l], w_group=w_group[l], w_router=w_router[l],
                    w1=w1[l], w3=w3[l], w2=w2[l])
    return h2.reshape(batch, seq, d_model)
```

```python
import functools

import jax
import jax.numpy as jnp
import numpy as np
from jax import lax
from jax.experimental import pallas as pl
from jax.experimental.pallas import tpu as pltpu

F32 = jnp.float32
BF16 = jnp.bfloat16

GRID_W = 64
EPS = 1e-6
N_HEADS = 8
N_KV_HEADS = 2
HEAD_DIM = 128
ROPE_THETA = 10000.0
N_GROUPS = 8
EXPERTS_PER_GROUP = 8
N_EXPERTS = N_GROUPS * EXPERTS_PER_GROUP
TOP_K = 2
LOG2_E = 1.4426950408889634

LANES = 128
MXU_DIM = 256
BF16_SUBLANES = 16
V7X_VMEM_LIMIT_BYTES = 56000 * 1024

TM_PROJ = 512
TQ_ATTN = 512
TK_ATTN = 512
TB_RANK = 1024
RUN_CHUNK = 8
SPARE_ROWS = RUN_CHUNK
MOE_ROWS = 256
WEIGHT_RING = 3
WEIGHT_DMA_PRIORITY = 1
DIGIT = 256.0


def _cparams(sem):
    return pltpu.CompilerParams(dimension_semantics=sem,
                                vmem_limit_bytes=V7X_VMEM_LIMIT_BYTES)


ROW_TILE = 8


def _row_tiles(shape2d):
    rows, width = shape2d
    assert width == ROW_TILE * LANES
    return (rows * ROW_TILE, LANES)


def _rows(ref, r, n=1):
    return ref.at[pl.ds(pl.multiple_of(r * ROW_TILE, ROW_TILE), n * ROW_TILE)]


def _load_row_tiles(ref):
    rows = ref.shape[0] // ROW_TILE
    return jnp.concatenate(
        [ref[pl.ds(s, rows, stride=ROW_TILE), :] for s in range(ROW_TILE)], axis=1)


def _store_row_tiles(ref, value):
    rows = value.shape[0]
    for s in range(ROW_TILE):
        ref[pl.ds(s, rows, stride=ROW_TILE), :] = value[:, s * LANES:(s + 1) * LANES]


def _resident(shape):
    nd = len(shape)
    return pl.BlockSpec(shape, lambda *_: (0,) * nd, pipeline_mode=pl.Buffered(1))


def _lane_prefix(row):
    r = lax.broadcasted_iota(jnp.int32, (LANES, LANES), 0)
    col = lax.broadcasted_iota(jnp.int32, (LANES, LANES), 1)
    upper = jnp.where(r < col, 1.0, 0.0).astype(BF16)
    return jnp.dot(jnp.broadcast_to(row, (8, LANES)).astype(BF16), upper,
                   preferred_element_type=F32)[0:1, :]


def _head_norm_rope(xh, g, c, sa, sb):
    ms = jnp.mean(xh * xh, axis=-1, keepdims=True)
    y = xh * lax.rsqrt(ms + EPS) * g
    y_next = pltpu.roll(y, HEAD_DIM - 1, axis=1)
    y_prev = pltpu.roll(y, 1, axis=1)
    return y * c + y_next * sa + y_prev * sb


def _sigmoid(x):
    return 0.5 * jnp.tanh(0.5 * x) + 0.5


def _inproj_kernel(x_ref, g1_ref, w_ref, gq_ref, gk_ref,
                   cq_ref, saq_ref, sbq_ref, ck_ref, sak_ref, sbk_ref,
                   cb_ref, z_ref, q_ref, k_ref, v_ref, sgc_ref, sga_ref,
                   *, d_conv, d_q, d_kv, d_model):
    x = x_ref[...]
    ms = jnp.mean(x * x, axis=-1, keepdims=True)
    u = (x * lax.rsqrt(ms + EPS) * g1_ref[...]).astype(BF16)

    def proj(lo, width):
        return jnp.dot(u, w_ref[:, lo:lo + width], preferred_element_type=F32)

    o_cb, o_cc, o_cx = 0, d_conv, 2 * d_conv
    o_q = 3 * d_conv
    o_k = o_q + d_q
    o_v = o_k + d_kv
    o_gc = o_v + d_kv
    o_ga = o_gc + d_model

    sgc_ref[...] = _sigmoid(proj(o_gc, d_model)).astype(BF16)
    sga_ref[...] = _sigmoid(proj(o_ga, d_model)).astype(BF16)

    q = proj(o_q, d_q)
    gq = gq_ref[...]
    cq, saq, sbq = cq_ref[...], saq_ref[...], sbq_ref[...]
    for h in range(d_q // HEAD_DIM):
        sl = slice(h * HEAD_DIM, (h + 1) * HEAD_DIM)
        q_ref[:, sl] = _head_norm_rope(q[:, sl], gq, cq, saq, sbq).astype(BF16)

    k = proj(o_k, d_kv)
    gk = gk_ref[...]
    ck, sak, sbk = ck_ref[...], sak_ref[...], sbk_ref[...]
    for h in range(d_kv // HEAD_DIM):
        sl = slice(h * HEAD_DIM, (h + 1) * HEAD_DIM)
        k_ref[:, sl] = _head_norm_rope(k[:, sl], gk, ck, sak, sbk).astype(BF16)

    z_ref[...] = (proj(o_cc, d_conv) * proj(o_cx, d_conv)).astype(BF16)
    v_ref[...] = proj(o_v, d_kv).astype(BF16)
    cb_ref[...] = proj(o_cb, d_conv).astype(BF16)


def _rope_tables(seq):
    rows = seq // GRID_W
    axis_dim = HEAD_DIM // 2
    row = np.repeat(np.arange(rows, dtype=np.float32), GRID_W)
    col = np.tile(np.arange(GRID_W, dtype=np.float32), rows)
    inv = (np.float32(ROPE_THETA)
           ** (-np.arange(0, axis_dim, 2, dtype=np.float32) / np.float32(axis_dim)))
    ang = np.concatenate([row[:, None] * inv, col[:, None] * inv], axis=-1)
    ang = ang.astype(np.float32)
    cos, sin = np.cos(ang), np.sin(ang)
    zero = np.zeros_like(sin)
    c = np.repeat(cos, 2, axis=-1)
    sa = np.stack([-sin, zero], axis=-1).reshape(seq, HEAD_DIM)
    sb = np.stack([zero, sin], axis=-1).reshape(seq, HEAD_DIM)
    return c, sa, sb


def _inproj(x2, g1, w_in_bf, gq, gk, tables_q, tables_k, *, seq):
    t, d_model = x2.shape
    d_q = N_HEADS * HEAD_DIM
    d_kv = N_KV_HEADS * HEAD_DIM
    d_in = w_in_bf.shape[1]
    d_conv = (d_in - d_q - 2 * d_kv - 2 * d_model) // 3
    tm = TM_PROJ
    nseq = seq // tm

    def row(width):
        return pl.BlockSpec((tm, width), lambda i: (i, 0))

    table = pl.BlockSpec((tm, HEAD_DIM), lambda i: (i % nseq, 0))
    kern = functools.partial(_inproj_kernel, d_conv=d_conv, d_q=d_q, d_kv=d_kv,
                             d_model=d_model)
    out_shape = [jax.ShapeDtypeStruct((t, w), BF16)
                 for w in (d_conv, d_conv, d_q, d_kv, d_kv, d_model, d_model)]
    return pl.pallas_call(
        kern,
        grid=(t // tm,),
        in_specs=[row(d_model), _resident((1, d_model)), _resident((d_model, d_in)),
                  _resident((1, HEAD_DIM)), _resident((1, HEAD_DIM)),
                  table, table, table, table, table, table],
        out_specs=[row(d_conv), row(d_conv), row(d_q), row(d_kv), row(d_kv),
                   row(d_model), row(d_model)],
        out_shape=out_shape,
        compiler_params=_cparams(("arbitrary",)),
        name="inproj",
    )(x2, g1, w_in_bf, gq, gk, *tables_q, *tables_k)


def _attn_kernel(q_ref, k_ref, v_ref, o_ref, qs_ref, vext_ref, m_ref, acc_ref,
                 *, tq, chunks, group):
    @pl.when(pl.program_id(2) == 0)
    def _():
        vext_ref[:, :HEAD_DIM] = v_ref[...]
        vext_ref[:, HEAD_DIM:] = jnp.ones((vext_ref.shape[0], HEAD_DIM), BF16)

    for g in range(group):
        qs_ref[g * tq:(g + 1) * tq, :] = q_ref[:, g * HEAD_DIM:(g + 1) * HEAD_DIM]
    m_ref[...] = jnp.full(m_ref.shape, -jnp.inf, F32)
    acc_ref[...] = jnp.zeros(acc_ref.shape, F32)

    lo = 0
    for tk in chunks:
        keys = slice(lo, lo + tk)
        lo += tk
        s = lax.dot_general(qs_ref[...], k_ref[keys, :], (((1,), (1,)), ((), ())),
                            preferred_element_type=F32)
        m_prev = m_ref[...]
        m_new = jnp.maximum(m_prev, jnp.max(s, axis=-1, keepdims=True))
        alpha = jnp.exp2(m_prev - m_new)
        p = jnp.concatenate(
            [jnp.exp2(s[:, c * LANES:(c + 1) * LANES] - m_new) for c in range(tk // LANES)],
            axis=1).astype(BF16)
        pv = jnp.dot(p, vext_ref[keys, :], preferred_element_type=F32)
        acc_ref[...] = jnp.concatenate([alpha, alpha], axis=1) * acc_ref[...] + pv
        m_ref[...] = m_new

    out = acc_ref[:, :HEAD_DIM] / acc_ref[:, HEAD_DIM:]
    for g in range(group):
        o_ref[:, g * HEAD_DIM:(g + 1) * HEAD_DIM] = out[g * tq:(g + 1) * tq].astype(BF16)


def _attention(q, k, v, *, batch, seq):
    t = q.shape[0]
    group = N_HEADS // N_KV_HEADS
    tq, tk = TQ_ATTN, TK_ATTN
    nq = seq // tq
    gw = group * HEAD_DIM
    chunks = (tk // 2,) + (tk,) * (seq // tk - 1) + (tk // 2,)
    kern = functools.partial(_attn_kernel, tq=tq, chunks=chunks, group=group)
    return pl.pallas_call(
        kern,
        grid=(batch, N_KV_HEADS, nq),
        in_specs=[pl.BlockSpec((tq, gw), lambda b, h, i: (b * nq + i, h)),
                  pl.BlockSpec((seq, HEAD_DIM), lambda b, h, i: (b, h)),
                  pl.BlockSpec((seq, HEAD_DIM), lambda b, h, i: (b, h))],
        out_specs=pl.BlockSpec((tq, gw), lambda b, h, i: (b * nq + i, h)),
        out_shape=jax.ShapeDtypeStruct((t, N_HEADS * HEAD_DIM), BF16),
        scratch_shapes=[pltpu.VMEM((group * tq, HEAD_DIM), BF16),
                        pltpu.VMEM((seq, 2 * HEAD_DIM), BF16),
                        pltpu.VMEM((group * tq, LANES), F32),
                        pltpu.VMEM((group * tq, 2 * HEAD_DIM), F32)],
        compiler_params=_cparams(("arbitrary", "arbitrary", "arbitrary")),
        name="attention",
    )(q, k, v)


def _route(logits):
    rows = logits.shape[0]
    lane = lax.broadcasted_iota(jnp.int32, (rows, LANES), 1).astype(F32)
    neg = -jnp.inf
    big = float(2 * LANES)
    is_group = lane < N_GROUPS
    gl = jnp.where(is_group, logits, neg)
    gmax = jnp.max(gl, axis=-1, keepdims=True)
    gidx = jnp.min(jnp.where(gl == gmax, lane, big), axis=-1, keepdims=True)
    gsum = jnp.sum(jnp.where(is_group, jnp.exp(logits - gmax), 0.0), axis=-1,
                   keepdims=True)
    pg = 1.0 / gsum
    lane_group = jnp.floor(lane * (1.0 / EXPERTS_PER_GROUP)) - 1.0
    mine = (lane_group == gidx) & (lane >= N_GROUPS) & (lane < N_GROUPS + N_EXPERTS)
    sel = jnp.where(mine, logits, neg)
    v1 = jnp.max(sel, axis=-1, keepdims=True)
    i1 = jnp.min(jnp.where(sel == v1, lane, big), axis=-1, keepdims=True)
    sel2 = jnp.where(lane == i1, neg, sel)
    v2 = jnp.max(sel2, axis=-1, keepdims=True)
    i2 = jnp.min(jnp.where(sel2 == v2, lane, big), axis=-1, keepdims=True)
    t2 = jnp.exp(v2 - v1)
    den = 1.0 + t2
    wgt1 = pg * (1.0 / den)
    wgt2 = pg * (t2 / den)
    e1 = i1 - N_GROUPS
    e2 = i2 - N_GROUPS
    meta = jnp.where(lane == 0, e1,
                     jnp.where(lane == 1, e2,
                               jnp.where(lane == 2, wgt1,
                                         jnp.where(lane == 3, wgt2, 0.0))))
    picked = jnp.where((lane == e1) | (lane == e2), 1.0, 0.0)
    return meta, jnp.sum(picked, axis=0, keepdims=True)


def _post_kernel(x_ref, cb_ref, z_ref, zprev_ref, znext_ref, o_ref, sgc_ref, sga_ref,
                 cw_ref, wc_ref, wa_ref, wo_ref, g2_ref, wr_ref,
                 x1_ref, u2_ref, meta_ref, cnt_ref, *, tm, nseq):
    i = pl.program_id(0)
    at_start = (i % nseq) == 0
    at_end = (i % nseq) == nseq - 1
    rowid = lax.broadcasted_iota(jnp.int32, (tm, 1), 0)

    @pl.when(i == 0)
    def _():
        cnt_ref[...] = jnp.zeros(cnt_ref.shape, F32)

    y_attn = jnp.dot(o_ref[...], wa_ref[...], preferred_element_type=F32)

    y_conv = None
    for c in range(z_ref.shape[1] // MXU_DIM):
        sl = slice(c * MXU_DIM, (c + 1) * MXU_DIM)
        z = z_ref[:, sl].astype(F32)
        prev_row = zprev_ref[BF16_SUBLANES - 1:BF16_SUBLANES, sl].astype(F32)
        next_row = znext_ref[0:1, sl].astype(F32)
        prev_row = jnp.where(at_start, 0.0, prev_row)
        next_row = jnp.where(at_end, 0.0, next_row)
        zp = jnp.where(rowid == 0, prev_row, pltpu.roll(z, 1, axis=0))
        zn = jnp.where(rowid == tm - 1, next_row, pltpu.roll(z, tm - 1, axis=0))
        conv = cw_ref[0:1, sl] * zp + cw_ref[1:2, sl] * z + cw_ref[2:3, sl] * zn
        cbz = (cb_ref[:, sl].astype(F32) * conv).astype(BF16)
        part = jnp.dot(cbz, wc_ref[sl, :], preferred_element_type=F32)
        y_conv = part if y_conv is None else y_conv + part
    merged = (sgc_ref[...].astype(F32) * y_conv
              + sga_ref[...].astype(F32) * y_attn).astype(BF16)
    x1 = x_ref[...] + jnp.dot(merged, wo_ref[...], preferred_element_type=F32)
    x1_ref[...] = x1

    ms = jnp.mean(x1 * x1, axis=-1, keepdims=True)
    u2 = x1 * lax.rsqrt(ms + EPS) * g2_ref[...]
    _store_row_tiles(u2_ref, u2)

    u2_hi = u2.astype(BF16)
    u2_lo = (u2 - u2_hi.astype(F32)).astype(BF16)
    hi_part = jnp.dot(u2_hi, wr_ref[...], preferred_element_type=F32)
    lo_part = jnp.dot(u2_lo, wr_ref[:, :LANES], preferred_element_type=F32)
    logits = hi_part[:, :LANES] + (hi_part[:, LANES:] + lo_part)
    meta, picked = _route(logits)
    meta_ref[...] = meta
    cnt_ref[...] += picked


def _post(x2, cb, z, o, sgc, sga, conv_w, wc, wa, wo, g2, wr, *, seq):
    t, d_model = x2.shape
    tm = TM_PROJ
    nseq = seq // tm
    hb = tm // BF16_SUBLANES
    nhalo = t // BF16_SUBLANES
    d_conv = cb.shape[1]
    d_q = o.shape[1]

    def row(width):
        return pl.BlockSpec((tm, width), lambda i: (i, 0))

    kern = functools.partial(_post_kernel, tm=tm, nseq=nseq)
    return pl.pallas_call(
        kern,
        grid=(t // tm,),
        in_specs=[row(d_model), row(d_conv), row(d_conv),
                  pl.BlockSpec((BF16_SUBLANES, d_conv),
                               lambda i: (jnp.maximum(i * hb - 1, 0), 0)),
                  pl.BlockSpec((BF16_SUBLANES, d_conv),
                               lambda i: (jnp.minimum((i + 1) * hb, nhalo - 1), 0)),
                  row(d_q), row(d_model), row(d_model),
                  _resident(conv_w.shape), _resident(wc.shape), _resident(wa.shape),
                  _resident(wo.shape), _resident(g2.shape), _resident(wr.shape)],
        out_specs=[row(d_model),
                   pl.BlockSpec(_row_tiles((tm, d_model)), lambda i: (i, 0)),
                   row(LANES),
                   pl.BlockSpec((1, LANES), lambda i: (0, 0))],
        out_shape=[jax.ShapeDtypeStruct((t, d_model), F32),
                   jax.ShapeDtypeStruct(_row_tiles((t, d_model)), F32),
                   jax.ShapeDtypeStruct((t, LANES), F32),
                   jax.ShapeDtypeStruct((1, LANES), F32)],
        compiler_params=_cparams(("arbitrary",)),
        name="post",
    )(x2, cb, z, z, z, o, sgc, sga, conv_w, wc, wa, wo, g2, wr)


def _rank_kernel(meta_ref, cnt_ref, l0_ref, l1_ref, runs_ref,
                 carry_ref, pstart_ref, lower_ref, *, tb):
    i = pl.program_id(0)
    lane = lax.broadcasted_iota(jnp.int32, (tb, LANES), 1).astype(F32)
    meta = meta_ref[...]
    oh1 = jnp.where(lane == meta[:, 0:1], 1.0, 0.0)
    oh2 = jnp.where(lane == meta[:, 1:2], 1.0, 0.0)
    c = oh1 + oh2

    @pl.when(i == 0)
    def _():
        cnt = cnt_ref[...]
        high = jnp.floor(cnt * (1.0 / DIGIT))
        pstart_ref[...] = _lane_prefix(high) * DIGIT + _lane_prefix(cnt - high * DIGIT)
        carry_ref[...] = jnp.zeros(carry_ref.shape, F32)
        r = lax.broadcasted_iota(jnp.int32, (tb, tb), 0)
        col = lax.broadcasted_iota(jnp.int32, (tb, tb), 1)
        lower_ref[...] = jnp.where(col < r, 1.0, 0.0).astype(BF16)

    prefix = jnp.dot(lower_ref[...], c.astype(BF16), preferred_element_type=F32)
    run_start = carry_ref[...] + pstart_ref[...]
    run_len = jnp.sum(c, axis=0, keepdims=True)

    nchunk = jnp.floor((run_len + (RUN_CHUNK - 1)) * (1.0 / RUN_CHUNK))
    local = prefix + _lane_prefix(nchunk) * RUN_CHUNK

    v0 = jnp.sum(oh1 * local, axis=-1, keepdims=True)
    v1 = jnp.sum(oh2 * local, axis=-1, keepdims=True)
    rows = jnp.transpose(jnp.where(lane == 0, v0, jnp.where(lane == 1, v1, 0.0)))
    l0_ref[...] = rows[0:1, :].astype(jnp.int32)
    l1_ref[...] = rows[1:2, :].astype(jnp.int32)
    row = lax.broadcasted_iota(jnp.int32, (8, LANES), 0)
    runs = jnp.where(row == 0, run_start, jnp.where(row == 1, run_len, 0.0))
    runs_ref[...] = runs.astype(jnp.int32)
    carry_ref[...] += run_len


def _rank(meta, cnt):
    t = meta.shape[0]
    tb = TB_RANK
    kern = functools.partial(_rank_kernel, tb=tb)
    per_token = pl.BlockSpec((1, tb), lambda i: (0, i))
    return pl.pallas_call(
        kern,
        grid=(t // tb,),
        in_specs=[pl.BlockSpec((tb, LANES), lambda i: (i, 0)),
                  pl.BlockSpec((1, LANES), lambda i: (0, 0))],
        out_specs=[per_token] * 2 + [pl.BlockSpec((8, LANES), lambda i: (i, 0))],
        out_shape=[jax.ShapeDtypeStruct((1, t), jnp.int32)] * 2
                  + [jax.ShapeDtypeStruct((t // tb * 8, LANES), jnp.int32)],
        scratch_shapes=[pltpu.VMEM((1, LANES), F32), pltpu.VMEM((1, LANES), F32),
                        pltpu.VMEM((tb, tb), BF16)],
        compiler_params=_cparams(("arbitrary",)),
        name="rank",
    )(meta, cnt)


def _dispatch_kernel(local0_sm, local1_sm, full_src_sm, full_dst_sm, nfull_sm,
                     rem_src_sm, rem_dst_sm, rem_len_sm,
                     u2_ref, xs_ref, buf_ref, sem, *, tb, max_full):
    step = pl.program_id(0)
    nsteps = pl.num_programs(0)
    base = step * tb
    slot = step % 2
    tile_buf = buf_ref.at[slot]

    def wait_tile(s):
        pltpu.make_async_copy(_rows(buf_ref.at[s], 0, tb * TOP_K),
                              _rows(xs_ref, 0, tb * TOP_K), sem.at[s]).wait()

    @pl.when(step >= 2)
    def _():
        wait_tile(slot)

    def move(r, carry):
        row = _rows(u2_ref, r)[...]
        _rows(tile_buf, local0_sm[base + r])[...] = row
        _rows(tile_buf, local1_sm[base + r])[...] = row
        return carry

    lax.fori_loop(0, tb, move, 0, unroll=8)

    def full_chunk(c, carry):
        pltpu.make_async_copy(
            _rows(tile_buf, full_src_sm[step * max_full + c], RUN_CHUNK),
            _rows(xs_ref, full_dst_sm[step * max_full + c], RUN_CHUNK),
            sem.at[slot]).start()
        return carry

    lax.fori_loop(0, nfull_sm[step], full_chunk, 0)

    def remainder(e, carry):
        idx = step * N_EXPERTS + e
        src, dst, length = rem_src_sm[idx], rem_dst_sm[idx], rem_len_sm[idx]
        for bit in range(RUN_CHUNK.bit_length() - 1):
            size = 1 << bit
            higher = lax.shift_left(lax.shift_right_logical(length, bit + 1), bit + 1)

            @pl.when((lax.shift_right_logical(length, bit) & 1) == 1)
            def _():
                pltpu.make_async_copy(_rows(tile_buf, src + higher, size),
                                      _rows(xs_ref, dst + higher, size),
                                      sem.at[slot]).start()
        return carry

    lax.fori_loop(0, N_EXPERTS, remainder, 0)

    @pl.when(step == nsteps - 1)
    def _():
        @pl.when(step >= 1)
        def _():
            wait_tile(1 - slot)

        wait_tile(slot)


def _tile_tables(run_start, run_len, *, tb):
    shift = RUN_CHUNK.bit_length() - 1
    nchunk = (run_len + RUN_CHUNK - 1) >> shift
    cum = jnp.cumsum(nchunk, axis=1)
    first = cum - nchunk
    local_start = first * RUN_CHUNK

    def flat_list(count, max_count, value_at):
        ccum = jnp.cumsum(count, axis=1)
        cfirst = (ccum - count)[:, None, :]
        c_idx = jnp.arange(max_count, dtype=jnp.int32)[None, :, None]
        owns = (cfirst <= c_idx) & (c_idx < ccum[:, None, :])
        return [jnp.sum(jnp.where(owns, v[:, None, :] + (c_idx - cfirst) * RUN_CHUNK, 0),
                        axis=2).astype(jnp.int32) for v in value_at], ccum[:, -1]

    max_chunks = tb * TOP_K // RUN_CHUNK + N_EXPERTS
    (chunk_src,), tile_chunks = flat_list(nchunk, max_chunks, [run_start])
    max_full = tb * TOP_K // RUN_CHUNK
    nfull = run_len >> shift
    (full_src, full_dst), tile_full = flat_list(nfull, max_full, [local_start, run_start])
    whole = nfull * RUN_CHUNK
    return dict(
        max_chunks=max_chunks, chunk_src=chunk_src.reshape(-1),
        tile_chunks=tile_chunks.astype(jnp.int32),
        max_full=max_full, full_src=full_src.reshape(-1), full_dst=full_dst.reshape(-1),
        tile_full=tile_full.astype(jnp.int32),
        rem_src=(local_start + whole).reshape(-1).astype(jnp.int32),
        rem_dst=(run_start + whole).reshape(-1).astype(jnp.int32),
        rem_len=(run_len - whole).reshape(-1).astype(jnp.int32))


def _dispatch(local0, local1, tables, u2, *, data_rows):
    t = u2.shape[0] // ROW_TILE
    tb = TB_RANK
    d_model = ROW_TILE * LANES
    kern = functools.partial(_dispatch_kernel, tb=tb, max_full=tables["max_full"])
    buf_rows = tables["max_chunks"] * RUN_CHUNK
    return pl.pallas_call(
        kern,
        grid_spec=pltpu.PrefetchScalarGridSpec(
            num_scalar_prefetch=8,
            grid=(t // tb,),
            in_specs=[pl.BlockSpec((tb * ROW_TILE, LANES), lambda i, *_: (i, 0))],
            out_specs=pl.BlockSpec(memory_space=pl.ANY),
            scratch_shapes=[pltpu.VMEM((2,) + _row_tiles((buf_rows, d_model)), F32),
                            pltpu.SemaphoreType.DMA((2,))]),
        out_shape=jax.ShapeDtypeStruct((data_rows * ROW_TILE, LANES), F32),
        compiler_params=_cparams(("arbitrary",)),
        name="dispatch",
    )(local0, local1, tables["full_src"], tables["full_dst"], tables["tile_full"],
      tables["rem_src"], tables["rem_dst"], tables["rem_len"], u2)


def _experts_kernel(start_sm, nvalid_sm, expert_sm, first_sm, ahead_sm, slot_sm, head_sm,
                    xs_hbm, w1_hbm, w3_hbm, w2_hbm, ys_hbm,
                    xbuf_ref, ybuf_ref, zrow_ref, w1f_ref, w3f_ref, w2f_ref,
                    w13b_ref, w2b_ref, wsem, isem, osem, zsem,
                    *, d_expert, blk, data_rows):
    v = pl.program_id(0)
    nsteps = pl.num_programs(0)
    nbits = blk.bit_length()

    def block_copies(step, inbound, visit):
        n, start, slot = nvalid_sm[step], start_sm[step], lax.rem(step, 2)
        for bit in range(nbits):
            size = 1 << bit
            higher = lax.shift_left(lax.shift_right_logical(n, bit + 1), bit + 1)

            @pl.when((lax.shift_right_logical(n, bit) & 1) == 1)
            def _():
                if inbound:
                    visit(pltpu.make_async_copy(_rows(xs_hbm, start + higher, size),
                                                _rows(xbuf_ref.at[slot], higher, size),
                                                isem.at[slot]))
                else:
                    visit(pltpu.make_async_copy(_rows(ybuf_ref.at[slot], higher, size),
                                                _rows(ys_hbm, start + higher, size),
                                                osem.at[slot]))

    def weight_copies(expert, slot):
        return [pltpu.make_async_copy(src.at[expert], dst.at[slot], wsem.at[slot])
                for src, dst in ((w1_hbm, w1f_ref), (w3_hbm, w3f_ref), (w2_hbm, w2f_ref))]

    spare_fill = pltpu.make_async_copy(zrow_ref, _rows(ys_hbm, data_rows, SPARE_ROWS), zsem)

    @pl.when(v == 0)
    def _():
        xbuf_ref[...] = jnp.zeros(xbuf_ref.shape, F32)
        zrow_ref[...] = jnp.zeros(zrow_ref.shape, F32)
        spare_fill.start()
        for j in range(WEIGHT_RING - 1):
            @pl.when(head_sm[j] >= 0)
            def _():
                for c in weight_copies(head_sm[j], j):
                    c.start(priority=WEIGHT_DMA_PRIORITY)
        block_copies(v, True, lambda c: c.start())

    @pl.when(v + 1 < nsteps)
    def _():
        block_copies(v + 1, True, lambda c: c.start())

    @pl.when(first_sm[v] == 1)
    def _():
        slot = slot_sm[v]
        for c in weight_copies(expert_sm[v], slot):
            c.wait()

        @pl.when(ahead_sm[v] >= 0)
        def _():
            ahead_slot = lax.rem(slot + (WEIGHT_RING - 1), WEIGHT_RING)
            for c in weight_copies(ahead_sm[v], ahead_slot):
                c.start(priority=WEIGHT_DMA_PRIORITY)

        w13b_ref[:, :d_expert] = w1f_ref[slot].astype(BF16)
        w13b_ref[:, d_expert:] = w3f_ref[slot].astype(BF16)
        w2b_ref[...] = w2f_ref[slot].astype(BF16)

    block_copies(v, True, lambda c: c.wait())

    @pl.when(v >= 2)
    def _():
        block_copies(v - 2, False, lambda c: c.wait())

    @pl.when(nvalid_sm[v] > 0)
    def _():
        slot = lax.rem(v, 2)
        xb = _load_row_tiles(xbuf_ref.at[slot]).astype(BF16)
        h = jnp.dot(xb, w13b_ref[...], preferred_element_type=F32)
        h1 = h[:, :d_expert]
        h3 = h[:, d_expert:]
        a = (h1 * _sigmoid(h1) * h3).astype(BF16)
        _store_row_tiles(ybuf_ref.at[slot],
                         jnp.dot(a, w2b_ref[...], preferred_element_type=F32))

    block_copies(v, False, lambda c: c.start())

    @pl.when(v == nsteps - 1)
    def _():
        @pl.when(v >= 1)
        def _():
            block_copies(v - 1, False, lambda c: c.wait())

        block_copies(v, False, lambda c: c.wait())
        spare_fill.wait()


def _experts(sched, xs, w1, w3, w2, *, blk):
    data_rows = xs.shape[0] // ROW_TILE
    d_model, d_expert = w1.shape[-2:]
    kern = functools.partial(_experts_kernel, d_expert=d_expert, blk=blk,
                             data_rows=data_rows)
    hbm = pl.BlockSpec(memory_space=pl.ANY)
    block_buf = pltpu.VMEM((2,) + _row_tiles((blk, d_model)), F32)
    return pl.pallas_call(
        kern,
        grid_spec=pltpu.PrefetchScalarGridSpec(
            num_scalar_prefetch=7,
            grid=(sched["start"].shape[0],),
            in_specs=[hbm, hbm, hbm, hbm],
            out_specs=hbm,
            scratch_shapes=[block_buf, block_buf,
                            pltpu.VMEM(_row_tiles((SPARE_ROWS, d_model)), F32),
                            pltpu.VMEM((WEIGHT_RING, d_model, d_expert), F32),
                            pltpu.VMEM((WEIGHT_RING, d_model, d_expert), F32),
                            pltpu.VMEM((WEIGHT_RING, d_expert, d_model), F32),
                            pltpu.VMEM((d_model, 2 * d_expert), BF16),
                            pltpu.VMEM((d_expert, d_model), BF16),
                            pltpu.SemaphoreType.DMA((WEIGHT_RING,)),
                            pltpu.SemaphoreType.DMA((2,)), pltpu.SemaphoreType.DMA((2,)),
                            pltpu.SemaphoreType.DMA]),
        out_shape=jax.ShapeDtypeStruct(((data_rows + SPARE_ROWS) * ROW_TILE, LANES), F32),
        compiler_params=_cparams(("arbitrary",)),
        name="experts",
    )(sched["start"], sched["nvalid"], sched["expert"], sched["first"], sched["ahead"],
      sched["slot"], sched["head"], xs, w1, w3, w2)


def _expert_schedule(counts, *, blk, data_blocks):
    seg_end = jnp.cumsum(counts)
    seg_start = seg_end - counts
    nblk = (counts + blk - 1) // blk
    vend = jnp.cumsum(nblk)
    vfirst = vend - nblk
    nsteps_live = vend[-1]
    total = data_blocks + N_EXPERTS
    v_idx = jnp.arange(total, dtype=jnp.int32)
    vi = jnp.maximum(jnp.minimum(v_idx, nsteps_live - 1), 0)
    expert = jnp.minimum(jnp.sum(vend[None, :] <= vi[:, None], axis=1),
                         N_EXPERTS - 1).astype(jnp.int32)
    live = v_idx < nsteps_live
    within = vi - vfirst[expert]
    start = seg_start[expert] + within * blk
    nvalid = jnp.where(live, jnp.clip(seg_end[expert] - start, 0, blk), 0)
    first = live & (within == 0)
    has_rows = nblk > 0
    used_rank = jnp.cumsum(has_rows) - 1
    j_idx = jnp.arange(N_EXPERTS + WEIGHT_RING, dtype=jnp.int32)
    e_idx = jnp.arange(N_EXPERTS, dtype=jnp.int32)
    hit = has_rows[None, :] & (used_rank[None, :] == j_idx[:, None])
    used_list = jnp.sum(jnp.where(hit, e_idx[None, :] + 1, 0), axis=1) - 1
    rank_v = used_rank[expert]
    ahead = jnp.where(live, used_list[rank_v + (WEIGHT_RING - 1)], -1)
    as_i32 = lambda a: a.astype(jnp.int32)
    return dict(start=as_i32(jnp.where(live, start, 0)), nvalid=as_i32(nvalid),
                expert=expert, first=as_i32(first), ahead=as_i32(ahead),
                slot=as_i32(rank_v % WEIGHT_RING), head=as_i32(used_list[:WEIGHT_RING - 1]))


def _combine_kernel(local0_sm, local1_sm, chunk_src_sm, nchunk_sm,
                    x1_ref, meta_ref, ys_ref, out_ref,
                    buf_ref, g0_ref, g1_ref, sem, *, tb, max_chunks):
    step = pl.program_id(0)
    nsteps = pl.num_programs(0)

    def start_chunks(tile):
        slot = tile % 2

        def per_chunk(c, carry):
            pltpu.make_async_copy(
                _rows(ys_ref, chunk_src_sm[tile * max_chunks + c], RUN_CHUNK),
                _rows(buf_ref.at[slot], c * RUN_CHUNK, RUN_CHUNK),
                sem.at[slot]).start()
            return carry

        lax.fori_loop(0, nchunk_sm[tile], per_chunk, 0)

    def wait_chunks(tile):
        slot = tile % 2
        n = nchunk_sm[tile]
        for bit in range(max_chunks.bit_length()):
            @pl.when((lax.shift_right_logical(n, bit) & 1) == 1)
            def _():
                rows = RUN_CHUNK << bit
                pltpu.make_async_copy(_rows(ys_ref, 0, rows),
                                      _rows(buf_ref.at[slot], 0, rows),
                                      sem.at[slot]).wait()

    @pl.when(step == 0)
    def _():
        start_chunks(step)

    @pl.when(step + 1 < nsteps)
    def _():
        start_chunks(step + 1)

    wait_chunks(step)

    base = step * tb
    tile_buf = buf_ref.at[step % 2]

    def move(r, carry):
        for g_ref, local_sm in ((g0_ref, local0_sm), (g1_ref, local1_sm)):
            _rows(g_ref, r)[...] = _rows(tile_buf, local_sm[base + r])[...]
        return carry

    lax.fori_loop(0, tb, move, 0, unroll=8)
    meta = meta_ref[...]
    moe = (_load_row_tiles(g0_ref) * meta[:, 2:3]
           + _load_row_tiles(g1_ref) * meta[:, 3:4])
    out_ref[...] = x1_ref[...] + moe


def _combine(local0, local1, tables, x1, meta, ys):
    t, d_model = x1.shape
    tb = TB_RANK
    max_chunks = tables["max_chunks"]
    kern = functools.partial(_combine_kernel, tb=tb, max_chunks=max_chunks)
    buf_rows = max_chunks * RUN_CHUNK
    return pl.pallas_call(
        kern,
        grid_spec=pltpu.PrefetchScalarGridSpec(
            num_scalar_prefetch=4,
            grid=(t // tb,),
            in_specs=[pl.BlockSpec((tb, d_model), lambda i, *_: (i, 0)),
                      pl.BlockSpec((tb, LANES), lambda i, *_: (i, 0)),
                      pl.BlockSpec(memory_space=pl.ANY)],
            out_specs=pl.BlockSpec((tb, d_model), lambda i, *_: (i, 0)),
            scratch_shapes=[pltpu.VMEM((2,) + _row_tiles((buf_rows, d_model)), F32),
                            pltpu.VMEM(_row_tiles((tb, d_model)), F32),
                            pltpu.VMEM(_row_tiles((tb, d_model)), F32),
                            pltpu.SemaphoreType.DMA((2,))]),
        out_shape=jax.ShapeDtypeStruct((t, d_model), F32),
        compiler_params=_cparams(("arbitrary",)),
        name="combine",
    )(local0, local1, tables["chunk_src"], tables["tile_chunks"], x1, meta, ys)


def _layer(h2, *, batch, seq, norm1_g, w_in, conv_w, q_norm_g, k_norm_g,
           w_conv_out, w_attn_out, w_o, norm2_g, w_group, w_router, w1, w3, w2):
    t, d_model = h2.shape
    c, sa, sb = _rope_tables(seq)
    scale = HEAD_DIM ** -0.5 * LOG2_E
    tables_q = tuple(jnp.asarray(tab * np.float32(scale)) for tab in (c, sa, sb))
    tables_k = tuple(jnp.asarray(tab) for tab in (c, sa, sb))

    cb, z, q, k, v, sgc, sga = _inproj(
        h2, norm1_g[None, :], w_in.astype(BF16), q_norm_g[None, :], k_norm_g[None, :],
        tables_q, tables_k, seq=seq)
    o = _attention(q, k, v, batch=batch, seq=seq)

    n_route = N_GROUPS + N_EXPERTS
    wr = jnp.concatenate(
        [w_group, w_router, jnp.zeros((d_model, LANES - n_route), F32)], axis=1)
    wr_hi = wr.astype(BF16)
    wr = jnp.concatenate([wr_hi, (wr - wr_hi.astype(F32)).astype(BF16)], axis=1)
    x1, u2, meta, cnt = _post(h2, cb, z, o, sgc, sga, conv_w,
                              w_conv_out.astype(BF16), w_attn_out.astype(BF16),
                              w_o.astype(BF16), norm2_g[None, :], wr, seq=seq)

    blk = MOE_ROWS
    data_rows = t * TOP_K
    local0, local1, runs = _rank(meta, cnt)
    local0, local1 = local0.reshape(-1), local1.reshape(-1)
    runs = runs.reshape(-1, 8, LANES)
    tables = _tile_tables(runs[:, 0, :N_EXPERTS], runs[:, 1, :N_EXPERTS], tb=TB_RANK)
    sched = _expert_schedule(cnt[0, :N_EXPERTS].astype(jnp.int32), blk=blk,
                             data_blocks=data_rows // blk)

    xs = _dispatch(local0, local1, tables, u2, data_rows=data_rows)
    ys = _experts(sched, xs, w1, w3, w2, blk=blk)
    return _combine(local0, local1, tables, x1, meta, ys)


def kernel(x, norm1_g, w_in, conv_w, q_norm_g, k_norm_g, w_conv_out, w_attn_out, w_o,
           norm2_g, w_group, w_router, w1, w3, w2):
    batch, seq, d_model = x.shape
    h2 = x.reshape(batch * seq, d_model)
    for l in range(norm1_g.shape[0]):
        h2 = _layer(h2, batch=batch, seq=seq, norm1_g=norm1_g[l], w_in=w_in[l],
                    conv_w=conv_w[l], q_norm_g=q_norm_g[l], k_norm_g=k_norm_g[l],
                    w_conv_out=w_conv_out[l], w_attn_out=w_attn_out[l], w_o=w_o[l],
                    norm2_g=norm2_g[l], w_group=w_group[l], w_router=w_router[l],
                    w1=w1[l], w3=w3[l], w2=w2[l])
    return h2.reshape(batch, seq, d_model)
```

```python
import functools

import jax
import jax.numpy as jnp
import numpy as np
from jax import lax
from jax.experimental import pallas as pl
from jax.experimental.pallas import tpu as pltpu

F32 = jnp.float32
BF16 = jnp.bfloat16

GRID_W = 64
EPS = 1e-6
N_HEADS = 8
N_KV_HEADS = 2
HEAD_DIM = 128
ROPE_THETA = 10000.0
N_GROUPS = 8
EXPERTS_PER_GROUP = 8
N_EXPERTS = N_GROUPS * EXPERTS_PER_GROUP
TOP_K = 2
LOG2_E = 1.4426950408889634

LANES = 128
MXU_DIM = 256
BF16_SUBLANES = 16
V7X_VMEM_LIMIT_BYTES = 56000 * 1024

TM_PROJ = 512
TQ_ATTN = 512
TK_ATTN = 512
TB_RANK = 1024
RUN_CHUNK = 8
SPARE_ROWS = RUN_CHUNK
MOE_ROWS = 256
WEIGHT_RING = 3
WEIGHT_DMA_PRIORITY = 1
DIGIT = 256.0


def _cparams(sem):
    return pltpu.CompilerParams(dimension_semantics=sem,
                                vmem_limit_bytes=V7X_VMEM_LIMIT_BYTES)


ROW_TILE = 8


def _row_tiles(shape2d):
    rows, width = shape2d
    assert width == ROW_TILE * LANES
    return (rows * ROW_TILE, LANES)


def _rows(ref, r, n=1):
    return ref.at[pl.ds(pl.multiple_of(r * ROW_TILE, ROW_TILE), n * ROW_TILE)]


def _load_row_tiles(ref):
    rows = ref.shape[0] // ROW_TILE
    return jnp.concatenate(
        [ref[pl.ds(s, rows, stride=ROW_TILE), :] for s in range(ROW_TILE)], axis=1)


def _store_row_tiles(ref, value):
    rows = value.shape[0]
    for s in range(ROW_TILE):
        ref[pl.ds(s, rows, stride=ROW_TILE), :] = value[:, s * LANES:(s + 1) * LANES]


def _resident(shape):
    nd = len(shape)
    return pl.BlockSpec(shape, lambda *_: (0,) * nd, pipeline_mode=pl.Buffered(1))


def _lane_prefix(row):
    r = lax.broadcasted_iota(jnp.int32, (LANES, LANES), 0)
    col = lax.broadcasted_iota(jnp.int32, (LANES, LANES), 1)
    upper = jnp.where(r < col, 1.0, 0.0).astype(BF16)
    return jnp.dot(jnp.broadcast_to(row, (8, LANES)).astype(BF16), upper,
                   preferred_element_type=F32)[0:1, :]


def _head_norm_rope(xh, g, c, sa, sb):
    ms = jnp.mean(xh * xh, axis=-1, keepdims=True)
    y = xh * lax.rsqrt(ms + EPS) * g
    y_next = pltpu.roll(y, HEAD_DIM - 1, axis=1)
    y_prev = pltpu.roll(y, 1, axis=1)
    return y * c + y_next * sa + y_prev * sb


def _sigmoid(x):
    return 0.5 * jnp.tanh(0.5 * x) + 0.5


def _inproj_kernel(x_ref, g1_ref, w_ref, gq_ref, gk_ref,
                   cq_ref, saq_ref, sbq_ref, ck_ref, sak_ref, sbk_ref,
                   cb_ref, z_ref, q_ref, k_ref, v_ref, sgc_ref, sga_ref,
                   *, d_conv, d_q, d_kv, d_model):
    x = x_ref[...]
    ms = jnp.mean(x * x, axis=-1, keepdims=True)
    u = (x * lax.rsqrt(ms + EPS) * g1_ref[...]).astype(BF16)

    def proj(lo, width):
        return jnp.dot(u, w_ref[:, lo:lo + width], preferred_element_type=F32)

    o_cb, o_cc, o_cx = 0, d_conv, 2 * d_conv
    o_q = 3 * d_conv
    o_k = o_q + d_q
    o_v = o_k + d_kv
    o_gc = o_v + d_kv
    o_ga = o_gc + d_model

    sgc_ref[...] = _sigmoid(proj(o_gc, d_model)).astype(BF16)
    sga_ref[...] = _sigmoid(proj(o_ga, d_model)).astype(BF16)

    q = proj(o_q, d_q)
    gq = gq_ref[...]
    cq, saq, sbq = cq_ref[...], saq_ref[...], sbq_ref[...]
    for h in range(d_q // HEAD_DIM):
        sl = slice(h * HEAD_DIM, (h + 1) * HEAD_DIM)
        q_ref[:, sl] = _head_norm_rope(q[:, sl], gq, cq, saq, sbq).astype(BF16)

    k = proj(o_k, d_kv)
    gk = gk_ref[...]
    ck, sak, sbk = ck_ref[...], sak_ref[...], sbk_ref[...]
    for h in range(d_kv // HEAD_DIM):
        sl = slice(h * HEAD_DIM, (h + 1) * HEAD_DIM)
        k_ref[:, sl] = _head_norm_rope(k[:, sl], gk, ck, sak, sbk).astype(BF16)

    z_ref[...] = (proj(o_cc, d_conv) * proj(o_cx, d_conv)).astype(BF16)
    v_ref[...] = proj(o_v, d_kv).astype(BF16)
    cb_ref[...] = proj(o_cb, d_conv).astype(BF16)


def _rope_tables(seq):
    rows = seq // GRID_W
    axis_dim = HEAD_DIM // 2
    row = np.repeat(np.arange(rows, dtype=np.float32), GRID_W)
    col = np.tile(np.arange(GRID_W, dtype=np.float32), rows)
    inv = (np.float32(ROPE_THETA)
           ** (-np.arange(0, axis_dim, 2, dtype=np.float32) / np.float32(axis_dim)))
    ang = np.concatenate([row[:, None] * inv, col[:, None] * inv], axis=-1)
    ang = ang.astype(np.float32)
    cos, sin = np.cos(ang), np.sin(ang)
    zero = np.zeros_like(sin)
    c = np.repeat(cos, 2, axis=-1)
    sa = np.stack([-sin, zero], axis=-1).reshape(seq, HEAD_DIM)
    sb = np.stack([zero, sin], axis=-1).reshape(seq, HEAD_DIM)
    return c, sa, sb


def _inproj(x2, g1, w_in_bf, gq, gk, tables_q, tables_k, *, seq):
    t, d_model = x2.shape
    d_q = N_HEADS * HEAD_DIM
    d_kv = N_KV_HEADS * HEAD_DIM
    d_in = w_in_bf.shape[1]
    d_conv = (d_in - d_q - 2 * d_kv - 2 * d_model) // 3
    tm = TM_PROJ
    nseq = seq // tm

    def row(width):
        return pl.BlockSpec((tm, width), lambda i: (i, 0))

    table = pl.BlockSpec((tm, HEAD_DIM), lambda i: (i % nseq, 0))
    kern = functools.partial(_inproj_kernel, d_conv=d_conv, d_q=d_q, d_kv=d_kv,
                             d_model=d_model)
    out_shape = [jax.ShapeDtypeStruct((t, w), BF16)
                 for w in (d_conv, d_conv, d_q, d_kv, d_kv, d_model, d_model)]
    return pl.pallas_call(
        kern,
        grid=(t // tm,),
        in_specs=[row(d_model), _resident((1, d_model)), _resident((d_model, d_in)),
                  _resident((1, HEAD_DIM)), _resident((1, HEAD_DIM)),
                  table, table, table, table, table, table],
        out_specs=[row(d_conv), row(d_conv), row(d_q), row(d_kv), row(d_kv),
                   row(d_model), row(d_model)],
        out_shape=out_shape,
        compiler_params=_cparams(("arbitrary",)),
        name="inproj",
    )(x2, g1, w_in_bf, gq, gk, *tables_q, *tables_k)


def _attn_kernel(q_ref, k_ref, v_ref, o_ref, qs_ref, vext_ref, m_ref, acc_ref,
                 *, tq, chunks, group):
    @pl.when(pl.program_id(2) == 0)
    def _():
        vext_ref[:, :HEAD_DIM] = v_ref[...]
        vext_ref[:, HEAD_DIM:] = jnp.ones((vext_ref.shape[0], HEAD_DIM), BF16)

    for g in range(group):
        qs_ref[g * tq:(g + 1) * tq, :] = q_ref[:, g * HEAD_DIM:(g + 1) * HEAD_DIM]
    m_ref[...] = jnp.full(m_ref.shape, -jnp.inf, F32)
    acc_ref[...] = jnp.zeros(acc_ref.shape, F32)

    lo = 0
    for tk in chunks:
        keys = slice(lo, lo + tk)
        lo += tk
        s = lax.dot_general(qs_ref[...], k_ref[keys, :], (((1,), (1,)), ((), ())),
                            preferred_element_type=F32)
        m_prev = m_ref[...]
        m_new = jnp.maximum(m_prev, jnp.max(s, axis=-1, keepdims=True))
        alpha = jnp.exp2(m_prev - m_new)
        p = jnp.concatenate(
            [jnp.exp2(s[:, c * LANES:(c + 1) * LANES] - m_new) for c in range(tk // LANES)],
            axis=1).astype(BF16)
        pv = jnp.dot(p, vext_ref[keys, :], preferred_element_type=F32)
        acc_ref[...] = jnp.concatenate([alpha, alpha], axis=1) * acc_ref[...] + pv
        m_ref[...] = m_new

    out = acc_ref[:, :HEAD_DIM] / acc_ref[:, HEAD_DIM:]
    for g in range(group):
        o_ref[:, g * HEAD_DIM:(g + 1) * HEAD_DIM] = out[g * tq:(g + 1) * tq].astype(BF16)


def _attention(q, k, v, *, batch, seq):
    t = q.shape[0]
    group = N_HEADS // N_KV_HEADS
    tq, tk = TQ_ATTN, TK_ATTN
    nq = seq // tq
    gw = group * HEAD_DIM
    chunks = (tk // 2,) + (tk,) * (seq // tk - 1) + (tk // 2,)
    kern = functools.partial(_attn_kernel, tq=tq, chunks=chunks, group=group)
    return pl.pallas_call(
        kern,
        grid=(batch, N_KV_HEADS, nq),
        in_specs=[pl.BlockSpec((tq, gw), lambda b, h, i: (b * nq + i, h)),
                  pl.BlockSpec((seq, HEAD_DIM), lambda b, h, i: (b, h)),
                  pl.BlockSpec((seq, HEAD_DIM), lambda b, h, i: (b, h))],
        out_specs=pl.BlockSpec((tq, gw), lambda b, h, i: (b * nq + i, h)),
        out_shape=jax.ShapeDtypeStruct((t, N_HEADS * HEAD_DIM), BF16),
        scratch_shapes=[pltpu.VMEM((group * tq, HEAD_DIM), BF16),
                        pltpu.VMEM((seq, 2 * HEAD_DIM), BF16),
                        pltpu.VMEM((group * tq, LANES), F32),
                        pltpu.VMEM((group * tq, 2 * HEAD_DIM), F32)],
        compiler_params=_cparams(("arbitrary", "arbitrary", "arbitrary")),
        name="attention",
    )(q, k, v)


def _route(logits):
    rows = logits.shape[0]
    lane = lax.broadcasted_iota(jnp.int32, (rows, LANES), 1).astype(F32)
    neg = -jnp.inf
    big = float(2 * LANES)
    is_group = lane < N_GROUPS
    gl = jnp.where(is_group, logits, neg)
    gmax = jnp.max(gl, axis=-1, keepdims=True)
    gidx = jnp.min(jnp.where(gl == gmax, lane, big), axis=-1, keepdims=True)
    gsum = jnp.sum(jnp.where(is_group, jnp.exp(logits - gmax), 0.0), axis=-1,
                   keepdims=True)
    pg = 1.0 / gsum
    lane_group = jnp.floor(lane * (1.0 / EXPERTS_PER_GROUP)) - 1.0
    mine = (lane_group == gidx) & (lane >= N_GROUPS) & (lane < N_GROUPS + N_EXPERTS)
    sel = jnp.where(mine, logits, neg)
    v1 = jnp.max(sel, axis=-1, keepdims=True)
    i1 = jnp.min(jnp.where(sel == v1, lane, big), axis=-1, keepdims=True)
    sel2 = jnp.where(lane == i1, neg, sel)
    v2 = jnp.max(sel2, axis=-1, keepdims=True)
    i2 = jnp.min(jnp.where(sel2 == v2, lane, big), axis=-1, keepdims=True)
    t2 = jnp.exp(v2 - v1)
    den = 1.0 + t2
    wgt1 = pg * (1.0 / den)
    wgt2 = pg * (t2 / den)
    e1 = i1 - N_GROUPS
    e2 = i2 - N_GROUPS
    meta = jnp.where(lane == 0, e1,
                     jnp.where(lane == 1, e2,
                               jnp.where(lane == 2, wgt1,
                                         jnp.where(lane == 3, wgt2, 0.0))))
    picked = jnp.where((lane == e1) | (lane == e2), 1.0, 0.0)
    return meta, jnp.sum(picked, axis=0, keepdims=True)


def _post_kernel(x_ref, cb_ref, z_ref, zprev_ref, znext_ref, o_ref, sgc_ref, sga_ref,
                 cw_ref, wcat_ref, g2_ref, wr_ref,
                 x1_ref, u2_ref, meta_ref, cnt_ref, *, tm, nseq):
    d_conv, d_q = cb_ref.shape[1], o_ref.shape[1]
    wc_ref = wcat_ref.at[0:d_conv]
    wa_ref = wcat_ref.at[d_conv:d_conv + d_q]
    wo_ref = wcat_ref.at[d_conv + d_q:]
    i = pl.program_id(0)
    at_start = (i % nseq) == 0
    at_end = (i % nseq) == nseq - 1
    rowid = lax.broadcasted_iota(jnp.int32, (tm, 1), 0)

    @pl.when(i == 0)
    def _():
        cnt_ref[...] = jnp.zeros(cnt_ref.shape, F32)

    y_attn = jnp.dot(o_ref[...], wa_ref[...], preferred_element_type=F32)

    y_conv = None
    for c in range(z_ref.shape[1] // MXU_DIM):
        sl = slice(c * MXU_DIM, (c + 1) * MXU_DIM)
        z = z_ref[:, sl].astype(F32)
        prev_row = zprev_ref[BF16_SUBLANES - 1:BF16_SUBLANES, sl].astype(F32)
        next_row = znext_ref[0:1, sl].astype(F32)
        prev_row = jnp.where(at_start, 0.0, prev_row)
        next_row = jnp.where(at_end, 0.0, next_row)
        zp = jnp.where(rowid == 0, prev_row, pltpu.roll(z, 1, axis=0))
        zn = jnp.where(rowid == tm - 1, next_row, pltpu.roll(z, tm - 1, axis=0))
        conv = cw_ref[0:1, sl] * zp + cw_ref[1:2, sl] * z + cw_ref[2:3, sl] * zn
        cbz = (cb_ref[:, sl].astype(F32) * conv).astype(BF16)
        part = jnp.dot(cbz, wc_ref[sl, :], preferred_element_type=F32)
        y_conv = part if y_conv is None else y_conv + part
    merged = (sgc_ref[...].astype(F32) * y_conv
              + sga_ref[...].astype(F32) * y_attn).astype(BF16)
    x1 = x_ref[...] + jnp.dot(merged, wo_ref[...], preferred_element_type=F32)
    x1_ref[...] = x1

    ms = jnp.mean(x1 * x1, axis=-1, keepdims=True)
    u2 = x1 * lax.rsqrt(ms + EPS) * g2_ref[...]
    _store_row_tiles(u2_ref, u2)

    u2_hi = u2.astype(BF16)
    u2_lo = (u2 - u2_hi.astype(F32)).astype(BF16)
    hi_part = jnp.dot(u2_hi, wr_ref[...], preferred_element_type=F32)
    lo_part = jnp.dot(u2_lo, wr_ref[:, :LANES], preferred_element_type=F32)
    logits = hi_part[:, :LANES] + (hi_part[:, LANES:] + lo_part)
    meta, picked = _route(logits)
    meta_ref[...] = meta
    cnt_ref[...] += picked


def _post(x2, cb, z, o, sgc, sga, conv_w, wcat, g2, wr, *, seq):
    t, d_model = x2.shape
    tm = TM_PROJ
    nseq = seq // tm
    hb = tm // BF16_SUBLANES
    nhalo = t // BF16_SUBLANES
    d_conv = cb.shape[1]
    d_q = o.shape[1]

    def row(width):
        return pl.BlockSpec((tm, width), lambda i: (i, 0))

    kern = functools.partial(_post_kernel, tm=tm, nseq=nseq)
    return pl.pallas_call(
        kern,
        grid=(t // tm,),
        in_specs=[row(d_model), row(d_conv), row(d_conv),
                  pl.BlockSpec((BF16_SUBLANES, d_conv),
                               lambda i: (jnp.maximum(i * hb - 1, 0), 0)),
                  pl.BlockSpec((BF16_SUBLANES, d_conv),
                               lambda i: (jnp.minimum((i + 1) * hb, nhalo - 1), 0)),
                  row(d_q), row(d_model), row(d_model),
                  _resident(conv_w.shape), _resident(wcat.shape),
                  _resident(g2.shape), _resident(wr.shape)],
        out_specs=[row(d_model),
                   pl.BlockSpec(_row_tiles((tm, d_model)), lambda i: (i, 0)),
                   row(LANES),
                   pl.BlockSpec((1, LANES), lambda i: (0, 0))],
        out_shape=[jax.ShapeDtypeStruct((t, d_model), F32),
                   jax.ShapeDtypeStruct(_row_tiles((t, d_model)), F32),
                   jax.ShapeDtypeStruct((t, LANES), F32),
                   jax.ShapeDtypeStruct((1, LANES), F32)],
        compiler_params=_cparams(("arbitrary",)),
        name="post",
    )(x2, cb, z, z, z, o, sgc, sga, conv_w, wcat, g2, wr)


def _rank_kernel(meta_ref, cnt_ref, l0_ref, l1_ref, runs_ref,
                 carry_ref, pstart_ref, lower_ref, *, tb):
    i = pl.program_id(0)
    lane = lax.broadcasted_iota(jnp.int32, (tb, LANES), 1).astype(F32)
    meta = meta_ref[...]
    oh1 = jnp.where(lane == meta[:, 0:1], 1.0, 0.0)
    oh2 = jnp.where(lane == meta[:, 1:2], 1.0, 0.0)
    c = oh1 + oh2

    @pl.when(i == 0)
    def _():
        cnt = cnt_ref[...]
        high = jnp.floor(cnt * (1.0 / DIGIT))
        pstart_ref[...] = _lane_prefix(high) * DIGIT + _lane_prefix(cnt - high * DIGIT)
        carry_ref[...] = jnp.zeros(carry_ref.shape, F32)
        r = lax.broadcasted_iota(jnp.int32, (tb, tb), 0)
        col = lax.broadcasted_iota(jnp.int32, (tb, tb), 1)
        lower_ref[...] = jnp.where(col < r, 1.0, 0.0).astype(BF16)

    prefix = jnp.dot(lower_ref[...], c.astype(BF16), preferred_element_type=F32)
    run_start = carry_ref[...] + pstart_ref[...]
    run_len = jnp.sum(c, axis=0, keepdims=True)

    nchunk = jnp.floor((run_len + (RUN_CHUNK - 1)) * (1.0 / RUN_CHUNK))
    local = prefix + _lane_prefix(nchunk) * RUN_CHUNK

    v0 = jnp.sum(oh1 * local, axis=-1, keepdims=True)
    v1 = jnp.sum(oh2 * local, axis=-1, keepdims=True)
    rows = jnp.transpose(jnp.where(lane == 0, v0, jnp.where(lane == 1, v1, 0.0)))
    l0_ref[...] = rows[0:1, :].astype(jnp.int32)
    l1_ref[...] = rows[1:2, :].astype(jnp.int32)
    row = lax.broadcasted_iota(jnp.int32, (8, LANES), 0)
    runs = jnp.where(row == 0, run_start, jnp.where(row == 1, run_len, 0.0))
    runs_ref[...] = runs.astype(jnp.int32)
    carry_ref[...] += run_len


def _rank(meta, cnt):
    t = meta.shape[0]
    tb = TB_RANK
    kern = functools.partial(_rank_kernel, tb=tb)
    per_token = pl.BlockSpec((1, tb), lambda i: (0, i))
    return pl.pallas_call(
        kern,
        grid=(t // tb,),
        in_specs=[pl.BlockSpec((tb, LANES), lambda i: (i, 0)),
                  pl.BlockSpec((1, LANES), lambda i: (0, 0))],
        out_specs=[per_token] * 2 + [pl.BlockSpec((8, LANES), lambda i: (i, 0))],
        out_shape=[jax.ShapeDtypeStruct((1, t), jnp.int32)] * 2
                  + [jax.ShapeDtypeStruct((t // tb * 8, LANES), jnp.int32)],
        scratch_shapes=[pltpu.VMEM((1, LANES), F32), pltpu.VMEM((1, LANES), F32),
                        pltpu.VMEM((tb, tb), BF16)],
        compiler_params=_cparams(("arbitrary",)),
        name="rank",
    )(meta, cnt)


def _dispatch_kernel(local0_sm, local1_sm, full_src_sm, full_dst_sm, nfull_sm,
                     rem_src_sm, rem_dst_sm, rem_len_sm,
                     u2_ref, xs_ref, buf_ref, sem, *, tb, max_full):
    step = pl.program_id(0)
    nsteps = pl.num_programs(0)
    base = step * tb
    slot = step % 2
    tile_buf = buf_ref.at[slot]

    def wait_tile(s):
        pltpu.make_async_copy(_rows(buf_ref.at[s], 0, tb * TOP_K),
                              _rows(xs_ref, 0, tb * TOP_K), sem.at[s]).wait()

    @pl.when(step >= 2)
    def _():
        wait_tile(slot)

    def move(r, carry):
        row = _rows(u2_ref, r)[...]
        _rows(tile_buf, local0_sm[base + r])[...] = row
        _rows(tile_buf, local1_sm[base + r])[...] = row
        return carry

    lax.fori_loop(0, tb, move, 0, unroll=8)

    def full_chunk(c, carry):
        pltpu.make_async_copy(
            _rows(tile_buf, full_src_sm[step * max_full + c], RUN_CHUNK),
            _rows(xs_ref, full_dst_sm[step * max_full + c], RUN_CHUNK),
            sem.at[slot]).start()
        return carry

    lax.fori_loop(0, nfull_sm[step], full_chunk, 0)

    def remainder(e, carry):
        idx = step * N_EXPERTS + e
        src, dst, length = rem_src_sm[idx], rem_dst_sm[idx], rem_len_sm[idx]
        for bit in range(RUN_CHUNK.bit_length() - 1):
            size = 1 << bit
            higher = lax.shift_left(lax.shift_right_logical(length, bit + 1), bit + 1)

            @pl.when((lax.shift_right_logical(length, bit) & 1) == 1)
            def _():
                pltpu.make_async_copy(_rows(tile_buf, src + higher, size),
                                      _rows(xs_ref, dst + higher, size),
                                      sem.at[slot]).start()
        return carry

    lax.fori_loop(0, N_EXPERTS, remainder, 0)

    @pl.when(step == nsteps - 1)
    def _():
        @pl.when(step >= 1)
        def _():
            wait_tile(1 - slot)

        wait_tile(slot)


def _tile_tables(run_start, run_len, *, tb):
    shift = RUN_CHUNK.bit_length() - 1
    nchunk = (run_len + RUN_CHUNK - 1) >> shift
    cum = jnp.cumsum(nchunk, axis=1)
    first = cum - nchunk
    local_start = first * RUN_CHUNK

    def flat_list(count, max_count, value_at):
        ccum = jnp.cumsum(count, axis=1)
        cfirst = (ccum - count)[:, None, :]
        c_idx = jnp.arange(max_count, dtype=jnp.int32)[None, :, None]
        owns = (cfirst <= c_idx) & (c_idx < ccum[:, None, :])
        return [jnp.sum(jnp.where(owns, v[:, None, :] + (c_idx - cfirst) * RUN_CHUNK, 0),
                        axis=2).astype(jnp.int32) for v in value_at], ccum[:, -1]

    max_chunks = tb * TOP_K // RUN_CHUNK + N_EXPERTS
    (chunk_src,), tile_chunks = flat_list(nchunk, max_chunks, [run_start])
    max_full = tb * TOP_K // RUN_CHUNK
    nfull = run_len >> shift
    (full_src, full_dst), tile_full = flat_list(nfull, max_full, [local_start, run_start])
    whole = nfull * RUN_CHUNK
    return dict(
        max_chunks=max_chunks, chunk_src=chunk_src.reshape(-1),
        tile_chunks=tile_chunks.astype(jnp.int32),
        max_full=max_full, full_src=full_src.reshape(-1), full_dst=full_dst.reshape(-1),
        tile_full=tile_full.astype(jnp.int32),
        rem_src=(local_start + whole).reshape(-1).astype(jnp.int32),
        rem_dst=(run_start + whole).reshape(-1).astype(jnp.int32),
        rem_len=(run_len - whole).reshape(-1).astype(jnp.int32))


def _dispatch(local0, local1, tables, u2, *, data_rows):
    t = u2.shape[0] // ROW_TILE
    tb = TB_RANK
    d_model = ROW_TILE * LANES
    kern = functools.partial(_dispatch_kernel, tb=tb, max_full=tables["max_full"])
    buf_rows = tables["max_chunks"] * RUN_CHUNK
    return pl.pallas_call(
        kern,
        grid_spec=pltpu.PrefetchScalarGridSpec(
            num_scalar_prefetch=8,
            grid=(t // tb,),
            in_specs=[pl.BlockSpec((tb * ROW_TILE, LANES), lambda i, *_: (i, 0))],
            out_specs=pl.BlockSpec(memory_space=pl.ANY),
            scratch_shapes=[pltpu.VMEM((2,) + _row_tiles((buf_rows, d_model)), F32),
                            pltpu.SemaphoreType.DMA((2,))]),
        out_shape=jax.ShapeDtypeStruct((data_rows * ROW_TILE, LANES), F32),
        compiler_params=_cparams(("arbitrary",)),
        name="dispatch",
    )(local0, local1, tables["full_src"], tables["full_dst"], tables["tile_full"],
      tables["rem_src"], tables["rem_dst"], tables["rem_len"], u2)


def _experts_kernel(start_sm, nvalid_sm, expert_sm, first_sm, ahead_sm, slot_sm, head_sm,
                    xs_hbm, w1_hbm, w3_hbm, w2_hbm, ys_hbm,
                    xbuf_ref, ybuf_ref, zrow_ref, w1f_ref, w3f_ref, w2f_ref,
                    w13b_ref, w2b_ref, wsem, isem, osem, zsem,
                    *, d_expert, blk, data_rows):
    v = pl.program_id(0)
    nsteps = pl.num_programs(0)
    nbits = blk.bit_length()

    def block_copies(step, inbound, visit):
        n, start, slot = nvalid_sm[step], start_sm[step], lax.rem(step, 2)
        for bit in range(nbits):
            size = 1 << bit
            higher = lax.shift_left(lax.shift_right_logical(n, bit + 1), bit + 1)

            @pl.when((lax.shift_right_logical(n, bit) & 1) == 1)
            def _():
                if inbound:
                    visit(pltpu.make_async_copy(_rows(xs_hbm, start + higher, size),
                                                _rows(xbuf_ref.at[slot], higher, size),
                                                isem.at[slot]))
                else:
                    visit(pltpu.make_async_copy(_rows(ybuf_ref.at[slot], higher, size),
                                                _rows(ys_hbm, start + higher, size),
                                                osem.at[slot]))

    def weight_copies(expert, slot):
        return [pltpu.make_async_copy(src.at[expert], dst.at[slot], wsem.at[slot])
                for src, dst in ((w1_hbm, w1f_ref), (w3_hbm, w3f_ref), (w2_hbm, w2f_ref))]

    spare_fill = pltpu.make_async_copy(zrow_ref, _rows(ys_hbm, data_rows, SPARE_ROWS), zsem)

    @pl.when(v == 0)
    def _():
        xbuf_ref[...] = jnp.zeros(xbuf_ref.shape, F32)
        zrow_ref[...] = jnp.zeros(zrow_ref.shape, F32)
        spare_fill.start()
        for j in range(WEIGHT_RING - 1):
            @pl.when(head_sm[j] >= 0)
            def _():
                for c in weight_copies(head_sm[j], j):
                    c.start(priority=WEIGHT_DMA_PRIORITY)
        block_copies(v, True, lambda c: c.start())

    @pl.when(v + 1 < nsteps)
    def _():
        block_copies(v + 1, True, lambda c: c.start())

    @pl.when(first_sm[v] == 1)
    def _():
        slot = slot_sm[v]
        for c in weight_copies(expert_sm[v], slot):
            c.wait()

        @pl.when(ahead_sm[v] >= 0)
        def _():
            ahead_slot = lax.rem(slot + (WEIGHT_RING - 1), WEIGHT_RING)
            for c in weight_copies(ahead_sm[v], ahead_slot):
                c.start(priority=WEIGHT_DMA_PRIORITY)

        w13b_ref[:, :d_expert] = w1f_ref[slot].astype(BF16)
        w13b_ref[:, d_expert:] = w3f_ref[slot].astype(BF16)
        w2b_ref[...] = w2f_ref[slot].astype(BF16)

    block_copies(v, True, lambda c: c.wait())

    @pl.when(v >= 2)
    def _():
        block_copies(v - 2, False, lambda c: c.wait())

    @pl.when(nvalid_sm[v] > 0)
    def _():
        slot = lax.rem(v, 2)
        xb = _load_row_tiles(xbuf_ref.at[slot]).astype(BF16)
        h = jnp.dot(xb, w13b_ref[...], preferred_element_type=F32)
        h1 = h[:, :d_expert]
        h3 = h[:, d_expert:]
        a = (h1 * _sigmoid(h1) * h3).astype(BF16)
        _store_row_tiles(ybuf_ref.at[slot],
                         jnp.dot(a, w2b_ref[...], preferred_element_type=F32))

    block_copies(v, False, lambda c: c.start())

    @pl.when(v == nsteps - 1)
    def _():
        @pl.when(v >= 1)
        def _():
            block_copies(v - 1, False, lambda c: c.wait())

        block_copies(v, False, lambda c: c.wait())
        spare_fill.wait()


def _experts(sched, xs, w1, w3, w2, *, blk):
    data_rows = xs.shape[0] // ROW_TILE
    d_model, d_expert = w1.shape[-2:]
    kern = functools.partial(_experts_kernel, d_expert=d_expert, blk=blk,
                             data_rows=data_rows)
    hbm = pl.BlockSpec(memory_space=pl.ANY)
    block_buf = pltpu.VMEM((2,) + _row_tiles((blk, d_model)), F32)
    return pl.pallas_call(
        kern,
        grid_spec=pltpu.PrefetchScalarGridSpec(
            num_scalar_prefetch=7,
            grid=(sched["start"].shape[0],),
            in_specs=[hbm, hbm, hbm, hbm],
            out_specs=hbm,
            scratch_shapes=[block_buf, block_buf,
                            pltpu.VMEM(_row_tiles((SPARE_ROWS, d_model)), F32),
                            pltpu.VMEM((WEIGHT_RING, d_model, d_expert), F32),
                            pltpu.VMEM((WEIGHT_RING, d_model, d_expert), F32),
                            pltpu.VMEM((WEIGHT_RING, d_expert, d_model), F32),
                            pltpu.VMEM((d_model, 2 * d_expert), BF16),
                            pltpu.VMEM((d_expert, d_model), BF16),
                            pltpu.SemaphoreType.DMA((WEIGHT_RING,)),
                            pltpu.SemaphoreType.DMA((2,)), pltpu.SemaphoreType.DMA((2,)),
                            pltpu.SemaphoreType.DMA]),
        out_shape=jax.ShapeDtypeStruct(((data_rows + SPARE_ROWS) * ROW_TILE, LANES), F32),
        compiler_params=_cparams(("arbitrary",)),
        name="experts",
    )(sched["start"], sched["nvalid"], sched["expert"], sched["first"], sched["ahead"],
      sched["slot"], sched["head"], xs, w1, w3, w2)


def _expert_schedule(counts, *, blk, data_blocks):
    seg_end = jnp.cumsum(counts)
    seg_start = seg_end - counts
    nblk = (counts + blk - 1) // blk
    vend = jnp.cumsum(nblk)
    vfirst = vend - nblk
    nsteps_live = vend[-1]
    total = data_blocks + N_EXPERTS
    v_idx = jnp.arange(total, dtype=jnp.int32)
    vi = jnp.maximum(jnp.minimum(v_idx, nsteps_live - 1), 0)
    expert = jnp.minimum(jnp.sum(vend[None, :] <= vi[:, None], axis=1),
                         N_EXPERTS - 1).astype(jnp.int32)
    live = v_idx < nsteps_live
    within = vi - vfirst[expert]
    start = seg_start[expert] + within * blk
    nvalid = jnp.where(live, jnp.clip(seg_end[expert] - start, 0, blk), 0)
    first = live & (within == 0)
    has_rows = nblk > 0
    used_rank = jnp.cumsum(has_rows) - 1
    j_idx = jnp.arange(N_EXPERTS + WEIGHT_RING, dtype=jnp.int32)
    e_idx = jnp.arange(N_EXPERTS, dtype=jnp.int32)
    hit = has_rows[None, :] & (used_rank[None, :] == j_idx[:, None])
    used_list = jnp.sum(jnp.where(hit, e_idx[None, :] + 1, 0), axis=1) - 1
    rank_v = used_rank[expert]
    ahead = jnp.where(live, used_list[rank_v + (WEIGHT_RING - 1)], -1)
    as_i32 = lambda a: a.astype(jnp.int32)
    return dict(start=as_i32(jnp.where(live, start, 0)), nvalid=as_i32(nvalid),
                expert=expert, first=as_i32(first), ahead=as_i32(ahead),
                slot=as_i32(rank_v % WEIGHT_RING), head=as_i32(used_list[:WEIGHT_RING - 1]))


def _combine_kernel(local0_sm, local1_sm, chunk_src_sm, nchunk_sm,
                    x1_ref, meta_ref, ys_ref, out_ref,
                    buf_ref, g0_ref, g1_ref, sem, *, tb, max_chunks):
    step = pl.program_id(0)
    nsteps = pl.num_programs(0)

    def start_chunks(tile):
        slot = tile % 2

        def per_chunk(c, carry):
            pltpu.make_async_copy(
                _rows(ys_ref, chunk_src_sm[tile * max_chunks + c], RUN_CHUNK),
                _rows(buf_ref.at[slot], c * RUN_CHUNK, RUN_CHUNK),
                sem.at[slot]).start()
            return carry

        lax.fori_loop(0, nchunk_sm[tile], per_chunk, 0)

    def wait_chunks(tile):
        slot = tile % 2
        n = nchunk_sm[tile]
        for bit in range(max_chunks.bit_length()):
            @pl.when((lax.shift_right_logical(n, bit) & 1) == 1)
            def _():
                rows = RUN_CHUNK << bit
                pltpu.make_async_copy(_rows(ys_ref, 0, rows),
                                      _rows(buf_ref.at[slot], 0, rows),
                                      sem.at[slot]).wait()

    @pl.when(step == 0)
    def _():
        start_chunks(step)

    @pl.when(step + 1 < nsteps)
    def _():
        start_chunks(step + 1)

    wait_chunks(step)

    base = step * tb
    tile_buf = buf_ref.at[step % 2]

    def move(r, carry):
        for g_ref, local_sm in ((g0_ref, local0_sm), (g1_ref, local1_sm)):
            _rows(g_ref, r)[...] = _rows(tile_buf, local_sm[base + r])[...]
        return carry

    lax.fori_loop(0, tb, move, 0, unroll=8)
    meta = meta_ref[...]
    moe = (_load_row_tiles(g0_ref) * meta[:, 2:3]
           + _load_row_tiles(g1_ref) * meta[:, 3:4])
    out_ref[...] = x1_ref[...] + moe


def _combine(local0, local1, tables, x1, meta, ys):
    t, d_model = x1.shape
    tb = TB_RANK
    max_chunks = tables["max_chunks"]
    kern = functools.partial(_combine_kernel, tb=tb, max_chunks=max_chunks)
    buf_rows = max_chunks * RUN_CHUNK
    return pl.pallas_call(
        kern,
        grid_spec=pltpu.PrefetchScalarGridSpec(
            num_scalar_prefetch=4,
            grid=(t // tb,),
            in_specs=[pl.BlockSpec((tb, d_model), lambda i, *_: (i, 0)),
                      pl.BlockSpec((tb, LANES), lambda i, *_: (i, 0)),
                      pl.BlockSpec(memory_space=pl.ANY)],
            out_specs=pl.BlockSpec((tb, d_model), lambda i, *_: (i, 0)),
            scratch_shapes=[pltpu.VMEM((2,) + _row_tiles((buf_rows, d_model)), F32),
                            pltpu.VMEM(_row_tiles((tb, d_model)), F32),
                            pltpu.VMEM(_row_tiles((tb, d_model)), F32),
                            pltpu.SemaphoreType.DMA((2,))]),
        out_shape=jax.ShapeDtypeStruct((t, d_model), F32),
        compiler_params=_cparams(("arbitrary",)),
        name="combine",
    )(local0, local1, tables["chunk_src"], tables["tile_chunks"], x1, meta, ys)


def _layer(h2, *, batch, seq, norm1_g, w_in, conv_w, q_norm_g, k_norm_g,
           w_conv_out, w_attn_out, w_o, norm2_g, w_group, w_router, w1, w3, w2):
    t, d_model = h2.shape
    c, sa, sb = _rope_tables(seq)
    scale = HEAD_DIM ** -0.5 * LOG2_E
    tables_q = tuple(jnp.asarray(tab * np.float32(scale)) for tab in (c, sa, sb))
    tables_k = tuple(jnp.asarray(tab) for tab in (c, sa, sb))

    cb, z, q, k, v, sgc, sga = _inproj(
        h2, norm1_g[None, :], w_in.astype(BF16), q_norm_g[None, :], k_norm_g[None, :],
        tables_q, tables_k, seq=seq)
    o = _attention(q, k, v, batch=batch, seq=seq)

    n_route = N_GROUPS + N_EXPERTS
    wr = jnp.concatenate(
        [w_group, w_router, jnp.zeros((d_model, LANES - n_route), F32)], axis=1)
    wr_hi = wr.astype(BF16)
    wr = jnp.concatenate([wr_hi, (wr - wr_hi.astype(F32)).astype(BF16)], axis=1)
    wcat = jnp.concatenate([w_conv_out, w_attn_out, w_o], axis=0).astype(BF16)
    x1, u2, meta, cnt = _post(h2, cb, z, o, sgc, sga, conv_w, wcat,
                              norm2_g[None, :], wr, seq=seq)

    blk = MOE_ROWS
    data_rows = t * TOP_K
    local0, local1, runs = _rank(meta, cnt)
    local0, local1 = local0.reshape(-1), local1.reshape(-1)
    runs = runs.reshape(-1, 8, LANES)
    tables = _tile_tables(runs[:, 0, :N_EXPERTS], runs[:, 1, :N_EXPERTS], tb=TB_RANK)
    sched = _expert_schedule(cnt[0, :N_EXPERTS].astype(jnp.int32), blk=blk,
                             data_blocks=data_rows // blk)

    xs = _dispatch(local0, local1, tables, u2, data_rows=data_rows)
    ys = _experts(sched, xs, w1, w3, w2, blk=blk)
    return _combine(local0, local1, tables, x1, meta, ys)


def kernel(x, norm1_g, w_in, conv_w, q_norm_g, k_norm_g, w_conv_out, w_attn_out, w_o,
           norm2_g, w_group, w_router, w1, w3, w2):
    batch, seq, d_model = x.shape
    h2 = x.reshape(batch * seq, d_model)
    for l in range(norm1_g.shape[0]):
        h2 = _layer(h2, batch=batch, seq=seq, norm1_g=norm1_g[l], w_in=w_in[l],
                    conv_w=conv_w[l], q_norm_g=q_norm_g[l], k_norm_g=k_norm_g[l],
                    w_conv_out=w_conv_out[l], w_attn_out=w_attn_out[l], w_o=w_o[l],
                    norm2_g=norm2_g[l], w_group=w_group[l], w_router=w_router[l],
                    w1=w1[l], w3=w3[l], w2=w2[l])
    return h2.reshape(batch, seq, d_model)
```

```python
import functools

import jax
import jax.numpy as jnp
import numpy as np
from jax import lax
from jax.experimental import pallas as pl
from jax.experimental.pallas import tpu as pltpu

F32 = jnp.float32
BF16 = jnp.bfloat16

GRID_W = 64
EPS = 1e-6
N_HEADS = 8
N_KV_HEADS = 2
HEAD_DIM = 128
ROPE_THETA = 10000.0
N_GROUPS = 8
EXPERTS_PER_GROUP = 8
N_EXPERTS = N_GROUPS * EXPERTS_PER_GROUP
TOP_K = 2
LOG2_E = 1.4426950408889634

LANES = 128
MXU_DIM = 256
BF16_SUBLANES = 16
V7X_VMEM_LIMIT_BYTES = 56000 * 1024

TM_PROJ = 512
TQ_ATTN = 512
TK_ATTN = 512
TB_RANK = 1024
RUN_CHUNK = 8
SPARE_ROWS = RUN_CHUNK
MOE_ROWS = 512
WEIGHT_RING = 3
WEIGHT_DMA_PRIORITY = 1
DIGIT = 256.0


def _cparams(sem):
    return pltpu.CompilerParams(dimension_semantics=sem,
                                vmem_limit_bytes=V7X_VMEM_LIMIT_BYTES)


ROW_TILE = 8


def _row_tiles(shape2d):
    rows, width = shape2d
    assert width == ROW_TILE * LANES
    return (rows * ROW_TILE, LANES)


def _rows(ref, r, n=1):
    return ref.at[pl.ds(pl.multiple_of(r * ROW_TILE, ROW_TILE), n * ROW_TILE)]


def _load_row_tiles(ref):
    rows = ref.shape[0] // ROW_TILE
    return jnp.concatenate(
        [ref[pl.ds(s, rows, stride=ROW_TILE), :] for s in range(ROW_TILE)], axis=1)


def _store_row_tiles(ref, value):
    rows = value.shape[0]
    for s in range(ROW_TILE):
        ref[pl.ds(s, rows, stride=ROW_TILE), :] = value[:, s * LANES:(s + 1) * LANES]


def _resident(shape):
    nd = len(shape)
    return pl.BlockSpec(shape, lambda *_: (0,) * nd, pipeline_mode=pl.Buffered(1))


def _lane_prefix(row):
    r = lax.broadcasted_iota(jnp.int32, (LANES, LANES), 0)
    col = lax.broadcasted_iota(jnp.int32, (LANES, LANES), 1)
    upper = jnp.where(r < col, 1.0, 0.0).astype(BF16)
    return jnp.dot(jnp.broadcast_to(row, (8, LANES)).astype(BF16), upper,
                   preferred_element_type=F32)[0:1, :]


def _head_norm_rope(xh, g, c, sa, sb):
    ms = jnp.mean(xh * xh, axis=-1, keepdims=True)
    y = xh * lax.rsqrt(ms + EPS) * g
    y_next = pltpu.roll(y, HEAD_DIM - 1, axis=1)
    y_prev = pltpu.roll(y, 1, axis=1)
    return y * c + y_next * sa + y_prev * sb


def _sigmoid(x):
    return 0.5 * jnp.tanh(0.5 * x) + 0.5


def _inproj_kernel(x_ref, g1_ref, w_ref, gq_ref, gk_ref,
                   cq_ref, saq_ref, sbq_ref, ck_ref, sak_ref, sbk_ref,
                   cb_ref, z_ref, q_ref, k_ref, v_ref, sgc_ref, sga_ref,
                   *, d_conv, d_q, d_kv, d_model):
    x = x_ref[...]
    ms = jnp.mean(x * x, axis=-1, keepdims=True)
    u = (x * lax.rsqrt(ms + EPS) * g1_ref[...]).astype(BF16)

    def proj(lo, width):
        return jnp.dot(u, w_ref[:, lo:lo + width], preferred_element_type=F32)

    o_cb, o_cc, o_cx = 0, d_conv, 2 * d_conv
    o_q = 3 * d_conv
    o_k = o_q + d_q
    o_v = o_k + d_kv
    o_gc = o_v + d_kv
    o_ga = o_gc + d_model

    sgc_ref[...] = _sigmoid(proj(o_gc, d_model)).astype(BF16)
    sga_ref[...] = _sigmoid(proj(o_ga, d_model)).astype(BF16)

    q = proj(o_q, d_q)
    gq = gq_ref[...]
    cq, saq, sbq = cq_ref[...], saq_ref[...], sbq_ref[...]
    for h in range(d_q // HEAD_DIM):
        sl = slice(h * HEAD_DIM, (h + 1) * HEAD_DIM)
        q_ref[:, sl] = _head_norm_rope(q[:, sl], gq, cq, saq, sbq).astype(BF16)

    k = proj(o_k, d_kv)
    gk = gk_ref[...]
    ck, sak, sbk = ck_ref[...], sak_ref[...], sbk_ref[...]
    for h in range(d_kv // HEAD_DIM):
        sl = slice(h * HEAD_DIM, (h + 1) * HEAD_DIM)
        k_ref[:, sl] = _head_norm_rope(k[:, sl], gk, ck, sak, sbk).astype(BF16)

    z_ref[...] = (proj(o_cc, d_conv) * proj(o_cx, d_conv)).astype(BF16)
    v_ref[...] = proj(o_v, d_kv).astype(BF16)
    cb_ref[...] = proj(o_cb, d_conv).astype(BF16)


def _rope_tables(seq):
    rows = seq // GRID_W
    axis_dim = HEAD_DIM // 2
    row = np.repeat(np.arange(rows, dtype=np.float32), GRID_W)
    col = np.tile(np.arange(GRID_W, dtype=np.float32), rows)
    inv = (np.float32(ROPE_THETA)
           ** (-np.arange(0, axis_dim, 2, dtype=np.float32) / np.float32(axis_dim)))
    ang = np.concatenate([row[:, None] * inv, col[:, None] * inv], axis=-1)
    ang = ang.astype(np.float32)
    cos, sin = np.cos(ang), np.sin(ang)
    zero = np.zeros_like(sin)
    c = np.repeat(cos, 2, axis=-1)
    sa = np.stack([-sin, zero], axis=-1).reshape(seq, HEAD_DIM)
    sb = np.stack([zero, sin], axis=-1).reshape(seq, HEAD_DIM)
    return c, sa, sb


def _inproj(x2, g1, w_in_bf, gq, gk, tables_q, tables_k, *, seq):
    t, d_model = x2.shape
    d_q = N_HEADS * HEAD_DIM
    d_kv = N_KV_HEADS * HEAD_DIM
    d_in = w_in_bf.shape[1]
    d_conv = (d_in - d_q - 2 * d_kv - 2 * d_model) // 3
    tm = TM_PROJ
    nseq = seq // tm

    def row(width):
        return pl.BlockSpec((tm, width), lambda i: (i, 0))

    table = pl.BlockSpec((tm, HEAD_DIM), lambda i: (i % nseq, 0))
    kern = functools.partial(_inproj_kernel, d_conv=d_conv, d_q=d_q, d_kv=d_kv,
                             d_model=d_model)
    out_shape = [jax.ShapeDtypeStruct((t, w), BF16)
                 for w in (d_conv, d_conv, d_q, d_kv, d_kv, d_model, d_model)]
    return pl.pallas_call(
        kern,
        grid=(t // tm,),
        in_specs=[row(d_model), _resident((1, d_model)), _resident((d_model, d_in)),
                  _resident((1, HEAD_DIM)), _resident((1, HEAD_DIM)),
                  table, table, table, table, table, table],
        out_specs=[row(d_conv), row(d_conv), row(d_q), row(d_kv), row(d_kv),
                   row(d_model), row(d_model)],
        out_shape=out_shape,
        compiler_params=_cparams(("arbitrary",)),
        name="inproj",
    )(x2, g1, w_in_bf, gq, gk, *tables_q, *tables_k)


def _attn_kernel(q_ref, k_ref, v_ref, o_ref, qs_ref, vext_ref, m_ref, acc_ref,
                 *, tq, chunks, group):
    @pl.when(pl.program_id(2) == 0)
    def _():
        vext_ref[:, :HEAD_DIM] = v_ref[...]
        vext_ref[:, HEAD_DIM:] = jnp.ones((vext_ref.shape[0], HEAD_DIM), BF16)

    for g in range(group):
        qs_ref[g * tq:(g + 1) * tq, :] = q_ref[:, g * HEAD_DIM:(g + 1) * HEAD_DIM]
    m_ref[...] = jnp.full(m_ref.shape, -jnp.inf, F32)
    acc_ref[...] = jnp.zeros(acc_ref.shape, F32)

    lo = 0
    for tk in chunks:
        keys = slice(lo, lo + tk)
        lo += tk
        s = lax.dot_general(qs_ref[...], k_ref[keys, :], (((1,), (1,)), ((), ())),
                            preferred_element_type=F32)
        m_prev = m_ref[...]
        m_new = jnp.maximum(m_prev, jnp.max(s, axis=-1, keepdims=True))
        alpha = jnp.exp2(m_prev - m_new)
        p = jnp.concatenate(
            [jnp.exp2(s[:, c * LANES:(c + 1) * LANES] - m_new) for c in range(tk // LANES)],
            axis=1).astype(BF16)
        pv = jnp.dot(p, vext_ref[keys, :], preferred_element_type=F32)
        acc_ref[...] = jnp.concatenate([alpha, alpha], axis=1) * acc_ref[...] + pv
        m_ref[...] = m_new

    out = acc_ref[:, :HEAD_DIM] / acc_ref[:, HEAD_DIM:]
    for g in range(group):
        o_ref[:, g * HEAD_DIM:(g + 1) * HEAD_DIM] = out[g * tq:(g + 1) * tq].astype(BF16)


def _attention(q, k, v, *, batch, seq):
    t = q.shape[0]
    group = N_HEADS // N_KV_HEADS
    tq, tk = TQ_ATTN, TK_ATTN
    nq = seq // tq
    gw = group * HEAD_DIM
    chunks = (tk // 2,) + (tk,) * (seq // tk - 1) + (tk // 2,)
    kern = functools.partial(_attn_kernel, tq=tq, chunks=chunks, group=group)
    return pl.pallas_call(
        kern,
        grid=(batch, N_KV_HEADS, nq),
        in_specs=[pl.BlockSpec((tq, gw), lambda b, h, i: (b * nq + i, h)),
                  pl.BlockSpec((seq, HEAD_DIM), lambda b, h, i: (b, h)),
                  pl.BlockSpec((seq, HEAD_DIM), lambda b, h, i: (b, h))],
        out_specs=pl.BlockSpec((tq, gw), lambda b, h, i: (b * nq + i, h)),
        out_shape=jax.ShapeDtypeStruct((t, N_HEADS * HEAD_DIM), BF16),
        scratch_shapes=[pltpu.VMEM((group * tq, HEAD_DIM), BF16),
                        pltpu.VMEM((seq, 2 * HEAD_DIM), BF16),
                        pltpu.VMEM((group * tq, LANES), F32),
                        pltpu.VMEM((group * tq, 2 * HEAD_DIM), F32)],
        compiler_params=_cparams(("arbitrary", "arbitrary", "arbitrary")),
        name="attention",
    )(q, k, v)


def _route(logits):
    rows = logits.shape[0]
    lane = lax.broadcasted_iota(jnp.int32, (rows, LANES), 1).astype(F32)
    neg = -jnp.inf
    big = float(2 * LANES)
    is_group = lane < N_GROUPS
    gl = jnp.where(is_group, logits, neg)
    gmax = jnp.max(gl, axis=-1, keepdims=True)
    gidx = jnp.min(jnp.where(gl == gmax, lane, big), axis=-1, keepdims=True)
    gsum = jnp.sum(jnp.where(is_group, jnp.exp(logits - gmax), 0.0), axis=-1,
                   keepdims=True)
    pg = 1.0 / gsum
    lane_group = jnp.floor(lane * (1.0 / EXPERTS_PER_GROUP)) - 1.0
    mine = (lane_group == gidx) & (lane >= N_GROUPS) & (lane < N_GROUPS + N_EXPERTS)
    sel = jnp.where(mine, logits, neg)
    v1 = jnp.max(sel, axis=-1, keepdims=True)
    i1 = jnp.min(jnp.where(sel == v1, lane, big), axis=-1, keepdims=True)
    sel2 = jnp.where(lane == i1, neg, sel)
    v2 = jnp.max(sel2, axis=-1, keepdims=True)
    i2 = jnp.min(jnp.where(sel2 == v2, lane, big), axis=-1, keepdims=True)
    t2 = jnp.exp(v2 - v1)
    den = 1.0 + t2
    wgt1 = pg * (1.0 / den)
    wgt2 = pg * (t2 / den)
    e1 = i1 - N_GROUPS
    e2 = i2 - N_GROUPS
    meta = jnp.where(lane == 0, e1,
                     jnp.where(lane == 1, e2,
                               jnp.where(lane == 2, wgt1,
                                         jnp.where(lane == 3, wgt2, 0.0))))
    picked = jnp.where((lane == e1) | (lane == e2), 1.0, 0.0)
    return meta, jnp.sum(picked, axis=0, keepdims=True)


def _post_kernel(x_ref, cb_ref, z_ref, zprev_ref, znext_ref, o_ref, sgc_ref, sga_ref,
                 cw_ref, wc_ref, wa_ref, wo_ref, g2_ref, wr_ref,
                 x1_ref, u2_ref, meta_ref, cnt_ref, *, tm, nseq):
    i = pl.program_id(0)
    at_start = (i % nseq) == 0
    at_end = (i % nseq) == nseq - 1
    rowid = lax.broadcasted_iota(jnp.int32, (tm, 1), 0)

    @pl.when(i == 0)
    def _():
        cnt_ref[...] = jnp.zeros(cnt_ref.shape, F32)

    y_attn = jnp.dot(o_ref[...], wa_ref[...], preferred_element_type=F32)

    y_conv = None
    for c in range(z_ref.shape[1] // MXU_DIM):
        sl = slice(c * MXU_DIM, (c + 1) * MXU_DIM)
        z = z_ref[:, sl].astype(F32)
        prev_row = zprev_ref[BF16_SUBLANES - 1:BF16_SUBLANES, sl].astype(F32)
        next_row = znext_ref[0:1, sl].astype(F32)
        prev_row = jnp.where(at_start, 0.0, prev_row)
        next_row = jnp.where(at_end, 0.0, next_row)
        zp = jnp.where(rowid == 0, prev_row, pltpu.roll(z, 1, axis=0))
        zn = jnp.where(rowid == tm - 1, next_row, pltpu.roll(z, tm - 1, axis=0))
        conv = cw_ref[0:1, sl] * zp + cw_ref[1:2, sl] * z + cw_ref[2:3, sl] * zn
        cbz = (cb_ref[:, sl].astype(F32) * conv).astype(BF16)
        part = jnp.dot(cbz, wc_ref[sl, :], preferred_element_type=F32)
        y_conv = part if y_conv is None else y_conv + part
    merged = (sgc_ref[...].astype(F32) * y_conv
              + sga_ref[...].astype(F32) * y_attn).astype(BF16)
    x1 = x_ref[...] + jnp.dot(merged, wo_ref[...], preferred_element_type=F32)
    x1_ref[...] = x1

    ms = jnp.mean(x1 * x1, axis=-1, keepdims=True)
    u2 = x1 * lax.rsqrt(ms + EPS) * g2_ref[...]
    _store_row_tiles(u2_ref, u2)

    u2_hi = u2.astype(BF16)
    u2_lo = (u2 - u2_hi.astype(F32)).astype(BF16)
    hi_part = jnp.dot(u2_hi, wr_ref[...], preferred_element_type=F32)
    lo_part = jnp.dot(u2_lo, wr_ref[:, :LANES], preferred_element_type=F32)
    logits = hi_part[:, :LANES] + (hi_part[:, LANES:] + lo_part)
    meta, picked = _route(logits)
    meta_ref[...] = meta
    cnt_ref[...] += picked


def _post(x2, cb, z, o, sgc, sga, conv_w, wc, wa, wo, g2, wr, *, seq):
    t, d_model = x2.shape
    tm = TM_PROJ
    nseq = seq // tm
    hb = tm // BF16_SUBLANES
    nhalo = t // BF16_SUBLANES
    d_conv = cb.shape[1]
    d_q = o.shape[1]

    def row(width):
        return pl.BlockSpec((tm, width), lambda i: (i, 0))

    kern = functools.partial(_post_kernel, tm=tm, nseq=nseq)
    return pl.pallas_call(
        kern,
        grid=(t // tm,),
        in_specs=[row(d_model), row(d_conv), row(d_conv),
                  pl.BlockSpec((BF16_SUBLANES, d_conv),
                               lambda i: (jnp.maximum(i * hb - 1, 0), 0)),
                  pl.BlockSpec((BF16_SUBLANES, d_conv),
                               lambda i: (jnp.minimum((i + 1) * hb, nhalo - 1), 0)),
                  row(d_q), row(d_model), row(d_model),
                  _resident(conv_w.shape), _resident(wc.shape), _resident(wa.shape),
                  _resident(wo.shape), _resident(g2.shape), _resident(wr.shape)],
        out_specs=[row(d_model),
                   pl.BlockSpec(_row_tiles((tm, d_model)), lambda i: (i, 0)),
                   row(LANES),
                   pl.BlockSpec((1, LANES), lambda i: (0, 0))],
        out_shape=[jax.ShapeDtypeStruct((t, d_model), F32),
                   jax.ShapeDtypeStruct(_row_tiles((t, d_model)), F32),
                   jax.ShapeDtypeStruct((t, LANES), F32),
                   jax.ShapeDtypeStruct((1, LANES), F32)],
        compiler_params=_cparams(("arbitrary",)),
        name="post",
    )(x2, cb, z, z, z, o, sgc, sga, conv_w, wc, wa, wo, g2, wr)


def _rank_kernel(meta_ref, cnt_ref, l0_ref, l1_ref, runs_ref,
                 carry_ref, pstart_ref, lower_ref, *, tb):
    i = pl.program_id(0)
    lane = lax.broadcasted_iota(jnp.int32, (tb, LANES), 1).astype(F32)
    meta = meta_ref[...]
    oh1 = jnp.where(lane == meta[:, 0:1], 1.0, 0.0)
    oh2 = jnp.where(lane == meta[:, 1:2], 1.0, 0.0)
    c = oh1 + oh2

    @pl.when(i == 0)
    def _():
        cnt = cnt_ref[...]
        high = jnp.floor(cnt * (1.0 / DIGIT))
        pstart_ref[...] = _lane_prefix(high) * DIGIT + _lane_prefix(cnt - high * DIGIT)
        carry_ref[...] = jnp.zeros(carry_ref.shape, F32)
        r = lax.broadcasted_iota(jnp.int32, (tb, tb), 0)
        col = lax.broadcasted_iota(jnp.int32, (tb, tb), 1)
        lower_ref[...] = jnp.where(col < r, 1.0, 0.0).astype(BF16)

    prefix = jnp.dot(lower_ref[...], c.astype(BF16), preferred_element_type=F32)
    run_start = carry_ref[...] + pstart_ref[...]
    run_len = jnp.sum(c, axis=0, keepdims=True)

    nchunk = jnp.floor((run_len + (RUN_CHUNK - 1)) * (1.0 / RUN_CHUNK))
    local = prefix + _lane_prefix(nchunk) * RUN_CHUNK

    v0 = jnp.sum(oh1 * local, axis=-1, keepdims=True)
    v1 = jnp.sum(oh2 * local, axis=-1, keepdims=True)
    rows = jnp.transpose(jnp.where(lane == 0, v0, jnp.where(lane == 1, v1, 0.0)))
    l0_ref[...] = rows[0:1, :].astype(jnp.int32)
    l1_ref[...] = rows[1:2, :].astype(jnp.int32)
    row = lax.broadcasted_iota(jnp.int32, (8, LANES), 0)
    runs = jnp.where(row == 0, run_start, jnp.where(row == 1, run_len, 0.0))
    runs_ref[...] = runs.astype(jnp.int32)
    carry_ref[...] += run_len


def _rank(meta, cnt):
    t = meta.shape[0]
    tb = TB_RANK
    kern = functools.partial(_rank_kernel, tb=tb)
    per_token = pl.BlockSpec((1, tb), lambda i: (0, i))
    return pl.pallas_call(
        kern,
        grid=(t // tb,),
        in_specs=[pl.BlockSpec((tb, LANES), lambda i: (i, 0)),
                  pl.BlockSpec((1, LANES), lambda i: (0, 0))],
        out_specs=[per_token] * 2 + [pl.BlockSpec((8, LANES), lambda i: (i, 0))],
        out_shape=[jax.ShapeDtypeStruct((1, t), jnp.int32)] * 2
                  + [jax.ShapeDtypeStruct((t // tb * 8, LANES), jnp.int32)],
        scratch_shapes=[pltpu.VMEM((1, LANES), F32), pltpu.VMEM((1, LANES), F32),
                        pltpu.VMEM((tb, tb), BF16)],
        compiler_params=_cparams(("arbitrary",)),
        name="rank",
    )(meta, cnt)


def _dispatch_kernel(local0_sm, local1_sm, full_src_sm, full_dst_sm, nfull_sm,
                     rem_src_sm, rem_dst_sm, rem_len_sm,
                     u2_ref, xs_ref, buf_ref, sem, *, tb, max_full):
    step = pl.program_id(0)
    nsteps = pl.num_programs(0)
    base = step * tb
    slot = step % 2
    tile_buf = buf_ref.at[slot]

    def wait_tile(s):
        pltpu.make_async_copy(_rows(buf_ref.at[s], 0, tb * TOP_K),
                              _rows(xs_ref, 0, tb * TOP_K), sem.at[s]).wait()

    @pl.when(step >= 2)
    def _():
        wait_tile(slot)

    def move(r, carry):
        row = _rows(u2_ref, r)[...]
        _rows(tile_buf, local0_sm[base + r])[...] = row
        _rows(tile_buf, local1_sm[base + r])[...] = row
        return carry

    lax.fori_loop(0, tb, move, 0, unroll=8)

    def full_chunk(c, carry):
        pltpu.make_async_copy(
            _rows(tile_buf, full_src_sm[step * max_full + c], RUN_CHUNK),
            _rows(xs_ref, full_dst_sm[step * max_full + c], RUN_CHUNK),
            sem.at[slot]).start()
        return carry

    lax.fori_loop(0, nfull_sm[step], full_chunk, 0)

    def remainder(e, carry):
        idx = step * N_EXPERTS + e
        src, dst, length = rem_src_sm[idx], rem_dst_sm[idx], rem_len_sm[idx]
        for bit in range(RUN_CHUNK.bit_length() - 1):
            size = 1 << bit
            higher = lax.shift_left(lax.shift_right_logical(length, bit + 1), bit + 1)

            @pl.when((lax.shift_right_logical(length, bit) & 1) == 1)
            def _():
                pltpu.make_async_copy(_rows(tile_buf, src + higher, size),
                                      _rows(xs_ref, dst + higher, size),
                                      sem.at[slot]).start()
        return carry

    lax.fori_loop(0, N_EXPERTS, remainder, 0)

    @pl.when(step == nsteps - 1)
    def _():
        @pl.when(step >= 1)
        def _():
            wait_tile(1 - slot)

        wait_tile(slot)


def _tile_tables(run_start, run_len, *, tb):
    shift = RUN_CHUNK.bit_length() - 1
    nchunk = (run_len + RUN_CHUNK - 1) >> shift
    cum = jnp.cumsum(nchunk, axis=1)
    first = cum - nchunk
    local_start = first * RUN_CHUNK

    def flat_list(count, max_count, value_at):
        ccum = jnp.cumsum(count, axis=1)
        cfirst = (ccum - count)[:, None, :]
        c_idx = jnp.arange(max_count, dtype=jnp.int32)[None, :, None]
        owns = (cfirst <= c_idx) & (c_idx < ccum[:, None, :])
        return [jnp.sum(jnp.where(owns, v[:, None, :] + (c_idx - cfirst) * RUN_CHUNK, 0),
                        axis=2).astype(jnp.int32) for v in value_at], ccum[:, -1]

    max_chunks = tb * TOP_K // RUN_CHUNK + N_EXPERTS
    (chunk_src,), tile_chunks = flat_list(nchunk, max_chunks, [run_start])
    max_full = tb * TOP_K // RUN_CHUNK
    nfull = run_len >> shift
    (full_src, full_dst), tile_full = flat_list(nfull, max_full, [local_start, run_start])
    whole = nfull * RUN_CHUNK
    return dict(
        max_chunks=max_chunks, chunk_src=chunk_src.reshape(-1),
        tile_chunks=tile_chunks.astype(jnp.int32),
        max_full=max_full, full_src=full_src.reshape(-1), full_dst=full_dst.reshape(-1),
        tile_full=tile_full.astype(jnp.int32),
        rem_src=(local_start + whole).reshape(-1).astype(jnp.int32),
        rem_dst=(run_start + whole).reshape(-1).astype(jnp.int32),
        rem_len=(run_len - whole).reshape(-1).astype(jnp.int32))


def _dispatch(local0, local1, tables, u2, *, data_rows):
    t = u2.shape[0] // ROW_TILE
    tb = TB_RANK
    d_model = ROW_TILE * LANES
    kern = functools.partial(_dispatch_kernel, tb=tb, max_full=tables["max_full"])
    buf_rows = tables["max_chunks"] * RUN_CHUNK
    return pl.pallas_call(
        kern,
        grid_spec=pltpu.PrefetchScalarGridSpec(
            num_scalar_prefetch=8,
            grid=(t // tb,),
            in_specs=[pl.BlockSpec((tb * ROW_TILE, LANES), lambda i, *_: (i, 0))],
            out_specs=pl.BlockSpec(memory_space=pl.ANY),
            scratch_shapes=[pltpu.VMEM((2,) + _row_tiles((buf_rows, d_model)), F32),
                            pltpu.SemaphoreType.DMA((2,))]),
        out_shape=jax.ShapeDtypeStruct((data_rows * ROW_TILE, LANES), F32),
        compiler_params=_cparams(("arbitrary",)),
        name="dispatch",
    )(local0, local1, tables["full_src"], tables["full_dst"], tables["tile_full"],
      tables["rem_src"], tables["rem_dst"], tables["rem_len"], u2)


def _experts_kernel(start_sm, nvalid_sm, expert_sm, first_sm, ahead_sm, slot_sm, head_sm,
                    xs_hbm, w1_hbm, w3_hbm, w2_hbm, ys_hbm,
                    xbuf_ref, ybuf_ref, zrow_ref, w1f_ref, w3f_ref, w2f_ref,
                    w13b_ref, w2b_ref, wsem, isem, osem, zsem,
                    *, d_expert, blk, data_rows):
    v = pl.program_id(0)
    nsteps = pl.num_programs(0)
    nbits = blk.bit_length()

    def block_copies(step, inbound, visit):
        n, start, slot = nvalid_sm[step], start_sm[step], lax.rem(step, 2)
        for bit in range(nbits):
            size = 1 << bit
            higher = lax.shift_left(lax.shift_right_logical(n, bit + 1), bit + 1)

            @pl.when((lax.shift_right_logical(n, bit) & 1) == 1)
            def _():
                if inbound:
                    visit(pltpu.make_async_copy(_rows(xs_hbm, start + higher, size),
                                                _rows(xbuf_ref.at[slot], higher, size),
                                                isem.at[slot]))
                else:
                    visit(pltpu.make_async_copy(_rows(ybuf_ref.at[slot], higher, size),
                                                _rows(ys_hbm, start + higher, size),
                                                osem.at[slot]))

    def weight_copies(expert, slot):
        return [pltpu.make_async_copy(src.at[expert], dst.at[slot], wsem.at[slot])
                for src, dst in ((w1_hbm, w1f_ref), (w3_hbm, w3f_ref), (w2_hbm, w2f_ref))]

    spare_fill = pltpu.make_async_copy(zrow_ref, _rows(ys_hbm, data_rows, SPARE_ROWS), zsem)

    @pl.when(v == 0)
    def _():
        xbuf_ref[...] = jnp.zeros(xbuf_ref.shape, F32)
        zrow_ref[...] = jnp.zeros(zrow_ref.shape, F32)
        spare_fill.start()
        for j in range(WEIGHT_RING - 1):
            @pl.when(head_sm[j] >= 0)
            def _():
                for c in weight_copies(head_sm[j], j):
                    c.start(priority=WEIGHT_DMA_PRIORITY)
        block_copies(v, True, lambda c: c.start())

    @pl.when(v + 1 < nsteps)
    def _():
        block_copies(v + 1, True, lambda c: c.start())

    @pl.when(first_sm[v] == 1)
    def _():
        slot = slot_sm[v]
        for c in weight_copies(expert_sm[v], slot):
            c.wait()

        @pl.when(ahead_sm[v] >= 0)
        def _():
            ahead_slot = lax.rem(slot + (WEIGHT_RING - 1), WEIGHT_RING)
            for c in weight_copies(ahead_sm[v], ahead_slot):
                c.start(priority=WEIGHT_DMA_PRIORITY)

        w13b_ref[:, :d_expert] = w1f_ref[slot].astype(BF16)
        w13b_ref[:, d_expert:] = w3f_ref[slot].astype(BF16)
        w2b_ref[...] = w2f_ref[slot].astype(BF16)

    block_copies(v, True, lambda c: c.wait())

    @pl.when(v >= 2)
    def _():
        block_copies(v - 2, False, lambda c: c.wait())

    @pl.when(nvalid_sm[v] > 0)
    def _():
        slot = lax.rem(v, 2)
        xb = _load_row_tiles(xbuf_ref.at[slot]).astype(BF16)
        h = jnp.dot(xb, w13b_ref[...], preferred_element_type=F32)
        h1 = h[:, :d_expert]
        h3 = h[:, d_expert:]
        a = (h1 * _sigmoid(h1) * h3).astype(BF16)
        _store_row_tiles(ybuf_ref.at[slot],
                         jnp.dot(a, w2b_ref[...], preferred_element_type=F32))

    block_copies(v, False, lambda c: c.start())

    @pl.when(v == nsteps - 1)
    def _():
        @pl.when(v >= 1)
        def _():
            block_copies(v - 1, False, lambda c: c.wait())

        block_copies(v, False, lambda c: c.wait())
        spare_fill.wait()


def _experts(sched, xs, w1, w3, w2, *, blk):
    data_rows = xs.shape[0] // ROW_TILE
    d_model, d_expert = w1.shape[-2:]
    kern = functools.partial(_experts_kernel, d_expert=d_expert, blk=blk,
                             data_rows=data_rows)
    hbm = pl.BlockSpec(memory_space=pl.ANY)
    block_buf = pltpu.VMEM((2,) + _row_tiles((blk, d_model)), F32)
    return pl.pallas_call(
        kern,
        grid_spec=pltpu.PrefetchScalarGridSpec(
            num_scalar_prefetch=7,
            grid=(sched["start"].shape[0],),
            in_specs=[hbm, hbm, hbm, hbm],
            out_specs=hbm,
            scratch_shapes=[block_buf, block_buf,
                            pltpu.VMEM(_row_tiles((SPARE_ROWS, d_model)), F32),
                            pltpu.VMEM((WEIGHT_RING, d_model, d_expert), F32),
                            pltpu.VMEM((WEIGHT_RING, d_model, d_expert), F32),
                            pltpu.VMEM((WEIGHT_RING, d_expert, d_model), F32),
                            pltpu.VMEM((d_model, 2 * d_expert), BF16),
                            pltpu.VMEM((d_expert, d_model), BF16),
                            pltpu.SemaphoreType.DMA((WEIGHT_RING,)),
                            pltpu.SemaphoreType.DMA((2,)), pltpu.SemaphoreType.DMA((2,)),
                            pltpu.SemaphoreType.DMA]),
        out_shape=jax.ShapeDtypeStruct(((data_rows + SPARE_ROWS) * ROW_TILE, LANES), F32),
        compiler_params=_cparams(("arbitrary",)),
        name="experts",
    )(sched["start"], sched["nvalid"], sched["expert"], sched["first"], sched["ahead"],
      sched["slot"], sched["head"], xs, w1, w3, w2)


def _expert_schedule(counts, *, blk, data_blocks):
    seg_end = jnp.cumsum(counts)
    seg_start = seg_end - counts
    nblk = (counts + blk - 1) // blk
    vend = jnp.cumsum(nblk)
    vfirst = vend - nblk
    nsteps_live = vend[-1]
    total = data_blocks + N_EXPERTS
    v_idx = jnp.arange(total, dtype=jnp.int32)
    vi = jnp.maximum(jnp.minimum(v_idx, nsteps_live - 1), 0)
    expert = jnp.minimum(jnp.sum(vend[None, :] <= vi[:, None], axis=1),
                         N_EXPERTS - 1).astype(jnp.int32)
    live = v_idx < nsteps_live
    within = vi - vfirst[expert]
    start = seg_start[expert] + within * blk
    nvalid = jnp.where(live, jnp.clip(seg_end[expert] - start, 0, blk), 0)
    first = live & (within == 0)
    has_rows = nblk > 0
    used_rank = jnp.cumsum(has_rows) - 1
    j_idx = jnp.arange(N_EXPERTS + WEIGHT_RING, dtype=jnp.int32)
    e_idx = jnp.arange(N_EXPERTS, dtype=jnp.int32)
    hit = has_rows[None, :] & (used_rank[None, :] == j_idx[:, None])
    used_list = jnp.sum(jnp.where(hit, e_idx[None, :] + 1, 0), axis=1) - 1
    rank_v = used_rank[expert]
    ahead = jnp.where(live, used_list[rank_v + (WEIGHT_RING - 1)], -1)
    as_i32 = lambda a: a.astype(jnp.int32)
    return dict(start=as_i32(jnp.where(live, start, 0)), nvalid=as_i32(nvalid),
                expert=expert, first=as_i32(first), ahead=as_i32(ahead),
                slot=as_i32(rank_v % WEIGHT_RING), head=as_i32(used_list[:WEIGHT_RING - 1]))


def _combine_kernel(local0_sm, local1_sm, chunk_src_sm, nchunk_sm,
                    x1_ref, meta_ref, ys_ref, out_ref,
                    buf_ref, g0_ref, g1_ref, sem, *, tb, max_chunks):
    step = pl.program_id(0)
    nsteps = pl.num_programs(0)

    def start_chunks(tile):
        slot = tile % 2

        def per_chunk(c, carry):
            pltpu.make_async_copy(
                _rows(ys_ref, chunk_src_sm[tile * max_chunks + c], RUN_CHUNK),
                _rows(buf_ref.at[slot], c * RUN_CHUNK, RUN_CHUNK),
                sem.at[slot]).start()
            return carry

        lax.fori_loop(0, nchunk_sm[tile], per_chunk, 0)

    def wait_chunks(tile):
        slot = tile % 2
        n = nchunk_sm[tile]
        for bit in range(max_chunks.bit_length()):
            @pl.when((lax.shift_right_logical(n, bit) & 1) == 1)
            def _():
                rows = RUN_CHUNK << bit
                pltpu.make_async_copy(_rows(ys_ref, 0, rows),
                                      _rows(buf_ref.at[slot], 0, rows),
                                      sem.at[slot]).wait()

    @pl.when(step == 0)
    def _():
        start_chunks(step)

    @pl.when(step + 1 < nsteps)
    def _():
        start_chunks(step + 1)

    wait_chunks(step)

    base = step * tb
    tile_buf = buf_ref.at[step % 2]

    def move(r, carry):
        for g_ref, local_sm in ((g0_ref, local0_sm), (g1_ref, local1_sm)):
            _rows(g_ref, r)[...] = _rows(tile_buf, local_sm[base + r])[...]
        return carry

    lax.fori_loop(0, tb, move, 0, unroll=8)
    meta = meta_ref[...]
    moe = (_load_row_tiles(g0_ref) * meta[:, 2:3]
           + _load_row_tiles(g1_ref) * meta[:, 3:4])
    out_ref[...] = x1_ref[...] + moe


def _combine(local0, local1, tables, x1, meta, ys):
    t, d_model = x1.shape
    tb = TB_RANK
    max_chunks = tables["max_chunks"]
    kern = functools.partial(_combine_kernel, tb=tb, max_chunks=max_chunks)
    buf_rows = max_chunks * RUN_CHUNK
    return pl.pallas_call(
        kern,
        grid_spec=pltpu.PrefetchScalarGridSpec(
            num_scalar_prefetch=4,
            grid=(t // tb,),
            in_specs=[pl.BlockSpec((tb, d_model), lambda i, *_: (i, 0)),
                      pl.BlockSpec((tb, LANES), lambda i, *_: (i, 0)),
                      pl.BlockSpec(memory_space=pl.ANY)],
            out_specs=pl.BlockSpec((tb, d_model), lambda i, *_: (i, 0)),
            scratch_shapes=[pltpu.VMEM((2,) + _row_tiles((buf_rows, d_model)), F32),
                            pltpu.VMEM(_row_tiles((tb, d_model)), F32),
                            pltpu.VMEM(_row_tiles((tb, d_model)), F32),
                            pltpu.SemaphoreType.DMA((2,))]),
        out_shape=jax.ShapeDtypeStruct((t, d_model), F32),
        compiler_params=_cparams(("arbitrary",)),
        name="combine",
    )(local0, local1, tables["chunk_src"], tables["tile_chunks"], x1, meta, ys)


def _layer(h2, *, batch, seq, norm1_g, w_in, conv_w, q_norm_g, k_norm_g,
           w_conv_out, w_attn_out, w_o, norm2_g, w_group, w_router, w1, w3, w2):
    t, d_model = h2.shape
    c, sa, sb = _rope_tables(seq)
    scale = HEAD_DIM ** -0.5 * LOG2_E
    tables_q = tuple(jnp.asarray(tab * np.float32(scale)) for tab in (c, sa, sb))
    tables_k = tuple(jnp.asarray(tab) for tab in (c, sa, sb))

    cb, z, q, k, v, sgc, sga = _inproj(
        h2, norm1_g[None, :], w_in.astype(BF16), q_norm_g[None, :], k_norm_g[None, :],
        tables_q, tables_k, seq=seq)
    o = _attention(q, k, v, batch=batch, seq=seq)

    n_route = N_GROUPS + N_EXPERTS
    wr = jnp.concatenate(
        [w_group, w_router, jnp.zeros((d_model, LANES - n_route), F32)], axis=1)
    wr_hi = wr.astype(BF16)
    wr = jnp.concatenate([wr_hi, (wr - wr_hi.astype(F32)).astype(BF16)], axis=1)
    x1, u2, meta, cnt = _post(h2, cb, z, o, sgc, sga, conv_w,
                              w_conv_out.astype(BF16), w_attn_out.astype(BF16),
                              w_o.astype(BF16), norm2_g[None, :], wr, seq=seq)

    blk = MOE_ROWS
    data_rows = t * TOP_K
    local0, local1, runs = _rank(meta, cnt)
    local0, local1 = local0.reshape(-1), local1.reshape(-1)
    runs = runs.reshape(-1, 8, LANES)
    tables = _tile_tables(runs[:, 0, :N_EXPERTS], runs[:, 1, :N_EXPERTS], tb=TB_RANK)
    sched = _expert_schedule(cnt[0, :N_EXPERTS].astype(jnp.int32), blk=blk,
                             data_blocks=data_rows // blk)

    xs = _dispatch(local0, local1, tables, u2, data_rows=data_rows)
    ys = _experts(sched, xs, w1, w3, w2, blk=blk)
    return _combine(local0, local1, tables, x1, meta, ys)


def kernel(x, norm1_g, w_in, conv_w, q_norm_g, k_norm_g, w_conv_out, w_attn_out, w_o,
           norm2_g, w_group, w_router, w1, w3, w2):
    batch, seq, d_model = x.shape
    h2 = x.reshape(batch * seq, d_model)
    for l in range(norm1_g.shape[0]):
        h2 = _layer(h2, batch=batch, seq=seq, norm1_g=norm1_g[l], w_in=w_in[l],
                    conv_w=conv_w[l], q_norm_g=q_norm_g[l], k_norm_g=k_norm_g[l],
                    w_conv_out=w_conv_out[l], w_attn_out=w_attn_out[l], w_o=w_o[l],
                    norm2_g=norm2_g[l], w_group=w_group[l], w_router=w_router[l],
                    w1=w1[l], w3=w3[l], w2=w2[l])
    return h2.reshape(batch, seq, d_model)
```

```python
import functools

import jax
import jax.numpy as jnp
import numpy as np
from jax import lax
from jax.experimental import pallas as pl
from jax.experimental.pallas import tpu as pltpu

F32 = jnp.float32
BF16 = jnp.bfloat16

GRID_W = 64
EPS = 1e-6
N_HEADS = 8
N_KV_HEADS = 2
HEAD_DIM = 128
ROPE_THETA = 10000.0
N_GROUPS = 8
EXPERTS_PER_GROUP = 8
N_EXPERTS = N_GROUPS * EXPERTS_PER_GROUP
TOP_K = 2
LOG2_E = 1.4426950408889634

LANES = 128
MXU_DIM = 256
BF16_SUBLANES = 16
V7X_VMEM_LIMIT_BYTES = 56000 * 1024

TM_PROJ = 512
TQ_ATTN = 512
TK_ATTN = 512
TB_RANK = 1024
RUN_CHUNK = 8
SPARE_ROWS = RUN_CHUNK
MOE_ROWS = 640
WEIGHT_RING = 3
WEIGHT_DMA_PRIORITY = 1
DIGIT = 256.0


def _cparams(sem):
    return pltpu.CompilerParams(dimension_semantics=sem,
                                vmem_limit_bytes=V7X_VMEM_LIMIT_BYTES)


ROW_TILE = 8


def _row_tiles(shape2d):
    rows, width = shape2d
    assert width == ROW_TILE * LANES
    return (rows * ROW_TILE, LANES)


def _rows(ref, r, n=1):
    return ref.at[pl.ds(pl.multiple_of(r * ROW_TILE, ROW_TILE), n * ROW_TILE)]


def _load_row_tiles(ref):
    rows = ref.shape[0] // ROW_TILE
    return jnp.concatenate(
        [ref[pl.ds(s, rows, stride=ROW_TILE), :] for s in range(ROW_TILE)], axis=1)


def _store_row_tiles(ref, value):
    rows = value.shape[0]
    for s in range(ROW_TILE):
        ref[pl.ds(s, rows, stride=ROW_TILE), :] = value[:, s * LANES:(s + 1) * LANES]


def _resident(shape):
    nd = len(shape)
    return pl.BlockSpec(shape, lambda *_: (0,) * nd, pipeline_mode=pl.Buffered(1))


def _lane_prefix(row):
    r = lax.broadcasted_iota(jnp.int32, (LANES, LANES), 0)
    col = lax.broadcasted_iota(jnp.int32, (LANES, LANES), 1)
    upper = jnp.where(r < col, 1.0, 0.0).astype(BF16)
    return jnp.dot(jnp.broadcast_to(row, (8, LANES)).astype(BF16), upper,
                   preferred_element_type=F32)[0:1, :]


def _head_norm_rope(xh, g, c, sa, sb):
    ms = jnp.mean(xh * xh, axis=-1, keepdims=True)
    y = xh * lax.rsqrt(ms + EPS) * g
    y_next = pltpu.roll(y, HEAD_DIM - 1, axis=1)
    y_prev = pltpu.roll(y, 1, axis=1)
    return y * c + y_next * sa + y_prev * sb


def _sigmoid(x):
    return 0.5 * jnp.tanh(0.5 * x) + 0.5


def _inproj_kernel(x_ref, g1_ref, w_ref, gq_ref, gk_ref,
                   cq_ref, saq_ref, sbq_ref, ck_ref, sak_ref, sbk_ref,
                   cb_ref, z_ref, q_ref, k_ref, v_ref, sgc_ref, sga_ref,
                   *, d_conv, d_q, d_kv, d_model):
    x = x_ref[...]
    ms = jnp.mean(x * x, axis=-1, keepdims=True)
    u = (x * lax.rsqrt(ms + EPS) * g1_ref[...]).astype(BF16)

    def proj(lo, width):
        return jnp.dot(u, w_ref[:, lo:lo + width], preferred_element_type=F32)

    o_cb, o_cc, o_cx = 0, d_conv, 2 * d_conv
    o_q = 3 * d_conv
    o_k = o_q + d_q
    o_v = o_k + d_kv
    o_gc = o_v + d_kv
    o_ga = o_gc + d_model

    sgc_ref[...] = _sigmoid(proj(o_gc, d_model)).astype(BF16)
    sga_ref[...] = _sigmoid(proj(o_ga, d_model)).astype(BF16)

    q = proj(o_q, d_q)
    gq = gq_ref[...]
    cq, saq, sbq = cq_ref[...], saq_ref[...], sbq_ref[...]
    for h in range(d_q // HEAD_DIM):
        sl = slice(h * HEAD_DIM, (h + 1) * HEAD_DIM)
        q_ref[:, sl] = _head_norm_rope(q[:, sl], gq, cq, saq, sbq).astype(BF16)

    k = proj(o_k, d_kv)
    gk = gk_ref[...]
    ck, sak, sbk = ck_ref[...], sak_ref[...], sbk_ref[...]
    for h in range(d_kv // HEAD_DIM):
        sl = slice(h * HEAD_DIM, (h + 1) * HEAD_DIM)
        k_ref[:, sl] = _head_norm_rope(k[:, sl], gk, ck, sak, sbk).astype(BF16)

    z_ref[...] = (proj(o_cc, d_conv) * proj(o_cx, d_conv)).astype(BF16)
    v_ref[...] = proj(o_v, d_kv).astype(BF16)
    cb_ref[...] = proj(o_cb, d_conv).astype(BF16)


def _rope_tables(seq):
    rows = seq // GRID_W
    axis_dim = HEAD_DIM // 2
    row = np.repeat(np.arange(rows, dtype=np.float32), GRID_W)
    col = np.tile(np.arange(GRID_W, dtype=np.float32), rows)
    inv = (np.float32(ROPE_THETA)
           ** (-np.arange(0, axis_dim, 2, dtype=np.float32) / np.float32(axis_dim)))
    ang = np.concatenate([row[:, None] * inv, col[:, None] * inv], axis=-1)
    ang = ang.astype(np.float32)
    cos, sin = np.cos(ang), np.sin(ang)
    zero = np.zeros_like(sin)
    c = np.repeat(cos, 2, axis=-1)
    sa = np.stack([-sin, zero], axis=-1).reshape(seq, HEAD_DIM)
    sb = np.stack([zero, sin], axis=-1).reshape(seq, HEAD_DIM)
    return c, sa, sb


def _inproj(x2, g1, w_in_bf, gq, gk, tables_q, tables_k, *, seq):
    t, d_model = x2.shape
    d_q = N_HEADS * HEAD_DIM
    d_kv = N_KV_HEADS * HEAD_DIM
    d_in = w_in_bf.shape[1]
    d_conv = (d_in - d_q - 2 * d_kv - 2 * d_model) // 3
    tm = TM_PROJ
    nseq = seq // tm

    def row(width):
        return pl.BlockSpec((tm, width), lambda i: (i, 0))

    table = pl.BlockSpec((tm, HEAD_DIM), lambda i: (i % nseq, 0))
    kern = functools.partial(_inproj_kernel, d_conv=d_conv, d_q=d_q, d_kv=d_kv,
                             d_model=d_model)
    out_shape = [jax.ShapeDtypeStruct((t, w), BF16)
                 for w in (d_conv, d_conv, d_q, d_kv, d_kv, d_model, d_model)]
    return pl.pallas_call(
        kern,
        grid=(t // tm,),
        in_specs=[row(d_model), _resident((1, d_model)), _resident((d_model, d_in)),
                  _resident((1, HEAD_DIM)), _resident((1, HEAD_DIM)),
                  table, table, table, table, table, table],
        out_specs=[row(d_conv), row(d_conv), row(d_q), row(d_kv), row(d_kv),
                   row(d_model), row(d_model)],
        out_shape=out_shape,
        compiler_params=_cparams(("arbitrary",)),
        name="inproj",
    )(x2, g1, w_in_bf, gq, gk, *tables_q, *tables_k)


def _attn_kernel(q_ref, k_ref, v_ref, o_ref, qs_ref, vext_ref, m_ref, acc_ref,
                 *, tq, chunks, group):
    @pl.when(pl.program_id(2) == 0)
    def _():
        vext_ref[:, :HEAD_DIM] = v_ref[...]
        vext_ref[:, HEAD_DIM:] = jnp.ones((vext_ref.shape[0], HEAD_DIM), BF16)

    for g in range(group):
        qs_ref[g * tq:(g + 1) * tq, :] = q_ref[:, g * HEAD_DIM:(g + 1) * HEAD_DIM]
    m_ref[...] = jnp.full(m_ref.shape, -jnp.inf, F32)
    acc_ref[...] = jnp.zeros(acc_ref.shape, F32)

    lo = 0
    for tk in chunks:
        keys = slice(lo, lo + tk)
        lo += tk
        s = lax.dot_general(qs_ref[...], k_ref[keys, :], (((1,), (1,)), ((), ())),
                            preferred_element_type=F32)
        m_prev = m_ref[...]
        m_new = jnp.maximum(m_prev, jnp.max(s, axis=-1, keepdims=True))
        alpha = jnp.exp2(m_prev - m_new)
        p = jnp.concatenate(
            [jnp.exp2(s[:, c * LANES:(c + 1) * LANES] - m_new) for c in range(tk // LANES)],
            axis=1).astype(BF16)
        pv = jnp.dot(p, vext_ref[keys, :], preferred_element_type=F32)
        acc_ref[...] = jnp.concatenate([alpha, alpha], axis=1) * acc_ref[...] + pv
        m_ref[...] = m_new

    out = acc_ref[:, :HEAD_DIM] / acc_ref[:, HEAD_DIM:]
    for g in range(group):
        o_ref[:, g * HEAD_DIM:(g + 1) * HEAD_DIM] = out[g * tq:(g + 1) * tq].astype(BF16)


def _attention(q, k, v, *, batch, seq):
    t = q.shape[0]
    group = N_HEADS // N_KV_HEADS
    tq, tk = TQ_ATTN, TK_ATTN
    nq = seq // tq
    gw = group * HEAD_DIM
    chunks = (tk // 2,) + (tk,) * (seq // tk - 1) + (tk // 2,)
    kern = functools.partial(_attn_kernel, tq=tq, chunks=chunks, group=group)
    return pl.pallas_call(
        kern,
        grid=(batch, N_KV_HEADS, nq),
        in_specs=[pl.BlockSpec((tq, gw), lambda b, h, i: (b * nq + i, h)),
                  pl.BlockSpec((seq, HEAD_DIM), lambda b, h, i: (b, h)),
                  pl.BlockSpec((seq, HEAD_DIM), lambda b, h, i: (b, h))],
        out_specs=pl.BlockSpec((tq, gw), lambda b, h, i: (b * nq + i, h)),
        out_shape=jax.ShapeDtypeStruct((t, N_HEADS * HEAD_DIM), BF16),
        scratch_shapes=[pltpu.VMEM((group * tq, HEAD_DIM), BF16),
                        pltpu.VMEM((seq, 2 * HEAD_DIM), BF16),
                        pltpu.VMEM((group * tq, LANES), F32),
                        pltpu.VMEM((group * tq, 2 * HEAD_DIM), F32)],
        compiler_params=_cparams(("arbitrary", "arbitrary", "arbitrary")),
        name="attention",
    )(q, k, v)


def _route(logits):
    rows = logits.shape[0]
    lane = lax.broadcasted_iota(jnp.int32, (rows, LANES), 1).astype(F32)
    neg = -jnp.inf
    big = float(2 * LANES)
    is_group = lane < N_GROUPS
    gl = jnp.where(is_group, logits, neg)
    gmax = jnp.max(gl, axis=-1, keepdims=True)
    gidx = jnp.min(jnp.where(gl == gmax, lane, big), axis=-1, keepdims=True)
    gsum = jnp.sum(jnp.where(is_group, jnp.exp(logits - gmax), 0.0), axis=-1,
                   keepdims=True)
    pg = 1.0 / gsum
    lane_group = jnp.floor(lane * (1.0 / EXPERTS_PER_GROUP)) - 1.0
    mine = (lane_group == gidx) & (lane >= N_GROUPS) & (lane < N_GROUPS + N_EXPERTS)
    sel = jnp.where(mine, logits, neg)
    v1 = jnp.max(sel, axis=-1, keepdims=True)
    i1 = jnp.min(jnp.where(sel == v1, lane, big), axis=-1, keepdims=True)
    sel2 = jnp.where(lane == i1, neg, sel)
    v2 = jnp.max(sel2, axis=-1, keepdims=True)
    i2 = jnp.min(jnp.where(sel2 == v2, lane, big), axis=-1, keepdims=True)
    t2 = jnp.exp(v2 - v1)
    den = 1.0 + t2
    wgt1 = pg * (1.0 / den)
    wgt2 = pg * (t2 / den)
    e1 = i1 - N_GROUPS
    e2 = i2 - N_GROUPS
    meta = jnp.where(lane == 0, e1,
                     jnp.where(lane == 1, e2,
                               jnp.where(lane == 2, wgt1,
                                         jnp.where(lane == 3, wgt2, 0.0))))
    picked = jnp.where((lane == e1) | (lane == e2), 1.0, 0.0)
    return meta, jnp.sum(picked, axis=0, keepdims=True)


def _post_kernel(x_ref, cb_ref, z_ref, zprev_ref, znext_ref, o_ref, sgc_ref, sga_ref,
                 cw_ref, wc_ref, wa_ref, wo_ref, g2_ref, wr_ref,
                 x1_ref, u2_ref, meta_ref, cnt_ref, *, tm, nseq):
    i = pl.program_id(0)
    at_start = (i % nseq) == 0
    at_end = (i % nseq) == nseq - 1
    rowid = lax.broadcasted_iota(jnp.int32, (tm, 1), 0)

    @pl.when(i == 0)
    def _():
        cnt_ref[...] = jnp.zeros(cnt_ref.shape, F32)

    y_attn = jnp.dot(o_ref[...], wa_ref[...], preferred_element_type=F32)

    y_conv = None
    for c in range(z_ref.shape[1] // MXU_DIM):
        sl = slice(c * MXU_DIM, (c + 1) * MXU_DIM)
        z = z_ref[:, sl].astype(F32)
        prev_row = zprev_ref[BF16_SUBLANES - 1:BF16_SUBLANES, sl].astype(F32)
        next_row = znext_ref[0:1, sl].astype(F32)
        prev_row = jnp.where(at_start, 0.0, prev_row)
        next_row = jnp.where(at_end, 0.0, next_row)
        zp = jnp.where(rowid == 0, prev_row, pltpu.roll(z, 1, axis=0))
        zn = jnp.where(rowid == tm - 1, next_row, pltpu.roll(z, tm - 1, axis=0))
        conv = cw_ref[0:1, sl] * zp + cw_ref[1:2, sl] * z + cw_ref[2:3, sl] * zn
        cbz = (cb_ref[:, sl].astype(F32) * conv).astype(BF16)
        part = jnp.dot(cbz, wc_ref[sl, :], preferred_element_type=F32)
        y_conv = part if y_conv is None else y_conv + part
    merged = (sgc_ref[...].astype(F32) * y_conv
              + sga_ref[...].astype(F32) * y_attn).astype(BF16)
    x1 = x_ref[...] + jnp.dot(merged, wo_ref[...], preferred_element_type=F32)
    x1_ref[...] = x1

    ms = jnp.mean(x1 * x1, axis=-1, keepdims=True)
    u2 = x1 * lax.rsqrt(ms + EPS) * g2_ref[...]
    _store_row_tiles(u2_ref, u2)

    u2_hi = u2.astype(BF16)
    u2_lo = (u2 - u2_hi.astype(F32)).astype(BF16)
    hi_part = jnp.dot(u2_hi, wr_ref[...], preferred_element_type=F32)
    lo_part = jnp.dot(u2_lo, wr_ref[:, :LANES], preferred_element_type=F32)
    logits = hi_part[:, :LANES] + (hi_part[:, LANES:] + lo_part)
    meta, picked = _route(logits)
    meta_ref[...] = meta
    cnt_ref[...] += picked


def _post(x2, cb, z, o, sgc, sga, conv_w, wc, wa, wo, g2, wr, *, seq):
    t, d_model = x2.shape
    tm = TM_PROJ
    nseq = seq // tm
    hb = tm // BF16_SUBLANES
    nhalo = t // BF16_SUBLANES
    d_conv = cb.shape[1]
    d_q = o.shape[1]

    def row(width):
        return pl.BlockSpec((tm, width), lambda i: (i, 0))

    kern = functools.partial(_post_kernel, tm=tm, nseq=nseq)
    return pl.pallas_call(
        kern,
        grid=(t // tm,),
        in_specs=[row(d_model), row(d_conv), row(d_conv),
                  pl.BlockSpec((BF16_SUBLANES, d_conv),
                               lambda i: (jnp.maximum(i * hb - 1, 0), 0)),
                  pl.BlockSpec((BF16_SUBLANES, d_conv),
                               lambda i: (jnp.minimum((i + 1) * hb, nhalo - 1), 0)),
                  row(d_q), row(d_model), row(d_model),
                  _resident(conv_w.shape), _resident(wc.shape), _resident(wa.shape),
                  _resident(wo.shape), _resident(g2.shape), _resident(wr.shape)],
        out_specs=[row(d_model),
                   pl.BlockSpec(_row_tiles((tm, d_model)), lambda i: (i, 0)),
                   row(LANES),
                   pl.BlockSpec((1, LANES), lambda i: (0, 0))],
        out_shape=[jax.ShapeDtypeStruct((t, d_model), F32),
                   jax.ShapeDtypeStruct(_row_tiles((t, d_model)), F32),
                   jax.ShapeDtypeStruct((t, LANES), F32),
                   jax.ShapeDtypeStruct((1, LANES), F32)],
        compiler_params=_cparams(("arbitrary",)),
        name="post",
    )(x2, cb, z, z, z, o, sgc, sga, conv_w, wc, wa, wo, g2, wr)


def _rank_kernel(meta_ref, cnt_ref, l0_ref, l1_ref, runs_ref,
                 carry_ref, pstart_ref, lower_ref, *, tb):
    i = pl.program_id(0)
    lane = lax.broadcasted_iota(jnp.int32, (tb, LANES), 1).astype(F32)
    meta = meta_ref[...]
    oh1 = jnp.where(lane == meta[:, 0:1], 1.0, 0.0)
    oh2 = jnp.where(lane == meta[:, 1:2], 1.0, 0.0)
    c = oh1 + oh2

    @pl.when(i == 0)
    def _():
        cnt = cnt_ref[...]
        high = jnp.floor(cnt * (1.0 / DIGIT))
        pstart_ref[...] = _lane_prefix(high) * DIGIT + _lane_prefix(cnt - high * DIGIT)
        carry_ref[...] = jnp.zeros(carry_ref.shape, F32)
        r = lax.broadcasted_iota(jnp.int32, (tb, tb), 0)
        col = lax.broadcasted_iota(jnp.int32, (tb, tb), 1)
        lower_ref[...] = jnp.where(col < r, 1.0, 0.0).astype(BF16)

    prefix = jnp.dot(lower_ref[...], c.astype(BF16), preferred_element_type=F32)
    run_start = carry_ref[...] + pstart_ref[...]
    run_len = jnp.sum(c, axis=0, keepdims=True)

    nchunk = jnp.floor((run_len + (RUN_CHUNK - 1)) * (1.0 / RUN_CHUNK))
    local = prefix + _lane_prefix(nchunk) * RUN_CHUNK

    v0 = jnp.sum(oh1 * local, axis=-1, keepdims=True)
    v1 = jnp.sum(oh2 * local, axis=-1, keepdims=True)
    rows = jnp.transpose(jnp.where(lane == 0, v0, jnp.where(lane == 1, v1, 0.0)))
    l0_ref[...] = rows[0:1, :].astype(jnp.int32)
    l1_ref[...] = rows[1:2, :].astype(jnp.int32)
    row = lax.broadcasted_iota(jnp.int32, (8, LANES), 0)
    runs = jnp.where(row == 0, run_start, jnp.where(row == 1, run_len, 0.0))
    runs_ref[...] = runs.astype(jnp.int32)
    carry_ref[...] += run_len


def _rank(meta, cnt):
    t = meta.shape[0]
    tb = TB_RANK
    kern = functools.partial(_rank_kernel, tb=tb)
    per_token = pl.BlockSpec((1, tb), lambda i: (0, i))
    return pl.pallas_call(
        kern,
        grid=(t // tb,),
        in_specs=[pl.BlockSpec((tb, LANES), lambda i: (i, 0)),
                  pl.BlockSpec((1, LANES), lambda i: (0, 0))],
        out_specs=[per_token] * 2 + [pl.BlockSpec((8, LANES), lambda i: (i, 0))],
        out_shape=[jax.ShapeDtypeStruct((1, t), jnp.int32)] * 2
                  + [jax.ShapeDtypeStruct((t // tb * 8, LANES), jnp.int32)],
        scratch_shapes=[pltpu.VMEM((1, LANES), F32), pltpu.VMEM((1, LANES), F32),
                        pltpu.VMEM((tb, tb), BF16)],
        compiler_params=_cparams(("arbitrary",)),
        name="rank",
    )(meta, cnt)


def _dispatch_kernel(local0_sm, local1_sm, full_src_sm, full_dst_sm, nfull_sm,
                     rem_src_sm, rem_dst_sm, rem_len_sm,
                     u2_ref, xs_ref, buf_ref, sem, *, tb, max_full):
    step = pl.program_id(0)
    nsteps = pl.num_programs(0)
    base = step * tb
    slot = step % 2
    tile_buf = buf_ref.at[slot]

    def wait_tile(s):
        pltpu.make_async_copy(_rows(buf_ref.at[s], 0, tb * TOP_K),
                              _rows(xs_ref, 0, tb * TOP_K), sem.at[s]).wait()

    @pl.when(step >= 2)
    def _():
        wait_tile(slot)

    def move(r, carry):
        row = _rows(u2_ref, r)[...]
        _rows(tile_buf, local0_sm[base + r])[...] = row
        _rows(tile_buf, local1_sm[base + r])[...] = row
        return carry

    lax.fori_loop(0, tb, move, 0, unroll=8)

    def full_chunk(c, carry):
        pltpu.make_async_copy(
            _rows(tile_buf, full_src_sm[step * max_full + c], RUN_CHUNK),
            _rows(xs_ref, full_dst_sm[step * max_full + c], RUN_CHUNK),
            sem.at[slot]).start()
        return carry

    lax.fori_loop(0, nfull_sm[step], full_chunk, 0)

    def remainder(e, carry):
        idx = step * N_EXPERTS + e
        src, dst, length = rem_src_sm[idx], rem_dst_sm[idx], rem_len_sm[idx]
        for bit in range(RUN_CHUNK.bit_length() - 1):
            size = 1 << bit
            higher = lax.shift_left(lax.shift_right_logical(length, bit + 1), bit + 1)

            @pl.when((lax.shift_right_logical(length, bit) & 1) == 1)
            def _():
                pltpu.make_async_copy(_rows(tile_buf, src + higher, size),
                                      _rows(xs_ref, dst + higher, size),
                                      sem.at[slot]).start()
        return carry

    lax.fori_loop(0, N_EXPERTS, remainder, 0)

    @pl.when(step == nsteps - 1)
    def _():
        @pl.when(step >= 1)
        def _():
            wait_tile(1 - slot)

        wait_tile(slot)


def _tile_tables(run_start, run_len, *, tb):
    shift = RUN_CHUNK.bit_length() - 1
    nchunk = (run_len + RUN_CHUNK - 1) >> shift
    cum = jnp.cumsum(nchunk, axis=1)
    first = cum - nchunk
    local_start = first * RUN_CHUNK

    def flat_list(count, max_count, value_at):
        ccum = jnp.cumsum(count, axis=1)
        cfirst = (ccum - count)[:, None, :]
        c_idx = jnp.arange(max_count, dtype=jnp.int32)[None, :, None]
        owns = (cfirst <= c_idx) & (c_idx < ccum[:, None, :])
        return [jnp.sum(jnp.where(owns, v[:, None, :] + (c_idx - cfirst) * RUN_CHUNK, 0),
                        axis=2).astype(jnp.int32) for v in value_at], ccum[:, -1]

    max_chunks = tb * TOP_K // RUN_CHUNK + N_EXPERTS
    (chunk_src,), tile_chunks = flat_list(nchunk, max_chunks, [run_start])
    max_full = tb * TOP_K // RUN_CHUNK
    nfull = run_len >> shift
    (full_src, full_dst), tile_full = flat_list(nfull, max_full, [local_start, run_start])
    whole = nfull * RUN_CHUNK
    return dict(
        max_chunks=max_chunks, chunk_src=chunk_src.reshape(-1),
        tile_chunks=tile_chunks.astype(jnp.int32),
        max_full=max_full, full_src=full_src.reshape(-1), full_dst=full_dst.reshape(-1),
        tile_full=tile_full.astype(jnp.int32),
        rem_src=(local_start + whole).reshape(-1).astype(jnp.int32),
        rem_dst=(run_start + whole).reshape(-1).astype(jnp.int32),
        rem_len=(run_len - whole).reshape(-1).astype(jnp.int32))


def _dispatch(local0, local1, tables, u2, *, data_rows):
    t = u2.shape[0] // ROW_TILE
    tb = TB_RANK
    d_model = ROW_TILE * LANES
    kern = functools.partial(_dispatch_kernel, tb=tb, max_full=tables["max_full"])
    buf_rows = tables["max_chunks"] * RUN_CHUNK
    return pl.pallas_call(
        kern,
        grid_spec=pltpu.PrefetchScalarGridSpec(
            num_scalar_prefetch=8,
            grid=(t // tb,),
            in_specs=[pl.BlockSpec((tb * ROW_TILE, LANES), lambda i, *_: (i, 0))],
            out_specs=pl.BlockSpec(memory_space=pl.ANY),
            scratch_shapes=[pltpu.VMEM((2,) + _row_tiles((buf_rows, d_model)), F32),
                            pltpu.SemaphoreType.DMA((2,))]),
        out_shape=jax.ShapeDtypeStruct((data_rows * ROW_TILE, LANES), F32),
        compiler_params=_cparams(("arbitrary",)),
        name="dispatch",
    )(local0, local1, tables["full_src"], tables["full_dst"], tables["tile_full"],
      tables["rem_src"], tables["rem_dst"], tables["rem_len"], u2)


def _experts_kernel(start_sm, nvalid_sm, expert_sm, first_sm, ahead_sm, slot_sm, head_sm,
                    xs_hbm, w1_hbm, w3_hbm, w2_hbm, ys_hbm,
                    xbuf_ref, ybuf_ref, zrow_ref, w1f_ref, w3f_ref, w2f_ref,
                    w13b_ref, w2b_ref, wsem, isem, osem, zsem,
                    *, d_expert, blk, data_rows):
    v = pl.program_id(0)
    nsteps = pl.num_programs(0)
    nbits = blk.bit_length()

    def block_copies(step, inbound, visit):
        n, start, slot = nvalid_sm[step], start_sm[step], lax.rem(step, 2)
        for bit in range(nbits):
            size = 1 << bit
            higher = lax.shift_left(lax.shift_right_logical(n, bit + 1), bit + 1)

            @pl.when((lax.shift_right_logical(n, bit) & 1) == 1)
            def _():
                if inbound:
                    visit(pltpu.make_async_copy(_rows(xs_hbm, start + higher, size),
                                                _rows(xbuf_ref.at[slot], higher, size),
                                                isem.at[slot]))
                else:
                    visit(pltpu.make_async_copy(_rows(ybuf_ref.at[slot], higher, size),
                                                _rows(ys_hbm, start + higher, size),
                                                osem.at[slot]))

    def weight_copies(expert, slot):
        return [pltpu.make_async_copy(src.at[expert], dst.at[slot], wsem.at[slot])
                for src, dst in ((w1_hbm, w1f_ref), (w3_hbm, w3f_ref), (w2_hbm, w2f_ref))]

    spare_fill = pltpu.make_async_copy(zrow_ref, _rows(ys_hbm, data_rows, SPARE_ROWS), zsem)

    @pl.when(v == 0)
    def _():
        xbuf_ref[...] = jnp.zeros(xbuf_ref.shape, F32)
        zrow_ref[...] = jnp.zeros(zrow_ref.shape, F32)
        spare_fill.start()
        for j in range(WEIGHT_RING - 1):
            @pl.when(head_sm[j] >= 0)
            def _():
                for c in weight_copies(head_sm[j], j):
                    c.start(priority=WEIGHT_DMA_PRIORITY)
        block_copies(v, True, lambda c: c.start())

    @pl.when(v + 1 < nsteps)
    def _():
        block_copies(v + 1, True, lambda c: c.start())

    @pl.when(first_sm[v] == 1)
    def _():
        slot = slot_sm[v]
        for c in weight_copies(expert_sm[v], slot):
            c.wait()

        @pl.when(ahead_sm[v] >= 0)
        def _():
            ahead_slot = lax.rem(slot + (WEIGHT_RING - 1), WEIGHT_RING)
            for c in weight_copies(ahead_sm[v], ahead_slot):
                c.start(priority=WEIGHT_DMA_PRIORITY)

        w13b_ref[:, :d_expert] = w1f_ref[slot].astype(BF16)
        w13b_ref[:, d_expert:] = w3f_ref[slot].astype(BF16)
        w2b_ref[...] = w2f_ref[slot].astype(BF16)

    block_copies(v, True, lambda c: c.wait())

    @pl.when(v >= 2)
    def _():
        block_copies(v - 2, False, lambda c: c.wait())

    @pl.when(nvalid_sm[v] > 0)
    def _():
        slot = lax.rem(v, 2)
        xb = _load_row_tiles(xbuf_ref.at[slot]).astype(BF16)
        h = jnp.dot(xb, w13b_ref[...], preferred_element_type=F32)
        h1 = h[:, :d_expert]
        h3 = h[:, d_expert:]
        a = (h1 * _sigmoid(h1) * h3).astype(BF16)
        _store_row_tiles(ybuf_ref.at[slot],
                         jnp.dot(a, w2b_ref[...], preferred_element_type=F32))

    block_copies(v, False, lambda c: c.start())

    @pl.when(v == nsteps - 1)
    def _():
        @pl.when(v >= 1)
        def _():
            block_copies(v - 1, False, lambda c: c.wait())

        block_copies(v, False, lambda c: c.wait())
        spare_fill.wait()


def _experts(sched, xs, w1, w3, w2, *, blk):
    data_rows = xs.shape[0] // ROW_TILE
    d_model, d_expert = w1.shape[-2:]
    kern = functools.partial(_experts_kernel, d_expert=d_expert, blk=blk,
                             data_rows=data_rows)
    hbm = pl.BlockSpec(memory_space=pl.ANY)
    block_buf = pltpu.VMEM((2,) + _row_tiles((blk, d_model)), F32)
    return pl.pallas_call(
        kern,
        grid_spec=pltpu.PrefetchScalarGridSpec(
            num_scalar_prefetch=7,
            grid=(sched["start"].shape[0],),
            in_specs=[hbm, hbm, hbm, hbm],
            out_specs=hbm,
            scratch_shapes=[block_buf, block_buf,
                            pltpu.VMEM(_row_tiles((SPARE_ROWS, d_model)), F32),
                            pltpu.VMEM((WEIGHT_RING, d_model, d_expert), F32),
                            pltpu.VMEM((WEIGHT_RING, d_model, d_expert), F32),
                            pltpu.VMEM((WEIGHT_RING, d_expert, d_model), F32),
                            pltpu.VMEM((d_model, 2 * d_expert), BF16),
                            pltpu.VMEM((d_expert, d_model), BF16),
                            pltpu.SemaphoreType.DMA((WEIGHT_RING,)),
                            pltpu.SemaphoreType.DMA((2,)), pltpu.SemaphoreType.DMA((2,)),
                            pltpu.SemaphoreType.DMA]),
        out_shape=jax.ShapeDtypeStruct(((data_rows + SPARE_ROWS) * ROW_TILE, LANES), F32),
        compiler_params=_cparams(("arbitrary",)),
        name="experts",
    )(sched["start"], sched["nvalid"], sched["expert"], sched["first"], sched["ahead"],
      sched["slot"], sched["head"], xs, w1, w3, w2)


def _expert_schedule(counts, *, blk, data_blocks):
    seg_end = jnp.cumsum(counts)
    seg_start = seg_end - counts
    nblk = (counts + blk - 1) // blk
    vend = jnp.cumsum(nblk)
    vfirst = vend - nblk
    nsteps_live = vend[-1]
    total = data_blocks + N_EXPERTS
    v_idx = jnp.arange(total, dtype=jnp.int32)
    vi = jnp.maximum(jnp.minimum(v_idx, nsteps_live - 1), 0)
    expert = jnp.minimum(jnp.sum(vend[None, :] <= vi[:, None], axis=1),
                         N_EXPERTS - 1).astype(jnp.int32)
    live = v_idx < nsteps_live
    within = vi - vfirst[expert]
    start = seg_start[expert] + within * blk
    nvalid = jnp.where(live, jnp.clip(seg_end[expert] - start, 0, blk), 0)
    first = live & (within == 0)
    has_rows = nblk > 0
    used_rank = jnp.cumsum(has_rows) - 1
    j_idx = jnp.arange(N_EXPERTS + WEIGHT_RING, dtype=jnp.int32)
    e_idx = jnp.arange(N_EXPERTS, dtype=jnp.int32)
    hit = has_rows[None, :] & (used_rank[None, :] == j_idx[:, None])
    used_list = jnp.sum(jnp.where(hit, e_idx[None, :] + 1, 0), axis=1) - 1
    rank_v = used_rank[expert]
    ahead = jnp.where(live, used_list[rank_v + (WEIGHT_RING - 1)], -1)
    as_i32 = lambda a: a.astype(jnp.int32)
    return dict(start=as_i32(jnp.where(live, start, 0)), nvalid=as_i32(nvalid),
                expert=expert, first=as_i32(first), ahead=as_i32(ahead),
                slot=as_i32(rank_v % WEIGHT_RING), head=as_i32(used_list[:WEIGHT_RING - 1]))


def _combine_kernel(local0_sm, local1_sm, chunk_src_sm, nchunk_sm,
                    x1_ref, meta_ref, ys_ref, out_ref,
                    buf_ref, g0_ref, g1_ref, sem, *, tb, max_chunks):
    step = pl.program_id(0)
    nsteps = pl.num_programs(0)

    def start_chunks(tile):
        slot = tile % 2

        def per_chunk(c, carry):
            pltpu.make_async_copy(
                _rows(ys_ref, chunk_src_sm[tile * max_chunks + c], RUN_CHUNK),
                _rows(buf_ref.at[slot], c * RUN_CHUNK, RUN_CHUNK),
                sem.at[slot]).start()
            return carry

        lax.fori_loop(0, nchunk_sm[tile], per_chunk, 0)

    def wait_chunks(tile):
        slot = tile % 2
        n = nchunk_sm[tile]
        for bit in range(max_chunks.bit_length()):
            @pl.when((lax.shift_right_logical(n, bit) & 1) == 1)
            def _():
                rows = RUN_CHUNK << bit
                pltpu.make_async_copy(_rows(ys_ref, 0, rows),
                                      _rows(buf_ref.at[slot], 0, rows),
                                      sem.at[slot]).wait()

    @pl.when(step == 0)
    def _():
        start_chunks(step)

    @pl.when(step + 1 < nsteps)
    def _():
        start_chunks(step + 1)

    wait_chunks(step)

    base = step * tb
    tile_buf = buf_ref.at[step % 2]

    def move(r, carry):
        for g_ref, local_sm in ((g0_ref, local0_sm), (g1_ref, local1_sm)):
            _rows(g_ref, r)[...] = _rows(tile_buf, local_sm[base + r])[...]
        return carry

    lax.fori_loop(0, tb, move, 0, unroll=8)
    meta = meta_ref[...]
    moe = (_load_row_tiles(g0_ref) * meta[:, 2:3]
           + _load_row_tiles(g1_ref) * meta[:, 3:4])
    out_ref[...] = x1_ref[...] + moe


def _combine(local0, local1, tables, x1, meta, ys):
    t, d_model = x1.shape
    tb = TB_RANK
    max_chunks = tables["max_chunks"]
    kern = functools.partial(_combine_kernel, tb=tb, max_chunks=max_chunks)
    buf_rows = max_chunks * RUN_CHUNK
    return pl.pallas_call(
        kern,
        grid_spec=pltpu.PrefetchScalarGridSpec(
            num_scalar_prefetch=4,
            grid=(t // tb,),
            in_specs=[pl.BlockSpec((tb, d_model), lambda i, *_: (i, 0)),
                      pl.BlockSpec((tb, LANES), lambda i, *_: (i, 0)),
                      pl.BlockSpec(memory_space=pl.ANY)],
            out_specs=pl.BlockSpec((tb, d_model), lambda i, *_: (i, 0)),
            scratch_shapes=[pltpu.VMEM((2,) + _row_tiles((buf_rows, d_model)), F32),
                            pltpu.VMEM(_row_tiles((tb, d_model)), F32),
                            pltpu.VMEM(_row_tiles((tb, d_model)), F32),
                            pltpu.SemaphoreType.DMA((2,))]),
        out_shape=jax.ShapeDtypeStruct((t, d_model), F32),
        compiler_params=_cparams(("arbitrary",)),
        name="combine",
    )(local0, local1, tables["chunk_src"], tables["tile_chunks"], x1, meta, ys)


def _layer(h2, *, batch, seq, norm1_g, w_in, conv_w, q_norm_g, k_norm_g,
           w_conv_out, w_attn_out, w_o, norm2_g, w_group, w_router, w1, w3, w2):
    t, d_model = h2.shape
    c, sa, sb = _rope_tables(seq)
    scale = HEAD_DIM ** -0.5 * LOG2_E
    tables_q = tuple(jnp.asarray(tab * np.float32(scale)) for tab in (c, sa, sb))
    tables_k = tuple(jnp.asarray(tab) for tab in (c, sa, sb))

    cb, z, q, k, v, sgc, sga = _inproj(
        h2, norm1_g[None, :], w_in.astype(BF16), q_norm_g[None, :], k_norm_g[None, :],
        tables_q, tables_k, seq=seq)
    o = _attention(q, k, v, batch=batch, seq=seq)

    n_route = N_GROUPS + N_EXPERTS
    wr = jnp.concatenate(
        [w_group, w_router, jnp.zeros((d_model, LANES - n_route), F32)], axis=1)
    wr_hi = wr.astype(BF16)
    wr = jnp.concatenate([wr_hi, (wr - wr_hi.astype(F32)).astype(BF16)], axis=1)
    x1, u2, meta, cnt = _post(h2, cb, z, o, sgc, sga, conv_w,
                              w_conv_out.astype(BF16), w_attn_out.astype(BF16),
                              w_o.astype(BF16), norm2_g[None, :], wr, seq=seq)

    blk = MOE_ROWS
    data_rows = t * TOP_K
    local0, local1, runs = _rank(meta, cnt)
    local0, local1 = local0.reshape(-1), local1.reshape(-1)
    runs = runs.reshape(-1, 8, LANES)
    tables = _tile_tables(runs[:, 0, :N_EXPERTS], runs[:, 1, :N_EXPERTS], tb=TB_RANK)
    sched = _expert_schedule(cnt[0, :N_EXPERTS].astype(jnp.int32), blk=blk,
                             data_blocks=data_rows // blk)

    xs = _dispatch(local0, local1, tables, u2, data_rows=data_rows)
    ys = _experts(sched, xs, w1, w3, w2, blk=blk)
    return _combine(local0, local1, tables, x1, meta, ys)


def kernel(x, norm1_g, w_in, conv_w, q_norm_g, k_norm_g, w_conv_out, w_attn_out, w_o,
           norm2_g, w_group, w_router, w1, w3, w2):
    batch, seq, d_model = x.shape
    h2 = x.reshape(batch * seq, d_model)
    for l in range(norm1_g.shape[0]):
        h2 = _layer(h2, batch=batch, seq=seq, norm1_g=norm1_g[l], w_in=w_in[l],
                    conv_w=conv_w[l], q_norm_g=q_norm_g[l], k_norm_g=k_norm_g[l],
                    w_conv_out=w_conv_out[l], w_attn_out=w_attn_out[l], w_o=w_o[l],
                    norm2_g=norm2_g[l], w_group=w_group[l], w_router=w_router[l],
                    w1=w1[l], w3=w3[l], w2=w2[l])
    return h2.reshape(batch, seq, d_model)
```
